```python
import jax, jax.numpy as jnp
from jax import lax
import numpy as np

D_MODEL = 1024
BATCH = 4
SEQ = 8192
DEPTH = 1

GRID_W = 64
CTX_LEN = 256
D_FF = 2816
MACARON_WEIGHT = 0.5
S5_WIDTH = 512
S5_GROUP = 16
S5_GROUPS = S5_WIDTH // S5_GROUP
S5_STATE = 64
GLA_HEADS = 4
GLA_DK = 64
GLA_DV = 128
GLA_KEY = GLA_HEADS * GLA_DK
GLA_VAL = GLA_HEADS * GLA_DV
GLA_GATE_RANK = 16
GLA_GATE_NORM = 16.0
GLA_CHUNK = 64
N_DIR = 2
N_BRANCH = 2
N_MOD = 9
RMS_EPS = 1e-6
D_IN = S5_WIDTH + 2 * GLA_KEY + 2 * GLA_VAL + N_DIR * GLA_GATE_RANK + N_BRANCH * D_MODEL

kernel_name = 'hybrid_s5_gla_macaron_dit_layer'

f32 = jnp.float32


def rms_norm(x, g):
    xf = x.astype(f32)
    y = xf * lax.rsqrt(jnp.mean(xf * xf, axis=-1, keepdims=True) + RMS_EPS)
    return (y * g.astype(f32)).astype(x.dtype)


def modulate(h, shift, scale):
    return h * (1 + scale) + shift


def swiglu(h, w_gate, w_up, w_down):
    return (jax.nn.silu(h @ w_gate) * (h @ w_up)) @ w_down


def ffn_sublayer(t, shift, scale, gate, norm_g, w_gate, w_up, w_down):
    h = modulate(rms_norm(t, norm_g), shift, scale)
    return t + gate * (MACARON_WEIGHT * swiglu(h, w_gate, w_up, w_down))


def to_col_major(t, rows):
    b, l, ch = t.shape
    return t.reshape(b, rows, GRID_W, ch).swapaxes(1, 2).reshape(b, l, ch)


def from_col_major(t, rows):
    b, l, ch = t.shape
    return t.reshape(b, GRID_W, rows, ch).swapaxes(1, 2).reshape(b, l, ch)


def _linear_recurrence(e1, e2):
    a1, b1 = e1
    a2, b2 = e2
    return a1 * a2, a2 * b1 + b2


def s5_scan(bu, lam_bar, h0):
    bu = bu.at[:, 0].add(lam_bar * h0)
    a = jnp.broadcast_to(lam_bar, bu.shape)
    _, h = lax.associative_scan(_linear_recurrence, (a, bu), axis=1)
    return h


def s5_discretise(lam_re, lam_im, log_dt, b_re, b_im):
    lam = lax.complex(jnp.minimum(lam_re.astype(f32), -1e-4), lam_im.astype(f32))
    dt = jnp.exp(log_dt.astype(f32))[:, None]
    lam_bar = jnp.exp(lam * dt)
    b = lax.complex(b_re.astype(f32), b_im.astype(f32))
    b_bar = ((lam_bar - 1.0) / lam)[..., None] * b
    return lam_bar, b_bar


def s5_branch(u_lat, u_ctx, lam_re, lam_im, log_dt, b_re, b_im, c_re, c_im, d_skip, glu_w, glu_b, need_ctx):
    def grouped(u):
        b, l, _ = u.shape
        return u.astype(f32).reshape(b, l, S5_GROUPS, S5_GROUP)

    def finish(y, u):
        b, l = u.shape[:2]
        y = y.reshape(b, l, S5_WIDTH) + d_skip.astype(f32) * u.astype(f32)
        y = jax.nn.gelu(y)
        y = y * jax.nn.sigmoid(y @ glu_w.astype(f32) + glu_b.astype(f32))
        return y.astype(u.dtype)

    bsz = u_lat.shape[0]
    ul, uc = grouped(u_lat), grouped(u_ctx)
    y_l = jnp.zeros(ul.shape, f32)
    y_c = jnp.zeros(uc.shape, f32)
    for d in range(N_DIR):
        lam_bar, b_bar = s5_discretise(lam_re[d], lam_im[d], log_dt[d], b_re[d], b_im[d])
        c_mat = lax.complex(c_re[d].astype(f32), c_im[d].astype(f32))
        bu_l = jnp.einsum('blgc,gpc->blgp', ul.astype(jnp.complex64), b_bar)
        bu_c = jnp.einsum('blgc,gpc->blgp', uc.astype(jnp.complex64), b_bar)
        if d == 1:
            bu_l, bu_c = jnp.flip(bu_l, 1), jnp.flip(bu_c, 1)
        h_c = s5_scan(bu_c, lam_bar, jnp.zeros((bsz, S5_GROUPS, S5_STATE), jnp.complex64))
        h_l = s5_scan(bu_l, lam_bar, h_c[:, -1])
        r_l = jnp.einsum('blgp,gcp->blgc', h_l, c_mat).real
        y_l = y_l + (jnp.flip(r_l, 1) if d == 1 else r_l)
        if need_ctx:
            r_c = jnp.einsum('blgp,gcp->blgc', h_c, c_mat).real
            y_c = y_c + (jnp.flip(r_c, 1) if d == 1 else r_c)
    return finish(y_l, u_lat), (finish(y_c, u_ctx) if need_ctx else None)


def gla_chunked(q, k, v, g, s0):
    bsz, nh, l, dk = q.shape
    dv = v.shape[-1]
    n = l // GLA_CHUNK
    q, k, g = [t.reshape(bsz, nh, n, GLA_CHUNK, dk) for t in (q, k, g)]
    v = v.reshape(bsz, nh, n, GLA_CHUNK, dv)
    gc = jnp.cumsum(g, axis=3)
    g_ref = gc[:, :, :, GLA_CHUNK // 2:GLA_CHUNK // 2 + 1]
    g_last = gc[:, :, :, -1:]
    scores = jnp.einsum('bhnid,bhnjd->bhnij', q * jnp.exp(gc - g_ref), k * jnp.exp(g_ref - gc))
    upto_i = jnp.tril(jnp.ones((GLA_CHUNK, GLA_CHUNK), dtype=bool))
    scores = jnp.where(upto_i, scores, 0.0)
    o_intra = jnp.einsum('bhnij,bhnje->bhnie', scores, v)
    kv = jnp.einsum('bhnjd,bhnje->bhnde', k * jnp.exp(g_last - gc), v)
    decay = jnp.exp(g_last[:, :, :, 0])

    def step(s, inp):
        dec, kv_n = inp
        return dec[..., None] * s + kv_n, s

    s_final, s_start = lax.scan(step, s0, (jnp.moveaxis(decay, 2, 0), jnp.moveaxis(kv, 2, 0)))
    s_start = jnp.moveaxis(s_start, 0, 2)
    o_inter = jnp.einsum('bhnid,bhnde->bhnie', q * jnp.exp(gc), s_start)
    return (o_intra + o_inter).reshape(bsz, nh, l, dv), s_final


def gla_branch(q_l, k_l, v_l, r_l, glr_l, q_c, k_c, v_c, r_c, glr_c, gate_up, gate_b, norm_g, need_ctx):
    def heads(t, dh):
        b, l, _ = t.shape
        return t.astype(f32).reshape(b, l, GLA_HEADS, dh).transpose(0, 2, 1, 3)

    def log_decay(glr, d):
        z = glr[..., d * GLA_GATE_RANK:(d + 1) * GLA_GATE_RANK] @ gate_up[d] + gate_b[d]
        return heads(jax.nn.log_sigmoid(z.astype(f32)) / GLA_GATE_NORM, GLA_DK)

    def readout(o, r):
        o = o.transpose(0, 2, 1, 3)
        o = o * lax.rsqrt(jnp.mean(o * o, axis=-1, keepdims=True) + RMS_EPS)
        o = o * norm_g.astype(f32).reshape(GLA_HEADS, GLA_DV)
        b, l = o.shape[:2]
        return (o.reshape(b, l, GLA_VAL) * jax.nn.silu(r.astype(f32))).astype(r.dtype)

    scale = GLA_DK ** -0.5
    lat = (heads(q_l, GLA_DK) * scale, heads(k_l, GLA_DK), heads(v_l, GLA_DV))
    cx = (heads(q_c, GLA_DK) * scale, heads(k_c, GLA_DK), heads(v_c, GLA_DV))
    bsz = q_l.shape[0]
    o_l = jnp.zeros(lat[2].shape, f32)
    o_c = jnp.zeros(cx[2].shape, f32)
    for d in range(N_DIR):
        seq_l = lat + (log_decay(glr_l, d),)
        seq_c = cx + (log_decay(glr_c, d),)
        if d == 1:
            seq_l = tuple(jnp.flip(t, 2) for t in seq_l)
            seq_c = tuple(jnp.flip(t, 2) for t in seq_c)
        oc_d, s_ctx = gla_chunked(*seq_c, jnp.zeros((bsz, GLA_HEADS, GLA_DK, GLA_DV), f32))
        ol_d, _ = gla_chunked(*seq_l, s_ctx)
        o_l = o_l + (jnp.flip(ol_d, 2) if d == 1 else ol_d)
        if need_ctx:
            o_c = o_c + (jnp.flip(oc_d, 2) if d == 1 else oc_d)
    return readout(o_l, r_l), (readout(o_c, r_c) if need_ctx else None)


def token_mixing(h_lat, h_ctx, w_in, s5_lambda_re, s5_lambda_im, s5_log_dt, s5_b_re, s5_b_im, s5_c_re, s5_c_im,
                 s5_d, s5_glu_w, s5_glu_b, s5_out, gla_gate_up, gla_gate_b, gla_norm, gla_out, w_o, rows, need_ctx):
    sizes = (S5_WIDTH, GLA_KEY, GLA_KEY, GLA_VAL, GLA_VAL, N_DIR * GLA_GATE_RANK, D_MODEL, D_MODEL)
    points = np.cumsum(sizes)[:-1].tolist()
    u_l, q_l, k_l, v_l, r_l, glr_l, ga_l, gb_l = jnp.split(h_lat @ w_in, points, axis=-1)
    u_c, q_c, k_c, v_c, r_c, glr_c, ga_c, gb_c = jnp.split(h_ctx @ w_in, points, axis=-1)
    ya_l, ya_c = s5_branch(u_l, u_c, s5_lambda_re, s5_lambda_im, s5_log_dt, s5_b_re, s5_b_im,
                           s5_c_re, s5_c_im, s5_d, s5_glu_w, s5_glu_b, need_ctx)
    q_l, k_l, v_l, r_l, glr_l = [to_col_major(t, rows) for t in (q_l, k_l, v_l, r_l, glr_l)]
    yb_l, yb_c = gla_branch(q_l, k_l, v_l, r_l, glr_l, q_c, k_c, v_c, r_c, glr_c,
                            gla_gate_up, gla_gate_b, gla_norm, need_ctx)
    yb_l = from_col_major(yb_l, rows)

    def merge(ya, yb, ga, gb):
        return (jax.nn.sigmoid(ga) * (ya @ s5_out) + jax.nn.sigmoid(gb) * (yb @ gla_out)) @ w_o

    y_ctx = merge(ya_c, yb_c, ga_c, gb_c) if need_ctx else None
    return merge(ya_l, yb_l, ga_l, gb_l), y_ctx


def setup_inputs(seed: int = 0) -> dict:
    key = jax.random.key(seed)
    ks = iter(jax.random.split(key, 40))

    def nrm(shape, scale):
        return jax.random.normal(next(ks), shape, f32) * scale

    L, D, F = DEPTH, D_MODEL, D_FF
    G, P, C = S5_GROUPS, S5_STATE, S5_GROUP
    lam_im = jnp.broadcast_to(jnp.pi * jnp.arange(P, dtype=f32), (L, N_DIR, G, P)) + nrm((L, N_DIR, G, P), 0.01)
    log_dt = jax.random.uniform(next(ks), (L, N_DIR, G), f32, float(np.log(1e-3)), float(np.log(1e-1)))
    return {
        'x': nrm((BATCH, SEQ, D), 1.0),
        'c': nrm((BATCH, D), 1.0),
        'ctx': nrm((BATCH, CTX_LEN, D), 1.0),
        'c_ctx': nrm((D,), 1.0),
        'ada_w': nrm((L, D, N_MOD * D), D ** -0.5),
        'ada_b': nrm((L, N_MOD * D), 0.02),
        'ffn1_norm': 1.0 + nrm((L, D), 0.02),
        'ffn1_w_gate': nrm((L, D, F), D ** -0.5),
        'ffn1_w_up': nrm((L, D, F), D ** -0.5),
        'ffn1_w_down': nrm((L, F, D), F ** -0.5),
        'mix_norm': 1.0 + nrm((L, D), 0.02),
        'w_in': nrm((L, D, D_IN), D ** -0.5),
        's5_lambda_re': -0.5 + nrm((L, N_DIR, G, P), 0.01),
        's5_lambda_im': lam_im,
        's5_log_dt': log_dt,
        's5_b_re': nrm((L, N_DIR, G, P, C), (2 * C) ** -0.5),
        's5_b_im': nrm((L, N_DIR, G, P, C), (2 * C) ** -0.5),
        's5_c_re': nrm((L, N_DIR, G, C, P), (2 * P) ** -0.5),
        's5_c_im': nrm((L, N_DIR, G, C, P), (2 * P) ** -0.5),
        's5_d': nrm((L, S5_WIDTH), 1.0),
        's5_glu_w': nrm((L, S5_WIDTH, S5_WIDTH), S5_WIDTH ** -0.5),
        's5_glu_b': nrm((L, S5_WIDTH), 0.02),
        's5_out': nrm((L, S5_WIDTH, D), S5_WIDTH ** -0.5),
        'gla_gate_up': nrm((L, N_DIR, GLA_GATE_RANK, GLA_KEY), GLA_GATE_RANK ** -0.5),
        'gla_gate_b': nrm((L, N_DIR, GLA_KEY), 0.1),
        'gla_norm': 1.0 + nrm((L, GLA_VAL), 0.02),
        'gla_out': nrm((L, GLA_VAL, D), GLA_VAL ** -0.5),
        'w_o': nrm((L, D, D), D ** -0.5),
        'ffn2_norm': 1.0 + nrm((L, D), 0.02),
        'ffn2_w_gate': nrm((L, D, F), D ** -0.5),
        'ffn2_w_up': nrm((L, D, F), D ** -0.5),
        'ffn2_w_down': nrm((L, F, D), F ** -0.5),
        'final_norm': 1.0 + nrm((D,), 0.02),
    }


def reference(x, c, ctx, c_ctx, ada_w, ada_b, ffn1_norm, ffn1_w_gate, ffn1_w_up, ffn1_w_down, mix_norm, w_in,
              s5_lambda_re, s5_lambda_im, s5_log_dt, s5_b_re, s5_b_im, s5_c_re, s5_c_im, s5_d, s5_glu_w, s5_glu_b,
              s5_out, gla_gate_up, gla_gate_b, gla_norm, gla_out, w_o, ffn2_norm, ffn2_w_gate, ffn2_w_up,
              ffn2_w_down, final_norm):
    rows = x.shape[1] // GRID_W
    for i in range(DEPTH):
        last = i == DEPTH - 1
        ml = jnp.split((jax.nn.silu(c) @ ada_w[i] + ada_b[i])[:, None, :], N_MOD, axis=-1)
        mc = jnp.split((jax.nn.silu(c_ctx) @ ada_w[i] + ada_b[i])[None, None, :], N_MOD, axis=-1)
        x = ffn_sublayer(x, ml[0], ml[1], ml[2], ffn1_norm[i], ffn1_w_gate[i], ffn1_w_up[i], ffn1_w_down[i])
        ctx = ffn_sublayer(ctx, mc[0], mc[1], mc[2], ffn1_norm[i], ffn1_w_gate[i], ffn1_w_up[i], ffn1_w_down[i])
        h_lat = modulate(rms_norm(x, mix_norm[i]), ml[3], ml[4])
        h_ctx = modulate(rms_norm(ctx, mix_norm[i]), mc[3], mc[4])
        y_lat, y_ctx = token_mixing(h_lat, h_ctx, w_in[i], s5_lambda_re[i], s5_lambda_im[i], s5_log_dt[i],
                                    s5_b_re[i], s5_b_im[i], s5_c_re[i], s5_c_im[i], s5_d[i], s5_glu_w[i],
                                    s5_glu_b[i], s5_out[i], gla_gate_up[i], gla_gate_b[i], gla_norm[i],
                                    gla_out[i], w_o[i], rows, not last)
        x = x + ml[5] * y_lat
        x = ffn_sublayer(x, ml[6], ml[7], ml[8], ffn2_norm[i], ffn2_w_gate[i], ffn2_w_up[i], ffn2_w_down[i])
        if not last:
            ctx = ctx + mc[5] * y_ctx
            ctx = ffn_sublayer(ctx, mc[6], mc[7], mc[8], ffn2_norm[i], ffn2_w_gate[i], ffn2_w_up[i], ffn2_w_down[i])
    return rms_norm(x, final_norm)
```

```python
import functools

import jax
import jax.numpy as jnp
from jax import lax
from jax.experimental import pallas as pl
from jax.experimental.pallas import tpu as pltpu

F32 = jnp.float32
BF16 = jnp.bfloat16

RMS_EPS = 1e-6
MACARON_WEIGHT = 0.5
GRID_W = 64
N_MOD = 9
S5_GROUP = 16
S5_STATE = 64
S5_CHUNK = 16
GLA_HEADS = 4
GLA_DK = 64
GLA_DV = 128
GLA_CHUNK = 64
GLA_GATE_RANK = 16
GLA_GATE_NORM = 16.0
GLA_KEY = GLA_HEADS * GLA_DK
GLA_VAL = GLA_HEADS * GLA_DV
PROJ_W = 4096
V7X_VMEM_LIMIT_BYTES = 56 * 1024 * 1024


def _params(*sem):
    return pltpu.CompilerParams(dimension_semantics=sem, vmem_limit_bytes=V7X_VMEM_LIMIT_BYTES)


def _const_spec(shape):
    nd = len(shape)
    return pl.BlockSpec(shape, lambda *_: (0,) * nd, pipeline_mode=pl.Buffered(1))


def _rms_mod(x, g, shift, scale):
    ms = jnp.mean(x * x, axis=-1, keepdims=True)
    return (x * lax.rsqrt(ms + RMS_EPS) * g) * (1.0 + scale) + shift


def _swiglu_residual(x, h, gate, wg_ref, wu_ref, wd_ref, fchunk):
    acc = None
    for f0 in range(0, wg_ref.shape[1], fchunk):
        gg = jnp.dot(h, wg_ref[:, f0:f0 + fchunk], preferred_element_type=F32)
        uu = jnp.dot(h, wu_ref[:, f0:f0 + fchunk], preferred_element_type=F32)
        a = (gg * jax.nn.sigmoid(gg) * uu).astype(BF16)
        o = jnp.dot(a, wd_ref[f0:f0 + fchunk, :], preferred_element_type=F32)
        acc = o if acc is None else acc + o
    return x + gate * (MACARON_WEIGHT * acc)


def _ada_kernel(c_ref, w_ref, b_ref, o_ref):
    cv = c_ref[...]
    s = cv * jax.nn.sigmoid(cv)
    o_ref[...] = jnp.dot(s, w_ref[...], preferred_element_type=F32,
                         precision=lax.Precision.HIGHEST) + b_ref[...]


def _ada(cvec, ada_w, ada_b):
    rows, d = cvec.shape
    n = ada_w.shape[1]
    bn = n // 8
    return pl.pallas_call(
        _ada_kernel,
        grid=(n // bn,),
        in_specs=[pl.BlockSpec((rows, d), lambda j: (0, 0)),
                  pl.BlockSpec((d, bn), lambda j: (0, j)),
                  pl.BlockSpec((1, bn), lambda j: (0, j))],
        out_specs=pl.BlockSpec((rows, bn), lambda j: (0, j)),
        out_shape=jax.ShapeDtypeStruct((rows, n), F32),
        compiler_params=_params("arbitrary"),
    )(cvec, ada_w, ada_b.reshape(1, n))


def _front_kernel(x_ref, mod_ref, n1_ref, wg_ref, wu_ref, wd_ref, n2_ref, win_ref, wglr_ref, gup_ref,
                  gbias_ref, x1_ref, proj_ref, g_ref, *, fchunk):
    x = x_ref[...]
    m = mod_ref[...]
    h = _rms_mod(x, n1_ref[...], m[0:1], m[1:2]).astype(BF16)
    x1 = _swiglu_residual(x, h, m[2:3], wg_ref, wu_ref, wd_ref, fchunk)
    x1_ref[...] = x1
    h2 = _rms_mod(x1, n2_ref[...], m[3:4], m[4:5]).astype(BF16)
    for c0 in range(0, PROJ_W, 1024):
        p = jnp.dot(h2, win_ref[:, c0:c0 + 1024], preferred_element_type=F32)
        proj_ref[:, c0:c0 + 1024] = p.astype(BF16)
    glr = jnp.dot(h2, wglr_ref[...], preferred_element_type=F32)
    z = jnp.dot(glr.astype(BF16), gup_ref[...], preferred_element_type=F32) + gbias_ref[...]
    log_sig = jnp.minimum(z, 0.0) - jnp.log1p(jnp.exp(-jnp.abs(z)))
    g_ref[...] = log_sig * (1.0 / GLA_GATE_NORM)


def _front(x, mods3, mod_row, w, tm, fchunk):
    b, l, d = x.shape
    f = w["wg1"].shape[1]
    tok = lambda width: pl.BlockSpec((None, tm, width), lambda bi, i: (bi, i, 0))
    return pl.pallas_call(
        functools.partial(_front_kernel, fchunk=fchunk),
        grid=(b, l // tm),
        in_specs=[tok(d),
                  pl.BlockSpec((None, N_MOD, d), lambda bi, i: (mod_row(bi), 0, 0)),
                  _const_spec((1, d)), _const_spec((d, f)), _const_spec((d, f)), _const_spec((f, d)),
                  _const_spec((1, d)), _const_spec((d, PROJ_W)), _const_spec((d, 128)),
                  _const_spec((128, 2 * GLA_KEY)), _const_spec((1, 2 * GLA_KEY))],
        out_specs=[tok(d), tok(PROJ_W), tok(2 * GLA_KEY)],
        out_shape=[jax.ShapeDtypeStruct((b, l, d), F32),
                   jax.ShapeDtypeStruct((b, l, PROJ_W), BF16),
                   jax.ShapeDtypeStruct((b, l, 2 * GLA_KEY), F32)],
        compiler_params=_params("arbitrary", "arbitrary"),
    )(x, mods3, w["n1"], w["wg1"], w["wu1"], w["wd1"], w["n2"], w["win"], w["wglr"], w["gup"], w["gbias"])


def _cmul_add(lr, li, hr, hi, sr, si):
    return lr * hr - li * hi + sr, lr * hi + li * hr + si


def _s5_kernel(xc_ref, xl_ref, toep_ref, ws_ref, cp_ref, lam_ref, y_ref, s_ref, *, rc, rl):
    npair = ws_ref.shape[0]
    for pp in range(npair):
        for src, r0, nr in ((xc_ref, 0, rc), (xl_ref, rc, rl)):
            s = jnp.dot(src[2 * pp], ws_ref[pp, 0:256, :], preferred_element_type=F32)
            s = s + jnp.dot(src[2 * pp + 1], ws_ref[pp, 256:512, :], preferred_element_type=F32)
            s_ref[pp, r0:r0 + nr, :] = s

    nct = rc // 8
    nt = (rc + rl) // 8
    lo = lax.broadcasted_iota(jnp.int32, (8, 128), 0) < 4
    lam = [[jnp.broadcast_to(lam_ref[pp, :, k * 128:(k + 1) * 128], (8, 128)) for k in range(4)]
           for pp in range(npair)]
    swap = lambda v: pltpu.roll(v, 4, 0)

    def step(it, carry):
        rf = pl.multiple_of(it * 8, 8)
        mb = jnp.where(it < nct, nct - 1 - it, nt - 1 - (it - nct))
        rb = pl.multiple_of(mb * 8, 8)
        out = []
        for pp in range(npair):
            hfr, hfi, hbr, hbi = carry[pp]
            lfr, lfi, lbr, lbi = lam[pp]
            vr = s_ref[pp, pl.ds(rf, 8), 0:128]
            vi = s_ref[pp, pl.ds(rf, 8), 128:256]
            tr, ti = _cmul_add(lfr, lfi, hfr, hfi, swap(vr), swap(vi))
            s_ref[pp, pl.ds(rf, 8), 0:128] = jnp.where(lo, hfr, tr)
            s_ref[pp, pl.ds(rf, 8), 128:256] = jnp.where(lo, hfi, ti)
            nr_, ni_ = _cmul_add(lfr, lfi, tr, ti, vr, vi)
            hfr = jnp.where(lo, swap(nr_), nr_)
            hfi = jnp.where(lo, swap(ni_), ni_)
            vr = s_ref[pp, pl.ds(rb, 8), 256:384]
            vi = s_ref[pp, pl.ds(rb, 8), 384:512]
            tr, ti = _cmul_add(lbr, lbi, hbr, hbi, swap(vr), swap(vi))
            s_ref[pp, pl.ds(rb, 8), 256:384] = jnp.where(lo, tr, hbr)
            s_ref[pp, pl.ds(rb, 8), 384:512] = jnp.where(lo, ti, hbi)
            nr_, ni_ = _cmul_add(lbr, lbi, tr, ti, vr, vi)
            hbr = jnp.where(lo, nr_, swap(nr_))
            hbi = jnp.where(lo, ni_, swap(ni_))
            out.append((hfr, hfi, hbr, hbi))
        return tuple(out)

    zero = jnp.zeros((8, 128), F32)
    lax.fori_loop(0, nt, step, tuple((zero, zero, zero, zero) for _ in range(npair)))

    for pp in range(npair):
        hin = s_ref[pp, rc:rc + rl, :].astype(BF16)
        yp = jnp.dot(hin, cp_ref[pp], preferred_element_type=F32)
        for gi in range(2):
            g = 2 * pp + gi
            y = yp[:, gi * 256:(gi + 1) * 256] + jnp.dot(xl_ref[g], toep_ref[g], preferred_element_type=F32)
            y_ref[g] = y.astype(BF16)


def _s5(xc, xl, toep, ws, cp, lam16, gps):
    ng, rc, _ = xc.shape
    rl = xl.shape[1]
    npair = gps // 2
    return pl.pallas_call(
        functools.partial(_s5_kernel, rc=rc, rl=rl),
        grid=(ng // gps,),
        in_specs=[pl.BlockSpec((gps, rc, 256), lambda i: (i, 0, 0)),
                  pl.BlockSpec((gps, rl, 256), lambda i: (i, 0, 0)),
                  pl.BlockSpec((gps, 256, 256), lambda i: (i, 0, 0)),
                  pl.BlockSpec((npair, 512, 512), lambda i: (i, 0, 0)),
                  pl.BlockSpec((npair, 512, 512), lambda i: (i, 0, 0)),
                  pl.BlockSpec((npair, 1, 512), lambda i: (i, 0, 0))],
        out_specs=pl.BlockSpec((gps, rl, 256), lambda i: (i, 0, 0)),
        out_shape=jax.ShapeDtypeStruct((ng, rl, 256), BF16),
        scratch_shapes=[pltpu.VMEM((npair, rc + rl, 512), F32)],
        compiler_params=_params("arbitrary"),
    )(xc, xl, toep, ws, cp, lam16)


def _s5_weights(lam_re, lam_im, log_dt, b_re, b_im, c_re, c_im):
    nd, ng, p = lam_re.shape
    t = S5_CHUNK
    lam = lax.complex(jnp.minimum(lam_re.astype(F32), -1e-4), lam_im.astype(F32))
    dt = jnp.exp(log_dt.astype(F32))[..., None]
    lam_bar = jnp.exp(lam * dt)
    b_bar = ((lam_bar - 1.0) / lam)[..., None] * lax.complex(b_re.astype(F32), b_im.astype(F32))
    c_mat = lax.complex(c_re.astype(F32), c_im.astype(F32))
    pows = [jnp.ones_like(lam_bar)]
    for _ in range(t):
        pows.append(pows[-1] * lam_bar)
    pows = jnp.stack(pows)

    kern = jnp.einsum("dgxp,kdgp,dgpc->dgkxc", c_mat, pows[:t], b_bar).real
    diff = jnp.arange(t)[None, :] - jnp.arange(t)[:, None]
    tf = jnp.where((diff >= 0)[None, :, :, None, None], kern[0][:, jnp.clip(diff, 0, t - 1)], 0.0)
    tb = jnp.where((diff <= 0)[None, :, :, None, None], kern[1][:, jnp.clip(-diff, 0, t - 1)], 0.0)
    toep = (tf + tb).transpose(0, 1, 4, 2, 3).reshape(ng, t * S5_GROUP, t * S5_GROUP)

    wf = jnp.einsum("sgp,gpc->gscp", pows[:t, 0][::-1], b_bar[0])
    wb = jnp.einsum("sgp,gpc->gscp", pows[:t, 1], b_bar[1])
    eye2 = jnp.eye(2, dtype=F32)

    def pair_rows(a):
        a = a.reshape(ng // 2, 2, t * S5_GROUP, p)
        return a[:, :, :, None, :] * eye2[None, :, None, :, None]

    ws = jnp.stack([pair_rows(wf.real), pair_rows(wf.imag), pair_rows(wb.real), pair_rows(wb.imag)], axis=3)
    ws = ws.reshape(ng // 2, 2 * t * S5_GROUP, 4 * 2 * p)

    mf = jnp.einsum("gxp,tgp->gptx", c_mat[0], pows[1:t + 1, 0])
    mb = jnp.einsum("gxp,tgp->gptx", c_mat[1], pows[1:t + 1, 1][::-1])

    def pair_cols(a):
        a = a.reshape(ng // 2, 2, p, t * S5_GROUP)
        return a[:, :, :, None, :] * eye2[None, :, None, :, None]

    cp = jnp.stack([pair_cols(mf.real), pair_cols(-mf.imag), pair_cols(mb.real), pair_cols(-mb.imag)], axis=1)
    cp = cp.reshape(ng // 2, 4 * 2 * p, 2 * t * S5_GROUP)

    l16 = pows[t]
    lam16 = jnp.stack([l16[0].real, l16[0].imag, l16[1].real, l16[1].imag], axis=1)
    lam16 = lam16.reshape(ng // 2, 2, 4, p).transpose(0, 2, 1, 3).reshape(ng // 2, 1, 4 * 2 * p)
    return toep.astype(BF16), ws.astype(BF16), cp.astype(BF16), lam16


def _gla_chunk(q, k, v, g, st, rev, need_out):
    c = GLA_CHUNK
    ri = lax.broadcasted_iota(jnp.int32, (c, c), 0)
    ci = lax.broadcasted_iota(jnp.int32, (c, c), 1)
    tri = (ri <= ci) if rev else (ri >= ci)
    gc = jnp.dot(tri.astype(F32), g, preferred_element_type=F32, precision=lax.Precision.HIGHEST)
    i_ref = c // 2 - 1 if rev else c // 2
    i_last = 0 if rev else c - 1
    g_ref = gc[i_ref:i_ref + 1, :]
    g_last = gc[i_last:i_last + 1, :]
    lane_head = lax.broadcasted_iota(jnp.int32, (1, GLA_KEY), 1) // GLA_DK

    kl = (k * jnp.exp(g_last - gc)).astype(BF16)
    vt = v.astype(F32).T.astype(BF16)
    kv = jnp.dot(vt, kl, preferred_element_type=F32)
    st_new = st * jnp.exp(g_last)
    for h in range(GLA_HEADS):
        st_new = st_new + jnp.where(lane_head == h, kv[h * GLA_DV:(h + 1) * GLA_DV, :], 0.0)
    if not need_out:
        return None, st_new

    qe = q * jnp.exp(gc - g_ref)
    ke = (k * jnp.exp(g_ref - gc)).astype(BF16)
    qg = q * jnp.exp(gc)
    stack = lambda a: jnp.concatenate(
        [jnp.where(lane_head == h, a, 0.0) for h in range(GLA_HEADS)], axis=0).astype(BF16)
    nt_dims = (((1,), (1,)), ((), ()))
    sc = lax.dot_general(stack(qe), ke, nt_dims, preferred_element_type=F32)
    rs = lax.broadcasted_iota(jnp.int32, (GLA_HEADS * c, c), 0) % c
    cs = lax.broadcasted_iota(jnp.int32, (GLA_HEADS * c, c), 1)
    keep = (rs <= cs) if rev else (rs >= cs)
    sc = jnp.where(keep, sc, 0.0).astype(BF16)
    oi = jnp.dot(sc, v, preferred_element_type=F32)
    oo = lax.dot_general(stack(qg), st.astype(BF16), nt_dims, preferred_element_type=F32)
    o = jnp.concatenate(
        [oi[h * c:(h + 1) * c, h * GLA_DV:(h + 1) * GLA_DV] + oo[h * c:(h + 1) * c, :]
         for h in range(GLA_HEADS)], axis=1)
    return o, st_new


def _gla_kernel(qf_ref, kf_ref, vf_ref, gf_ref, qb_ref, kb_ref, vb_ref, gb_ref, kc_ref, vc_ref, gc_ref,
                of_ref, ob_ref, stf_ref, stb_ref):
    c = GLA_CHUNK

    @pl.when(pl.program_id(1) == 0)
    def _():
        nctx = kc_ref.shape[0] // c
        stf = jnp.zeros(stf_ref.shape, F32)
        stb = jnp.zeros(stb_ref.shape, F32)
        for n in range(nctx):
            rows = slice(n * c, (n + 1) * c)
            _, stf = _gla_chunk(None, kc_ref[rows, :].astype(F32), vc_ref[rows, :],
                                gc_ref[rows, 0:GLA_KEY], stf, False, False)
            rows = slice((nctx - 1 - n) * c, (nctx - n) * c)
            _, stb = _gla_chunk(None, kc_ref[rows, :].astype(F32), vc_ref[rows, :],
                                gc_ref[rows, GLA_KEY:2 * GLA_KEY], stb, True, False)
        stf_ref[...] = stf
        stb_ref[...] = stb

    nch = qf_ref.shape[0] // c
    stf = stf_ref[...]
    stb = stb_ref[...]
    for n in range(nch):
        rows = slice(n * c, (n + 1) * c)
        o, stf = _gla_chunk(qf_ref[rows, :].astype(F32), kf_ref[rows, :].astype(F32), vf_ref[rows, :],
                            gf_ref[rows, :], stf, False, True)
        of_ref[rows, :] = o.astype(of_ref.dtype)
        rows = slice((nch - 1 - n) * c, (nch - n) * c)
        o, stb = _gla_chunk(qb_ref[rows, :].astype(F32), kb_ref[rows, :].astype(F32), vb_ref[rows, :],
                            gb_ref[rows, :], stb, True, True)
        ob_ref[rows, :] = o.astype(ob_ref.dtype)
    stf_ref[...] = stf
    stb_ref[...] = stb


def _gla(proj, g, projc, gc):
    b, l, _ = proj.shape
    rows = l // GRID_W
    lc = projc.shape[1]
    pv = proj.reshape(b, rows, GRID_W * PROJ_W)
    gv = g.reshape(b, rows, GRID_W * 2 * GLA_KEY)
    last = GRID_W - 1
    kq = PROJ_W // GLA_KEY
    kv = PROJ_W // GLA_VAL
    col = lambda width, fn: pl.BlockSpec((None, rows, width), fn)
    in_specs = [
        col(GLA_KEY, lambda bi, ci: (bi, 0, ci * kq + 2)),
        col(GLA_KEY, lambda bi, ci: (bi, 0, ci * kq + 3)),
        col(GLA_VAL, lambda bi, ci: (bi, 0, ci * kv + 2)),
        col(GLA_KEY, lambda bi, ci: (bi, 0, ci * 2)),
        col(GLA_KEY, lambda bi, ci: (bi, 0, (last - ci) * kq + 2)),
        col(GLA_KEY, lambda bi, ci: (bi, 0, (last - ci) * kq + 3)),
        col(GLA_VAL, lambda bi, ci: (bi, 0, (last - ci) * kv + 2)),
        col(GLA_KEY, lambda bi, ci: (bi, 0, (last - ci) * 2 + 1)),
        pl.BlockSpec((None, lc, GLA_KEY), lambda bi, ci: (bi, 0, 3)),
        pl.BlockSpec((None, lc, GLA_VAL), lambda bi, ci: (bi, 0, 2)),
        pl.BlockSpec((None, lc, 2 * GLA_KEY), lambda bi, ci: (bi, 0, 0)),
    ]
    out_shape = jax.ShapeDtypeStruct((b, rows, GRID_W * GLA_VAL), BF16)
    of, ob = pl.pallas_call(
        _gla_kernel,
        grid=(b, GRID_W),
        in_specs=in_specs,
        out_specs=[col(GLA_VAL, lambda bi, ci: (bi, 0, ci)),
                   col(GLA_VAL, lambda bi, ci: (bi, 0, last - ci))],
        out_shape=[out_shape, out_shape],
        scratch_shapes=[pltpu.VMEM((GLA_DV, GLA_KEY), F32), pltpu.VMEM((GLA_DV, GLA_KEY), F32)],
        compiler_params=_params("arbitrary", "arbitrary"),
    )(pv, pv, pv, gv, pv, pv, pv, gv, projc, projc, gc)
    return of.reshape(b, l, GLA_VAL), ob.reshape(b, l, GLA_VAL)


def _back_kernel(x1_ref, ys_ref, u_ref, of_ref, ob_ref, r_ref, ga_ref, gb_ref, mod_ref, dskip_ref, gluw_ref,
                 glub_ref, s5out_ref, gnorm_ref, glaout_ref, wo_ref, n3_ref, wg_ref, wu_ref, wd_ref, fin_ref,
                 out_ref, *, fchunk):
    m = mod_ref[...]
    ya = ys_ref[...].astype(F32) + dskip_ref[...] * u_ref[...].astype(F32)
    ya = 0.5 * ya * (1.0 + jnp.tanh(0.7978845608028654 * (ya + 0.044715 * (ya * ya * ya))))
    gl = jnp.dot(ya.astype(BF16), gluw_ref[...], preferred_element_type=F32) + glub_ref[...]
    ya = ya * jax.nn.sigmoid(gl)

    o = of_ref[...].astype(F32) + ob_ref[...].astype(F32)
    heads = []
    for h in range(GLA_HEADS):
        oh = o[:, h * GLA_DV:(h + 1) * GLA_DV]
        heads.append(oh * lax.rsqrt(jnp.mean(oh * oh, axis=-1, keepdims=True) + RMS_EPS))
    r = r_ref[...].astype(F32)
    yb = jnp.concatenate(heads, axis=1) * gnorm_ref[...] * (r * jax.nn.sigmoid(r))

    pa = jnp.dot(ya.astype(BF16), s5out_ref[...], preferred_element_type=F32)
    pb = jnp.dot(yb.astype(BF16), glaout_ref[...], preferred_element_type=F32)
    mg = jax.nn.sigmoid(ga_ref[...].astype(F32)) * pa + jax.nn.sigmoid(gb_ref[...].astype(F32)) * pb
    y = jnp.dot(mg.astype(BF16), wo_ref[...], preferred_element_type=F32)
    x2 = x1_ref[...] + m[5:6] * y
    h = _rms_mod(x2, n3_ref[...], m[6:7], m[7:8]).astype(BF16)
    x3 = _swiglu_residual(x2, h, m[8:9], wg_ref, wu_ref, wd_ref, fchunk)
    ms = jnp.mean(x3 * x3, axis=-1, keepdims=True)
    out_ref[...] = x3 * lax.rsqrt(ms + RMS_EPS) * fin_ref[...]


def _back(x1, ys, proj, of, ob, mods3, w, tm, fchunk):
    b, l, d = x1.shape
    f = w["wg2"].shape[1]
    s5w = ys.shape[-1]
    tok = lambda width, blk=0: pl.BlockSpec((None, tm, width), lambda bi, i: (bi, i, blk))
    return pl.pallas_call(
        functools.partial(_back_kernel, fchunk=fchunk),
        grid=(b, l // tm),
        in_specs=[tok(d), tok(s5w), tok(s5w, 0), tok(GLA_VAL), tok(GLA_VAL), tok(GLA_VAL, 3),
                  tok(d, 2), tok(d, 3),
                  pl.BlockSpec((None, N_MOD, d), lambda bi, i: (bi, 0, 0)),
                  _const_spec((1, s5w)), _const_spec((s5w, s5w)), _const_spec((1, s5w)),
                  _const_spec((s5w, d)), _const_spec((1, GLA_VAL)), _const_spec((GLA_VAL, d)),
                  _const_spec((d, d)), _const_spec((1, d)), _const_spec((d, f)), _const_spec((d, f)),
                  _const_spec((f, d)), _const_spec((1, d))],
        out_specs=tok(d),
        out_shape=jax.ShapeDtypeStruct((b, l, d), F32),
        compiler_params=_params("arbitrary", "arbitrary"),
    )(x1, ys, proj, of, ob, proj, proj, proj, mods3, w["dskip"], w["gluw"], w["glub"], w["s5out"],
      w["gnorm"], w["glaout"], w["wo"], w["n3"], w["wg2"], w["wu2"], w["wd2"], w["fin"])


def _to_chunk_major(u, ng):
    b, l, _ = u.shape
    u = u.reshape(b, l // S5_CHUNK, S5_CHUNK, ng, S5_GROUP).transpose(3, 1, 0, 2, 4)
    return u.reshape(ng, (l // S5_CHUNK) * b, S5_CHUNK * S5_GROUP)


def _from_chunk_major(y, b):
    ng, r, _ = y.shape
    y = y.reshape(ng, r // b, b, S5_CHUNK, S5_GROUP).transpose(2, 1, 3, 0, 4)
    return y.reshape(b, (r // b) * S5_CHUNK, ng * S5_GROUP)


def kernel(x, c, ctx, c_ctx, ada_w, ada_b, ffn1_norm, ffn1_w_gate, ffn1_w_up, ffn1_w_down, mix_norm, w_in,
           s5_lambda_re, s5_lambda_im, s5_log_dt, s5_b_re, s5_b_im, s5_c_re, s5_c_im, s5_d, s5_glu_w, s5_glu_b,
           s5_out, gla_gate_up, gla_gate_b, gla_norm, gla_out, w_o, ffn2_norm, ffn2_w_gate, ffn2_w_up,
           ffn2_w_down, final_norm):
    b, l, d = x.shape
    assert ada_w.shape[0] == 1 and b == 4 and b + 1 <= 8
    s5w = s5_d.shape[-1]
    ng = s5w // S5_GROUP
    fchunk = ffn1_w_gate.shape[-1] // 2

    cvec = jnp.concatenate([c, c_ctx[None, :], jnp.zeros((8 - b - 1, d), F32)], axis=0)
    mods3 = _ada(cvec, ada_w[0], ada_b[0]).reshape(8, N_MOD, d)

    wi = w_in[0]
    o_glr = s5w + 2 * GLA_KEY + 2 * GLA_VAL
    win = jnp.concatenate([wi[:, :s5w], wi[:, s5w:s5w + GLA_KEY] * (GLA_DK ** -0.5),
                           wi[:, s5w + GLA_KEY:o_glr], wi[:, o_glr + 2 * GLA_GATE_RANK:]], axis=1)
    wglr = jnp.pad(wi[:, o_glr:o_glr + 2 * GLA_GATE_RANK], ((0, 0), (0, 128 - 2 * GLA_GATE_RANK)))
    gup = jnp.zeros((128, 2 * GLA_KEY), F32)
    gup = gup.at[0:GLA_GATE_RANK, 0:GLA_KEY].set(gla_gate_up[0, 0])
    gup = gup.at[GLA_GATE_RANK:2 * GLA_GATE_RANK, GLA_KEY:].set(gla_gate_up[0, 1])
    row = lambda v: v.reshape(1, -1).astype(F32)
    w = dict(
        n1=row(ffn1_norm[0]), wg1=ffn1_w_gate[0].astype(BF16), wu1=ffn1_w_up[0].astype(BF16),
        wd1=ffn1_w_down[0].astype(BF16), n2=row(mix_norm[0]), win=win.astype(BF16), wglr=wglr.astype(BF16),
        gup=gup.astype(BF16), gbias=row(gla_gate_b[0]),
        dskip=row(s5_d[0]), gluw=s5_glu_w[0].astype(BF16), glub=row(s5_glu_b[0]), s5out=s5_out[0].astype(BF16),
        gnorm=row(gla_norm[0]), glaout=gla_out[0].astype(BF16), wo=w_o[0].astype(BF16),
        n3=row(ffn2_norm[0]), wg2=ffn2_w_gate[0].astype(BF16), wu2=ffn2_w_up[0].astype(BF16),
        wd2=ffn2_w_down[0].astype(BF16), fin=row(final_norm))

    x1, proj, g = _front(x, mods3, lambda bi: bi, w, min(256, l), fchunk)
    _, projc, gcx = _front(ctx, mods3, lambda bi: b, w, min(256, ctx.shape[1]), fchunk)

    toep, ws, cp, lam16 = _s5_weights(s5_lambda_re[0], s5_lambda_im[0], s5_log_dt[0], s5_b_re[0], s5_b_im[0],
                                      s5_c_re[0], s5_c_im[0])
    xl = _to_chunk_major(proj[:, :, :s5w], ng)
    xc = _to_chunk_major(projc[:, :, :s5w], ng)
    ys = _from_chunk_major(_s5(xc, xl, toep, ws, cp, lam16, gps=4), b)

    of, ob = _gla(proj, g, projc, gcx)
    return _back(x1, ys, proj, of, ob, mods3, w, min(256, l), fchunk)
```

```python
import functools

import numpy as np
import jax
import jax.numpy as jnp
from jax import lax
from jax.experimental import pallas as pl
from jax.experimental.pallas import tpu as pltpu

F32 = jnp.float32
BF16 = jnp.bfloat16

RMS_EPS = 1e-6
MACARON_WEIGHT = 0.5
GRID_W = 64
N_MOD = 9
S5_GROUP = 16
S5_STATE = 64
S5_CHUNK = 16
S5_TILE = 8
GLA_HEADS = 4
GLA_DK = 64
GLA_DV = 128
GLA_CHUNK = 64
GLA_GATE_RANK = 16
GLA_GATE_NORM = 16.0
GLA_KEY = GLA_HEADS * GLA_DK
GLA_VAL = GLA_HEADS * GLA_DV
LANES = 128
NAT_W = 3072
CM_W = 2 * GLA_KEY + GLA_VAL + LANES
V7X_VMEM_LIMIT_BYTES = 56 * 1024 * 1024


def _params(*sem):
    return pltpu.CompilerParams(dimension_semantics=sem, vmem_limit_bytes=V7X_VMEM_LIMIT_BYTES)


def _const_spec(shape):
    nd = len(shape)
    return pl.BlockSpec(shape, lambda *_: (0,) * nd, pipeline_mode=pl.Buffered(1))


def _rms_mod(x, g, shift, scale):
    ms = jnp.mean(x * x, axis=-1, keepdims=True)
    return (x * lax.rsqrt(ms + RMS_EPS) * g) * (1.0 + scale) + shift


def _swiglu_residual(x, h, gate, wg_ref, wu_ref, wd_ref, fchunk):
    acc = None
    for f0 in range(0, wg_ref.shape[1], fchunk):
        gg = jnp.dot(h, wg_ref[:, f0:f0 + fchunk], preferred_element_type=F32)
        uu = jnp.dot(h, wu_ref[:, f0:f0 + fchunk], preferred_element_type=F32)
        a = (gg * jax.nn.sigmoid(gg) * uu).astype(BF16)
        o = jnp.dot(a, wd_ref[f0:f0 + fchunk, :], preferred_element_type=F32)
        acc = o if acc is None else acc + o
    return x + gate * (MACARON_WEIGHT * acc)


def _log_sigmoid(z):
    return jnp.minimum(z, 0.0) - jnp.log1p(jnp.exp(-jnp.abs(z)))


def _ada_kernel(c_ref, w_ref, b_ref, o_ref):
    cv = c_ref[...]
    s = cv * jax.nn.sigmoid(cv)
    o_ref[...] = jnp.dot(s, w_ref[...], preferred_element_type=F32,
                         precision=lax.Precision.HIGHEST) + b_ref[...]


def _ada(cvec, ada_w, ada_b):
    rows, d = cvec.shape
    n = ada_w.shape[1]
    bn = n // 8
    return pl.pallas_call(
        _ada_kernel,
        grid=(n // bn,),
        in_specs=[pl.BlockSpec((rows, d), lambda j: (0, 0)),
                  pl.BlockSpec((d, bn), lambda j: (0, j)),
                  pl.BlockSpec((1, bn), lambda j: (0, j))],
        out_specs=pl.BlockSpec((rows, bn), lambda j: (0, j)),
        out_shape=jax.ShapeDtypeStruct((rows, n), F32),
        compiler_params=_params("arbitrary"),
    )(cvec, ada_w, ada_b.reshape(1, n))


def _ffn_kernel(x_ref, mod_ref, n_ref, wg_ref, wu_ref, wd_ref, o_ref, *, fchunk, mod0):
    x = x_ref[...]
    m = mod_ref[...]
    h = _rms_mod(x, n_ref[...], m[mod0:mod0 + 1], m[mod0 + 1:mod0 + 2]).astype(BF16)
    o_ref[...] = _swiglu_residual(x, h, m[mod0 + 2:mod0 + 3], wg_ref, wu_ref, wd_ref, fchunk)


def _ffn(x, mods3, mod_row, norm, wg, wu, wd, tm, fchunk, mod0):
    b, l, d = x.shape
    f = wg.shape[1]
    tok = pl.BlockSpec((None, tm, d), lambda bi, i: (bi, i, 0))
    return pl.pallas_call(
        functools.partial(_ffn_kernel, fchunk=fchunk, mod0=mod0),
        grid=(b, l // tm),
        in_specs=[tok, pl.BlockSpec((None, N_MOD, d), lambda bi, i: (mod_row(bi), 0, 0)),
                  _const_spec((1, d)), _const_spec((d, f)), _const_spec((d, f)), _const_spec((f, d))],
        out_specs=tok,
        out_shape=jax.ShapeDtypeStruct((b, l, d), F32),
        compiler_params=_params("arbitrary", "arbitrary"),
    )(x, mods3, norm, wg, wu, wd)


def _piece_masks(rows):
    piece = lax.broadcasted_iota(jnp.int32, (rows, LANES), 1) // S5_GROUP
    return [piece == p for p in range(LANES // S5_GROUP)]


def _to_chunk_major(up, store):
    nj = up.shape[0] // S5_CHUNK
    npc = LANES // S5_GROUP
    masks = _piece_masks(nj)
    for gg in range(up.shape[1] // LANES):
        for tt in range(S5_CHUNK // npc):
            src = [up[(npc * tt + p) * nj:(npc * tt + p + 1) * nj, gg * LANES:(gg + 1) * LANES] for p in range(npc)]
            for gl in range(npc):
                acc = None
                for p in range(npc):
                    sh = ((p - gl) * S5_GROUP) % LANES
                    r = src[p] if sh == 0 else pltpu.roll(src[p], sh, 1)
                    acc = r if acc is None else jnp.where(masks[p], r, acc)
                store(gg * npc + gl, tt, acc)


def _from_chunk_major(load, ng, nj):
    npc = LANES // S5_GROUP
    masks = _piece_masks(nj)
    row_blocks = []
    for tt in range(S5_CHUNK // npc):
        per_p = [[] for _ in range(npc)]
        for gg in range(ng // npc):
            src = [load(gg * npc + gl, tt) for gl in range(npc)]
            for p in range(npc):
                acc = None
                for gl in range(npc):
                    sh = ((gl - p) * S5_GROUP) % LANES
                    r = src[gl] if sh == 0 else pltpu.roll(src[gl], sh, 1)
                    acc = r if acc is None else jnp.where(masks[gl], r, acc)
                per_p[p].append(acc)
        row_blocks += [jnp.concatenate(blk, axis=1) for blk in per_p]
    return jnp.concatenate(row_blocks, axis=0)


def _chunk_perm(tm):
    nj = tm // S5_CHUNK
    r = np.arange(tm)
    p = np.zeros((tm, tm), np.float32)
    p[r, S5_CHUNK * (r % nj) + r // nj] = 1.0
    return p


def _column_perm(tm):
    na = tm // GRID_W
    r = np.arange(tm)
    p = np.zeros((tm, tm), np.float32)
    p[r, GRID_W * (r % na) + r // na] = 1.0
    return p


def _proj_kernel(x_ref, mod_ref, n_ref, wnat_ref, wcm_ref, p16_ref, pcol_ref, nat_ref, xs5_ref, cm_ref, *,
                 col_major):
    x = x_ref[...]
    m = mod_ref[...]
    tm = x.shape[0]
    h2 = _rms_mod(x, n_ref[...], m[3:4], m[4:5]).astype(BF16)
    u_bf = None
    for c0 in range(0, NAT_W, 1024):
        p = jnp.dot(h2, wnat_ref[:, c0:c0 + 1024], preferred_element_type=F32).astype(BF16)
        nat_ref[:, c0:c0 + 1024] = p
        if c0 == 0:
            u_bf = p[:, 0:xs5_ref.shape[0] * S5_GROUP]
    pc = jnp.dot(h2, wcm_ref[...], preferred_element_type=F32).astype(BF16)
    if col_major:
        pcm = jnp.dot(pcol_ref[...], pc, preferred_element_type=F32)
        cm_ref[...] = pcm.reshape(GRID_W, tm // GRID_W, CM_W)
    else:
        cm_ref[...] = pc.astype(F32)
    up = jnp.dot(p16_ref[...], u_bf, preferred_element_type=F32)

    def store(g, tt, v):
        xs5_ref[g, :, tt * LANES:(tt + 1) * LANES] = v.astype(BF16)

    _to_chunk_major(up, store)


def _proj(x1, mods3, mod_row, w, tm, col_major):
    b, l, d = x1.shape
    ng = w["dskip"].shape[-1] // S5_GROUP
    nj = tm // S5_CHUNK
    nblk = l // tm
    tok = lambda width: pl.BlockSpec((None, tm, width), lambda bi, i: (bi, i, 0))
    if col_major:
        cm_spec = pl.BlockSpec((None, GRID_W, tm // GRID_W, CM_W), lambda bi, i: (bi, 0, i, 0))
        cm_shape = jax.ShapeDtypeStruct((b, GRID_W, l // GRID_W, CM_W), F32)
        pcol = jnp.asarray(_column_perm(tm), BF16)
    else:
        cm_spec = tok(CM_W)
        cm_shape = jax.ShapeDtypeStruct((b, l, CM_W), F32)
        pcol = jnp.zeros((8, LANES), BF16)
    return pl.pallas_call(
        functools.partial(_proj_kernel, col_major=col_major),
        grid=(b, nblk),
        in_specs=[tok(d), pl.BlockSpec((None, N_MOD, d), lambda bi, i: (mod_row(bi), 0, 0)),
                  _const_spec((1, d)), _const_spec((d, NAT_W)), _const_spec((d, CM_W)),
                  _const_spec((tm, tm)), _const_spec(pcol.shape)],
        out_specs=[tok(NAT_W),
                   pl.BlockSpec((ng, nj, S5_CHUNK * S5_GROUP), lambda bi, i: (0, bi * nblk + i, 0)),
                   cm_spec],
        out_shape=[jax.ShapeDtypeStruct((b, l, NAT_W), BF16),
                   jax.ShapeDtypeStruct((ng, b * (l // S5_CHUNK), S5_CHUNK * S5_GROUP), BF16),
                   cm_shape],
        compiler_params=_params("arbitrary", "arbitrary"),
    )(x1, mods3, w["n2"], w["wnat"], w["wcm"], jnp.asarray(_chunk_perm(tm), BF16), pcol)


def _s5_tile_scan(sr, si, hr, hi, cst, fwd):
    row = lax.broadcasted_iota(jnp.int32, sr.shape, 0)
    ar, ai = sr, si
    for lvl, dist in enumerate((1, 2, 4)):
        sh = dist if fwd else S5_TILE - dist
        rr, ri = pltpu.roll(ar, sh, 0), pltpu.roll(ai, sh, 0)
        lr, li = cst[2 * lvl], cst[2 * lvl + 1]
        ar, ai = ar + (lr * rr - li * ri), ai + (lr * ri + li * rr)
    keep = (row >= 1) if fwd else (row <= S5_TILE - 2)
    sh = 1 if fwd else S5_TILE - 1
    pr = jnp.where(keep, pltpu.roll(ar, sh, 0), 0.0)
    pi = jnp.where(keep, pltpu.roll(ai, sh, 0), 0.0)
    lpr, lpi, l8r, l8i = cst[6:10]
    hin_r = lpr * hr - lpi * hi + pr
    hin_i = lpr * hi + lpi * hr + pi
    e = S5_TILE - 1 if fwd else 0
    er = jnp.broadcast_to(ar[e:e + 1, :], ar.shape)
    ei = jnp.broadcast_to(ai[e:e + 1, :], ai.shape)
    return hin_r, hin_i, l8r * hr - l8i * hi + er, l8r * hi + l8i * hr + ei


def _s5_kernel(xc_ref, xl_ref, toep_ref, ws_ref, cp_ref, lam_ref, y_ref, s_ref, *, nb):
    npair = ws_ref.shape[0]
    jc = xc_ref.shape[1] // nb
    jl = xl_ref.shape[1] // nb
    jt = jc + jl
    for pp in range(npair):
        for src, j0, nj in ((xc_ref, 0, jc), (xl_ref, jc, jl)):
            s = jnp.dot(src[2 * pp], ws_ref[pp, 0:256, :], preferred_element_type=F32)
            s = s + jnp.dot(src[2 * pp + 1], ws_ref[pp, 256:512, :], preferred_element_type=F32)
            for bi in range(nb):
                s_ref[pp, bi * jt + j0:bi * jt + j0 + nj, :] = s[bi * nj:(bi + 1) * nj, :]

    nct = jc // S5_TILE
    nt = jt // S5_TILE
    zero = jnp.zeros((S5_TILE, LANES), F32)
    for pp in range(npair):

        def step(it, carry, pp=pp):
            mb = jnp.where(it < nct, nct - 1 - it, nt - 1 - (it - nct))
            cf = [lam_ref[pp, k] for k in range(10)]
            cb = [lam_ref[pp, 10 + k] for k in range(10)]
            out = []
            for bi in range(nb):
                hfr, hfi, hbr, hbi = carry[bi]
                rf = pl.multiple_of(bi * jt + it * S5_TILE, S5_TILE)
                rb = pl.multiple_of(bi * jt + mb * S5_TILE, S5_TILE)
                fr, fi, hfr, hfi = _s5_tile_scan(s_ref[pp, pl.ds(rf, S5_TILE), 0:128],
                                                 s_ref[pp, pl.ds(rf, S5_TILE), 128:256], hfr, hfi, cf, True)
                s_ref[pp, pl.ds(rf, S5_TILE), 0:128] = fr
                s_ref[pp, pl.ds(rf, S5_TILE), 128:256] = fi
                br, bim, hbr, hbi = _s5_tile_scan(s_ref[pp, pl.ds(rb, S5_TILE), 256:384],
                                                  s_ref[pp, pl.ds(rb, S5_TILE), 384:512], hbr, hbi, cb, False)
                s_ref[pp, pl.ds(rb, S5_TILE), 256:384] = br
                s_ref[pp, pl.ds(rb, S5_TILE), 384:512] = bim
                out.append((hfr, hfi, hbr, hbi))
            return tuple(out)

        lax.fori_loop(0, nt, step, tuple((zero, zero, zero, zero) for _ in range(nb)))

    for pp in range(npair):
        for bi in range(nb):
            hin = s_ref[pp, bi * jt + jc:(bi + 1) * jt, :].astype(BF16)
            yp = jnp.dot(hin, cp_ref[pp], preferred_element_type=F32)
            for gi in range(2):
                g = 2 * pp + gi
                y = yp[:, gi * 256:(gi + 1) * 256] + jnp.dot(xl_ref[g, bi * jl:(bi + 1) * jl, :], toep_ref[g],
                                                             preferred_element_type=F32)
                y_ref[g, bi * jl:(bi + 1) * jl, :] = y.astype(BF16)


def _s5(xc, xl, toep, ws, cp, lamc, gps, nb):
    ng, rc, _ = xc.shape
    rl = xl.shape[1]
    npair = gps // 2
    return pl.pallas_call(
        functools.partial(_s5_kernel, nb=nb),
        grid=(ng // gps,),
        in_specs=[pl.BlockSpec((gps, rc, 256), lambda i: (i, 0, 0)),
                  pl.BlockSpec((gps, rl, 256), lambda i: (i, 0, 0)),
                  pl.BlockSpec((gps, 256, 256), lambda i: (i, 0, 0)),
                  pl.BlockSpec((npair, 512, 512), lambda i: (i, 0, 0)),
                  pl.BlockSpec((npair, 512, 512), lambda i: (i, 0, 0)),
                  pl.BlockSpec((npair, 20, S5_TILE, LANES), lambda i: (i, 0, 0, 0))],
        out_specs=pl.BlockSpec((gps, rl, 256), lambda i: (i, 0, 0)),
        out_shape=jax.ShapeDtypeStruct((ng, rl, 256), BF16),
        scratch_shapes=[pltpu.VMEM((npair, rc + rl, 512), F32)],
        compiler_params=_params("arbitrary"),
    )(xc, xl, toep, ws, cp, lamc)


def _cmul(ar, ai, br, bi):
    return ar * br - ai * bi, ar * bi + ai * br


def _s5_weights(lam_re, lam_im, log_dt, b_re, b_im, c_re, c_im):
    nd, ng, p = lam_re.shape
    t = S5_CHUNK
    tc = t * S5_GROUP
    lr = jnp.minimum(lam_re.astype(F32), -1e-4)
    li = lam_im.astype(F32)
    dt = jnp.exp(log_dt.astype(F32))[..., None]
    mag = jnp.exp(lr * dt)
    lbr, lbi = mag * jnp.cos(li * dt), mag * jnp.sin(li * dt)
    den = lr * lr + li * li
    fr = ((lbr - 1.0) * lr + lbi * li) / den
    fi = (lbi * lr - (lbr - 1.0) * li) / den
    bbr, bbi = _cmul(fr[..., None], fi[..., None], b_re.astype(F32), b_im.astype(F32))
    cr, ci = c_re.astype(F32), c_im.astype(F32)

    pr, pi = [jnp.ones_like(lbr)], [jnp.zeros_like(lbr)]
    for _ in range(t):
        nr, ni = _cmul(pr[-1], pi[-1], lbr, lbi)
        pr.append(nr)
        pi.append(ni)
    pr, pi = jnp.stack(pr), jnp.stack(pi)
    wr, wi = _cmul(pr[:t, ..., None], pi[:t, ..., None], bbr[None], bbi[None])

    kern = jnp.einsum("dgxp,kdgpc->dgkxc", cr, wr) - jnp.einsum("dgxp,kdgpc->dgkxc", ci, wi)
    diff = jnp.arange(t)[None, :] - jnp.arange(t)[:, None]
    tf = jnp.where((diff >= 0)[None, :, :, None, None], kern[0][:, jnp.clip(diff, 0, t - 1)], 0.0)
    tb = jnp.where((diff <= 0)[None, :, :, None, None], kern[1][:, jnp.clip(-diff, 0, t - 1)], 0.0)
    toep = (tf + tb).transpose(0, 1, 4, 2, 3).reshape(ng, tc, tc)

    eye2 = jnp.eye(2, dtype=F32)

    def pair_rows(a):
        a = a.reshape(ng // 2, 2, tc, p)
        return a[:, :, :, None, :] * eye2[None, :, None, :, None]

    gscp = lambda a: a.transpose(1, 0, 3, 2).reshape(ng, tc, p)
    ws = jnp.stack([pair_rows(gscp(wr[::-1, 0])), pair_rows(gscp(wi[::-1, 0])),
                    pair_rows(gscp(wr[:, 1])), pair_rows(gscp(wi[:, 1]))], axis=3)
    ws = ws.reshape(ng // 2, 2 * tc, 4 * 2 * p)

    def pair_cols(a):
        a = a.reshape(ng // 2, 2, p, tc)
        return a[:, :, :, None, :] * eye2[None, :, None, :, None]

    def readout(d, powr, powi):
        mr, mi = _cmul(cr[d][None], ci[d][None], powr[:, :, None, :], powi[:, :, None, :])
        to_gptx = lambda a: a.transpose(1, 3, 0, 2).reshape(ng, p, tc)
        return pair_cols(to_gptx(mr)), pair_cols(to_gptx(-mi))

    cp = jnp.stack(readout(0, pr[1:t + 1, 0], pi[1:t + 1, 0])
                   + readout(1, pr[1:t + 1, 1][::-1], pi[1:t + 1, 1][::-1]), axis=1)
    cp = cp.reshape(ng // 2, 4 * 2 * p, 2 * tc)

    row = jnp.arange(S5_TILE)
    l1 = (pr[t], pi[t])
    l2 = _cmul(*l1, *l1)
    l4 = _cmul(*l2, *l2)
    l8 = _cmul(*l4, *l4)
    rp = [(jnp.ones_like(lbr), jnp.zeros_like(lbr))]
    for _ in range(S5_TILE - 1):
        rp.append(_cmul(*rp[-1], *l1))
    planes = []
    for d in range(nd):
        valid = (lambda dist: row >= dist) if d == 0 else (lambda dist: row <= S5_TILE - 1 - dist)
        for (qr, qi), dist in ((l1, 1), (l2, 2), (l4, 4)):
            m = valid(dist).astype(F32)[:, None, None]
            planes += [m * qr[d][None], m * qi[d][None]]
        order = row if d == 0 else row[::-1]
        planes += [jnp.stack([rp[k][0][d] for k in range(S5_TILE)])[order],
                   jnp.stack([rp[k][1][d] for k in range(S5_TILE)])[order]]
        planes += [jnp.broadcast_to(l8[0][d], (S5_TILE, ng, p)), jnp.broadcast_to(l8[1][d], (S5_TILE, ng, p))]
    lamc = jnp.stack(planes)
    lamc = lamc.reshape(20, S5_TILE, ng // 2, 2 * p).transpose(2, 0, 1, 3)
    return toep.astype(BF16), ws.astype(BF16), cp.astype(BF16), lamc


def _gla_chunk(q, k, v, glr, gup, gbias, st, rev, need_out):
    c = GLA_CHUNK
    z = jnp.dot(glr.astype(BF16), gup, preferred_element_type=F32) + gbias
    g = _log_sigmoid(z) * (1.0 / GLA_GATE_NORM)
    ri = lax.broadcasted_iota(jnp.int32, (c, c), 0)
    ci = lax.broadcasted_iota(jnp.int32, (c, c), 1)
    tri = (ri <= ci) if rev else (ri >= ci)
    gc = jnp.dot(tri.astype(F32), g, preferred_element_type=F32, precision=lax.Precision.HIGHEST)
    i_ref = c // 2 - 1 if rev else c // 2
    i_last = 0 if rev else c - 1
    g_ref = gc[i_ref:i_ref + 1, :]
    g_last = gc[i_last:i_last + 1, :]
    lane_head = lax.broadcasted_iota(jnp.int32, (1, GLA_KEY), 1) // GLA_DK

    kl = (k * jnp.exp(g_last - gc)).astype(BF16)
    vt = v.T.astype(BF16)
    kv = jnp.dot(vt, kl, preferred_element_type=F32)
    st_new = st * jnp.exp(g_last)
    for h in range(GLA_HEADS):
        st_new = st_new + jnp.where(lane_head == h, kv[h * GLA_DV:(h + 1) * GLA_DV, :], 0.0)
    if not need_out:
        return None, st_new

    qe = q * jnp.exp(gc - g_ref)
    ke = (k * jnp.exp(g_ref - gc)).astype(BF16)
    qg = q * jnp.exp(gc)
    stack = lambda a: jnp.concatenate(
        [jnp.where(lane_head == h, a, 0.0) for h in range(GLA_HEADS)], axis=0).astype(BF16)
    nt_dims = (((1,), (1,)), ((), ()))
    sc = lax.dot_general(stack(qe), ke, nt_dims, preferred_element_type=F32)
    rs = lax.broadcasted_iota(jnp.int32, (GLA_HEADS * c, c), 0) % c
    cs = lax.broadcasted_iota(jnp.int32, (GLA_HEADS * c, c), 1)
    keep = (rs <= cs) if rev else (rs >= cs)
    sc = jnp.where(keep, sc, 0.0).astype(BF16)
    oi = jnp.dot(sc, v.astype(BF16), preferred_element_type=F32)
    oo = lax.dot_general(stack(qg), st.astype(BF16), nt_dims, preferred_element_type=F32)
    o = jnp.concatenate(
        [oi[h * c:(h + 1) * c, h * GLA_DV:(h + 1) * GLA_DV] + oo[h * c:(h + 1) * c, :]
         for h in range(GLA_HEADS)], axis=1)
    return o, st_new


def _gla_kernel(cf_ref, cb_ref, cc_ref, gup_ref, gbias_ref, of_ref, ob_ref, stf_ref, stb_ref):
    c = GLA_CHUNK
    q0, k0, v0, r0 = 0, GLA_KEY, 2 * GLA_KEY, 2 * GLA_KEY + GLA_VAL

    def run(ref, rows, d, st, need_out):
        q = ref[rows, q0:k0] if need_out else None
        return _gla_chunk(q, ref[rows, k0:v0], ref[rows, v0:r0], ref[rows, r0:CM_W],
                          gup_ref[d], gbias_ref[d], st, d == 1, need_out)

    @pl.when(pl.program_id(1) == 0)
    def _():
        nctx = cc_ref.shape[0] // c
        stf = jnp.zeros(stf_ref.shape, F32)
        stb = jnp.zeros(stb_ref.shape, F32)
        for n in range(nctx):
            _, stf = run(cc_ref, slice(n * c, (n + 1) * c), 0, stf, False)
            _, stb = run(cc_ref, slice((nctx - 1 - n) * c, (nctx - n) * c), 1, stb, False)
        stf_ref[...] = stf
        stb_ref[...] = stb

    nch = cf_ref.shape[0] // c
    stf = stf_ref[...]
    stb = stb_ref[...]
    for n in range(nch):
        rows = slice(n * c, (n + 1) * c)
        o, stf = run(cf_ref, rows, 0, stf, True)
        of_ref[rows, :] = o
        rows = slice((nch - 1 - n) * c, (nch - n) * c)
        o, stb = run(cb_ref, rows, 1, stb, True)
        ob_ref[rows, :] = o
    stf_ref[...] = stf
    stb_ref[...] = stb


def _gla(cm, cmc, gup, gbias):
    b, ncol, rows, _ = cm.shape
    lc = cmc.shape[1]
    last = ncol - 1
    col = lambda width, fn: pl.BlockSpec((None, None, rows, width), fn)
    out_shape = jax.ShapeDtypeStruct((b, ncol, rows, GLA_VAL), F32)
    return pl.pallas_call(
        _gla_kernel,
        grid=(b, ncol),
        in_specs=[col(CM_W, lambda bi, ci: (bi, ci, 0, 0)),
                  col(CM_W, lambda bi, ci: (bi, last - ci, 0, 0)),
                  pl.BlockSpec((None, lc, CM_W), lambda bi, ci: (bi, 0, 0)),
                  _const_spec(gup.shape), _const_spec(gbias.shape)],
        out_specs=[col(GLA_VAL, lambda bi, ci: (bi, ci, 0, 0)),
                   col(GLA_VAL, lambda bi, ci: (bi, last - ci, 0, 0))],
        out_shape=[out_shape, out_shape],
        scratch_shapes=[pltpu.VMEM((GLA_DV, GLA_KEY), F32), pltpu.VMEM((GLA_DV, GLA_KEY), F32)],
        compiler_params=_params("arbitrary", "arbitrary"),
    )(cm, cm, cmc, gup, gbias)


def _back_kernel(x1_ref, nat_ref, y_ref, of_ref, ob_ref, mod_ref, p16t_ref, pcolt_ref, dskip_ref, gluw_ref,
                 glub_ref, s5out_ref, gnorm_ref, glaout_ref, wo_ref, n3_ref, wg_ref, wu_ref, wd_ref, fin_ref,
                 out_ref, *, fchunk):
    m = mod_ref[...]
    tm, d = x1_ref.shape
    ng, nj, _ = y_ref.shape
    s5w = ng * S5_GROUP

    yp = _from_chunk_major(lambda g, tt: y_ref[g, :, tt * LANES:(tt + 1) * LANES].astype(F32), ng, nj)
    ys = jnp.dot(p16t_ref[...], yp.astype(BF16), preferred_element_type=F32)
    ya = ys + dskip_ref[...] * nat_ref[:, 0:s5w].astype(F32)
    ya = 0.5 * ya * (1.0 + jnp.tanh(0.7978845608028654 * (ya + 0.044715 * (ya * ya * ya))))
    gl = jnp.dot(ya.astype(BF16), gluw_ref[...], preferred_element_type=F32) + glub_ref[...]
    ya = ya * jax.nn.sigmoid(gl)

    ocm = (of_ref[...] + ob_ref[...]).reshape(tm, GLA_VAL).astype(BF16)
    o = jnp.dot(pcolt_ref[...], ocm, preferred_element_type=F32)
    heads = []
    for h in range(GLA_HEADS):
        oh = o[:, h * GLA_DV:(h + 1) * GLA_DV]
        heads.append(oh * lax.rsqrt(jnp.mean(oh * oh, axis=-1, keepdims=True) + RMS_EPS))
    r = nat_ref[:, s5w:s5w + GLA_VAL].astype(F32)
    yb = jnp.concatenate(heads, axis=1) * gnorm_ref[...] * (r * jax.nn.sigmoid(r))

    pa = jnp.dot(ya.astype(BF16), s5out_ref[...], preferred_element_type=F32)
    pb = jnp.dot(yb.astype(BF16), glaout_ref[...], preferred_element_type=F32)
    ga = nat_ref[:, s5w + GLA_VAL:s5w + GLA_VAL + d].astype(F32)
    gb = nat_ref[:, s5w + GLA_VAL + d:s5w + GLA_VAL + 2 * d].astype(F32)
    mg = jax.nn.sigmoid(ga) * pa + jax.nn.sigmoid(gb) * pb
    y = jnp.dot(mg.astype(BF16), wo_ref[...], preferred_element_type=F32)
    x2 = x1_ref[...] + m[5:6] * y
    h = _rms_mod(x2, n3_ref[...], m[6:7], m[7:8]).astype(BF16)
    x3 = _swiglu_residual(x2, h, m[8:9], wg_ref, wu_ref, wd_ref, fchunk)
    ms = jnp.mean(x3 * x3, axis=-1, keepdims=True)
    out_ref[...] = x3 * lax.rsqrt(ms + RMS_EPS) * fin_ref[...]


def _back(x1, nat, ys5, of, ob, mods3, w, tm, fchunk):
    b, l, d = x1.shape
    f = w["wg2"].shape[1]
    s5w = w["dskip"].shape[-1]
    ng = s5w // S5_GROUP
    nblk = l // tm
    tok = lambda width: pl.BlockSpec((None, tm, width), lambda bi, i: (bi, i, 0))
    colblk = pl.BlockSpec((None, GRID_W, tm // GRID_W, GLA_VAL), lambda bi, i: (bi, 0, i, 0))
    return pl.pallas_call(
        functools.partial(_back_kernel, fchunk=fchunk),
        grid=(b, nblk),
        in_specs=[tok(d), tok(NAT_W),
                  pl.BlockSpec((ng, tm // S5_CHUNK, S5_CHUNK * S5_GROUP), lambda bi, i: (0, bi * nblk + i, 0)),
                  colblk, colblk,
                  pl.BlockSpec((None, N_MOD, d), lambda bi, i: (bi, 0, 0)),
                  _const_spec((tm, tm)), _const_spec((tm, tm)),
                  _const_spec((1, s5w)), _const_spec((s5w, s5w)), _const_spec((1, s5w)),
                  _const_spec((s5w, d)), _const_spec((1, GLA_VAL)), _const_spec((GLA_VAL, d)),
                  _const_spec((d, d)), _const_spec((1, d)), _const_spec((d, f)), _const_spec((d, f)),
                  _const_spec((f, d)), _const_spec((1, d))],
        out_specs=tok(d),
        out_shape=jax.ShapeDtypeStruct((b, l, d), F32),
        compiler_params=_params("arbitrary", "arbitrary"),
    )(x1, nat, ys5, of, ob, mods3, jnp.asarray(_chunk_perm(tm).T, BF16), jnp.asarray(_column_perm(tm).T, BF16),
      w["dskip"], w["gluw"], w["glub"], w["s5out"], w["gnorm"], w["glaout"], w["wo"], w["n3"], w["wg2"],
      w["wu2"], w["wd2"], w["fin"])


def kernel(x, c, ctx, c_ctx, ada_w, ada_b, ffn1_norm, ffn1_w_gate, ffn1_w_up, ffn1_w_down, mix_norm, w_in,
           s5_lambda_re, s5_lambda_im, s5_log_dt, s5_b_re, s5_b_im, s5_c_re, s5_c_im, s5_d, s5_glu_w, s5_glu_b,
           s5_out, gla_gate_up, gla_gate_b, gla_norm, gla_out, w_o, ffn2_norm, ffn2_w_gate, ffn2_w_up,
           ffn2_w_down, final_norm):
    b, l, d = x.shape
    lc = ctx.shape[1]
    assert ada_w.shape[0] == 1 and b + 1 <= 8
    s5w = s5_d.shape[-1]
    fchunk = ffn1_w_gate.shape[-1] // 2
    tm = min(512, l)
    assert tm % (GRID_W * 8) == 0 and l % tm == 0 and lc % (S5_CHUNK * S5_TILE) == 0 and lc % GLA_CHUNK == 0

    cvec = jnp.concatenate([c, c_ctx[None, :], jnp.zeros((8 - b - 1, d), F32)], axis=0)
    mods3 = _ada(cvec, ada_w[0], ada_b[0]).reshape(8, N_MOD, d)

    wi = w_in[0]
    o_q, o_k, o_v, o_r = s5w, s5w + GLA_KEY, s5w + 2 * GLA_KEY, s5w + 2 * GLA_KEY + GLA_VAL
    o_glr = o_r + GLA_VAL
    o_ga = o_glr + 2 * GLA_GATE_RANK
    wnat = jnp.concatenate([wi[:, :o_q], wi[:, o_r:o_glr], wi[:, o_ga:]], axis=1)
    wcm = jnp.concatenate([wi[:, o_q:o_k] * (GLA_DK ** -0.5), wi[:, o_k:o_r],
                           jnp.pad(wi[:, o_glr:o_ga], ((0, 0), (0, LANES - 2 * GLA_GATE_RANK)))], axis=1)
    gup = jnp.zeros((2, LANES, GLA_KEY), F32)
    gup = gup.at[0, 0:GLA_GATE_RANK].set(gla_gate_up[0, 0])
    gup = gup.at[1, GLA_GATE_RANK:2 * GLA_GATE_RANK].set(gla_gate_up[0, 1])
    row = lambda v: v.reshape(1, -1).astype(F32)
    w = dict(
        n1=row(ffn1_norm[0]), wg1=ffn1_w_gate[0].astype(BF16), wu1=ffn1_w_up[0].astype(BF16),
        wd1=ffn1_w_down[0].astype(BF16), n2=row(mix_norm[0]), wnat=wnat.astype(BF16), wcm=wcm.astype(BF16),
        dskip=row(s5_d[0]), gluw=s5_glu_w[0].astype(BF16), glub=row(s5_glu_b[0]), s5out=s5_out[0].astype(BF16),
        gnorm=row(gla_norm[0]), glaout=gla_out[0].astype(BF16), wo=w_o[0].astype(BF16),
        n3=row(ffn2_norm[0]), wg2=ffn2_w_gate[0].astype(BF16), wu2=ffn2_w_up[0].astype(BF16),
        wd2=ffn2_w_down[0].astype(BF16), fin=row(final_norm))

    lat_row = lambda bi: bi
    ctx_row = lambda bi: b
    x1 = _ffn(x, mods3, lat_row, w["n1"], w["wg1"], w["wu1"], w["wd1"], tm, fchunk, 0)
    c1 = _ffn(ctx, mods3, ctx_row, w["n1"], w["wg1"], w["wu1"], w["wd1"], lc, fchunk, 0)
    nat, xl, cm = _proj(x1, mods3, lat_row, w, tm, True)
    _, xc, cmc = _proj(c1, mods3, ctx_row, w, lc, False)

    toep, ws, cp, lamc = _s5_weights(s5_lambda_re[0], s5_lambda_im[0], s5_log_dt[0], s5_b_re[0], s5_b_im[0],
                                     s5_c_re[0], s5_c_im[0])
    ys5 = _s5(xc, xl, toep, ws, cp, lamc, gps=4, nb=b)

    of, ob = _gla(cm, cmc, gup.astype(BF16), gla_gate_b[0].reshape(2, 1, GLA_KEY).astype(F32))
    return _back(x1, nat, ys5, of, ob, mods3, w, tm, fchunk)
```

```python
import functools

import numpy as np
import jax
import jax.numpy as jnp
from jax import lax
from jax.experimental import pallas as pl
from jax.experimental.pallas import tpu as pltpu

F32 = jnp.float32
BF16 = jnp.bfloat16

RMS_EPS = 1e-6
MACARON_WEIGHT = 0.5
GRID_W = 64
N_MOD = 9
S5_GROUP = 16
S5_STATE = 64
S5_CHUNK = 16
S5_TILE = 8
GLA_HEADS = 4
GLA_DK = 64
GLA_DV = 128
GLA_CHUNK = 64
GLA_GATE_RANK = 16
GLA_GATE_NORM = 16.0
GLA_KEY = GLA_HEADS * GLA_DK
GLA_VAL = GLA_HEADS * GLA_DV
LANES = 128
NAT_W = 3072
CM_W = 2 * GLA_KEY + GLA_VAL + LANES
V7X_VMEM_LIMIT_BYTES = 56 * 1024 * 1024


def _params(*sem):
    return pltpu.CompilerParams(dimension_semantics=sem, vmem_limit_bytes=V7X_VMEM_LIMIT_BYTES)


def _const_spec(shape):
    nd = len(shape)
    return pl.BlockSpec(shape, lambda *_: (0,) * nd, pipeline_mode=pl.Buffered(1))


def _rms_mod(x, g, shift, scale):
    ms = jnp.mean(x * x, axis=-1, keepdims=True)
    return (x * lax.rsqrt(ms + RMS_EPS) * g) * (1.0 + scale) + shift


def _swiglu_residual(x, h, gate, wg_ref, wu_ref, wd_ref, fchunk):
    acc = None
    for f0 in range(0, wg_ref.shape[1], fchunk):
        gg = jnp.dot(h, wg_ref[:, f0:f0 + fchunk], preferred_element_type=F32)
        uu = jnp.dot(h, wu_ref[:, f0:f0 + fchunk], preferred_element_type=F32)
        a = (gg * jax.nn.sigmoid(gg) * uu).astype(BF16)
        o = jnp.dot(a, wd_ref[f0:f0 + fchunk, :], preferred_element_type=F32)
        acc = o if acc is None else acc + o
    return x + gate * (MACARON_WEIGHT * acc)


def _ada_kernel(c_ref, w_ref, b_ref, o_ref):
    cv = c_ref[...]
    s = cv * jax.nn.sigmoid(cv)
    o_ref[...] = jnp.dot(s, w_ref[...], preferred_element_type=F32,
                         precision=lax.Precision.HIGHEST) + b_ref[...]


def _ada(cvec, ada_w, ada_b):
    rows, d = cvec.shape
    n = ada_w.shape[1]
    bn = n // 8
    return pl.pallas_call(
        _ada_kernel,
        grid=(n // bn,),
        in_specs=[pl.BlockSpec((rows, d), lambda j: (0, 0)),
                  pl.BlockSpec((d, bn), lambda j: (0, j)),
                  pl.BlockSpec((1, bn), lambda j: (0, j))],
        out_specs=pl.BlockSpec((rows, bn), lambda j: (0, j)),
        out_shape=jax.ShapeDtypeStruct((rows, n), F32),
        compiler_params=_params("arbitrary"),
    )(cvec, ada_w, ada_b.reshape(1, n))


def _ffn_kernel(x_ref, mod_ref, n_ref, wg_ref, wu_ref, wd_ref, o_ref, *, fchunk, mod0):
    x = x_ref[...]
    m = mod_ref[...]
    h = _rms_mod(x, n_ref[...], m[mod0:mod0 + 1], m[mod0 + 1:mod0 + 2]).astype(BF16)
    o_ref[...] = _swiglu_residual(x, h, m[mod0 + 2:mod0 + 3], wg_ref, wu_ref, wd_ref, fchunk)


def _ffn(x, mods3, mod_row, norm, wg, wu, wd, tm, fchunk, mod0):
    b, l, d = x.shape
    f = wg.shape[1]
    tok = pl.BlockSpec((None, tm, d), lambda bi, i: (bi, i, 0))
    return pl.pallas_call(
        functools.partial(_ffn_kernel, fchunk=fchunk, mod0=mod0),
        grid=(b, l // tm),
        in_specs=[tok, pl.BlockSpec((None, N_MOD, d), lambda bi, i: (mod_row(bi), 0, 0)),
                  _const_spec((1, d)), _const_spec((d, f)), _const_spec((d, f)), _const_spec((f, d))],
        out_specs=tok,
        out_shape=jax.ShapeDtypeStruct((b, l, d), F32),
        compiler_params=_params("arbitrary", "arbitrary"),
    )(x, mods3, norm, wg, wu, wd)


def _piece_masks(rows):
    piece = lax.broadcasted_iota(jnp.int32, (rows, LANES), 1) // S5_GROUP
    return [piece == p for p in range(LANES // S5_GROUP)]


def _to_chunk_major(up, store):
    nj = up.shape[0] // S5_CHUNK
    npc = LANES // S5_GROUP
    masks = _piece_masks(nj)
    for gg in range(up.shape[1] // LANES):
        for tt in range(S5_CHUNK // npc):
            src = [up[(npc * tt + p) * nj:(npc * tt + p + 1) * nj, gg * LANES:(gg + 1) * LANES] for p in range(npc)]
            for gl in range(npc):
                acc = None
                for p in range(npc):
                    sh = ((p - gl) * S5_GROUP) % LANES
                    r = src[p] if sh == 0 else pltpu.roll(src[p], sh, 1)
                    acc = r if acc is None else jnp.where(masks[p], r, acc)
                store(gg * npc + gl, tt, acc)


def _from_chunk_major(load, ng, nj):
    npc = LANES // S5_GROUP
    masks = _piece_masks(nj)
    row_blocks = []
    for tt in range(S5_CHUNK // npc):
        per_p = [[] for _ in range(npc)]
        for gg in range(ng // npc):
            src = [load(gg * npc + gl, tt) for gl in range(npc)]
            for p in range(npc):
                acc = None
                for gl in range(npc):
                    sh = ((gl - p) * S5_GROUP) % LANES
                    r = src[gl] if sh == 0 else pltpu.roll(src[gl], sh, 1)
                    acc = r if acc is None else jnp.where(masks[gl], r, acc)
                per_p[p].append(acc)
        row_blocks += [jnp.concatenate(blk, axis=1) for blk in per_p]
    return jnp.concatenate(row_blocks, axis=0)


def _chunk_perm(tm):
    nj = tm // S5_CHUNK
    r = np.arange(tm)
    p = np.zeros((tm, tm), np.float32)
    p[r, S5_CHUNK * (r % nj) + r // nj] = 1.0
    return p


def _column_perm(tm):
    na = tm // GRID_W
    r = np.arange(tm)
    p = np.zeros((tm, tm), np.float32)
    p[r, GRID_W * (r % na) + r // na] = 1.0
    return p


def _proj_kernel(x_ref, mod_ref, n_ref, wnat_ref, wcm_ref, p16_ref, pcol_ref, nat_ref, xs5_ref, cm_ref, *,
                 col_major):
    x = x_ref[...]
    m = mod_ref[...]
    tm = x.shape[0]
    h2 = _rms_mod(x, n_ref[...], m[3:4], m[4:5]).astype(BF16)
    u_bf = None
    for c0 in range(0, NAT_W, 1024):
        p = jnp.dot(h2, wnat_ref[:, c0:c0 + 1024], preferred_element_type=F32).astype(BF16)
        nat_ref[:, c0:c0 + 1024] = p
        if c0 == 0:
            u_bf = p[:, 0:xs5_ref.shape[0] * S5_GROUP]
    pc = jnp.dot(h2, wcm_ref[...], preferred_element_type=F32).astype(BF16)
    if col_major:
        pcm = jnp.dot(pcol_ref[...], pc, preferred_element_type=F32)
        cm_ref[...] = pcm.reshape(GRID_W, tm // GRID_W, CM_W)
    else:
        cm_ref[...] = pc.astype(F32)
    up = jnp.dot(p16_ref[...], u_bf, preferred_element_type=F32)

    def store(g, tt, v):
        xs5_ref[g, :, tt * LANES:(tt + 1) * LANES] = v.astype(BF16)

    _to_chunk_major(up, store)


def _proj(x1, mods3, mod_row, w, tm, col_major):
    b, l, d = x1.shape
    ng = w["dskip"].shape[-1] // S5_GROUP
    nj = tm // S5_CHUNK
    nblk = l // tm
    tok = lambda width: pl.BlockSpec((None, tm, width), lambda bi, i: (bi, i, 0))
    if col_major:
        cm_spec = pl.BlockSpec((None, GRID_W, tm // GRID_W, CM_W), lambda bi, i: (bi, 0, i, 0))
        cm_shape = jax.ShapeDtypeStruct((b, GRID_W, l // GRID_W, CM_W), F32)
        pcol = jnp.asarray(_column_perm(tm), BF16)
    else:
        cm_spec = tok(CM_W)
        cm_shape = jax.ShapeDtypeStruct((b, l, CM_W), F32)
        pcol = jnp.zeros((8, LANES), BF16)
    return pl.pallas_call(
        functools.partial(_proj_kernel, col_major=col_major),
        grid=(b, nblk),
        in_specs=[tok(d), pl.BlockSpec((None, N_MOD, d), lambda bi, i: (mod_row(bi), 0, 0)),
                  _const_spec((1, d)), _const_spec((d, NAT_W)), _const_spec((d, CM_W)),
                  _const_spec((tm, tm)), _const_spec(pcol.shape)],
        out_specs=[tok(NAT_W),
                   pl.BlockSpec((ng, nj, S5_CHUNK * S5_GROUP), lambda bi, i: (0, bi * nblk + i, 0)),
                   cm_spec],
        out_shape=[jax.ShapeDtypeStruct((b, l, NAT_W), BF16),
                   jax.ShapeDtypeStruct((ng, b * (l // S5_CHUNK), S5_CHUNK * S5_GROUP), BF16),
                   cm_shape],
        compiler_params=_params("arbitrary", "arbitrary"),
    )(x1, mods3, w["n2"], w["wnat"], w["wcm"], jnp.asarray(_chunk_perm(tm), BF16), pcol)


def _s5_tile_scan(sr, si, hr, hi, cst, fwd):
    row = lax.broadcasted_iota(jnp.int32, sr.shape, 0)
    ar, ai = sr, si
    for lvl, dist in enumerate((1, 2, 4)):
        sh = dist if fwd else S5_TILE - dist
        rr, ri = pltpu.roll(ar, sh, 0), pltpu.roll(ai, sh, 0)
        lr, li = cst[2 * lvl], cst[2 * lvl + 1]
        ar, ai = ar + (lr * rr - li * ri), ai + (lr * ri + li * rr)
    keep = (row >= 1) if fwd else (row <= S5_TILE - 2)
    sh = 1 if fwd else S5_TILE - 1
    pr = jnp.where(keep, pltpu.roll(ar, sh, 0), 0.0)
    pi = jnp.where(keep, pltpu.roll(ai, sh, 0), 0.0)
    lpr, lpi, l8r, l8i = cst[6:10]
    hin_r = lpr * hr - lpi * hi + pr
    hin_i = lpr * hi + lpi * hr + pi
    e = S5_TILE - 1 if fwd else 0
    er = jnp.broadcast_to(ar[e:e + 1, :], ar.shape)
    ei = jnp.broadcast_to(ai[e:e + 1, :], ai.shape)
    return hin_r, hin_i, l8r * hr - l8i * hi + er, l8r * hi + l8i * hr + ei


def _s5_kernel(xc_ref, xl_ref, toep_ref, ws_ref, cp_ref, lam_ref, y_ref, s_ref, *, nb):
    npair = ws_ref.shape[0]
    jc = xc_ref.shape[1] // nb
    jl = xl_ref.shape[1] // nb
    jt = jc + jl
    for pp in range(npair):
        for src, j0, nj in ((xc_ref, 0, jc), (xl_ref, jc, jl)):
            s = jnp.dot(src[2 * pp], ws_ref[pp, 0:256, :], preferred_element_type=F32)
            s = s + jnp.dot(src[2 * pp + 1], ws_ref[pp, 256:512, :], preferred_element_type=F32)
            for bi in range(nb):
                s_ref[pp, bi * jt + j0:bi * jt + j0 + nj, :] = s[bi * nj:(bi + 1) * nj, :]

    nct = jc // S5_TILE
    nt = jt // S5_TILE
    zero = jnp.zeros((S5_TILE, LANES), F32)
    for pp in range(npair):

        def step(it, carry, pp=pp):
            mb = jnp.where(it < nct, nct - 1 - it, nt - 1 - (it - nct))
            cf = [lam_ref[pp, k] for k in range(10)]
            cb = [lam_ref[pp, 10 + k] for k in range(10)]
            out = []
            for bi in range(nb):
                hfr, hfi, hbr, hbi = carry[bi]
                rf = pl.multiple_of(bi * jt + it * S5_TILE, S5_TILE)
                rb = pl.multiple_of(bi * jt + mb * S5_TILE, S5_TILE)
                fr, fi, hfr, hfi = _s5_tile_scan(s_ref[pp, pl.ds(rf, S5_TILE), 0:128],
                                                 s_ref[pp, pl.ds(rf, S5_TILE), 128:256], hfr, hfi, cf, True)
                s_ref[pp, pl.ds(rf, S5_TILE), 0:128] = fr
                s_ref[pp, pl.ds(rf, S5_TILE), 128:256] = fi
                br, bim, hbr, hbi = _s5_tile_scan(s_ref[pp, pl.ds(rb, S5_TILE), 256:384],
                                                  s_ref[pp, pl.ds(rb, S5_TILE), 384:512], hbr, hbi, cb, False)
                s_ref[pp, pl.ds(rb, S5_TILE), 256:384] = br
                s_ref[pp, pl.ds(rb, S5_TILE), 384:512] = bim
                out.append((hfr, hfi, hbr, hbi))
            return tuple(out)

        lax.fori_loop(0, nt, step, tuple((zero, zero, zero, zero) for _ in range(nb)))

    for pp in range(npair):
        for bi in range(nb):
            hin = s_ref[pp, bi * jt + jc:(bi + 1) * jt, :].astype(BF16)
            yp = jnp.dot(hin, cp_ref[pp], preferred_element_type=F32)
            for gi in range(2):
                g = 2 * pp + gi
                y = yp[:, gi * 256:(gi + 1) * 256] + jnp.dot(xl_ref[g, bi * jl:(bi + 1) * jl, :], toep_ref[g],
                                                             preferred_element_type=F32)
                y_ref[g, bi * jl:(bi + 1) * jl, :] = y.astype(BF16)


def _s5(xc, xl, toep, ws, cp, lamc, gps, nb):
    ng, rc, _ = xc.shape
    rl = xl.shape[1]
    npair = gps // 2
    return pl.pallas_call(
        functools.partial(_s5_kernel, nb=nb),
        grid=(ng // gps,),
        in_specs=[pl.BlockSpec((gps, rc, 256), lambda i: (i, 0, 0)),
                  pl.BlockSpec((gps, rl, 256), lambda i: (i, 0, 0)),
                  pl.BlockSpec((gps, 256, 256), lambda i: (i, 0, 0)),
                  pl.BlockSpec((npair, 512, 512), lambda i: (i, 0, 0)),
                  pl.BlockSpec((npair, 512, 512), lambda i: (i, 0, 0)),
                  pl.BlockSpec((npair, 20, S5_TILE, LANES), lambda i: (i, 0, 0, 0))],
        out_specs=pl.BlockSpec((gps, rl, 256), lambda i: (i, 0, 0)),
        out_shape=jax.ShapeDtypeStruct((ng, rl, 256), BF16),
        scratch_shapes=[pltpu.VMEM((npair, rc + rl, 512), F32)],
        compiler_params=_params("arbitrary"),
    )(xc, xl, toep, ws, cp, lamc)


def _cmul(ar, ai, br, bi):
    return ar * br - ai * bi, ar * bi + ai * br


def _s5_weights(lam_re, lam_im, log_dt, b_re, b_im, c_re, c_im):
    nd, ng, p = lam_re.shape
    t = S5_CHUNK
    tc = t * S5_GROUP
    lr = jnp.minimum(lam_re.astype(F32), -1e-4)
    li = lam_im.astype(F32)
    dt = jnp.exp(log_dt.astype(F32))[..., None]
    mag = jnp.exp(lr * dt)
    lbr, lbi = mag * jnp.cos(li * dt), mag * jnp.sin(li * dt)
    den = lr * lr + li * li
    fr = ((lbr - 1.0) * lr + lbi * li) / den
    fi = (lbi * lr - (lbr - 1.0) * li) / den
    bbr, bbi = _cmul(fr[..., None], fi[..., None], b_re.astype(F32), b_im.astype(F32))
    cr, ci = c_re.astype(F32), c_im.astype(F32)

    pr, pi = [jnp.ones_like(lbr)], [jnp.zeros_like(lbr)]
    for _ in range(t):
        nr, ni = _cmul(pr[-1], pi[-1], lbr, lbi)
        pr.append(nr)
        pi.append(ni)
    pr, pi = jnp.stack(pr), jnp.stack(pi)
    wr, wi = _cmul(pr[:t, ..., None], pi[:t, ..., None], bbr[None], bbi[None])

    kern = jnp.einsum("dgxp,kdgpc->dgkxc", cr, wr) - jnp.einsum("dgxp,kdgpc->dgkxc", ci, wi)
    diff = jnp.arange(t)[None, :] - jnp.arange(t)[:, None]
    tf = jnp.where((diff >= 0)[None, :, :, None, None], kern[0][:, jnp.clip(diff, 0, t - 1)], 0.0)
    tb = jnp.where((diff <= 0)[None, :, :, None, None], kern[1][:, jnp.clip(-diff, 0, t - 1)], 0.0)
    toep = (tf + tb).transpose(0, 1, 4, 2, 3).reshape(ng, tc, tc)

    eye2 = jnp.eye(2, dtype=F32)

    def pair_rows(a):
        a = a.reshape(ng // 2, 2, tc, p)
        return a[:, :, :, None, :] * eye2[None, :, None, :, None]

    gscp = lambda a: a.transpose(1, 0, 3, 2).reshape(ng, tc, p)
    ws = jnp.stack([pair_rows(gscp(wr[::-1, 0])), pair_rows(gscp(wi[::-1, 0])),
                    pair_rows(gscp(wr[:, 1])), pair_rows(gscp(wi[:, 1]))], axis=3)
    ws = ws.reshape(ng // 2, 2 * tc, 4 * 2 * p)

    def pair_cols(a):
        a = a.reshape(ng // 2, 2, p, tc)
        return a[:, :, :, None, :] * eye2[None, :, None, :, None]

    def readout(d, powr, powi):
        mr, mi = _cmul(cr[d][None], ci[d][None], powr[:, :, None, :], powi[:, :, None, :])
        to_gptx = lambda a: a.transpose(1, 3, 0, 2).reshape(ng, p, tc)
        return pair_cols(to_gptx(mr)), pair_cols(to_gptx(-mi))

    cp = jnp.stack(readout(0, pr[1:t + 1, 0], pi[1:t + 1, 0])
                   + readout(1, pr[1:t + 1, 1][::-1], pi[1:t + 1, 1][::-1]), axis=1)
    cp = cp.reshape(ng // 2, 4 * 2 * p, 2 * tc)

    row = jnp.arange(S5_TILE)
    l1 = (pr[t], pi[t])
    l2 = _cmul(*l1, *l1)
    l4 = _cmul(*l2, *l2)
    l8 = _cmul(*l4, *l4)
    rp = [(jnp.ones_like(lbr), jnp.zeros_like(lbr))]
    for _ in range(S5_TILE - 1):
        rp.append(_cmul(*rp[-1], *l1))
    planes = []
    for d in range(nd):
        valid = (lambda dist: row >= dist) if d == 0 else (lambda dist: row <= S5_TILE - 1 - dist)
        for (qr, qi), dist in ((l1, 1), (l2, 2), (l4, 4)):
            m = valid(dist).astype(F32)[:, None, None]
            planes += [m * qr[d][None], m * qi[d][None]]
        order = row if d == 0 else row[::-1]
        planes += [jnp.stack([rp[k][0][d] for k in range(S5_TILE)])[order],
                   jnp.stack([rp[k][1][d] for k in range(S5_TILE)])[order]]
        planes += [jnp.broadcast_to(l8[0][d], (S5_TILE, ng, p)), jnp.broadcast_to(l8[1][d], (S5_TILE, ng, p))]
    lamc = jnp.stack(planes)
    lamc = lamc.reshape(20, S5_TILE, ng // 2, 2 * p).transpose(2, 0, 1, 3)
    return toep.astype(BF16), ws.astype(BF16), cp.astype(BF16), lamc


def _chunk_cumsum(g, rev):
    sub = 8
    nt = g.shape[0] // sub
    row = lax.broadcasted_iota(jnp.int32, (sub, g.shape[1]), 0)
    out = [None] * nt
    off = None
    for kk in (range(nt - 1, -1, -1) if rev else range(nt)):
        x = g[kk * sub:(kk + 1) * sub, :]
        for dist in (1, 2, 4):
            if rev:
                x = x + jnp.where(row < sub - dist, pltpu.roll(x, sub - dist, 0), 0.0)
            else:
                x = x + jnp.where(row >= dist, pltpu.roll(x, dist, 0), 0.0)
        if off is not None:
            x = x + off
        out[kk] = x
        e = 0 if rev else sub - 1
        off = jnp.broadcast_to(x[e:e + 1, :], x.shape)
    return jnp.concatenate(out, axis=0)


def _gla_log_decay(glr, gup, gbias, rev):
    c = GLA_CHUNK
    z = jnp.dot(glr.astype(BF16), gup, preferred_element_type=F32) + gbias
    g = (jnp.minimum(z, 0.0) - jnp.log(1.0 + jnp.exp(-jnp.abs(z)))) * (1.0 / GLA_GATE_NORM)
    return [_chunk_cumsum(g[n * c:(n + 1) * c, :], rev) for n in range(g.shape[0] // c)]


def _gla_chunk(q, k, v, gc, st, rev, need_out):
    c = GLA_CHUNK
    i_ref = c // 2 - 1 if rev else c // 2
    i_last = 0 if rev else c - 1
    g_ref = gc[i_ref:i_ref + 1, :]
    g_last = gc[i_last:i_last + 1, :]
    lane_head = lax.broadcasted_iota(jnp.int32, (1, GLA_KEY), 1) // GLA_DK

    kl = (k * jnp.exp(g_last - gc)).astype(BF16)
    vt = v.T.astype(BF16)
    kv = jnp.dot(vt, kl, preferred_element_type=F32)
    st_new = st * jnp.exp(g_last)
    for h in range(GLA_HEADS):
        st_new = st_new + jnp.where(lane_head == h, kv[h * GLA_DV:(h + 1) * GLA_DV, :], 0.0)
    if not need_out:
        return None, st_new

    qe = q * jnp.exp(gc - g_ref)
    ke = (k * jnp.exp(g_ref - gc)).astype(BF16)
    qg = q * jnp.exp(gc)
    stack = lambda a: jnp.concatenate(
        [jnp.where(lane_head == h, a, 0.0) for h in range(GLA_HEADS)], axis=0).astype(BF16)
    nt_dims = (((1,), (1,)), ((), ()))
    sc = lax.dot_general(stack(qe), ke, nt_dims, preferred_element_type=F32)
    rs = lax.broadcasted_iota(jnp.int32, (GLA_HEADS * c, c), 0) % c
    cs = lax.broadcasted_iota(jnp.int32, (GLA_HEADS * c, c), 1)
    keep = (rs <= cs) if rev else (rs >= cs)
    sc = jnp.where(keep, sc, 0.0).astype(BF16)
    oi = jnp.dot(sc, v.astype(BF16), preferred_element_type=F32)
    oo = lax.dot_general(stack(qg), st.astype(BF16), nt_dims, preferred_element_type=F32)
    o = jnp.concatenate(
        [oi[h * c:(h + 1) * c, h * GLA_DV:(h + 1) * GLA_DV] + oo[h * c:(h + 1) * c, :]
         for h in range(GLA_HEADS)], axis=1)
    return o, st_new


def _gla_kernel(cf_ref, cb_ref, cc_ref, gup_ref, gbias_ref, of_ref, ob_ref, stf_ref, stb_ref):
    c = GLA_CHUNK
    q0, k0, v0, r0 = 0, GLA_KEY, 2 * GLA_KEY, 2 * GLA_KEY + GLA_VAL

    def decay(ref, d):
        return _gla_log_decay(ref[:, r0:CM_W], gup_ref[d], gbias_ref[d], d == 1)

    def run(ref, n, gcs, d, st, need_out):
        rows = slice(n * c, (n + 1) * c)
        q = ref[rows, q0:k0] if need_out else None
        return _gla_chunk(q, ref[rows, k0:v0], ref[rows, v0:r0], gcs[n], st, d == 1, need_out)

    @pl.when(pl.program_id(1) == 0)
    def _():
        nctx = cc_ref.shape[0] // c
        gf, gb = decay(cc_ref, 0), decay(cc_ref, 1)
        stf = jnp.zeros(stf_ref.shape, F32)
        stb = jnp.zeros(stb_ref.shape, F32)
        for n in range(nctx):
            _, stf = run(cc_ref, n, gf, 0, stf, False)
            _, stb = run(cc_ref, nctx - 1 - n, gb, 1, stb, False)
        stf_ref[...] = stf
        stb_ref[...] = stb

    nch = cf_ref.shape[0] // c
    gf, gb = decay(cf_ref, 0), decay(cb_ref, 1)
    stf = stf_ref[...]
    stb = stb_ref[...]
    for n in range(nch):
        o, stf = run(cf_ref, n, gf, 0, stf, True)
        of_ref[n * c:(n + 1) * c, :] = o
        m = nch - 1 - n
        o, stb = run(cb_ref, m, gb, 1, stb, True)
        ob_ref[m * c:(m + 1) * c, :] = o
    stf_ref[...] = stf
    stb_ref[...] = stb


def _gla(cm, cmc, gup, gbias):
    b, ncol, rows, _ = cm.shape
    lc = cmc.shape[1]
    last = ncol - 1
    col = lambda width, fn: pl.BlockSpec((None, None, rows, width), fn)
    out_shape = jax.ShapeDtypeStruct((b, ncol, rows, GLA_VAL), F32)
    return pl.pallas_call(
        _gla_kernel,
        grid=(b, ncol),
        in_specs=[col(CM_W, lambda bi, ci: (bi, ci, 0, 0)),
                  col(CM_W, lambda bi, ci: (bi, last - ci, 0, 0)),
                  pl.BlockSpec((None, lc, CM_W), lambda bi, ci: (bi, 0, 0)),
                  _const_spec(gup.shape), _const_spec(gbias.shape)],
        out_specs=[col(GLA_VAL, lambda bi, ci: (bi, ci, 0, 0)),
                   col(GLA_VAL, lambda bi, ci: (bi, last - ci, 0, 0))],
        out_shape=[out_shape, out_shape],
        scratch_shapes=[pltpu.VMEM((GLA_DV, GLA_KEY), F32), pltpu.VMEM((GLA_DV, GLA_KEY), F32)],
        compiler_params=_params("arbitrary", "arbitrary"),
    )(cm, cm, cmc, gup, gbias)


def _back_kernel(x1_ref, nat_ref, y_ref, of_ref, ob_ref, mod_ref, p16t_ref, pcolt_ref, dskip_ref, gluw_ref,
                 glub_ref, s5out_ref, gnorm_ref, glaout_ref, wo_ref, n3_ref, wg_ref, wu_ref, wd_ref, fin_ref,
                 out_ref, *, fchunk):
    m = mod_ref[...]
    tm, d = x1_ref.shape
    ng, nj, _ = y_ref.shape
    s5w = ng * S5_GROUP

    yp = _from_chunk_major(lambda g, tt: y_ref[g, :, tt * LANES:(tt + 1) * LANES].astype(F32), ng, nj)
    ys = jnp.dot(p16t_ref[...], yp.astype(BF16), preferred_element_type=F32)
    ya = ys + dskip_ref[...] * nat_ref[:, 0:s5w].astype(F32)
    ya = 0.5 * ya * (1.0 + jnp.tanh(0.7978845608028654 * (ya + 0.044715 * (ya * ya * ya))))
    gl = jnp.dot(ya.astype(BF16), gluw_ref[...], preferred_element_type=F32) + glub_ref[...]
    ya = ya * jax.nn.sigmoid(gl)

    ocm = (of_ref[...] + ob_ref[...]).reshape(tm, GLA_VAL).astype(BF16)
    o = jnp.dot(pcolt_ref[...], ocm, preferred_element_type=F32)
    heads = []
    for h in range(GLA_HEADS):
        oh = o[:, h * GLA_DV:(h + 1) * GLA_DV]
        heads.append(oh * lax.rsqrt(jnp.mean(oh * oh, axis=-1, keepdims=True) + RMS_EPS))
    r = nat_ref[:, s5w:s5w + GLA_VAL].astype(F32)
    yb = jnp.concatenate(heads, axis=1) * gnorm_ref[...] * (r * jax.nn.sigmoid(r))

    pa = jnp.dot(ya.astype(BF16), s5out_ref[...], preferred_element_type=F32)
    pb = jnp.dot(yb.astype(BF16), glaout_ref[...], preferred_element_type=F32)
    ga = nat_ref[:, s5w + GLA_VAL:s5w + GLA_VAL + d].astype(F32)
    gb = nat_ref[:, s5w + GLA_VAL + d:s5w + GLA_VAL + 2 * d].astype(F32)
    mg = jax.nn.sigmoid(ga) * pa + jax.nn.sigmoid(gb) * pb
    y = jnp.dot(mg.astype(BF16), wo_ref[...], preferred_element_type=F32)
    x2 = x1_ref[...] + m[5:6] * y
    h = _rms_mod(x2, n3_ref[...], m[6:7], m[7:8]).astype(BF16)
    x3 = _swiglu_residual(x2, h, m[8:9], wg_ref, wu_ref, wd_ref, fchunk)
    ms = jnp.mean(x3 * x3, axis=-1, keepdims=True)
    out_ref[...] = x3 * lax.rsqrt(ms + RMS_EPS) * fin_ref[...]


def _back(x1, nat, ys5, of, ob, mods3, w, tm, fchunk):
    b, l, d = x1.shape
    f = w["wg2"].shape[1]
    s5w = w["dskip"].shape[-1]
    ng = s5w // S5_GROUP
    nblk = l // tm
    tok = lambda width: pl.BlockSpec((None, tm, width), lambda bi, i: (bi, i, 0))
    colblk = pl.BlockSpec((None, GRID_W, tm // GRID_W, GLA_VAL), lambda bi, i: (bi, 0, i, 0))
    return pl.pallas_call(
        functools.partial(_back_kernel, fchunk=fchunk),
        grid=(b, nblk),
        in_specs=[tok(d), tok(NAT_W),
                  pl.BlockSpec((ng, tm // S5_CHUNK, S5_CHUNK * S5_GROUP), lambda bi, i: (0, bi * nblk + i, 0)),
                  colblk, colblk,
                  pl.BlockSpec((None, N_MOD, d), lambda bi, i: (bi, 0, 0)),
                  _const_spec((tm, tm)), _const_spec((tm, tm)),
                  _const_spec((1, s5w)), _const_spec((s5w, s5w)), _const_spec((1, s5w)),
                  _const_spec((s5w, d)), _const_spec((1, GLA_VAL)), _const_spec((GLA_VAL, d)),
                  _const_spec((d, d)), _const_spec((1, d)), _const_spec((d, f)), _const_spec((d, f)),
                  _const_spec((f, d)), _const_spec((1, d))],
        out_specs=tok(d),
        out_shape=jax.ShapeDtypeStruct((b, l, d), F32),
        compiler_params=_params("arbitrary", "arbitrary"),
    )(x1, nat, ys5, of, ob, mods3, jnp.asarray(_chunk_perm(tm).T, BF16), jnp.asarray(_column_perm(tm).T, BF16),
      w["dskip"], w["gluw"], w["glub"], w["s5out"], w["gnorm"], w["glaout"], w["wo"], w["n3"], w["wg2"],
      w["wu2"], w["wd2"], w["fin"])


def kernel(x, c, ctx, c_ctx, ada_w, ada_b, ffn1_norm, ffn1_w_gate, ffn1_w_up, ffn1_w_down, mix_norm, w_in,
           s5_lambda_re, s5_lambda_im, s5_log_dt, s5_b_re, s5_b_im, s5_c_re, s5_c_im, s5_d, s5_glu_w, s5_glu_b,
           s5_out, gla_gate_up, gla_gate_b, gla_norm, gla_out, w_o, ffn2_norm, ffn2_w_gate, ffn2_w_up,
           ffn2_w_down, final_norm):
    b, l, d = x.shape
    lc = ctx.shape[1]
    assert ada_w.shape[0] == 1 and b + 1 <= 8
    s5w = s5_d.shape[-1]
    fchunk = ffn1_w_gate.shape[-1] // 2
    tm = min(512, l)
    assert tm % (GRID_W * 8) == 0 and l % tm == 0 and lc % (S5_CHUNK * S5_TILE) == 0 and lc % GLA_CHUNK == 0

    cvec = jnp.concatenate([c, c_ctx[None, :], jnp.zeros((8 - b - 1, d), F32)], axis=0)
    mods3 = _ada(cvec, ada_w[0], ada_b[0]).reshape(8, N_MOD, d)

    wi = w_in[0]
    o_q, o_k, o_v, o_r = s5w, s5w + GLA_KEY, s5w + 2 * GLA_KEY, s5w + 2 * GLA_KEY + GLA_VAL
    o_glr = o_r + GLA_VAL
    o_ga = o_glr + 2 * GLA_GATE_RANK
    wnat = jnp.concatenate([wi[:, :o_q], wi[:, o_r:o_glr], wi[:, o_ga:]], axis=1)
    wcm = jnp.concatenate([wi[:, o_q:o_k] * (GLA_DK ** -0.5), wi[:, o_k:o_r],
                           jnp.pad(wi[:, o_glr:o_ga], ((0, 0), (0, LANES - 2 * GLA_GATE_RANK)))], axis=1)
    gup = jnp.zeros((2, LANES, GLA_KEY), F32)
    gup = gup.at[0, 0:GLA_GATE_RANK].set(gla_gate_up[0, 0])
    gup = gup.at[1, GLA_GATE_RANK:2 * GLA_GATE_RANK].set(gla_gate_up[0, 1])
    row = lambda v: v.reshape(1, -1).astype(F32)
    w = dict(
        n1=row(ffn1_norm[0]), wg1=ffn1_w_gate[0].astype(BF16), wu1=ffn1_w_up[0].astype(BF16),
        wd1=ffn1_w_down[0].astype(BF16), n2=row(mix_norm[0]), wnat=wnat.astype(BF16), wcm=wcm.astype(BF16),
        dskip=row(s5_d[0]), gluw=s5_glu_w[0].astype(BF16), glub=row(s5_glu_b[0]), s5out=s5_out[0].astype(BF16),
        gnorm=row(gla_norm[0]), glaout=gla_out[0].astype(BF16), wo=w_o[0].astype(BF16),
        n3=row(ffn2_norm[0]), wg2=ffn2_w_gate[0].astype(BF16), wu2=ffn2_w_up[0].astype(BF16),
        wd2=ffn2_w_down[0].astype(BF16), fin=row(final_norm))

    lat_row = lambda bi: bi
    ctx_row = lambda bi: b
    x1 = _ffn(x, mods3, lat_row, w["n1"], w["wg1"], w["wu1"], w["wd1"], tm, fchunk, 0)
    c1 = _ffn(ctx, mods3, ctx_row, w["n1"], w["wg1"], w["wu1"], w["wd1"], lc, fchunk, 0)
    nat, xl, cm = _proj(x1, mods3, lat_row, w, tm, True)
    _, xc, cmc = _proj(c1, mods3, ctx_row, w, lc, False)

    toep, ws, cp, lamc = _s5_weights(s5_lambda_re[0], s5_lambda_im[0], s5_log_dt[0], s5_b_re[0], s5_b_im[0],
                                     s5_c_re[0], s5_c_im[0])
    ys5 = _s5(xc, xl, toep, ws, cp, lamc, gps=4, nb=b)

    of, ob = _gla(cm, cmc, gup.astype(BF16), gla_gate_b[0].reshape(2, 1, GLA_KEY).astype(F32))
    return _back(x1, nat, ys5, of, ob, mods3, w, tm, fchunk)
```

```python
import functools

import numpy as np
import jax
import jax.numpy as jnp
from jax import lax
from jax.experimental import pallas as pl
from jax.experimental.pallas import tpu as pltpu

F32 = jnp.float32
BF16 = jnp.bfloat16

RMS_EPS = 1e-6
MACARON_WEIGHT = 0.5
GRID_W = 64
N_MOD = 9
S5_GROUP = 16
S5_STATE = 64
S5_CHUNK = 16
S5_TILE = 8
GLA_HEADS = 4
GLA_DK = 64
GLA_DV = 128
GLA_CHUNK = 64
GLA_GATE_RANK = 16
GLA_GATE_NORM = 16.0
GLA_KEY = GLA_HEADS * GLA_DK
GLA_VAL = GLA_HEADS * GLA_DV
LANES = 128
NAT_W = 3072
CM_W = 2 * GLA_KEY + GLA_VAL + LANES
V7X_VMEM_LIMIT_BYTES = 56 * 1024 * 1024


def _params(*sem):
    return pltpu.CompilerParams(dimension_semantics=sem, vmem_limit_bytes=V7X_VMEM_LIMIT_BYTES)


def _const_spec(shape):
    nd = len(shape)
    return pl.BlockSpec(shape, lambda *_: (0,) * nd, pipeline_mode=pl.Buffered(1))


def _rms_mod(x, g, shift, scale):
    ms = jnp.mean(x * x, axis=-1, keepdims=True)
    return (x * lax.rsqrt(ms + RMS_EPS) * g) * (1.0 + scale) + shift


def _swiglu_residual(x, h, gate, wg_ref, wu_ref, wd_ref, fchunk):
    acc = None
    for f0 in range(0, wg_ref.shape[1], fchunk):
        gg = jnp.dot(h, wg_ref[:, f0:f0 + fchunk], preferred_element_type=F32)
        uu = jnp.dot(h, wu_ref[:, f0:f0 + fchunk], preferred_element_type=F32)
        a = (gg * jax.nn.sigmoid(gg) * uu).astype(BF16)
        o = jnp.dot(a, wd_ref[f0:f0 + fchunk, :], preferred_element_type=F32)
        acc = o if acc is None else acc + o
    return x + gate * (MACARON_WEIGHT * acc)


def _ada_kernel(c_ref, w_ref, b_ref, o_ref):
    cv = c_ref[...]
    s = cv * jax.nn.sigmoid(cv)
    o_ref[...] = jnp.dot(s, w_ref[...], preferred_element_type=F32,
                         precision=lax.Precision.HIGHEST) + b_ref[...]


def _ada(cvec, ada_w, ada_b):
    rows, d = cvec.shape
    n = ada_w.shape[1]
    bn = n // 8
    return pl.pallas_call(
        _ada_kernel,
        grid=(n // bn,),
        in_specs=[pl.BlockSpec((rows, d), lambda j: (0, 0)),
                  pl.BlockSpec((d, bn), lambda j: (0, j)),
                  pl.BlockSpec((1, bn), lambda j: (0, j))],
        out_specs=pl.BlockSpec((rows, bn), lambda j: (0, j)),
        out_shape=jax.ShapeDtypeStruct((rows, n), F32),
        compiler_params=_params("arbitrary"),
    )(cvec, ada_w, ada_b.reshape(1, n))


def _ffn_kernel(x_ref, mod_ref, n_ref, wg_ref, wu_ref, wd_ref, o_ref, *, fchunk, mod0):
    x = x_ref[...]
    m = mod_ref[...]
    h = _rms_mod(x, n_ref[...], m[mod0:mod0 + 1], m[mod0 + 1:mod0 + 2]).astype(BF16)
    o_ref[...] = _swiglu_residual(x, h, m[mod0 + 2:mod0 + 3], wg_ref, wu_ref, wd_ref, fchunk)


def _ffn(x, mods3, mod_row, norm, wg, wu, wd, tm, fchunk, mod0):
    b, l, d = x.shape
    f = wg.shape[1]
    tok = pl.BlockSpec((None, tm, d), lambda bi, i: (bi, i, 0))
    return pl.pallas_call(
        functools.partial(_ffn_kernel, fchunk=fchunk, mod0=mod0),
        grid=(b, l // tm),
        in_specs=[tok, pl.BlockSpec((None, N_MOD, d), lambda bi, i: (mod_row(bi), 0, 0)),
                  _const_spec((1, d)), _const_spec((d, f)), _const_spec((d, f)), _const_spec((f, d))],
        out_specs=tok,
        out_shape=jax.ShapeDtypeStruct((b, l, d), F32),
        compiler_params=_params("arbitrary", "arbitrary"),
    )(x, mods3, norm, wg, wu, wd)


def _piece_masks(rows):
    piece = lax.broadcasted_iota(jnp.int32, (rows, LANES), 1) // S5_GROUP
    return [piece == p for p in range(LANES // S5_GROUP)]


def _to_chunk_major(up, store):
    nj = up.shape[0] // S5_CHUNK
    npc = LANES // S5_GROUP
    masks = _piece_masks(nj)
    for gg in range(up.shape[1] // LANES):
        for tt in range(S5_CHUNK // npc):
            src = [up[(npc * tt + p) * nj:(npc * tt + p + 1) * nj, gg * LANES:(gg + 1) * LANES] for p in range(npc)]
            for gl in range(npc):
                acc = None
                for p in range(npc):
                    sh = ((p - gl) * S5_GROUP) % LANES
                    r = src[p] if sh == 0 else pltpu.roll(src[p], sh, 1)
                    acc = r if acc is None else jnp.where(masks[p], r, acc)
                store(gg * npc + gl, tt, acc)


def _from_chunk_major(load, ng, nj):
    npc = LANES // S5_GROUP
    masks = _piece_masks(nj)
    row_blocks = []
    for tt in range(S5_CHUNK // npc):
        per_p = [[] for _ in range(npc)]
        for gg in range(ng // npc):
            src = [load(gg * npc + gl, tt) for gl in range(npc)]
            for p in range(npc):
                acc = None
                for gl in range(npc):
                    sh = ((gl - p) * S5_GROUP) % LANES
                    r = src[gl] if sh == 0 else pltpu.roll(src[gl], sh, 1)
                    acc = r if acc is None else jnp.where(masks[gl], r, acc)
                per_p[p].append(acc)
        row_blocks += [jnp.concatenate(blk, axis=1) for blk in per_p]
    return jnp.concatenate(row_blocks, axis=0)


def _chunk_perm(tm):
    nj = tm // S5_CHUNK
    r = np.arange(tm)
    p = np.zeros((tm, tm), np.float32)
    p[r, S5_CHUNK * (r % nj) + r // nj] = 1.0
    return p


def _column_perm(tm):
    na = tm // GRID_W
    r = np.arange(tm)
    p = np.zeros((tm, tm), np.float32)
    p[r, GRID_W * (r % na) + r // na] = 1.0
    return p


def _proj_kernel(x_ref, mod_ref, n_ref, wnat_ref, wcm_ref, p16_ref, pcol_ref, nat_ref, xs5_ref, cm_ref, *,
                 col_major):
    x = x_ref[...]
    m = mod_ref[...]
    tm = x.shape[0]
    h2 = _rms_mod(x, n_ref[...], m[3:4], m[4:5]).astype(BF16)
    u_bf = None
    for c0 in range(0, NAT_W, 1024):
        p = jnp.dot(h2, wnat_ref[:, c0:c0 + 1024], preferred_element_type=F32).astype(BF16)
        nat_ref[:, c0:c0 + 1024] = p
        if c0 == 0:
            u_bf = p[:, 0:xs5_ref.shape[0] * S5_GROUP]
    pc = jnp.dot(h2, wcm_ref[...], preferred_element_type=F32).astype(BF16)
    if col_major:
        pcm = jnp.dot(pcol_ref[...], pc, preferred_element_type=F32)
        cm_ref[...] = pcm.reshape(GRID_W, tm // GRID_W, CM_W)
    else:
        cm_ref[...] = pc.astype(F32)
    up = jnp.dot(p16_ref[...], u_bf, preferred_element_type=F32)

    def store(g, tt, v):
        xs5_ref[g, :, tt * LANES:(tt + 1) * LANES] = v.astype(BF16)

    _to_chunk_major(up, store)


def _proj(x1, mods3, mod_row, w, tm, col_major):
    b, l, d = x1.shape
    ng = w["dskip"].shape[-1] // S5_GROUP
    nj = tm // S5_CHUNK
    nblk = l // tm
    tok = lambda width: pl.BlockSpec((None, tm, width), lambda bi, i: (bi, i, 0))
    if col_major:
        cm_spec = pl.BlockSpec((None, GRID_W, tm // GRID_W, CM_W), lambda bi, i: (bi, 0, i, 0))
        cm_shape = jax.ShapeDtypeStruct((b, GRID_W, l // GRID_W, CM_W), F32)
        pcol = jnp.asarray(_column_perm(tm), BF16)
    else:
        cm_spec = tok(CM_W)
        cm_shape = jax.ShapeDtypeStruct((b, l, CM_W), F32)
        pcol = jnp.zeros((8, LANES), BF16)
    return pl.pallas_call(
        functools.partial(_proj_kernel, col_major=col_major),
        grid=(b, nblk),
        in_specs=[tok(d), pl.BlockSpec((None, N_MOD, d), lambda bi, i: (mod_row(bi), 0, 0)),
                  _const_spec((1, d)), _const_spec((d, NAT_W)), _const_spec((d, CM_W)),
                  _const_spec((tm, tm)), _const_spec(pcol.shape)],
        out_specs=[tok(NAT_W),
                   pl.BlockSpec((ng, nj, S5_CHUNK * S5_GROUP), lambda bi, i: (0, bi * nblk + i, 0)),
                   cm_spec],
        out_shape=[jax.ShapeDtypeStruct((b, l, NAT_W), BF16),
                   jax.ShapeDtypeStruct((ng, b * (l // S5_CHUNK), S5_CHUNK * S5_GROUP), BF16),
                   cm_shape],
        compiler_params=_params("arbitrary", "arbitrary"),
    )(x1, mods3, w["n2"], w["wnat"], w["wcm"], jnp.asarray(_chunk_perm(tm), BF16), pcol)


def _s5_tile_prefix(sr, si, cst, fwd):
    ar, ai = sr, si
    for lvl, dist in enumerate((1, 2, 4)):
        sh = dist if fwd else S5_TILE - dist
        rr, ri = pltpu.roll(ar, sh, 0), pltpu.roll(ai, sh, 0)
        lr, li = cst[2 * lvl], cst[2 * lvl + 1]
        ar, ai = ar + (lr * rr - li * ri), ai + (lr * ri + li * rr)
    return ar, ai


def _s5_tile_carry(ar, ai, hr, hi, cst, fwd):
    row = lax.broadcasted_iota(jnp.int32, ar.shape, 0)
    keep = (row >= 1) if fwd else (row <= S5_TILE - 2)
    sh = 1 if fwd else S5_TILE - 1
    pr = jnp.where(keep, pltpu.roll(ar, sh, 0), 0.0)
    pi = jnp.where(keep, pltpu.roll(ai, sh, 0), 0.0)
    lpr, lpi, l8r, l8i = cst[6:10]
    hin_r = lpr * hr - lpi * hi + pr
    hin_i = lpr * hi + lpi * hr + pi
    e = S5_TILE - 1 if fwd else 0
    er = jnp.broadcast_to(ar[e:e + 1, :], ar.shape)
    ei = jnp.broadcast_to(ai[e:e + 1, :], ai.shape)
    return hin_r, hin_i, l8r * hr - l8i * hi + er, l8r * hi + l8i * hr + ei


def _s5_kernel(xc_ref, xl_ref, kt_ref, ws_ref, cp_ref, lam_ref, y_ref, s_ref, *, nb):
    npair = ws_ref.shape[0]
    jc = xc_ref.shape[1] // nb
    jl = xl_ref.shape[1] // nb
    jt = jc + jl
    for pp in range(npair):
        for src, j0, nj in ((xc_ref, 0, jc), (xl_ref, jc, jl)):
            s = jnp.dot(src[2 * pp], ws_ref[pp, 0:256, :], preferred_element_type=F32)
            s = s + jnp.dot(src[2 * pp + 1], ws_ref[pp, 256:512, :], preferred_element_type=F32)
            for bi in range(nb):
                s_ref[pp, bi * jt + j0:bi * jt + j0 + nj, :] = s[bi * nj:(bi + 1) * nj, :]

    nct = jc // S5_TILE
    nt = jt // S5_TILE
    zero = jnp.zeros((S5_TILE, LANES), F32)
    for pp in range(npair):

        def prefix(it, carry, pp=pp):
            r0 = pl.multiple_of(it * S5_TILE, S5_TILE)
            for c0, fwd in ((0, True), (256, False)):
                cst = [lam_ref[pp, (0 if fwd else 10) + k] for k in range(6)]
                ar, ai = _s5_tile_prefix(s_ref[pp, pl.ds(r0, S5_TILE), c0:c0 + 128],
                                         s_ref[pp, pl.ds(r0, S5_TILE), c0 + 128:c0 + 256], cst, fwd)
                s_ref[pp, pl.ds(r0, S5_TILE), c0:c0 + 128] = ar
                s_ref[pp, pl.ds(r0, S5_TILE), c0 + 128:c0 + 256] = ai
            return carry

        lax.fori_loop(0, nb * nt, prefix, 0, unroll=4)

        def step(it, carry, pp=pp):
            mb = jnp.where(it < nct, nct - 1 - it, nt - 1 - (it - nct))
            cf = [lam_ref[pp, k] for k in range(10)]
            cb = [lam_ref[pp, 10 + k] for k in range(10)]
            out = []
            for bi in range(nb):
                hfr, hfi, hbr, hbi = carry[bi]
                rf = pl.multiple_of(bi * jt + it * S5_TILE, S5_TILE)
                rb = pl.multiple_of(bi * jt + mb * S5_TILE, S5_TILE)
                fr, fi, hfr, hfi = _s5_tile_carry(s_ref[pp, pl.ds(rf, S5_TILE), 0:128],
                                                  s_ref[pp, pl.ds(rf, S5_TILE), 128:256], hfr, hfi, cf, True)
                s_ref[pp, pl.ds(rf, S5_TILE), 0:128] = fr
                s_ref[pp, pl.ds(rf, S5_TILE), 128:256] = fi
                br, bim, hbr, hbi = _s5_tile_carry(s_ref[pp, pl.ds(rb, S5_TILE), 256:384],
                                                   s_ref[pp, pl.ds(rb, S5_TILE), 384:512], hbr, hbi, cb, False)
                s_ref[pp, pl.ds(rb, S5_TILE), 256:384] = br
                s_ref[pp, pl.ds(rb, S5_TILE), 384:512] = bim
                out.append((hfr, hfi, hbr, hbi))
            return tuple(out)

        lax.fori_loop(0, nt, step, tuple((zero, zero, zero, zero) for _ in range(nb)))

    tc = S5_CHUNK * S5_GROUP
    width = kt_ref.shape[2]
    for pp in range(npair):
        toeps = []
        for gi in range(2):
            kt = kt_ref[2 * pp + gi]
            blocks = []
            for s in range(S5_CHUNK):
                sh = (width - S5_GROUP * (S5_CHUNK - 1 - s)) % width
                blocks.append((kt if sh == 0 else pltpu.roll(kt, sh, 1))[:, 0:tc])
            toeps.append(jnp.concatenate(blocks, axis=0).astype(BF16))
        for bi in range(nb):
            hin = s_ref[pp, bi * jt + jc:(bi + 1) * jt, :].astype(BF16)
            yp = jnp.dot(hin, cp_ref[pp], preferred_element_type=F32)
            for gi in range(2):
                g = 2 * pp + gi
                y = yp[:, gi * 256:(gi + 1) * 256] + jnp.dot(xl_ref[g, bi * jl:(bi + 1) * jl, :], toeps[gi],
                                                             preferred_element_type=F32)
                y_ref[g, bi * jl:(bi + 1) * jl, :] = y.astype(BF16)


def _s5(xc, xl, kt, ws, cp, lamc, gps, nb):
    ng, rc, _ = xc.shape
    rl = xl.shape[1]
    npair = gps // 2
    return pl.pallas_call(
        functools.partial(_s5_kernel, nb=nb),
        grid=(ng // gps,),
        in_specs=[pl.BlockSpec((gps, rc, 256), lambda i: (i, 0, 0)),
                  pl.BlockSpec((gps, rl, 256), lambda i: (i, 0, 0)),
                  pl.BlockSpec((gps,) + kt.shape[1:], lambda i: (i, 0, 0)),
                  pl.BlockSpec((npair, 512, 512), lambda i: (i, 0, 0)),
                  pl.BlockSpec((npair, 512, 512), lambda i: (i, 0, 0)),
                  pl.BlockSpec((npair, 20, S5_TILE, LANES), lambda i: (i, 0, 0, 0))],
        out_specs=pl.BlockSpec((gps, rl, 256), lambda i: (i, 0, 0)),
        out_shape=jax.ShapeDtypeStruct((ng, rl, 256), BF16),
        scratch_shapes=[pltpu.VMEM((npair, rc + rl, 512), F32)],
        compiler_params=_params("arbitrary"),
    )(xc, xl, kt, ws, cp, lamc)


def _cmul(ar, ai, br, bi):
    return ar * br - ai * bi, ar * bi + ai * br


def _s5_weights(lam_re, lam_im, log_dt, b_re, b_im, c_re, c_im):
    nd, ng, p = lam_re.shape
    t = S5_CHUNK
    tc = t * S5_GROUP
    lr = jnp.minimum(lam_re.astype(F32), -1e-4)
    li = lam_im.astype(F32)
    dt = jnp.exp(log_dt.astype(F32))[..., None]
    mag = jnp.exp(lr * dt)
    lbr, lbi = mag * jnp.cos(li * dt), mag * jnp.sin(li * dt)
    den = lr * lr + li * li
    fr = ((lbr - 1.0) * lr + lbi * li) / den
    fi = (lbi * lr - (lbr - 1.0) * li) / den
    bbr, bbi = _cmul(fr[..., None], fi[..., None], b_re.astype(F32), b_im.astype(F32))
    cr, ci = c_re.astype(F32), c_im.astype(F32)

    pr, pi = [jnp.ones_like(lbr)], [jnp.zeros_like(lbr)]
    for _ in range(t):
        nr, ni = _cmul(pr[-1], pi[-1], lbr, lbi)
        pr.append(nr)
        pi.append(ni)
    pr, pi = jnp.stack(pr), jnp.stack(pi)
    wr, wi = _cmul(pr[:t, ..., None], pi[:t, ..., None], bbr[None], bbi[None])

    kern = jnp.einsum("dgxp,kdgpc->dgkxc", cr, wr) - jnp.einsum("dgxp,kdgpc->dgkxc", ci, wi)
    ktf = kern[0].transpose(0, 3, 1, 2)
    ktb = kern[1][:, ::-1].transpose(0, 3, 1, 2)
    kt = jnp.concatenate([ktb[:, :, :t - 1], ktf[:, :, 0:1] + ktb[:, :, t - 1:t], ktf[:, :, 1:]], axis=2)
    kt = kt.reshape(ng, S5_GROUP, (2 * t - 1) * S5_GROUP)
    kt = jnp.pad(kt, ((0, 0), (0, 0), (0, 2 * tc - kt.shape[-1])))

    parity = (jnp.arange(ng) % 2)[:, None, None]

    gscp = lambda a: a.transpose(1, 0, 3, 2).reshape(ng, tc, p)
    parts = [gscp(wr[::-1, 0]), gscp(wi[::-1, 0]), gscp(wr[:, 1]), gscp(wi[:, 1])]
    ws = jnp.concatenate([a for a in parts for _ in range(2)], axis=-1)
    own = (jnp.arange(8 * p) // p) % 2 == parity
    ws = jnp.where(own, ws, 0.0).reshape(ng // 2, 2 * tc, 8 * p)

    def readout(d, powr, powi):
        mr, mi = _cmul(cr[d][None], ci[d][None], powr[:, :, None, :], powi[:, :, None, :])
        to_gptx = lambda a: a.transpose(1, 3, 0, 2).reshape(ng, p, tc)
        return [to_gptx(mr), to_gptx(-mi)]

    parts = readout(0, pr[1:t + 1, 0], pi[1:t + 1, 0]) + readout(1, pr[1:t + 1, 1][::-1], pi[1:t + 1, 1][::-1])
    own = jnp.arange(2 * tc) // tc == parity
    cp = jnp.stack([jnp.where(own, jnp.concatenate([a, a], axis=-1), 0.0) for a in parts], axis=1)
    cp = cp.reshape(ng // 2, 2, 4, p, 2 * tc).transpose(0, 2, 1, 3, 4).reshape(ng // 2, 8 * p, 2 * tc)

    row = jnp.arange(S5_TILE)
    l1 = (pr[t], pi[t])
    l2 = _cmul(*l1, *l1)
    l4 = _cmul(*l2, *l2)
    l8 = _cmul(*l4, *l4)
    rp = [(jnp.ones_like(lbr), jnp.zeros_like(lbr))]
    for _ in range(S5_TILE - 1):
        rp.append(_cmul(*rp[-1], *l1))
    planes = []
    for d in range(nd):
        valid = (lambda dist: row >= dist) if d == 0 else (lambda dist: row <= S5_TILE - 1 - dist)
        for (qr, qi), dist in ((l1, 1), (l2, 2), (l4, 4)):
            m = valid(dist).astype(F32)[:, None, None]
            planes += [m * qr[d][None], m * qi[d][None]]
        order = row if d == 0 else row[::-1]
        planes += [jnp.stack([rp[k][0][d] for k in range(S5_TILE)])[order],
                   jnp.stack([rp[k][1][d] for k in range(S5_TILE)])[order]]
        planes += [jnp.broadcast_to(l8[0][d], (S5_TILE, ng, p)), jnp.broadcast_to(l8[1][d], (S5_TILE, ng, p))]
    lamc = jnp.stack(planes)
    lamc = lamc.reshape(20, S5_TILE, ng // 2, 2 * p).transpose(2, 0, 1, 3)
    return kt, ws.astype(BF16), cp.astype(BF16), lamc


def _chunk_cumsum(g, rev):
    sub = 8
    nt = g.shape[0] // sub
    row = lax.broadcasted_iota(jnp.int32, (sub, g.shape[1]), 0)
    out = [None] * nt
    off = None
    for kk in (range(nt - 1, -1, -1) if rev else range(nt)):
        x = g[kk * sub:(kk + 1) * sub, :]
        for dist in (1, 2, 4):
            if rev:
                x = x + jnp.where(row < sub - dist, pltpu.roll(x, sub - dist, 0), 0.0)
            else:
                x = x + jnp.where(row >= dist, pltpu.roll(x, dist, 0), 0.0)
        if off is not None:
            x = x + off
        out[kk] = x
        e = 0 if rev else sub - 1
        off = jnp.broadcast_to(x[e:e + 1, :], x.shape)
    return jnp.concatenate(out, axis=0)


def _gla_log_decay(glr, gup, gbias, rev):
    c = GLA_CHUNK
    z = jnp.dot(glr.astype(BF16), gup, preferred_element_type=F32) + gbias
    g = (jnp.minimum(z, 0.0) - jnp.log(1.0 + jnp.exp(-jnp.abs(z)))) * (1.0 / GLA_GATE_NORM)
    return [_chunk_cumsum(g[n * c:(n + 1) * c, :], rev) for n in range(g.shape[0] // c)]


def _gla_chunk(q, k, v, gc, st, rev, need_out):
    c = GLA_CHUNK
    i_ref = c // 2 - 1 if rev else c // 2
    i_last = 0 if rev else c - 1
    g_ref = gc[i_ref:i_ref + 1, :]
    g_last = gc[i_last:i_last + 1, :]
    lane_head = lax.broadcasted_iota(jnp.int32, (1, GLA_KEY), 1) // GLA_DK

    kl = (k * jnp.exp(g_last - gc)).astype(BF16)
    vt = v.T.astype(BF16)
    kv = jnp.dot(vt, kl, preferred_element_type=F32)
    st_new = st * jnp.exp(g_last)
    for h in range(GLA_HEADS):
        st_new = st_new + jnp.where(lane_head == h, kv[h * GLA_DV:(h + 1) * GLA_DV, :], 0.0)
    if not need_out:
        return None, st_new

    qe = q * jnp.exp(gc - g_ref)
    ke = (k * jnp.exp(g_ref - gc)).astype(BF16)
    qg = q * jnp.exp(gc)
    stack = lambda a: jnp.concatenate(
        [jnp.where(lane_head == h, a, 0.0) for h in range(GLA_HEADS)], axis=0).astype(BF16)
    nt_dims = (((1,), (1,)), ((), ()))
    sc = lax.dot_general(stack(qe), ke, nt_dims, preferred_element_type=F32)
    rs = lax.broadcasted_iota(jnp.int32, (GLA_HEADS * c, c), 0) % c
    cs = lax.broadcasted_iota(jnp.int32, (GLA_HEADS * c, c), 1)
    keep = (rs <= cs) if rev else (rs >= cs)
    sc = jnp.where(keep, sc, 0.0).astype(BF16)
    oi = jnp.dot(sc, v.astype(BF16), preferred_element_type=F32)
    oo = lax.dot_general(stack(qg), st.astype(BF16), nt_dims, preferred_element_type=F32)
    o = jnp.concatenate(
        [oi[h * c:(h + 1) * c, h * GLA_DV:(h + 1) * GLA_DV] + oo[h * c:(h + 1) * c, :]
         for h in range(GLA_HEADS)], axis=1)
    return o, st_new


def _gla_kernel(cf_ref, cb_ref, cc_ref, gup_ref, gbias_ref, of_ref, ob_ref, stf_ref, stb_ref):
    c = GLA_CHUNK
    q0, k0, v0, r0 = 0, GLA_KEY, 2 * GLA_KEY, 2 * GLA_KEY + GLA_VAL

    def decay(ref, d):
        return _gla_log_decay(ref[:, r0:CM_W], gup_ref[d], gbias_ref[d], d == 1)

    def run(ref, n, gcs, d, st, need_out):
        rows = slice(n * c, (n + 1) * c)
        q = ref[rows, q0:k0] if need_out else None
        return _gla_chunk(q, ref[rows, k0:v0], ref[rows, v0:r0], gcs[n], st, d == 1, need_out)

    @pl.when(pl.program_id(1) == 0)
    def _():
        nctx = cc_ref.shape[0] // c
        gf, gb = decay(cc_ref, 0), decay(cc_ref, 1)
        stf = jnp.zeros(stf_ref.shape, F32)
        stb = jnp.zeros(stb_ref.shape, F32)
        for n in range(nctx):
            _, stf = run(cc_ref, n, gf, 0, stf, False)
            _, stb = run(cc_ref, nctx - 1 - n, gb, 1, stb, False)
        stf_ref[...] = stf
        stb_ref[...] = stb

    nch = cf_ref.shape[0] // c
    gf, gb = decay(cf_ref, 0), decay(cb_ref, 1)
    stf = stf_ref[...]
    stb = stb_ref[...]
    for n in range(nch):
        o, stf = run(cf_ref, n, gf, 0, stf, True)
        of_ref[n * c:(n + 1) * c, :] = o
        m = nch - 1 - n
        o, stb = run(cb_ref, m, gb, 1, stb, True)
        ob_ref[m * c:(m + 1) * c, :] = o
    stf_ref[...] = stf
    stb_ref[...] = stb


def _gla(cm, cmc, gup, gbias):
    b, ncol, rows, _ = cm.shape
    lc = cmc.shape[1]
    last = ncol - 1
    col = lambda width, fn: pl.BlockSpec((None, None, rows, width), fn)
    out_shape = jax.ShapeDtypeStruct((b, ncol, rows, GLA_VAL), F32)
    return pl.pallas_call(
        _gla_kernel,
        grid=(b, ncol),
        in_specs=[col(CM_W, lambda bi, ci: (bi, ci, 0, 0)),
                  col(CM_W, lambda bi, ci: (bi, last - ci, 0, 0)),
                  pl.BlockSpec((None, lc, CM_W), lambda bi, ci: (bi, 0, 0)),
                  _const_spec(gup.shape), _const_spec(gbias.shape)],
        out_specs=[col(GLA_VAL, lambda bi, ci: (bi, ci, 0, 0)),
                   col(GLA_VAL, lambda bi, ci: (bi, last - ci, 0, 0))],
        out_shape=[out_shape, out_shape],
        scratch_shapes=[pltpu.VMEM((GLA_DV, GLA_KEY), F32), pltpu.VMEM((GLA_DV, GLA_KEY), F32)],
        compiler_params=_params("arbitrary", "arbitrary"),
    )(cm, cm, cmc, gup, gbias)


def _back_kernel(x1_ref, nat_ref, y_ref, of_ref, ob_ref, mod_ref, p16t_ref, pcolt_ref, dskip_ref, gluw_ref,
                 glub_ref, s5out_ref, gnorm_ref, glaout_ref, wo_ref, n3_ref, wg_ref, wu_ref, wd_ref, fin_ref,
                 out_ref, *, fchunk):
    m = mod_ref[...]
    tm, d = x1_ref.shape
    ng, nj, _ = y_ref.shape
    s5w = ng * S5_GROUP

    yp = _from_chunk_major(lambda g, tt: y_ref[g, :, tt * LANES:(tt + 1) * LANES].astype(F32), ng, nj)
    ys = jnp.dot(p16t_ref[...], yp.astype(BF16), preferred_element_type=F32)
    ya = ys + dskip_ref[...] * nat_ref[:, 0:s5w].astype(F32)
    ya = 0.5 * ya * (1.0 + jnp.tanh(0.7978845608028654 * (ya + 0.044715 * (ya * ya * ya))))
    gl = jnp.dot(ya.astype(BF16), gluw_ref[...], preferred_element_type=F32) + glub_ref[...]
    ya = ya * jax.nn.sigmoid(gl)

    ocm = (of_ref[...] + ob_ref[...]).reshape(tm, GLA_VAL).astype(BF16)
    o = jnp.dot(pcolt_ref[...], ocm, preferred_element_type=F32)
    heads = []
    for h in range(GLA_HEADS):
        oh = o[:, h * GLA_DV:(h + 1) * GLA_DV]
        heads.append(oh * lax.rsqrt(jnp.mean(oh * oh, axis=-1, keepdims=True) + RMS_EPS))
    r = nat_ref[:, s5w:s5w + GLA_VAL].astype(F32)
    yb = jnp.concatenate(heads, axis=1) * gnorm_ref[...] * (r * jax.nn.sigmoid(r))

    pa = jnp.dot(ya.astype(BF16), s5out_ref[...], preferred_element_type=F32)
    pb = jnp.dot(yb.astype(BF16), glaout_ref[...], preferred_element_type=F32)
    ga = nat_ref[:, s5w + GLA_VAL:s5w + GLA_VAL + d].astype(F32)
    gb = nat_ref[:, s5w + GLA_VAL + d:s5w + GLA_VAL + 2 * d].astype(F32)
    mg = jax.nn.sigmoid(ga) * pa + jax.nn.sigmoid(gb) * pb
    y = jnp.dot(mg.astype(BF16), wo_ref[...], preferred_element_type=F32)
    x2 = x1_ref[...] + m[5:6] * y
    h = _rms_mod(x2, n3_ref[...], m[6:7], m[7:8]).astype(BF16)
    x3 = _swiglu_residual(x2, h, m[8:9], wg_ref, wu_ref, wd_ref, fchunk)
    ms = jnp.mean(x3 * x3, axis=-1, keepdims=True)
    out_ref[...] = x3 * lax.rsqrt(ms + RMS_EPS) * fin_ref[...]


def _back(x1, nat, ys5, of, ob, mods3, w, tm, fchunk):
    b, l, d = x1.shape
    f = w["wg2"].shape[1]
    s5w = w["dskip"].shape[-1]
    ng = s5w // S5_GROUP
    nblk = l // tm
    tok = lambda width: pl.BlockSpec((None, tm, width), lambda bi, i: (bi, i, 0))
    colblk = pl.BlockSpec((None, GRID_W, tm // GRID_W, GLA_VAL), lambda bi, i: (bi, 0, i, 0))
    return pl.pallas_call(
        functools.partial(_back_kernel, fchunk=fchunk),
        grid=(b, nblk),
        in_specs=[tok(d), tok(NAT_W),
                  pl.BlockSpec((ng, tm // S5_CHUNK, S5_CHUNK * S5_GROUP), lambda bi, i: (0, bi * nblk + i, 0)),
                  colblk, colblk,
                  pl.BlockSpec((None, N_MOD, d), lambda bi, i: (bi, 0, 0)),
                  _const_spec((tm, tm)), _const_spec((tm, tm)),
                  _const_spec((1, s5w)), _const_spec((s5w, s5w)), _const_spec((1, s5w)),
                  _const_spec((s5w, d)), _const_spec((1, GLA_VAL)), _const_spec((GLA_VAL, d)),
                  _const_spec((d, d)), _const_spec((1, d)), _const_spec((d, f)), _const_spec((d, f)),
                  _const_spec((f, d)), _const_spec((1, d))],
        out_specs=tok(d),
        out_shape=jax.ShapeDtypeStruct((b, l, d), F32),
        compiler_params=_params("arbitrary", "arbitrary"),
    )(x1, nat, ys5, of, ob, mods3, jnp.asarray(_chunk_perm(tm).T, BF16), jnp.asarray(_column_perm(tm).T, BF16),
      w["dskip"], w["gluw"], w["glub"], w["s5out"], w["gnorm"], w["glaout"], w["wo"], w["n3"], w["wg2"],
      w["wu2"], w["wd2"], w["fin"])


def kernel(x, c, ctx, c_ctx, ada_w, ada_b, ffn1_norm, ffn1_w_gate, ffn1_w_up, ffn1_w_down, mix_norm, w_in,
           s5_lambda_re, s5_lambda_im, s5_log_dt, s5_b_re, s5_b_im, s5_c_re, s5_c_im, s5_d, s5_glu_w, s5_glu_b,
           s5_out, gla_gate_up, gla_gate_b, gla_norm, gla_out, w_o, ffn2_norm, ffn2_w_gate, ffn2_w_up,
           ffn2_w_down, final_norm):
    b, l, d = x.shape
    lc = ctx.shape[1]
    assert ada_w.shape[0] == 1 and b + 1 <= 8
    s5w = s5_d.shape[-1]
    fchunk = ffn1_w_gate.shape[-1] // 2
    tm = min(512, l)
    assert tm % (GRID_W * 8) == 0 and l % tm == 0 and lc % (S5_CHUNK * S5_TILE) == 0 and lc % GLA_CHUNK == 0

    cvec = jnp.concatenate([c, c_ctx[None, :], jnp.zeros((8 - b - 1, d), F32)], axis=0)
    mods3 = _ada(cvec, ada_w[0], ada_b[0]).reshape(8, N_MOD, d)

    wi = w_in[0]
    o_q, o_k, o_v, o_r = s5w, s5w + GLA_KEY, s5w + 2 * GLA_KEY, s5w + 2 * GLA_KEY + GLA_VAL
    o_glr = o_r + GLA_VAL
    o_ga = o_glr + 2 * GLA_GATE_RANK
    wnat = jnp.concatenate([wi[:, :o_q], wi[:, o_r:o_glr], wi[:, o_ga:]], axis=1)
    wcm = jnp.concatenate([wi[:, o_q:o_k] * (GLA_DK ** -0.5), wi[:, o_k:o_r],
                           jnp.pad(wi[:, o_glr:o_ga], ((0, 0), (0, LANES - 2 * GLA_GATE_RANK)))], axis=1)
    gup = jnp.zeros((2, LANES, GLA_KEY), F32)
    gup = gup.at[0, 0:GLA_GATE_RANK].set(gla_gate_up[0, 0])
    gup = gup.at[1, GLA_GATE_RANK:2 * GLA_GATE_RANK].set(gla_gate_up[0, 1])
    row = lambda v: v.reshape(1, -1).astype(F32)
    w = dict(
        n1=row(ffn1_norm[0]), wg1=ffn1_w_gate[0].astype(BF16), wu1=ffn1_w_up[0].astype(BF16),
        wd1=ffn1_w_down[0].astype(BF16), n2=row(mix_norm[0]), wnat=wnat.astype(BF16), wcm=wcm.astype(BF16),
        dskip=row(s5_d[0]), gluw=s5_glu_w[0].astype(BF16), glub=row(s5_glu_b[0]), s5out=s5_out[0].astype(BF16),
        gnorm=row(gla_norm[0]), glaout=gla_out[0].astype(BF16), wo=w_o[0].astype(BF16),
        n3=row(ffn2_norm[0]), wg2=ffn2_w_gate[0].astype(BF16), wu2=ffn2_w_up[0].astype(BF16),
        wd2=ffn2_w_down[0].astype(BF16), fin=row(final_norm))

    lat_row = lambda bi: bi
    ctx_row = lambda bi: b
    x1 = _ffn(x, mods3, lat_row, w["n1"], w["wg1"], w["wu1"], w["wd1"], tm, fchunk, 0)
    c1 = _ffn(ctx, mods3, ctx_row, w["n1"], w["wg1"], w["wu1"], w["wd1"], lc, fchunk, 0)
    nat, xl, cm = _proj(x1, mods3, lat_row, w, tm, True)
    _, xc, cmc = _proj(c1, mods3, ctx_row, w, lc, False)

    kt, ws, cp, lamc = _s5_weights(s5_lambda_re[0], s5_lambda_im[0], s5_log_dt[0], s5_b_re[0], s5_b_im[0],
                                     s5_c_re[0], s5_c_im[0])
    ys5 = _s5(xc, xl, kt, ws, cp, lamc, gps=4, nb=b)

    of, ob = _gla(cm, cmc, gup.astype(BF16), gla_gate_b[0].reshape(2, 1, GLA_KEY).astype(F32))
    return _back(x1, nat, ys5, of, ob, mods3, w, tm, fchunk)
```

```python
import functools

import numpy as np
import jax
import jax.numpy as jnp
from jax import lax
from jax.experimental import pallas as pl
from jax.experimental.pallas import tpu as pltpu

F32 = jnp.float32
BF16 = jnp.bfloat16

RMS_EPS = 1e-6
MACARON_WEIGHT = 0.5
GRID_W = 64
N_MOD = 9
S5_GROUP = 16
S5_STATE = 64
S5_CHUNK = 16
S5_TILE = 8
GLA_HEADS = 4
GLA_DK = 64
GLA_DV = 128
GLA_CHUNK = 64
GLA_GATE_RANK = 16
GLA_GATE_NORM = 16.0
GLA_COLS_PER_STEP = 4
GLA_KEY = GLA_HEADS * GLA_DK
GLA_VAL = GLA_HEADS * GLA_DV
LANES = 128
NAT_W = 3072
CM_W = 2 * GLA_KEY + GLA_VAL + LANES
V7X_VMEM_LIMIT_BYTES = 56 * 1024 * 1024


def _params(*sem):
    return pltpu.CompilerParams(dimension_semantics=sem, vmem_limit_bytes=V7X_VMEM_LIMIT_BYTES)


def _const_spec(shape):
    nd = len(shape)
    return pl.BlockSpec(shape, lambda *_: (0,) * nd, pipeline_mode=pl.Buffered(1))


def _rms_mod(x, g, shift, scale):
    ms = jnp.mean(x * x, axis=-1, keepdims=True)
    return (x * lax.rsqrt(ms + RMS_EPS) * g) * (1.0 + scale) + shift


def _swiglu_residual(x, h, gate, wg_ref, wu_ref, wd_ref, fchunk):
    acc = None
    for f0 in range(0, wg_ref.shape[1], fchunk):
        gg = jnp.dot(h, wg_ref[:, f0:f0 + fchunk], preferred_element_type=F32)
        uu = jnp.dot(h, wu_ref[:, f0:f0 + fchunk], preferred_element_type=F32)
        a = (gg * jax.nn.sigmoid(gg) * uu).astype(BF16)
        o = jnp.dot(a, wd_ref[f0:f0 + fchunk, :], preferred_element_type=F32)
        acc = o if acc is None else acc + o
    return x + gate * (MACARON_WEIGHT * acc)


def _ada_kernel(c_ref, w_ref, b_ref, o_ref):
    cv = c_ref[...]
    s = cv * jax.nn.sigmoid(cv)
    o_ref[...] = jnp.dot(s, w_ref[...], preferred_element_type=F32,
                         precision=lax.Precision.HIGHEST) + b_ref[...]


def _ada(cvec, ada_w, ada_b):
    rows, d = cvec.shape
    n = ada_w.shape[1]
    bn = n // 8
    return pl.pallas_call(
        _ada_kernel,
        grid=(n // bn,),
        in_specs=[pl.BlockSpec((rows, d), lambda j: (0, 0)),
                  pl.BlockSpec((d, bn), lambda j: (0, j)),
                  pl.BlockSpec((1, bn), lambda j: (0, j))],
        out_specs=pl.BlockSpec((rows, bn), lambda j: (0, j)),
        out_shape=jax.ShapeDtypeStruct((rows, n), F32),
        compiler_params=_params("arbitrary"),
    )(cvec, ada_w, ada_b.reshape(1, n))


def _ffn_kernel(x_ref, mod_ref, n_ref, wg_ref, wu_ref, wd_ref, o_ref, *, fchunk, mod0):
    x = x_ref[...]
    m = mod_ref[...]
    h = _rms_mod(x, n_ref[...], m[mod0:mod0 + 1], m[mod0 + 1:mod0 + 2]).astype(BF16)
    o_ref[...] = _swiglu_residual(x, h, m[mod0 + 2:mod0 + 3], wg_ref, wu_ref, wd_ref, fchunk)


def _ffn(x, mods3, mod_row, norm, wg, wu, wd, tm, fchunk, mod0):
    b, l, d = x.shape
    f = wg.shape[1]
    tok = pl.BlockSpec((None, tm, d), lambda bi, i: (bi, i, 0))
    return pl.pallas_call(
        functools.partial(_ffn_kernel, fchunk=fchunk, mod0=mod0),
        grid=(b, l // tm),
        in_specs=[tok, pl.BlockSpec((None, N_MOD, d), lambda bi, i: (mod_row(bi), 0, 0)),
                  _const_spec((1, d)), _const_spec((d, f)), _const_spec((d, f)), _const_spec((f, d))],
        out_specs=tok,
        out_shape=jax.ShapeDtypeStruct((b, l, d), F32),
        compiler_params=_params("arbitrary", "arbitrary"),
    )(x, mods3, norm, wg, wu, wd)


def _piece_masks(rows):
    piece = lax.broadcasted_iota(jnp.int32, (rows, LANES), 1) // S5_GROUP
    return [piece == p for p in range(LANES // S5_GROUP)]


def _to_chunk_major(up, store):
    nj = up.shape[0] // S5_CHUNK
    npc = LANES // S5_GROUP
    masks = _piece_masks(nj)
    for gg in range(up.shape[1] // LANES):
        for tt in range(S5_CHUNK // npc):
            src = [up[(npc * tt + p) * nj:(npc * tt + p + 1) * nj, gg * LANES:(gg + 1) * LANES] for p in range(npc)]
            for gl in range(npc):
                acc = None
                for p in range(npc):
                    sh = ((p - gl) * S5_GROUP) % LANES
                    r = src[p] if sh == 0 else pltpu.roll(src[p], sh, 1)
                    acc = r if acc is None else jnp.where(masks[p], r, acc)
                store(gg * npc + gl, tt, acc)


def _from_chunk_major(load, ng, nj):
    npc = LANES // S5_GROUP
    masks = _piece_masks(nj)
    row_blocks = []
    for tt in range(S5_CHUNK // npc):
        per_p = [[] for _ in range(npc)]
        for gg in range(ng // npc):
            src = [load(gg * npc + gl, tt) for gl in range(npc)]
            for p in range(npc):
                acc = None
                for gl in range(npc):
                    sh = ((gl - p) * S5_GROUP) % LANES
                    r = src[gl] if sh == 0 else pltpu.roll(src[gl], sh, 1)
                    acc = r if acc is None else jnp.where(masks[gl], r, acc)
                per_p[p].append(acc)
        row_blocks += [jnp.concatenate(blk, axis=1) for blk in per_p]
    return jnp.concatenate(row_blocks, axis=0)


def _chunk_perm(tm):
    nj = tm // S5_CHUNK
    r = np.arange(tm)
    p = np.zeros((tm, tm), np.float32)
    p[r, S5_CHUNK * (r % nj) + r // nj] = 1.0
    return p


def _column_perm(tm):
    na = tm // GRID_W
    r = np.arange(tm)
    p = np.zeros((tm, tm), np.float32)
    p[r, GRID_W * (r % na) + r // na] = 1.0
    return p


def _proj_kernel(x_ref, mod_ref, n_ref, wnat_ref, wcm_ref, p16_ref, pcol_ref, nat_ref, xs5_ref, cm_ref, *,
                 col_major):
    x = x_ref[...]
    m = mod_ref[...]
    tm = x.shape[0]
    h2 = _rms_mod(x, n_ref[...], m[3:4], m[4:5]).astype(BF16)
    u_bf = None
    for c0 in range(0, NAT_W, 1024):
        p = jnp.dot(h2, wnat_ref[:, c0:c0 + 1024], preferred_element_type=F32).astype(BF16)
        nat_ref[:, c0:c0 + 1024] = p
        if c0 == 0:
            u_bf = p[:, 0:xs5_ref.shape[0] * S5_GROUP]
    pc = jnp.dot(h2, wcm_ref[...], preferred_element_type=F32).astype(BF16)
    if col_major:
        pcm = jnp.dot(pcol_ref[...], pc, preferred_element_type=F32)
        cm_ref[...] = pcm.reshape(GRID_W, tm // GRID_W, CM_W)
    else:
        cm_ref[...] = pc.astype(F32)
    up = jnp.dot(p16_ref[...], u_bf, preferred_element_type=F32)

    def store(g, tt, v):
        xs5_ref[g, :, tt * LANES:(tt + 1) * LANES] = v.astype(BF16)

    _to_chunk_major(up, store)


def _proj(x1, mods3, mod_row, w, tm, col_major):
    b, l, d = x1.shape
    ng = w["dskip"].shape[-1] // S5_GROUP
    nj = tm // S5_CHUNK
    nblk = l // tm
    tok = lambda width: pl.BlockSpec((None, tm, width), lambda bi, i: (bi, i, 0))
    if col_major:
        cm_spec = pl.BlockSpec((None, GRID_W, tm // GRID_W, CM_W), lambda bi, i: (bi, 0, i, 0))
        cm_shape = jax.ShapeDtypeStruct((b, GRID_W, l // GRID_W, CM_W), F32)
        pcol = jnp.asarray(_column_perm(tm), BF16)
    else:
        cm_spec = tok(CM_W)
        cm_shape = jax.ShapeDtypeStruct((b, l, CM_W), F32)
        pcol = jnp.zeros((8, LANES), BF16)
    return pl.pallas_call(
        functools.partial(_proj_kernel, col_major=col_major),
        grid=(b, nblk),
        in_specs=[tok(d), pl.BlockSpec((None, N_MOD, d), lambda bi, i: (mod_row(bi), 0, 0)),
                  _const_spec((1, d)), _const_spec((d, NAT_W)), _const_spec((d, CM_W)),
                  _const_spec((tm, tm)), _const_spec(pcol.shape)],
        out_specs=[tok(NAT_W),
                   pl.BlockSpec((ng, nj, S5_CHUNK * S5_GROUP), lambda bi, i: (0, bi * nblk + i, 0)),
                   cm_spec],
        out_shape=[jax.ShapeDtypeStruct((b, l, NAT_W), BF16),
                   jax.ShapeDtypeStruct((ng, b * (l // S5_CHUNK), S5_CHUNK * S5_GROUP), BF16),
                   cm_shape],
        compiler_params=_params("arbitrary", "arbitrary"),
    )(x1, mods3, w["n2"], w["wnat"], w["wcm"], jnp.asarray(_chunk_perm(tm), BF16), pcol)


def _s5_tile_prefix(sr, si, cst, fwd):
    ar, ai = sr, si
    for lvl, dist in enumerate((1, 2, 4)):
        sh = dist if fwd else S5_TILE - dist
        rr, ri = pltpu.roll(ar, sh, 0), pltpu.roll(ai, sh, 0)
        lr, li = cst[2 * lvl], cst[2 * lvl + 1]
        ar, ai = ar + (lr * rr - li * ri), ai + (lr * ri + li * rr)
    return ar, ai


def _s5_tile_carry(ar, ai, hr, hi, cst, fwd):
    row = lax.broadcasted_iota(jnp.int32, ar.shape, 0)
    keep = (row >= 1) if fwd else (row <= S5_TILE - 2)
    sh = 1 if fwd else S5_TILE - 1
    pr = jnp.where(keep, pltpu.roll(ar, sh, 0), 0.0)
    pi = jnp.where(keep, pltpu.roll(ai, sh, 0), 0.0)
    lpr, lpi, l8r, l8i = cst[6:10]
    hin_r = lpr * hr - lpi * hi + pr
    hin_i = lpr * hi + lpi * hr + pi
    e = S5_TILE - 1 if fwd else 0
    er = jnp.broadcast_to(ar[e:e + 1, :], ar.shape)
    ei = jnp.broadcast_to(ai[e:e + 1, :], ai.shape)
    return hin_r, hin_i, l8r * hr - l8i * hi + er, l8r * hi + l8i * hr + ei


def _s5_kernel(xc_ref, xl_ref, kt_ref, ws_ref, cp_ref, lam_ref, y_ref, s_ref, *, nb):
    npair = ws_ref.shape[0]
    jc = xc_ref.shape[1] // nb
    jl = xl_ref.shape[1] // nb
    jt = jc + jl
    for pp in range(npair):
        for src, j0, nj in ((xc_ref, 0, jc), (xl_ref, jc, jl)):
            s = jnp.dot(src[2 * pp], ws_ref[pp, 0:256, :], preferred_element_type=F32)
            s = s + jnp.dot(src[2 * pp + 1], ws_ref[pp, 256:512, :], preferred_element_type=F32)
            for bi in range(nb):
                s_ref[pp, bi * jt + j0:bi * jt + j0 + nj, :] = s[bi * nj:(bi + 1) * nj, :]

    nct = jc // S5_TILE
    nt = jt // S5_TILE
    zero = jnp.zeros((S5_TILE, LANES), F32)
    for pp in range(npair):

        def prefix(it, carry, pp=pp):
            r0 = pl.multiple_of(it * S5_TILE, S5_TILE)
            for c0, fwd in ((0, True), (256, False)):
                cst = [lam_ref[pp, (0 if fwd else 10) + k] for k in range(6)]
                ar, ai = _s5_tile_prefix(s_ref[pp, pl.ds(r0, S5_TILE), c0:c0 + 128],
                                         s_ref[pp, pl.ds(r0, S5_TILE), c0 + 128:c0 + 256], cst, fwd)
                s_ref[pp, pl.ds(r0, S5_TILE), c0:c0 + 128] = ar
                s_ref[pp, pl.ds(r0, S5_TILE), c0 + 128:c0 + 256] = ai
            return carry

        lax.fori_loop(0, nb * nt, prefix, 0, unroll=8)

        def step(it, carry, pp=pp):
            mb = jnp.where(it < nct, nct - 1 - it, nt - 1 - (it - nct))
            cf = [lam_ref[pp, k] for k in range(10)]
            cb = [lam_ref[pp, 10 + k] for k in range(10)]
            out = []
            for bi in range(nb):
                hfr, hfi, hbr, hbi = carry[bi]
                rf = pl.multiple_of(bi * jt + it * S5_TILE, S5_TILE)
                rb = pl.multiple_of(bi * jt + mb * S5_TILE, S5_TILE)
                fr, fi, hfr, hfi = _s5_tile_carry(s_ref[pp, pl.ds(rf, S5_TILE), 0:128],
                                                  s_ref[pp, pl.ds(rf, S5_TILE), 128:256], hfr, hfi, cf, True)
                s_ref[pp, pl.ds(rf, S5_TILE), 0:128] = fr
                s_ref[pp, pl.ds(rf, S5_TILE), 128:256] = fi
                br, bim, hbr, hbi = _s5_tile_carry(s_ref[pp, pl.ds(rb, S5_TILE), 256:384],
                                                   s_ref[pp, pl.ds(rb, S5_TILE), 384:512], hbr, hbi, cb, False)
                s_ref[pp, pl.ds(rb, S5_TILE), 256:384] = br
                s_ref[pp, pl.ds(rb, S5_TILE), 384:512] = bim
                out.append((hfr, hfi, hbr, hbi))
            return tuple(out)

        lax.fori_loop(0, nt, step, tuple((zero, zero, zero, zero) for _ in range(nb)))

    tc = S5_CHUNK * S5_GROUP
    width = kt_ref.shape[2]
    for pp in range(npair):
        toeps = []
        for gi in range(2):
            kt = kt_ref[2 * pp + gi]
            blocks = []
            for s in range(S5_CHUNK):
                sh = (width - S5_GROUP * (S5_CHUNK - 1 - s)) % width
                blocks.append((kt if sh == 0 else pltpu.roll(kt, sh, 1))[:, 0:tc])
            toeps.append(jnp.concatenate(blocks, axis=0).astype(BF16))
        for bi in range(nb):
            hin = s_ref[pp, bi * jt + jc:(bi + 1) * jt, :].astype(BF16)
            yp = jnp.dot(hin, cp_ref[pp], preferred_element_type=F32)
            for gi in range(2):
                g = 2 * pp + gi
                y = yp[:, gi * 256:(gi + 1) * 256] + jnp.dot(xl_ref[g, bi * jl:(bi + 1) * jl, :], toeps[gi],
                                                             preferred_element_type=F32)
                y_ref[g, bi * jl:(bi + 1) * jl, :] = y.astype(BF16)


def _s5(xc, xl, kt, ws, cp, lamc, gps, nb):
    ng, rc, _ = xc.shape
    rl = xl.shape[1]
    npair = gps // 2
    return pl.pallas_call(
        functools.partial(_s5_kernel, nb=nb),
        grid=(ng // gps,),
        in_specs=[pl.BlockSpec((gps, rc, 256), lambda i: (i, 0, 0)),
                  pl.BlockSpec((gps, rl, 256), lambda i: (i, 0, 0)),
                  pl.BlockSpec((gps,) + kt.shape[1:], lambda i: (i, 0, 0)),
                  pl.BlockSpec((npair, 512, 512), lambda i: (i, 0, 0)),
                  pl.BlockSpec((npair, 512, 512), lambda i: (i, 0, 0)),
                  pl.BlockSpec((npair, 20, S5_TILE, LANES), lambda i: (i, 0, 0, 0))],
        out_specs=pl.BlockSpec((gps, rl, 256), lambda i: (i, 0, 0)),
        out_shape=jax.ShapeDtypeStruct((ng, rl, 256), BF16),
        scratch_shapes=[pltpu.VMEM((npair, rc + rl, 512), F32)],
        compiler_params=_params("arbitrary"),
    )(xc, xl, kt, ws, cp, lamc)


def _cmul(ar, ai, br, bi):
    return ar * br - ai * bi, ar * bi + ai * br


def _s5_weights(lam_re, lam_im, log_dt, b_re, b_im, c_re, c_im):
    nd, ng, p = lam_re.shape
    t = S5_CHUNK
    tc = t * S5_GROUP
    lr = jnp.minimum(lam_re.astype(F32), -1e-4)
    li = lam_im.astype(F32)
    dt = jnp.exp(log_dt.astype(F32))[..., None]
    mag = jnp.exp(lr * dt)
    lbr, lbi = mag * jnp.cos(li * dt), mag * jnp.sin(li * dt)
    den = lr * lr + li * li
    fr = ((lbr - 1.0) * lr + lbi * li) / den
    fi = (lbi * lr - (lbr - 1.0) * li) / den
    bbr, bbi = _cmul(fr[..., None], fi[..., None], b_re.astype(F32), b_im.astype(F32))
    cr, ci = c_re.astype(F32), c_im.astype(F32)

    pr, pi = [jnp.ones_like(lbr)], [jnp.zeros_like(lbr)]
    for _ in range(t):
        nr, ni = _cmul(pr[-1], pi[-1], lbr, lbi)
        pr.append(nr)
        pi.append(ni)
    pr, pi = jnp.stack(pr), jnp.stack(pi)
    wr, wi = _cmul(pr[:t, ..., None], pi[:t, ..., None], bbr[None], bbi[None])

    kern = jnp.einsum("dgxp,kdgpc->dgkxc", cr, wr) - jnp.einsum("dgxp,kdgpc->dgkxc", ci, wi)
    ktf = kern[0].transpose(0, 3, 1, 2)
    ktb = kern[1][:, ::-1].transpose(0, 3, 1, 2)
    kt = jnp.concatenate([ktb[:, :, :t - 1], ktf[:, :, 0:1] + ktb[:, :, t - 1:t], ktf[:, :, 1:]], axis=2)
    kt = kt.reshape(ng, S5_GROUP, (2 * t - 1) * S5_GROUP)
    kt = jnp.pad(kt, ((0, 0), (0, 0), (0, 2 * tc - kt.shape[-1])))

    parity = (jnp.arange(ng) % 2)[:, None, None]

    gscp = lambda a: a.transpose(1, 0, 3, 2).reshape(ng, tc, p)
    parts = [gscp(wr[::-1, 0]), gscp(wi[::-1, 0]), gscp(wr[:, 1]), gscp(wi[:, 1])]
    ws = jnp.concatenate([a for a in parts for _ in range(2)], axis=-1)
    own = (jnp.arange(8 * p) // p) % 2 == parity
    ws = jnp.where(own, ws, 0.0).reshape(ng // 2, 2 * tc, 8 * p)

    def readout(d, powr, powi):
        mr, mi = _cmul(cr[d][None], ci[d][None], powr[:, :, None, :], powi[:, :, None, :])
        to_gptx = lambda a: a.transpose(1, 3, 0, 2).reshape(ng, p, tc)
        return [to_gptx(mr), to_gptx(-mi)]

    parts = readout(0, pr[1:t + 1, 0], pi[1:t + 1, 0]) + readout(1, pr[1:t + 1, 1][::-1], pi[1:t + 1, 1][::-1])
    own = jnp.arange(2 * tc) // tc == parity
    cp = jnp.stack([jnp.where(own, jnp.concatenate([a, a], axis=-1), 0.0) for a in parts], axis=1)
    cp = cp.reshape(ng // 2, 2, 4, p, 2 * tc).transpose(0, 2, 1, 3, 4).reshape(ng // 2, 8 * p, 2 * tc)

    row = jnp.arange(S5_TILE)
    l1 = (pr[t], pi[t])
    l2 = _cmul(*l1, *l1)
    l4 = _cmul(*l2, *l2)
    l8 = _cmul(*l4, *l4)
    rp = [(jnp.ones_like(lbr), jnp.zeros_like(lbr))]
    for _ in range(S5_TILE - 1):
        rp.append(_cmul(*rp[-1], *l1))
    planes = []
    for d in range(nd):
        valid = (lambda dist: row >= dist) if d == 0 else (lambda dist: row <= S5_TILE - 1 - dist)
        for (qr, qi), dist in ((l1, 1), (l2, 2), (l4, 4)):
            m = valid(dist).astype(F32)[:, None, None]
            planes += [m * qr[d][None], m * qi[d][None]]
        order = row if d == 0 else row[::-1]
        planes += [jnp.stack([rp[k][0][d] for k in range(S5_TILE)])[order],
                   jnp.stack([rp[k][1][d] for k in range(S5_TILE)])[order]]
        planes += [jnp.broadcast_to(l8[0][d], (S5_TILE, ng, p)), jnp.broadcast_to(l8[1][d], (S5_TILE, ng, p))]
    lamc = jnp.stack(planes)
    lamc = lamc.reshape(20, S5_TILE, ng // 2, 2 * p).transpose(2, 0, 1, 3)
    return kt, ws.astype(BF16), cp.astype(BF16), lamc


def _chunk_cumsum(g, rev):
    sub = 8
    nt = g.shape[0] // sub
    row = lax.broadcasted_iota(jnp.int32, (sub, g.shape[1]), 0)
    out = [None] * nt
    off = None
    for kk in (range(nt - 1, -1, -1) if rev else range(nt)):
        x = g[kk * sub:(kk + 1) * sub, :]
        for dist in (1, 2, 4):
            if rev:
                x = x + jnp.where(row < sub - dist, pltpu.roll(x, sub - dist, 0), 0.0)
            else:
                x = x + jnp.where(row >= dist, pltpu.roll(x, dist, 0), 0.0)
        if off is not None:
            x = x + off
        out[kk] = x
        e = 0 if rev else sub - 1
        off = jnp.broadcast_to(x[e:e + 1, :], x.shape)
    return jnp.concatenate(out, axis=0)


def _gla_log_decay(glr, gup, gbias, rev):
    c = GLA_CHUNK
    z = jnp.dot(glr.astype(BF16), gup, preferred_element_type=F32) + gbias
    g = (jnp.minimum(z, 0.0) - jnp.log(1.0 + jnp.exp(-jnp.abs(z)))) * (1.0 / GLA_GATE_NORM)
    return [_chunk_cumsum(g[n * c:(n + 1) * c, :], rev) for n in range(g.shape[0] // c)]


def _gla_chunk(q, k, v, gc, st, rev, need_out):
    c = GLA_CHUNK
    i_ref = c // 2 - 1 if rev else c // 2
    i_last = 0 if rev else c - 1
    g_ref = gc[i_ref:i_ref + 1, :]
    g_last = gc[i_last:i_last + 1, :]
    lane_head = lax.broadcasted_iota(jnp.int32, (1, GLA_KEY), 1) // GLA_DK

    kl = k * jnp.exp(g_last - gc)
    vt = v.T.astype(BF16)
    kv = None
    for h in range(GLA_HEADS):
        klh = jnp.where(lane_head == h, kl, 0.0).astype(BF16)
        part = jnp.dot(vt[h * GLA_DV:(h + 1) * GLA_DV, :], klh, preferred_element_type=F32)
        kv = part if kv is None else kv + part
    st_new = st * jnp.exp(g_last) + kv
    if not need_out:
        return None, st_new

    qe = q * jnp.exp(gc - g_ref)
    ke = (k * jnp.exp(g_ref - gc)).astype(BF16)
    qg = q * jnp.exp(gc)
    stack = lambda a: jnp.concatenate(
        [jnp.where(lane_head == h, a, 0.0) for h in range(GLA_HEADS)], axis=0).astype(BF16)
    nt_dims = (((1,), (1,)), ((), ()))
    sc = lax.dot_general(stack(qe), ke, nt_dims, preferred_element_type=F32)
    rs = lax.broadcasted_iota(jnp.int32, (GLA_HEADS * c, c), 0) % c
    cs = lax.broadcasted_iota(jnp.int32, (GLA_HEADS * c, c), 1)
    keep = (rs <= cs) if rev else (rs >= cs)
    sc = jnp.where(keep, sc, 0.0).astype(BF16)
    vb = v.astype(BF16)
    oo = lax.dot_general(stack(qg), st.astype(BF16), nt_dims, preferred_element_type=F32)
    o = jnp.concatenate(
        [jnp.dot(sc[h * c:(h + 1) * c, :], vb[:, h * GLA_DV:(h + 1) * GLA_DV], preferred_element_type=F32)
         + oo[h * c:(h + 1) * c, :] for h in range(GLA_HEADS)], axis=1)
    return o, st_new


def _gla_kernel(cf_ref, cb_ref, cc_ref, gup_ref, gbias_ref, of_ref, ob_ref, stf_ref, stb_ref):
    c = GLA_CHUNK
    q0, k0, v0, r0 = 0, GLA_KEY, 2 * GLA_KEY, 2 * GLA_KEY + GLA_VAL

    def decay(ref, d):
        return _gla_log_decay(ref[:, r0:CM_W], gup_ref[d], gbias_ref[d], d == 1)

    def run(ref, n, gcs, d, st, need_out):
        rows = slice(n * c, (n + 1) * c)
        q = ref[rows, q0:k0] if need_out else None
        return _gla_chunk(q, ref[rows, k0:v0], ref[rows, v0:r0], gcs[n], st, d == 1, need_out)

    @pl.when(pl.program_id(1) == 0)
    def _():
        nctx = cc_ref.shape[0] // c
        gf, gb = decay(cc_ref, 0), decay(cc_ref, 1)
        stf = jnp.zeros(stf_ref.shape, F32)
        stb = jnp.zeros(stb_ref.shape, F32)
        for n in range(nctx):
            _, stf = run(cc_ref, n, gf, 0, stf, False)
            _, stb = run(cc_ref, nctx - 1 - n, gb, 1, stb, False)
        stf_ref[...] = stf
        stb_ref[...] = stb

    ncols, rows_per_col, _ = cf_ref.shape
    nch = rows_per_col // c
    stf = stf_ref[...]
    stb = stb_ref[...]
    for j in range(ncols):
        jb = ncols - 1 - j
        gf, gb = decay(cf_ref.at[j], 0), decay(cb_ref.at[jb], 1)
        for n in range(nch):
            o, stf = run(cf_ref.at[j], n, gf, 0, stf, True)
            of_ref[j, n * c:(n + 1) * c, :] = o
            m = nch - 1 - n
            o, stb = run(cb_ref.at[jb], m, gb, 1, stb, True)
            ob_ref[jb, m * c:(m + 1) * c, :] = o
    stf_ref[...] = stf
    stb_ref[...] = stb


def _gla(cm, cmc, gup, gbias):
    b, ncol, rows, _ = cm.shape
    lc = cmc.shape[1]
    cps = GLA_COLS_PER_STEP
    last = ncol // cps - 1
    col = lambda width, fn: pl.BlockSpec((None, cps, rows, width), fn)
    out_shape = jax.ShapeDtypeStruct((b, ncol, rows, GLA_VAL), F32)
    return pl.pallas_call(
        _gla_kernel,
        grid=(b, ncol // cps),
        in_specs=[col(CM_W, lambda bi, ci: (bi, ci, 0, 0)),
                  col(CM_W, lambda bi, ci: (bi, last - ci, 0, 0)),
                  pl.BlockSpec((None, lc, CM_W), lambda bi, ci: (bi, 0, 0)),
                  _const_spec(gup.shape), _const_spec(gbias.shape)],
        out_specs=[col(GLA_VAL, lambda bi, ci: (bi, ci, 0, 0)),
                   col(GLA_VAL, lambda bi, ci: (bi, last - ci, 0, 0))],
        out_shape=[out_shape, out_shape],
        scratch_shapes=[pltpu.VMEM((GLA_DV, GLA_KEY), F32), pltpu.VMEM((GLA_DV, GLA_KEY), F32)],
        compiler_params=_params("arbitrary", "arbitrary"),
    )(cm, cm, cmc, gup, gbias)


def _back_kernel(x1_ref, nat_ref, y_ref, of_ref, ob_ref, mod_ref, p16t_ref, pcolt_ref, dskip_ref, gluw_ref,
                 glub_ref, s5out_ref, gnorm_ref, glaout_ref, wo_ref, n3_ref, wg_ref, wu_ref, wd_ref, fin_ref,
                 out_ref, *, fchunk):
    m = mod_ref[...]
    tm, d = x1_ref.shape
    ng, nj, _ = y_ref.shape
    s5w = ng * S5_GROUP

    yp = _from_chunk_major(lambda g, tt: y_ref[g, :, tt * LANES:(tt + 1) * LANES].astype(F32), ng, nj)
    ys = jnp.dot(p16t_ref[...], yp.astype(BF16), preferred_element_type=F32)
    ya = ys + dskip_ref[...] * nat_ref[:, 0:s5w].astype(F32)
    ya = 0.5 * ya * (1.0 + jnp.tanh(0.7978845608028654 * (ya + 0.044715 * (ya * ya * ya))))
    gl = jnp.dot(ya.astype(BF16), gluw_ref[...], preferred_element_type=F32) + glub_ref[...]
    ya = ya * jax.nn.sigmoid(gl)

    ocm = (of_ref[...] + ob_ref[...]).reshape(tm, GLA_VAL).astype(BF16)
    o = jnp.dot(pcolt_ref[...], ocm, preferred_element_type=F32)
    heads = []
    for h in range(GLA_HEADS):
        oh = o[:, h * GLA_DV:(h + 1) * GLA_DV]
        heads.append(oh * lax.rsqrt(jnp.mean(oh * oh, axis=-1, keepdims=True) + RMS_EPS))
    r = nat_ref[:, s5w:s5w + GLA_VAL].astype(F32)
    yb = jnp.concatenate(heads, axis=1) * gnorm_ref[...] * (r * jax.nn.sigmoid(r))

    pa = jnp.dot(ya.astype(BF16), s5out_ref[...], preferred_element_type=F32)
    pb = jnp.dot(yb.astype(BF16), glaout_ref[...], preferred_element_type=F32)
    ga = nat_ref[:, s5w + GLA_VAL:s5w + GLA_VAL + d].astype(F32)
    gb = nat_ref[:, s5w + GLA_VAL + d:s5w + GLA_VAL + 2 * d].astype(F32)
    mg = jax.nn.sigmoid(ga) * pa + jax.nn.sigmoid(gb) * pb
    y = jnp.dot(mg.astype(BF16), wo_ref[...], preferred_element_type=F32)
    x2 = x1_ref[...] + m[5:6] * y
    h = _rms_mod(x2, n3_ref[...], m[6:7], m[7:8]).astype(BF16)
    x3 = _swiglu_residual(x2, h, m[8:9], wg_ref, wu_ref, wd_ref, fchunk)
    ms = jnp.mean(x3 * x3, axis=-1, keepdims=True)
    out_ref[...] = x3 * lax.rsqrt(ms + RMS_EPS) * fin_ref[...]


def _back(x1, nat, ys5, of, ob, mods3, w, tm, fchunk):
    b, l, d = x1.shape
    f = w["wg2"].shape[1]
    s5w = w["dskip"].shape[-1]
    ng = s5w // S5_GROUP
    nblk = l // tm
    tok = lambda width: pl.BlockSpec((None, tm, width), lambda bi, i: (bi, i, 0))
    colblk = pl.BlockSpec((None, GRID_W, tm // GRID_W, GLA_VAL), lambda bi, i: (bi, 0, i, 0))
    return pl.pallas_call(
        functools.partial(_back_kernel, fchunk=fchunk),
        grid=(b, nblk),
        in_specs=[tok(d), tok(NAT_W),
                  pl.BlockSpec((ng, tm // S5_CHUNK, S5_CHUNK * S5_GROUP), lambda bi, i: (0, bi * nblk + i, 0)),
                  colblk, colblk,
                  pl.BlockSpec((None, N_MOD, d), lambda bi, i: (bi, 0, 0)),
                  _const_spec((tm, tm)), _const_spec((tm, tm)),
                  _const_spec((1, s5w)), _const_spec((s5w, s5w)), _const_spec((1, s5w)),
                  _const_spec((s5w, d)), _const_spec((1, GLA_VAL)), _const_spec((GLA_VAL, d)),
                  _const_spec((d, d)), _const_spec((1, d)), _const_spec((d, f)), _const_spec((d, f)),
                  _const_spec((f, d)), _const_spec((1, d))],
        out_specs=tok(d),
        out_shape=jax.ShapeDtypeStruct((b, l, d), F32),
        compiler_params=_params("arbitrary", "arbitrary"),
    )(x1, nat, ys5, of, ob, mods3, jnp.asarray(_chunk_perm(tm).T, BF16), jnp.asarray(_column_perm(tm).T, BF16),
      w["dskip"], w["gluw"], w["glub"], w["s5out"], w["gnorm"], w["glaout"], w["wo"], w["n3"], w["wg2"],
      w["wu2"], w["wd2"], w["fin"])


def kernel(x, c, ctx, c_ctx, ada_w, ada_b, ffn1_norm, ffn1_w_gate, ffn1_w_up, ffn1_w_down, mix_norm, w_in,
           s5_lambda_re, s5_lambda_im, s5_log_dt, s5_b_re, s5_b_im, s5_c_re, s5_c_im, s5_d, s5_glu_w, s5_glu_b,
           s5_out, gla_gate_up, gla_gate_b, gla_norm, gla_out, w_o, ffn2_norm, ffn2_w_gate, ffn2_w_up,
           ffn2_w_down, final_norm):
    b, l, d = x.shape
    lc = ctx.shape[1]
    assert ada_w.shape[0] == 1 and b + 1 <= 8
    s5w = s5_d.shape[-1]
    fchunk = ffn1_w_gate.shape[-1] // 2
    tm = min(512, l)
    assert tm % (GRID_W * 8) == 0 and l % tm == 0 and lc % (S5_CHUNK * S5_TILE) == 0 and lc % GLA_CHUNK == 0

    cvec = jnp.concatenate([c, c_ctx[None, :], jnp.zeros((8 - b - 1, d), F32)], axis=0)
    mods3 = _ada(cvec, ada_w[0], ada_b[0]).reshape(8, N_MOD, d)

    wi = w_in[0]
    o_q, o_k, o_v, o_r = s5w, s5w + GLA_KEY, s5w + 2 * GLA_KEY, s5w + 2 * GLA_KEY + GLA_VAL
    o_glr = o_r + GLA_VAL
    o_ga = o_glr + 2 * GLA_GATE_RANK
    wnat = jnp.concatenate([wi[:, :o_q], wi[:, o_r:o_glr], wi[:, o_ga:]], axis=1)
    wcm = jnp.concatenate([wi[:, o_q:o_k] * (GLA_DK ** -0.5), wi[:, o_k:o_r],
                           jnp.pad(wi[:, o_glr:o_ga], ((0, 0), (0, LANES - 2 * GLA_GATE_RANK)))], axis=1)
    gup = jnp.zeros((2, LANES, GLA_KEY), F32)
    gup = gup.at[0, 0:GLA_GATE_RANK].set(gla_gate_up[0, 0])
    gup = gup.at[1, GLA_GATE_RANK:2 * GLA_GATE_RANK].set(gla_gate_up[0, 1])
    row = lambda v: v.reshape(1, -1).astype(F32)
    w = dict(
        n1=row(ffn1_norm[0]), wg1=ffn1_w_gate[0].astype(BF16), wu1=ffn1_w_up[0].astype(BF16),
        wd1=ffn1_w_down[0].astype(BF16), n2=row(mix_norm[0]), wnat=wnat.astype(BF16), wcm=wcm.astype(BF16),
        dskip=row(s5_d[0]), gluw=s5_glu_w[0].astype(BF16), glub=row(s5_glu_b[0]), s5out=s5_out[0].astype(BF16),
        gnorm=row(gla_norm[0]), glaout=gla_out[0].astype(BF16), wo=w_o[0].astype(BF16),
        n3=row(ffn2_norm[0]), wg2=ffn2_w_gate[0].astype(BF16), wu2=ffn2_w_up[0].astype(BF16),
        wd2=ffn2_w_down[0].astype(BF16), fin=row(final_norm))

    lat_row = lambda bi: bi
    ctx_row = lambda bi: b
    x1 = _ffn(x, mods3, lat_row, w["n1"], w["wg1"], w["wu1"], w["wd1"], tm, fchunk, 0)
    c1 = _ffn(ctx, mods3, ctx_row, w["n1"], w["wg1"], w["wu1"], w["wd1"], lc, fchunk, 0)
    nat, xl, cm = _proj(x1, mods3, lat_row, w, tm, True)
    _, xc, cmc = _proj(c1, mods3, ctx_row, w, lc, False)

    kt, ws, cp, lamc = _s5_weights(s5_lambda_re[0], s5_lambda_im[0], s5_log_dt[0], s5_b_re[0], s5_b_im[0],
                                     s5_c_re[0], s5_c_im[0])
    ys5 = _s5(xc, xl, kt, ws, cp, lamc, gps=4, nb=b)

    of, ob = _gla(cm, cmc, gup.astype(BF16), gla_gate_b[0].reshape(2, 1, GLA_KEY).astype(F32))
    return _back(x1, nat, ys5, of, ob, mods3, w, tm, fchunk)
```

```python
import functools

import numpy as np
import jax
import jax.numpy as jnp
from jax import lax
from jax.experimental import pallas as pl
from jax.experimental.pallas import tpu as pltpu

F32 = jnp.float32
BF16 = jnp.bfloat16

RMS_EPS = 1e-6
MACARON_WEIGHT = 0.5
GRID_W = 64
N_MOD = 9
S5_GROUP = 16
S5_STATE = 64
S5_CHUNK = 16
S5_TILE = 8
GLA_HEADS = 4
GLA_DK = 64
GLA_DV = 128
GLA_CHUNK = 64
GLA_GATE_RANK = 16
GLA_GATE_NORM = 16.0
GLA_COLS_PER_STEP = 4
BACK_ROW_SPLIT = 2
GLA_KEY = GLA_HEADS * GLA_DK
GLA_VAL = GLA_HEADS * GLA_DV
LANES = 128
NAT_W = 3072
CM_W = 2 * GLA_KEY + GLA_VAL + LANES
V7X_VMEM_LIMIT_BYTES = 56 * 1024 * 1024


def _params(*sem):
    return pltpu.CompilerParams(dimension_semantics=sem, vmem_limit_bytes=V7X_VMEM_LIMIT_BYTES)


def _const_spec(shape):
    nd = len(shape)
    return pl.BlockSpec(shape, lambda *_: (0,) * nd, pipeline_mode=pl.Buffered(1))


def _rms_mod(x, g, shift, scale):
    ms = jnp.mean(x * x, axis=-1, keepdims=True)
    return (x * lax.rsqrt(ms + RMS_EPS) * g) * (1.0 + scale) + shift


def _swiglu_residual(x, h, gate, wg_ref, wu_ref, wd_ref, fchunk):
    acc = None
    for f0 in range(0, wg_ref.shape[1], fchunk):
        gg = jnp.dot(h, wg_ref[:, f0:f0 + fchunk], preferred_element_type=F32)
        uu = jnp.dot(h, wu_ref[:, f0:f0 + fchunk], preferred_element_type=F32)
        a = (gg * jax.nn.sigmoid(gg) * uu).astype(BF16)
        o = jnp.dot(a, wd_ref[f0:f0 + fchunk, :], preferred_element_type=F32)
        acc = o if acc is None else acc + o
    return x + gate * (MACARON_WEIGHT * acc)


def _ada_kernel(c_ref, w_ref, b_ref, o_ref):
    cv = c_ref[...]
    s = cv * jax.nn.sigmoid(cv)
    o_ref[...] = jnp.dot(s, w_ref[...], preferred_element_type=F32,
                         precision=lax.Precision.HIGHEST) + b_ref[...]


def _ada(cvec, ada_w, ada_b):
    rows, d = cvec.shape
    n = ada_w.shape[1]
    bn = n // 8
    return pl.pallas_call(
        _ada_kernel,
        grid=(n // bn,),
        in_specs=[pl.BlockSpec((rows, d), lambda j: (0, 0)),
                  pl.BlockSpec((d, bn), lambda j: (0, j)),
                  pl.BlockSpec((1, bn), lambda j: (0, j))],
        out_specs=pl.BlockSpec((rows, bn), lambda j: (0, j)),
        out_shape=jax.ShapeDtypeStruct((rows, n), F32),
        compiler_params=_params("arbitrary"),
    )(cvec, ada_w, ada_b.reshape(1, n))


def _ffn_kernel(x_ref, mod_ref, n_ref, wg_ref, wu_ref, wd_ref, o_ref, *, fchunk, mod0):
    x = x_ref[...]
    m = mod_ref[...]
    h = _rms_mod(x, n_ref[...], m[mod0:mod0 + 1], m[mod0 + 1:mod0 + 2]).astype(BF16)
    o_ref[...] = _swiglu_residual(x, h, m[mod0 + 2:mod0 + 3], wg_ref, wu_ref, wd_ref, fchunk)


def _ffn(x, mods3, mod_row, norm, wg, wu, wd, tm, fchunk, mod0):
    b, l, d = x.shape
    f = wg.shape[1]
    tok = pl.BlockSpec((None, tm, d), lambda bi, i: (bi, i, 0))
    return pl.pallas_call(
        functools.partial(_ffn_kernel, fchunk=fchunk, mod0=mod0),
        grid=(b, l // tm),
        in_specs=[tok, pl.BlockSpec((None, N_MOD, d), lambda bi, i: (mod_row(bi), 0, 0)),
                  _const_spec((1, d)), _const_spec((d, f)), _const_spec((d, f)), _const_spec((f, d))],
        out_specs=tok,
        out_shape=jax.ShapeDtypeStruct((b, l, d), F32),
        compiler_params=_params("arbitrary", "arbitrary"),
    )(x, mods3, norm, wg, wu, wd)


def _piece_masks(rows):
    piece = lax.broadcasted_iota(jnp.int32, (rows, LANES), 1) // S5_GROUP
    return [piece == p for p in range(LANES // S5_GROUP)]


def _to_chunk_major(up, store):
    nj = up.shape[0] // S5_CHUNK
    npc = LANES // S5_GROUP
    masks = _piece_masks(nj)
    for gg in range(up.shape[1] // LANES):
        for tt in range(S5_CHUNK // npc):
            src = [up[(npc * tt + p) * nj:(npc * tt + p + 1) * nj, gg * LANES:(gg + 1) * LANES] for p in range(npc)]
            for gl in range(npc):
                acc = None
                for p in range(npc):
                    sh = ((p - gl) * S5_GROUP) % LANES
                    r = src[p] if sh == 0 else pltpu.roll(src[p], sh, 1)
                    acc = r if acc is None else jnp.where(masks[p], r, acc)
                store(gg * npc + gl, tt, acc)


def _from_chunk_major(load, ng, nj):
    npc = LANES // S5_GROUP
    masks = _piece_masks(nj)
    row_blocks = []
    for tt in range(S5_CHUNK // npc):
        per_p = [[] for _ in range(npc)]
        for gg in range(ng // npc):
            src = [load(gg * npc + gl, tt) for gl in range(npc)]
            for p in range(npc):
                acc = None
                for gl in range(npc):
                    sh = ((gl - p) * S5_GROUP) % LANES
                    r = src[gl] if sh == 0 else pltpu.roll(src[gl], sh, 1)
                    acc = r if acc is None else jnp.where(masks[gl], r, acc)
                per_p[p].append(acc)
        row_blocks += [jnp.concatenate(blk, axis=1) for blk in per_p]
    return jnp.concatenate(row_blocks, axis=0)


def _chunk_perm(tm):
    nj = tm // S5_CHUNK
    r = np.arange(tm)
    p = np.zeros((tm, tm), np.float32)
    p[r, S5_CHUNK * (r % nj) + r // nj] = 1.0
    return p


def _column_perm(tm):
    na = tm // GRID_W
    r = np.arange(tm)
    p = np.zeros((tm, tm), np.float32)
    p[r, GRID_W * (r % na) + r // na] = 1.0
    return p


def _proj_kernel(x_ref, mod_ref, n_ref, wnat_ref, wcm_ref, p16_ref, pcol_ref, nat_ref, xs5_ref, cm_ref, *,
                 col_major):
    x = x_ref[...]
    m = mod_ref[...]
    tm = x.shape[0]
    h2 = _rms_mod(x, n_ref[...], m[3:4], m[4:5]).astype(BF16)
    u_bf = None
    for c0 in range(0, NAT_W, 1024):
        p = jnp.dot(h2, wnat_ref[:, c0:c0 + 1024], preferred_element_type=F32).astype(BF16)
        nat_ref[:, c0:c0 + 1024] = p
        if c0 == 0:
            u_bf = p[:, 0:xs5_ref.shape[0] * S5_GROUP]
    pc = jnp.dot(h2, wcm_ref[...], preferred_element_type=F32).astype(BF16)
    if col_major:
        pcm = jnp.dot(pcol_ref[...], pc, preferred_element_type=F32)
        cm_ref[...] = pcm.reshape(GRID_W, tm // GRID_W, CM_W)
    else:
        cm_ref[...] = pc.astype(F32)
    up = jnp.dot(p16_ref[...], u_bf, preferred_element_type=F32)

    def store(g, tt, v):
        xs5_ref[g, :, tt * LANES:(tt + 1) * LANES] = v.astype(BF16)

    _to_chunk_major(up, store)


def _proj(x1, mods3, mod_row, w, tm, col_major):
    b, l, d = x1.shape
    ng = w["dskip"].shape[-1] // S5_GROUP
    nj = tm // S5_CHUNK
    nblk = l // tm
    tok = lambda width: pl.BlockSpec((None, tm, width), lambda bi, i: (bi, i, 0))
    if col_major:
        cm_spec = pl.BlockSpec((None, GRID_W, tm // GRID_W, CM_W), lambda bi, i: (bi, 0, i, 0))
        cm_shape = jax.ShapeDtypeStruct((b, GRID_W, l // GRID_W, CM_W), F32)
        pcol = jnp.asarray(_column_perm(tm), BF16)
    else:
        cm_spec = tok(CM_W)
        cm_shape = jax.ShapeDtypeStruct((b, l, CM_W), F32)
        pcol = jnp.zeros((8, LANES), BF16)
    return pl.pallas_call(
        functools.partial(_proj_kernel, col_major=col_major),
        grid=(b, nblk),
        in_specs=[tok(d), pl.BlockSpec((None, N_MOD, d), lambda bi, i: (mod_row(bi), 0, 0)),
                  _const_spec((1, d)), _const_spec((d, NAT_W)), _const_spec((d, CM_W)),
                  _const_spec((tm, tm)), _const_spec(pcol.shape)],
        out_specs=[tok(NAT_W),
                   pl.BlockSpec((ng, nj, S5_CHUNK * S5_GROUP), lambda bi, i: (0, bi * nblk + i, 0)),
                   cm_spec],
        out_shape=[jax.ShapeDtypeStruct((b, l, NAT_W), BF16),
                   jax.ShapeDtypeStruct((ng, b * (l // S5_CHUNK), S5_CHUNK * S5_GROUP), BF16),
                   cm_shape],
        compiler_params=_params("arbitrary", "arbitrary"),
    )(x1, mods3, w["n2"], w["wnat"], w["wcm"], jnp.asarray(_chunk_perm(tm), BF16), pcol)


def _s5_tile_prefix(sr, si, cst, fwd):
    ar, ai = sr, si
    for lvl, dist in enumerate((1, 2, 4)):
        sh = dist if fwd else S5_TILE - dist
        rr, ri = pltpu.roll(ar, sh, 0), pltpu.roll(ai, sh, 0)
        lr, li = cst[2 * lvl], cst[2 * lvl + 1]
        ar, ai = ar + (lr * rr - li * ri), ai + (lr * ri + li * rr)
    return ar, ai


def _s5_tile_carry(ar, ai, hr, hi, cst, fwd):
    row = lax.broadcasted_iota(jnp.int32, ar.shape, 0)
    keep = (row >= 1) if fwd else (row <= S5_TILE - 2)
    sh = 1 if fwd else S5_TILE - 1
    pr = jnp.where(keep, pltpu.roll(ar, sh, 0), 0.0)
    pi = jnp.where(keep, pltpu.roll(ai, sh, 0), 0.0)
    lpr, lpi, l8r, l8i = cst[6:10]
    hin_r = lpr * hr - lpi * hi + pr
    hin_i = lpr * hi + lpi * hr + pi
    e = S5_TILE - 1 if fwd else 0
    er = jnp.broadcast_to(ar[e:e + 1, :], ar.shape)
    ei = jnp.broadcast_to(ai[e:e + 1, :], ai.shape)
    return hin_r, hin_i, l8r * hr - l8i * hi + er, l8r * hi + l8i * hr + ei


def _s5_kernel(xc_ref, xl_ref, kt_ref, ws_ref, cp_ref, lam_ref, y_ref, s_ref, *, nb):
    npair = ws_ref.shape[0]
    jc = xc_ref.shape[1] // nb
    jl = xl_ref.shape[1] // nb
    jt = jc + jl
    for pp in range(npair):
        for src, j0, nj in ((xc_ref, 0, jc), (xl_ref, jc, jl)):
            s = jnp.dot(src[2 * pp], ws_ref[pp, 0:256, :], preferred_element_type=F32)
            s = s + jnp.dot(src[2 * pp + 1], ws_ref[pp, 256:512, :], preferred_element_type=F32)
            for bi in range(nb):
                s_ref[pp, bi * jt + j0:bi * jt + j0 + nj, :] = s[bi * nj:(bi + 1) * nj, :]

    nct = jc // S5_TILE
    nt = jt // S5_TILE
    zero = jnp.zeros((S5_TILE, LANES), F32)
    for pp in range(npair):

        def prefix(it, carry, pp=pp):
            r0 = pl.multiple_of(it * S5_TILE, S5_TILE)
            for c0, fwd in ((0, True), (256, False)):
                cst = [lam_ref[pp, (0 if fwd else 10) + k] for k in range(6)]
                ar, ai = _s5_tile_prefix(s_ref[pp, pl.ds(r0, S5_TILE), c0:c0 + 128],
                                         s_ref[pp, pl.ds(r0, S5_TILE), c0 + 128:c0 + 256], cst, fwd)
                s_ref[pp, pl.ds(r0, S5_TILE), c0:c0 + 128] = ar
                s_ref[pp, pl.ds(r0, S5_TILE), c0 + 128:c0 + 256] = ai
            return carry

        lax.fori_loop(0, nb * nt, prefix, 0, unroll=8)

        def step(it, carry, pp=pp):
            mb = jnp.where(it < nct, nct - 1 - it, nt - 1 - (it - nct))
            cf = [lam_ref[pp, k] for k in range(10)]
            cb = [lam_ref[pp, 10 + k] for k in range(10)]
            out = []
            for bi in range(nb):
                hfr, hfi, hbr, hbi = carry[bi]
                rf = pl.multiple_of(bi * jt + it * S5_TILE, S5_TILE)
                rb = pl.multiple_of(bi * jt + mb * S5_TILE, S5_TILE)
                fr, fi, hfr, hfi = _s5_tile_carry(s_ref[pp, pl.ds(rf, S5_TILE), 0:128],
                                                  s_ref[pp, pl.ds(rf, S5_TILE), 128:256], hfr, hfi, cf, True)
                s_ref[pp, pl.ds(rf, S5_TILE), 0:128] = fr
                s_ref[pp, pl.ds(rf, S5_TILE), 128:256] = fi
                br, bim, hbr, hbi = _s5_tile_carry(s_ref[pp, pl.ds(rb, S5_TILE), 256:384],
                                                   s_ref[pp, pl.ds(rb, S5_TILE), 384:512], hbr, hbi, cb, False)
                s_ref[pp, pl.ds(rb, S5_TILE), 256:384] = br
                s_ref[pp, pl.ds(rb, S5_TILE), 384:512] = bim
                out.append((hfr, hfi, hbr, hbi))
            return tuple(out)

        lax.fori_loop(0, nt, step, tuple((zero, zero, zero, zero) for _ in range(nb)))

    tc = S5_CHUNK * S5_GROUP
    width = kt_ref.shape[2]
    for pp in range(npair):
        toeps = []
        for gi in range(2):
            kt = kt_ref[2 * pp + gi]
            blocks = []
            for s in range(S5_CHUNK):
                sh = (width - S5_GROUP * (S5_CHUNK - 1 - s)) % width
                blocks.append((kt if sh == 0 else pltpu.roll(kt, sh, 1))[:, 0:tc])
            toeps.append(jnp.concatenate(blocks, axis=0).astype(BF16))
        for bi in range(nb):
            hin = s_ref[pp, bi * jt + jc:(bi + 1) * jt, :].astype(BF16)
            yp = jnp.dot(hin, cp_ref[pp], preferred_element_type=F32)
            for gi in range(2):
                g = 2 * pp + gi
                y = yp[:, gi * 256:(gi + 1) * 256] + jnp.dot(xl_ref[g, bi * jl:(bi + 1) * jl, :], toeps[gi],
                                                             preferred_element_type=F32)
                y_ref[g, bi * jl:(bi + 1) * jl, :] = y.astype(BF16)


def _s5(xc, xl, kt, ws, cp, lamc, gps, nb):
    ng, rc, _ = xc.shape
    rl = xl.shape[1]
    npair = gps // 2
    return pl.pallas_call(
        functools.partial(_s5_kernel, nb=nb),
        grid=(ng // gps,),
        in_specs=[pl.BlockSpec((gps, rc, 256), lambda i: (i, 0, 0)),
                  pl.BlockSpec((gps, rl, 256), lambda i: (i, 0, 0)),
                  pl.BlockSpec((gps,) + kt.shape[1:], lambda i: (i, 0, 0)),
                  pl.BlockSpec((npair, 512, 512), lambda i: (i, 0, 0)),
                  pl.BlockSpec((npair, 512, 512), lambda i: (i, 0, 0)),
                  pl.BlockSpec((npair, 20, S5_TILE, LANES), lambda i: (i, 0, 0, 0))],
        out_specs=pl.BlockSpec((gps, rl, 256), lambda i: (i, 0, 0)),
        out_shape=jax.ShapeDtypeStruct((ng, rl, 256), BF16),
        scratch_shapes=[pltpu.VMEM((npair, rc + rl, 512), F32)],
        compiler_params=_params("arbitrary"),
    )(xc, xl, kt, ws, cp, lamc)


def _cmul(ar, ai, br, bi):
    return ar * br - ai * bi, ar * bi + ai * br


def _s5_weights(lam_re, lam_im, log_dt, b_re, b_im, c_re, c_im):
    nd, ng, p = lam_re.shape
    t = S5_CHUNK
    tc = t * S5_GROUP
    lr = jnp.minimum(lam_re.astype(F32), -1e-4)
    li = lam_im.astype(F32)
    dt = jnp.exp(log_dt.astype(F32))[..., None]
    mag = jnp.exp(lr * dt)
    lbr, lbi = mag * jnp.cos(li * dt), mag * jnp.sin(li * dt)
    den = lr * lr + li * li
    fr = ((lbr - 1.0) * lr + lbi * li) / den
    fi = (lbi * lr - (lbr - 1.0) * li) / den
    bbr, bbi = _cmul(fr[..., None], fi[..., None], b_re.astype(F32), b_im.astype(F32))
    cr, ci = c_re.astype(F32), c_im.astype(F32)

    pr, pi = [jnp.ones_like(lbr)], [jnp.zeros_like(lbr)]
    for _ in range(t):
        nr, ni = _cmul(pr[-1], pi[-1], lbr, lbi)
        pr.append(nr)
        pi.append(ni)
    pr, pi = jnp.stack(pr), jnp.stack(pi)
    wr, wi = _cmul(pr[:t, ..., None], pi[:t, ..., None], bbr[None], bbi[None])

    kern = jnp.einsum("dgxp,kdgpc->dgkxc", cr, wr) - jnp.einsum("dgxp,kdgpc->dgkxc", ci, wi)
    ktf = kern[0].transpose(0, 3, 1, 2)
    ktb = kern[1][:, ::-1].transpose(0, 3, 1, 2)
    kt = jnp.concatenate([ktb[:, :, :t - 1], ktf[:, :, 0:1] + ktb[:, :, t - 1:t], ktf[:, :, 1:]], axis=2)
    kt = kt.reshape(ng, S5_GROUP, (2 * t - 1) * S5_GROUP)
    kt = jnp.pad(kt, ((0, 0), (0, 0), (0, 2 * tc - kt.shape[-1])))

    parity = (jnp.arange(ng) % 2)[:, None, None]

    gscp = lambda a: a.transpose(1, 0, 3, 2).reshape(ng, tc, p)
    parts = [gscp(wr[::-1, 0]), gscp(wi[::-1, 0]), gscp(wr[:, 1]), gscp(wi[:, 1])]
    ws = jnp.concatenate([a for a in parts for _ in range(2)], axis=-1)
    own = (jnp.arange(8 * p) // p) % 2 == parity
    ws = jnp.where(own, ws, 0.0).reshape(ng // 2, 2 * tc, 8 * p)

    def readout(d, powr, powi):
        mr, mi = _cmul(cr[d][None], ci[d][None], powr[:, :, None, :], powi[:, :, None, :])
        to_gptx = lambda a: a.transpose(1, 3, 0, 2).reshape(ng, p, tc)
        return [to_gptx(mr), to_gptx(-mi)]

    parts = readout(0, pr[1:t + 1, 0], pi[1:t + 1, 0]) + readout(1, pr[1:t + 1, 1][::-1], pi[1:t + 1, 1][::-1])
    own = jnp.arange(2 * tc) // tc == parity
    cp = jnp.stack([jnp.where(own, jnp.concatenate([a, a], axis=-1), 0.0) for a in parts], axis=1)
    cp = cp.reshape(ng // 2, 2, 4, p, 2 * tc).transpose(0, 2, 1, 3, 4).reshape(ng // 2, 8 * p, 2 * tc)

    row = jnp.arange(S5_TILE)
    l1 = (pr[t], pi[t])
    l2 = _cmul(*l1, *l1)
    l4 = _cmul(*l2, *l2)
    l8 = _cmul(*l4, *l4)
    rp = [(jnp.ones_like(lbr), jnp.zeros_like(lbr))]
    for _ in range(S5_TILE - 1):
        rp.append(_cmul(*rp[-1], *l1))
    planes = []
    for d in range(nd):
        valid = (lambda dist: row >= dist) if d == 0 else (lambda dist: row <= S5_TILE - 1 - dist)
        for (qr, qi), dist in ((l1, 1), (l2, 2), (l4, 4)):
            m = valid(dist).astype(F32)[:, None, None]
            planes += [m * qr[d][None], m * qi[d][None]]
        order = row if d == 0 else row[::-1]
        planes += [jnp.stack([rp[k][0][d] for k in range(S5_TILE)])[order],
                   jnp.stack([rp[k][1][d] for k in range(S5_TILE)])[order]]
        planes += [jnp.broadcast_to(l8[0][d], (S5_TILE, ng, p)), jnp.broadcast_to(l8[1][d], (S5_TILE, ng, p))]
    lamc = jnp.stack(planes)
    lamc = lamc.reshape(20, S5_TILE, ng // 2, 2 * p).transpose(2, 0, 1, 3)
    return kt, ws.astype(BF16), cp.astype(BF16), lamc


def _chunk_cumsum(g, rev):
    sub = 8
    nt = g.shape[0] // sub
    row = lax.broadcasted_iota(jnp.int32, (sub, g.shape[1]), 0)
    out = [None] * nt
    off = None
    for kk in (range(nt - 1, -1, -1) if rev else range(nt)):
        x = g[kk * sub:(kk + 1) * sub, :]
        for dist in (1, 2, 4):
            if rev:
                x = x + jnp.where(row < sub - dist, pltpu.roll(x, sub - dist, 0), 0.0)
            else:
                x = x + jnp.where(row >= dist, pltpu.roll(x, dist, 0), 0.0)
        if off is not None:
            x = x + off
        out[kk] = x
        e = 0 if rev else sub - 1
        off = jnp.broadcast_to(x[e:e + 1, :], x.shape)
    return jnp.concatenate(out, axis=0)


def _gla_log_decay(glr, gup, gbias, rev):
    c = GLA_CHUNK
    z = jnp.dot(glr.astype(BF16), gup, preferred_element_type=F32) + gbias
    g = (jnp.minimum(z, 0.0) - jnp.log(1.0 + jnp.exp(-jnp.abs(z)))) * (1.0 / GLA_GATE_NORM)
    return [_chunk_cumsum(g[n * c:(n + 1) * c, :], rev) for n in range(g.shape[0] // c)]


def _gla_chunk(q, k, v, gc, st, rev, need_out):
    c = GLA_CHUNK
    i_ref = c // 2 - 1 if rev else c // 2
    i_last = 0 if rev else c - 1
    g_ref = gc[i_ref:i_ref + 1, :]
    g_last = gc[i_last:i_last + 1, :]
    lane_head = lax.broadcasted_iota(jnp.int32, (1, GLA_KEY), 1) // GLA_DK

    stack = lambda a: jnp.concatenate(
        [jnp.where(lane_head == h, a, 0.0) for h in range(GLA_HEADS)], axis=0).astype(BF16)
    dv2 = 2 * GLA_DV
    vt = jnp.concatenate(
        [jnp.concatenate([v[:, p * dv2:p * dv2 + GLA_DV], v[:, p * dv2 + GLA_DV:(p + 1) * dv2]], axis=0).T
         for p in range(GLA_HEADS // 2)], axis=1).astype(BF16)
    kl = k * jnp.exp(g_last - gc)
    kv = jnp.dot(vt, stack(kl), preferred_element_type=F32)
    st_new = st * jnp.exp(g_last) + kv
    if not need_out:
        return None, st_new

    qe = q * jnp.exp(gc - g_ref)
    ke = (k * jnp.exp(g_ref - gc)).astype(BF16)
    qg = q * jnp.exp(gc)
    nt_dims = (((1,), (1,)), ((), ()))
    sc = lax.dot_general(stack(qe), ke, nt_dims, preferred_element_type=F32)
    rs = lax.broadcasted_iota(jnp.int32, (GLA_HEADS * c, c), 0) % c
    cs = lax.broadcasted_iota(jnp.int32, (GLA_HEADS * c, c), 1)
    keep = (rs <= cs) if rev else (rs >= cs)
    sc = jnp.where(keep, sc, 0.0).astype(BF16)
    vb = v.astype(BF16)
    oo = lax.dot_general(stack(qg), st.astype(BF16), nt_dims, preferred_element_type=F32)
    o = jnp.concatenate(
        [jnp.dot(sc[h * c:(h + 1) * c, :], vb[:, h * GLA_DV:(h + 1) * GLA_DV], preferred_element_type=F32)
         + oo[h * c:(h + 1) * c, :] for h in range(GLA_HEADS)], axis=1)
    return o, st_new


def _gla_kernel(cf_ref, cb_ref, cc_ref, gup_ref, gbias_ref, of_ref, ob_ref, stf_ref, stb_ref):
    c = GLA_CHUNK
    q0, k0, v0, r0 = 0, GLA_KEY, 2 * GLA_KEY, 2 * GLA_KEY + GLA_VAL

    def decay(ref, d):
        return _gla_log_decay(ref[:, r0:CM_W], gup_ref[d], gbias_ref[d], d == 1)

    def run(ref, n, gcs, d, st, need_out):
        rows = slice(n * c, (n + 1) * c)
        q = ref[rows, q0:k0] if need_out else None
        return _gla_chunk(q, ref[rows, k0:v0], ref[rows, v0:r0], gcs[n], st, d == 1, need_out)

    @pl.when(pl.program_id(1) == 0)
    def _():
        nctx = cc_ref.shape[0] // c
        gf, gb = decay(cc_ref, 0), decay(cc_ref, 1)
        stf = jnp.zeros(stf_ref.shape, F32)
        stb = jnp.zeros(stb_ref.shape, F32)
        for n in range(nctx):
            _, stf = run(cc_ref, n, gf, 0, stf, False)
            _, stb = run(cc_ref, nctx - 1 - n, gb, 1, stb, False)
        stf_ref[...] = stf
        stb_ref[...] = stb

    ncols, rows_per_col, _ = cf_ref.shape
    nch = rows_per_col // c
    stf = stf_ref[...]
    stb = stb_ref[...]
    for j in range(ncols):
        jb = ncols - 1 - j
        gf, gb = decay(cf_ref.at[j], 0), decay(cb_ref.at[jb], 1)
        for n in range(nch):
            o, stf = run(cf_ref.at[j], n, gf, 0, stf, True)
            of_ref[j, n * c:(n + 1) * c, :] = o
            m = nch - 1 - n
            o, stb = run(cb_ref.at[jb], m, gb, 1, stb, True)
            ob_ref[jb, m * c:(m + 1) * c, :] = o
    stf_ref[...] = stf
    stb_ref[...] = stb


def _gla(cm, cmc, gup, gbias):
    b, ncol, rows, _ = cm.shape
    lc = cmc.shape[1]
    cps = GLA_COLS_PER_STEP
    last = ncol // cps - 1
    col = lambda width, fn: pl.BlockSpec((None, cps, rows, width), fn)
    out_shape = jax.ShapeDtypeStruct((b, ncol, rows, GLA_VAL), F32)
    return pl.pallas_call(
        _gla_kernel,
        grid=(b, ncol // cps),
        in_specs=[col(CM_W, lambda bi, ci: (bi, ci, 0, 0)),
                  col(CM_W, lambda bi, ci: (bi, last - ci, 0, 0)),
                  pl.BlockSpec((None, lc, CM_W), lambda bi, ci: (bi, 0, 0)),
                  _const_spec(gup.shape), _const_spec(gbias.shape)],
        out_specs=[col(GLA_VAL, lambda bi, ci: (bi, ci, 0, 0)),
                   col(GLA_VAL, lambda bi, ci: (bi, last - ci, 0, 0))],
        out_shape=[out_shape, out_shape],
        scratch_shapes=[pltpu.VMEM((GLA_DV, GLA_KEY), F32), pltpu.VMEM((GLA_DV, GLA_KEY), F32)],
        compiler_params=_params("arbitrary", "arbitrary"),
    )(cm, cm, cmc, gup, gbias)


def _back_kernel(x1_ref, nat_ref, y_ref, of_ref, ob_ref, mod_ref, p16t_ref, pcolt_ref, dskip_ref, gluw_ref,
                 glub_ref, s5out_ref, gnorm_ref, glaout_ref, wo_ref, n3_ref, wg_ref, wu_ref, wd_ref, fin_ref,
                 out_ref, *, fchunk):
    m = mod_ref[...]
    tm, d = x1_ref.shape
    ng, nj, _ = y_ref.shape
    s5w = ng * S5_GROUP

    yp = _from_chunk_major(lambda g, tt: y_ref[g, :, tt * LANES:(tt + 1) * LANES].astype(F32), ng, nj)
    ys_all = jnp.dot(p16t_ref[...], yp.astype(BF16), preferred_element_type=F32)
    ocm = (of_ref[...] + ob_ref[...]).reshape(tm, GLA_VAL).astype(BF16)
    o_all = jnp.dot(pcolt_ref[...], ocm, preferred_element_type=F32)

    for r0 in range(0, tm, tm // BACK_ROW_SPLIT):
        rows = slice(r0, r0 + tm // BACK_ROW_SPLIT)
        ya = ys_all[rows, :] + dskip_ref[...] * nat_ref[rows, 0:s5w].astype(F32)
        ya = 0.5 * ya * (1.0 + jnp.tanh(0.7978845608028654 * (ya + 0.044715 * (ya * ya * ya))))
        gl = jnp.dot(ya.astype(BF16), gluw_ref[...], preferred_element_type=F32) + glub_ref[...]
        ya = ya * jax.nn.sigmoid(gl)

        heads = []
        for h in range(GLA_HEADS):
            oh = o_all[rows, h * GLA_DV:(h + 1) * GLA_DV]
            heads.append(oh * lax.rsqrt(jnp.mean(oh * oh, axis=-1, keepdims=True) + RMS_EPS))
        r = nat_ref[rows, s5w:s5w + GLA_VAL].astype(F32)
        yb = jnp.concatenate(heads, axis=1) * gnorm_ref[...] * (r * jax.nn.sigmoid(r))

        pa = jnp.dot(ya.astype(BF16), s5out_ref[...], preferred_element_type=F32)
        pb = jnp.dot(yb.astype(BF16), glaout_ref[...], preferred_element_type=F32)
        ga = nat_ref[rows, s5w + GLA_VAL:s5w + GLA_VAL + d].astype(F32)
        gb = nat_ref[rows, s5w + GLA_VAL + d:s5w + GLA_VAL + 2 * d].astype(F32)
        mg = jax.nn.sigmoid(ga) * pa + jax.nn.sigmoid(gb) * pb
        y = jnp.dot(mg.astype(BF16), wo_ref[...], preferred_element_type=F32)
        x2 = x1_ref[rows, :] + m[5:6] * y
        h = _rms_mod(x2, n3_ref[...], m[6:7], m[7:8]).astype(BF16)
        x3 = _swiglu_residual(x2, h, m[8:9], wg_ref, wu_ref, wd_ref, fchunk)
        ms = jnp.mean(x3 * x3, axis=-1, keepdims=True)
        out_ref[rows, :] = x3 * lax.rsqrt(ms + RMS_EPS) * fin_ref[...]


def _back(x1, nat, ys5, of, ob, mods3, w, tm, fchunk):
    b, l, d = x1.shape
    f = w["wg2"].shape[1]
    s5w = w["dskip"].shape[-1]
    ng = s5w // S5_GROUP
    nblk = l // tm
    tok = lambda width: pl.BlockSpec((None, tm, width), lambda bi, i: (bi, i, 0))
    colblk = pl.BlockSpec((None, GRID_W, tm // GRID_W, GLA_VAL), lambda bi, i: (bi, 0, i, 0))
    return pl.pallas_call(
        functools.partial(_back_kernel, fchunk=fchunk),
        grid=(b, nblk),
        in_specs=[tok(d), tok(NAT_W),
                  pl.BlockSpec((ng, tm // S5_CHUNK, S5_CHUNK * S5_GROUP), lambda bi, i: (0, bi * nblk + i, 0)),
                  colblk, colblk,
                  pl.BlockSpec((None, N_MOD, d), lambda bi, i: (bi, 0, 0)),
                  _const_spec((tm, tm)), _const_spec((tm, tm)),
                  _const_spec((1, s5w)), _const_spec((s5w, s5w)), _const_spec((1, s5w)),
                  _const_spec((s5w, d)), _const_spec((1, GLA_VAL)), _const_spec((GLA_VAL, d)),
                  _const_spec((d, d)), _const_spec((1, d)), _const_spec((d, f)), _const_spec((d, f)),
                  _const_spec((f, d)), _const_spec((1, d))],
        out_specs=tok(d),
        out_shape=jax.ShapeDtypeStruct((b, l, d), F32),
        compiler_params=_params("arbitrary", "arbitrary"),
    )(x1, nat, ys5, of, ob, mods3, jnp.asarray(_chunk_perm(tm).T, BF16), jnp.asarray(_column_perm(tm).T, BF16),
      w["dskip"], w["gluw"], w["glub"], w["s5out"], w["gnorm"], w["glaout"], w["wo"], w["n3"], w["wg2"],
      w["wu2"], w["wd2"], w["fin"])


def kernel(x, c, ctx, c_ctx, ada_w, ada_b, ffn1_norm, ffn1_w_gate, ffn1_w_up, ffn1_w_down, mix_norm, w_in,
           s5_lambda_re, s5_lambda_im, s5_log_dt, s5_b_re, s5_b_im, s5_c_re, s5_c_im, s5_d, s5_glu_w, s5_glu_b,
           s5_out, gla_gate_up, gla_gate_b, gla_norm, gla_out, w_o, ffn2_norm, ffn2_w_gate, ffn2_w_up,
           ffn2_w_down, final_norm):
    b, l, d = x.shape
    lc = ctx.shape[1]
    assert ada_w.shape[0] == 1 and b + 1 <= 8
    s5w = s5_d.shape[-1]
    fchunk = ffn1_w_gate.shape[-1] // 2
    tm = min(512, l)
    assert tm % (GRID_W * 8) == 0 and l % tm == 0 and lc % (S5_CHUNK * S5_TILE) == 0 and lc % GLA_CHUNK == 0

    cvec = jnp.concatenate([c, c_ctx[None, :], jnp.zeros((8 - b - 1, d), F32)], axis=0)
    mods3 = _ada(cvec, ada_w[0], ada_b[0]).reshape(8, N_MOD, d)

    wi = w_in[0]
    o_q, o_k, o_v, o_r = s5w, s5w + GLA_KEY, s5w + 2 * GLA_KEY, s5w + 2 * GLA_KEY + GLA_VAL
    o_glr = o_r + GLA_VAL
    o_ga = o_glr + 2 * GLA_GATE_RANK
    wnat = jnp.concatenate([wi[:, :o_q], wi[:, o_r:o_glr], wi[:, o_ga:]], axis=1)
    wcm = jnp.concatenate([wi[:, o_q:o_k] * (GLA_DK ** -0.5), wi[:, o_k:o_r],
                           jnp.pad(wi[:, o_glr:o_ga], ((0, 0), (0, LANES - 2 * GLA_GATE_RANK)))], axis=1)
    gup = jnp.zeros((2, LANES, GLA_KEY), F32)
    gup = gup.at[0, 0:GLA_GATE_RANK].set(gla_gate_up[0, 0])
    gup = gup.at[1, GLA_GATE_RANK:2 * GLA_GATE_RANK].set(gla_gate_up[0, 1])
    row = lambda v: v.reshape(1, -1).astype(F32)
    w = dict(
        n1=row(ffn1_norm[0]), wg1=ffn1_w_gate[0].astype(BF16), wu1=ffn1_w_up[0].astype(BF16),
        wd1=ffn1_w_down[0].astype(BF16), n2=row(mix_norm[0]), wnat=wnat.astype(BF16), wcm=wcm.astype(BF16),
        dskip=row(s5_d[0]), gluw=s5_glu_w[0].astype(BF16), glub=row(s5_glu_b[0]), s5out=s5_out[0].astype(BF16),
        gnorm=row(gla_norm[0]), glaout=gla_out[0].astype(BF16), wo=w_o[0].astype(BF16),
        n3=row(ffn2_norm[0]), wg2=ffn2_w_gate[0].astype(BF16), wu2=ffn2_w_up[0].astype(BF16),
        wd2=ffn2_w_down[0].astype(BF16), fin=row(final_norm))

    lat_row = lambda bi: bi
    ctx_row = lambda bi: b
    x1 = _ffn(x, mods3, lat_row, w["n1"], w["wg1"], w["wu1"], w["wd1"], tm, fchunk, 0)
    c1 = _ffn(ctx, mods3, ctx_row, w["n1"], w["wg1"], w["wu1"], w["wd1"], lc, fchunk, 0)
    nat, xl, cm = _proj(x1, mods3, lat_row, w, tm, True)
    _, xc, cmc = _proj(c1, mods3, ctx_row, w, lc, False)

    kt, ws, cp, lamc = _s5_weights(s5_lambda_re[0], s5_lambda_im[0], s5_log_dt[0], s5_b_re[0], s5_b_im[0],
                                     s5_c_re[0], s5_c_im[0])
    ys5 = _s5(xc, xl, kt, ws, cp, lamc, gps=4, nb=b)

    of, ob = _gla(cm, cmc, gup.astype(BF16), gla_gate_b[0].reshape(2, 1, GLA_KEY).astype(F32))
    return _back(x1, nat, ys5, of, ob, mods3, w, tm, fchunk)
```

```python
import functools

import numpy as np
import jax
import jax.numpy as jnp
from jax import lax
from jax.experimental import pallas as pl
from jax.experimental.pallas import tpu as pltpu

F32 = jnp.float32
BF16 = jnp.bfloat16

RMS_EPS = 1e-6
MACARON_WEIGHT = 0.5
GRID_W = 64
N_MOD = 9
S5_GROUP = 16
S5_STATE = 64
S5_CHUNK = 16
S5_TILE = 8
GLA_HEADS = 4
GLA_DK = 64
GLA_DV = 128
GLA_CHUNK = 64
GLA_GATE_RANK = 16
GLA_GATE_NORM = 16.0
GLA_COLS_PER_STEP = 4
BACK_ROW_SPLIT = 1
GLA_KEY = GLA_HEADS * GLA_DK
GLA_VAL = GLA_HEADS * GLA_DV
LANES = 128
NAT_W = 3072
CM_W = 2 * GLA_KEY + GLA_VAL + LANES
V7X_VMEM_LIMIT_BYTES = 56 * 1024 * 1024


def _params(*sem):
    return pltpu.CompilerParams(dimension_semantics=sem, vmem_limit_bytes=V7X_VMEM_LIMIT_BYTES)


def _const_spec(shape):
    nd = len(shape)
    return pl.BlockSpec(shape, lambda *_: (0,) * nd, pipeline_mode=pl.Buffered(1))


def _rms_mod(x, g, shift, scale):
    ms = jnp.mean(x * x, axis=-1, keepdims=True)
    return (x * lax.rsqrt(ms + RMS_EPS) * g) * (1.0 + scale) + shift


def _swiglu_residual(x, h, gate, wg_ref, wu_ref, wd_ref, fchunk):
    acc = None
    for f0 in range(0, wg_ref.shape[1], fchunk):
        gg = jnp.dot(h, wg_ref[:, f0:f0 + fchunk], preferred_element_type=F32)
        uu = jnp.dot(h, wu_ref[:, f0:f0 + fchunk], preferred_element_type=F32)
        a = (gg * jax.nn.sigmoid(gg) * uu).astype(BF16)
        o = jnp.dot(a, wd_ref[f0:f0 + fchunk, :], preferred_element_type=F32)
        acc = o if acc is None else acc + o
    return x + gate * (MACARON_WEIGHT * acc)


def _ada_kernel(c_ref, w_ref, b_ref, o_ref):
    cv = c_ref[...]
    s = cv * jax.nn.sigmoid(cv)
    o_ref[...] = jnp.dot(s, w_ref[...], preferred_element_type=F32,
                         precision=lax.Precision.HIGHEST) + b_ref[...]


def _ada(cvec, ada_w, ada_b):
    rows, d = cvec.shape
    n = ada_w.shape[1]
    bn = n // 8
    return pl.pallas_call(
        _ada_kernel,
        grid=(n // bn,),
        in_specs=[pl.BlockSpec((rows, d), lambda j: (0, 0)),
                  pl.BlockSpec((d, bn), lambda j: (0, j)),
                  pl.BlockSpec((1, bn), lambda j: (0, j))],
        out_specs=pl.BlockSpec((rows, bn), lambda j: (0, j)),
        out_shape=jax.ShapeDtypeStruct((rows, n), F32),
        compiler_params=_params("arbitrary"),
    )(cvec, ada_w, ada_b.reshape(1, n))


def _ffn_kernel(x_ref, mod_ref, n_ref, wg_ref, wu_ref, wd_ref, o_ref, *, fchunk, mod0):
    x = x_ref[...]
    m = mod_ref[...]
    h = _rms_mod(x, n_ref[...], m[mod0:mod0 + 1], m[mod0 + 1:mod0 + 2]).astype(BF16)
    o_ref[...] = _swiglu_residual(x, h, m[mod0 + 2:mod0 + 3], wg_ref, wu_ref, wd_ref, fchunk)


def _ffn(x, mods3, mod_row, norm, wg, wu, wd, tm, fchunk, mod0):
    b, l, d = x.shape
    f = wg.shape[1]
    tok = pl.BlockSpec((None, tm, d), lambda bi, i: (bi, i, 0))
    return pl.pallas_call(
        functools.partial(_ffn_kernel, fchunk=fchunk, mod0=mod0),
        grid=(b, l // tm),
        in_specs=[tok, pl.BlockSpec((None, N_MOD, d), lambda bi, i: (mod_row(bi), 0, 0)),
                  _const_spec((1, d)), _const_spec((d, f)), _const_spec((d, f)), _const_spec((f, d))],
        out_specs=tok,
        out_shape=jax.ShapeDtypeStruct((b, l, d), F32),
        compiler_params=_params("arbitrary", "arbitrary"),
    )(x, mods3, norm, wg, wu, wd)


def _piece_masks(rows):
    piece = lax.broadcasted_iota(jnp.int32, (rows, LANES), 1) // S5_GROUP
    return [piece == p for p in range(LANES // S5_GROUP)]


def _to_chunk_major(up, store):
    nj = up.shape[0] // S5_CHUNK
    npc = LANES // S5_GROUP
    masks = _piece_masks(nj)
    for gg in range(up.shape[1] // LANES):
        for tt in range(S5_CHUNK // npc):
            src = [up[(npc * tt + p) * nj:(npc * tt + p + 1) * nj, gg * LANES:(gg + 1) * LANES] for p in range(npc)]
            for gl in range(npc):
                acc = None
                for p in range(npc):
                    sh = ((p - gl) * S5_GROUP) % LANES
                    r = src[p] if sh == 0 else pltpu.roll(src[p], sh, 1)
                    acc = r if acc is None else jnp.where(masks[p], r, acc)
                store(gg * npc + gl, tt, acc)


def _from_chunk_major(load, ng, nj):
    npc = LANES // S5_GROUP
    masks = _piece_masks(nj)
    row_blocks = []
    for tt in range(S5_CHUNK // npc):
        per_p = [[] for _ in range(npc)]
        for gg in range(ng // npc):
            src = [load(gg * npc + gl, tt) for gl in range(npc)]
            for p in range(npc):
                acc = None
                for gl in range(npc):
                    sh = ((gl - p) * S5_GROUP) % LANES
                    r = src[gl] if sh == 0 else pltpu.roll(src[gl], sh, 1)
                    acc = r if acc is None else jnp.where(masks[gl], r, acc)
                per_p[p].append(acc)
        row_blocks += [jnp.concatenate(blk, axis=1) for blk in per_p]
    return jnp.concatenate(row_blocks, axis=0)


def _chunk_perm(tm):
    nj = tm // S5_CHUNK
    r = np.arange(tm)
    p = np.zeros((tm, tm), np.float32)
    p[r, S5_CHUNK * (r % nj) + r // nj] = 1.0
    return p


def _column_perm(tm):
    na = tm // GRID_W
    r = np.arange(tm)
    p = np.zeros((tm, tm), np.float32)
    p[r, GRID_W * (r % na) + r // na] = 1.0
    return p


def _proj_kernel(x_ref, mod_ref, n_ref, wnat_ref, wcm_ref, p16_ref, pcol_ref, nat_ref, xs5_ref, cm_ref, *,
                 col_major):
    x = x_ref[...]
    m = mod_ref[...]
    tm = x.shape[0]
    h2 = _rms_mod(x, n_ref[...], m[3:4], m[4:5]).astype(BF16)
    u_bf = None
    for c0 in range(0, NAT_W, 1024):
        p = jnp.dot(h2, wnat_ref[:, c0:c0 + 1024], preferred_element_type=F32).astype(BF16)
        nat_ref[:, c0:c0 + 1024] = p
        if c0 == 0:
            u_bf = p[:, 0:xs5_ref.shape[0] * S5_GROUP]
    pc = jnp.dot(h2, wcm_ref[...], preferred_element_type=F32).astype(BF16)
    if col_major:
        pcm = jnp.dot(pcol_ref[...], pc, preferred_element_type=F32)
        cm_ref[...] = pcm.reshape(GRID_W, tm // GRID_W, CM_W)
    else:
        cm_ref[...] = pc.astype(F32)
    up = jnp.dot(p16_ref[...], u_bf, preferred_element_type=F32)

    def store(g, tt, v):
        xs5_ref[g, :, tt * LANES:(tt + 1) * LANES] = v.astype(BF16)

    _to_chunk_major(up, store)


def _proj(x1, mods3, mod_row, w, tm, col_major):
    b, l, d = x1.shape
    ng = w["dskip"].shape[-1] // S5_GROUP
    nj = tm // S5_CHUNK
    nblk = l // tm
    tok = lambda width: pl.BlockSpec((None, tm, width), lambda bi, i: (bi, i, 0))
    if col_major:
        cm_spec = pl.BlockSpec((None, GRID_W, tm // GRID_W, CM_W), lambda bi, i: (bi, 0, i, 0))
        cm_shape = jax.ShapeDtypeStruct((b, GRID_W, l // GRID_W, CM_W), F32)
        pcol = jnp.asarray(_column_perm(tm), BF16)
    else:
        cm_spec = tok(CM_W)
        cm_shape = jax.ShapeDtypeStruct((b, l, CM_W), F32)
        pcol = jnp.zeros((8, LANES), BF16)
    return pl.pallas_call(
        functools.partial(_proj_kernel, col_major=col_major),
        grid=(b, nblk),
        in_specs=[tok(d), pl.BlockSpec((None, N_MOD, d), lambda bi, i: (mod_row(bi), 0, 0)),
                  _const_spec((1, d)), _const_spec((d, NAT_W)), _const_spec((d, CM_W)),
                  _const_spec((tm, tm)), _const_spec(pcol.shape)],
        out_specs=[tok(NAT_W),
                   pl.BlockSpec((ng, nj, S5_CHUNK * S5_GROUP), lambda bi, i: (0, bi * nblk + i, 0)),
                   cm_spec],
        out_shape=[jax.ShapeDtypeStruct((b, l, NAT_W), BF16),
                   jax.ShapeDtypeStruct((ng, b * (l // S5_CHUNK), S5_CHUNK * S5_GROUP), BF16),
                   cm_shape],
        compiler_params=_params("arbitrary", "arbitrary"),
    )(x1, mods3, w["n2"], w["wnat"], w["wcm"], jnp.asarray(_chunk_perm(tm), BF16), pcol)


def _s5_tile_prefix(sr, si, cst, fwd):
    ar, ai = sr, si
    for lvl, dist in enumerate((1, 2, 4)):
        sh = dist if fwd else S5_TILE - dist
        rr, ri = pltpu.roll(ar, sh, 0), pltpu.roll(ai, sh, 0)
        lr, li = cst[2 * lvl], cst[2 * lvl + 1]
        ar, ai = ar + (lr * rr - li * ri), ai + (lr * ri + li * rr)
    return ar, ai


def _s5_tile_carry(ar, ai, hr, hi, cst, fwd):
    row = lax.broadcasted_iota(jnp.int32, ar.shape, 0)
    keep = (row >= 1) if fwd else (row <= S5_TILE - 2)
    sh = 1 if fwd else S5_TILE - 1
    pr = jnp.where(keep, pltpu.roll(ar, sh, 0), 0.0)
    pi = jnp.where(keep, pltpu.roll(ai, sh, 0), 0.0)
    lpr, lpi, l8r, l8i = cst[6:10]
    hin_r = lpr * hr - lpi * hi + pr
    hin_i = lpr * hi + lpi * hr + pi
    e = S5_TILE - 1 if fwd else 0
    er = jnp.broadcast_to(ar[e:e + 1, :], ar.shape)
    ei = jnp.broadcast_to(ai[e:e + 1, :], ai.shape)
    return hin_r, hin_i, l8r * hr - l8i * hi + er, l8r * hi + l8i * hr + ei


def _s5_kernel(xc_ref, xl_ref, kt_ref, ws_ref, cp_ref, lam_ref, y_ref, s_ref, *, nb):
    npair = ws_ref.shape[0]
    jc = xc_ref.shape[1] // nb
    jl = xl_ref.shape[1] // nb
    jt = jc + jl
    for pp in range(npair):
        for src, j0, nj in ((xc_ref, 0, jc), (xl_ref, jc, jl)):
            s = jnp.dot(src[2 * pp], ws_ref[pp, 0:256, :], preferred_element_type=F32)
            s = s + jnp.dot(src[2 * pp + 1], ws_ref[pp, 256:512, :], preferred_element_type=F32)
            for bi in range(nb):
                s_ref[pp, bi * jt + j0:bi * jt + j0 + nj, :] = s[bi * nj:(bi + 1) * nj, :]

    nct = jc // S5_TILE
    nt = jt // S5_TILE
    zero = jnp.zeros((S5_TILE, LANES), F32)
    for pp in range(npair):

        def prefix(it, carry, pp=pp):
            r0 = pl.multiple_of(it * S5_TILE, S5_TILE)
            for c0, fwd in ((0, True), (256, False)):
                cst = [lam_ref[pp, (0 if fwd else 10) + k] for k in range(6)]
                ar, ai = _s5_tile_prefix(s_ref[pp, pl.ds(r0, S5_TILE), c0:c0 + 128],
                                         s_ref[pp, pl.ds(r0, S5_TILE), c0 + 128:c0 + 256], cst, fwd)
                s_ref[pp, pl.ds(r0, S5_TILE), c0:c0 + 128] = ar
                s_ref[pp, pl.ds(r0, S5_TILE), c0 + 128:c0 + 256] = ai
            return carry

        lax.fori_loop(0, nb * nt, prefix, 0, unroll=8)

        def step(it, carry, pp=pp):
            mb = jnp.where(it < nct, nct - 1 - it, nt - 1 - (it - nct))
            cf = [lam_ref[pp, k] for k in range(10)]
            cb = [lam_ref[pp, 10 + k] for k in range(10)]
            out = []
            for bi in range(nb):
                hfr, hfi, hbr, hbi = carry[bi]
                rf = pl.multiple_of(bi * jt + it * S5_TILE, S5_TILE)
                rb = pl.multiple_of(bi * jt + mb * S5_TILE, S5_TILE)
                fr, fi, hfr, hfi = _s5_tile_carry(s_ref[pp, pl.ds(rf, S5_TILE), 0:128],
                                                  s_ref[pp, pl.ds(rf, S5_TILE), 128:256], hfr, hfi, cf, True)
                s_ref[pp, pl.ds(rf, S5_TILE), 0:128] = fr
                s_ref[pp, pl.ds(rf, S5_TILE), 128:256] = fi
                br, bim, hbr, hbi = _s5_tile_carry(s_ref[pp, pl.ds(rb, S5_TILE), 256:384],
                                                   s_ref[pp, pl.ds(rb, S5_TILE), 384:512], hbr, hbi, cb, False)
                s_ref[pp, pl.ds(rb, S5_TILE), 256:384] = br
                s_ref[pp, pl.ds(rb, S5_TILE), 384:512] = bim
                out.append((hfr, hfi, hbr, hbi))
            return tuple(out)

        lax.fori_loop(0, nt, step, tuple((zero, zero, zero, zero) for _ in range(nb)))

    tc = S5_CHUNK * S5_GROUP
    width = kt_ref.shape[2]
    for pp in range(npair):
        toeps = []
        for gi in range(2):
            kt = kt_ref[2 * pp + gi]
            blocks = []
            for s in range(S5_CHUNK):
                sh = (width - S5_GROUP * (S5_CHUNK - 1 - s)) % width
                blocks.append((kt if sh == 0 else pltpu.roll(kt, sh, 1))[:, 0:tc])
            toeps.append(jnp.concatenate(blocks, axis=0).astype(BF16))
        for bi in range(nb):
            hin = s_ref[pp, bi * jt + jc:(bi + 1) * jt, :].astype(BF16)
            yp = jnp.dot(hin, cp_ref[pp], preferred_element_type=F32)
            for gi in range(2):
                g = 2 * pp + gi
                y = yp[:, gi * 256:(gi + 1) * 256] + jnp.dot(xl_ref[g, bi * jl:(bi + 1) * jl, :], toeps[gi],
                                                             preferred_element_type=F32)
                y_ref[g, bi * jl:(bi + 1) * jl, :] = y.astype(BF16)


def _s5(xc, xl, kt, ws, cp, lamc, gps, nb):
    ng, rc, _ = xc.shape
    rl = xl.shape[1]
    npair = gps // 2
    return pl.pallas_call(
        functools.partial(_s5_kernel, nb=nb),
        grid=(ng // gps,),
        in_specs=[pl.BlockSpec((gps, rc, 256), lambda i: (i, 0, 0)),
                  pl.BlockSpec((gps, rl, 256), lambda i: (i, 0, 0)),
                  pl.BlockSpec((gps,) + kt.shape[1:], lambda i: (i, 0, 0)),
                  pl.BlockSpec((npair, 512, 512), lambda i: (i, 0, 0)),
                  pl.BlockSpec((npair, 512, 512), lambda i: (i, 0, 0)),
                  pl.BlockSpec((npair, 20, S5_TILE, LANES), lambda i: (i, 0, 0, 0))],
        out_specs=pl.BlockSpec((gps, rl, 256), lambda i: (i, 0, 0)),
        out_shape=jax.ShapeDtypeStruct((ng, rl, 256), BF16),
        scratch_shapes=[pltpu.VMEM((npair, rc + rl, 512), F32)],
        compiler_params=_params("arbitrary"),
    )(xc, xl, kt, ws, cp, lamc)


def _cmul(ar, ai, br, bi):
    return ar * br - ai * bi, ar * bi + ai * br


def _s5_weights(lam_re, lam_im, log_dt, b_re, b_im, c_re, c_im):
    nd, ng, p = lam_re.shape
    t = S5_CHUNK
    tc = t * S5_GROUP
    lr = jnp.minimum(lam_re.astype(F32), -1e-4)
    li = lam_im.astype(F32)
    dt = jnp.exp(log_dt.astype(F32))[..., None]
    mag = jnp.exp(lr * dt)
    lbr, lbi = mag * jnp.cos(li * dt), mag * jnp.sin(li * dt)
    den = lr * lr + li * li
    fr = ((lbr - 1.0) * lr + lbi * li) / den
    fi = (lbi * lr - (lbr - 1.0) * li) / den
    bbr, bbi = _cmul(fr[..., None], fi[..., None], b_re.astype(F32), b_im.astype(F32))
    cr, ci = c_re.astype(F32), c_im.astype(F32)

    pr, pi = [jnp.ones_like(lbr)], [jnp.zeros_like(lbr)]
    for _ in range(t):
        nr, ni = _cmul(pr[-1], pi[-1], lbr, lbi)
        pr.append(nr)
        pi.append(ni)
    pr, pi = jnp.stack(pr), jnp.stack(pi)
    wr, wi = _cmul(pr[:t, ..., None], pi[:t, ..., None], bbr[None], bbi[None])

    kern = jnp.einsum("dgxp,kdgpc->dgkxc", cr, wr) - jnp.einsum("dgxp,kdgpc->dgkxc", ci, wi)
    ktf = kern[0].transpose(0, 3, 1, 2)
    ktb = kern[1][:, ::-1].transpose(0, 3, 1, 2)
    kt = jnp.concatenate([ktb[:, :, :t - 1], ktf[:, :, 0:1] + ktb[:, :, t - 1:t], ktf[:, :, 1:]], axis=2)
    kt = kt.reshape(ng, S5_GROUP, (2 * t - 1) * S5_GROUP)
    kt = jnp.pad(kt, ((0, 0), (0, 0), (0, 2 * tc - kt.shape[-1])))

    parity = (jnp.arange(ng) % 2)[:, None, None]

    gscp = lambda a: a.transpose(1, 0, 3, 2).reshape(ng, tc, p)
    parts = [gscp(wr[::-1, 0]), gscp(wi[::-1, 0]), gscp(wr[:, 1]), gscp(wi[:, 1])]
    ws = jnp.concatenate([a for a in parts for _ in range(2)], axis=-1)
    own = (jnp.arange(8 * p) // p) % 2 == parity
    ws = jnp.where(own, ws, 0.0).reshape(ng // 2, 2 * tc, 8 * p)

    def readout(d, powr, powi):
        mr, mi = _cmul(cr[d][None], ci[d][None], powr[:, :, None, :], powi[:, :, None, :])
        to_gptx = lambda a: a.transpose(1, 3, 0, 2).reshape(ng, p, tc)
        return [to_gptx(mr), to_gptx(-mi)]

    parts = readout(0, pr[1:t + 1, 0], pi[1:t + 1, 0]) + readout(1, pr[1:t + 1, 1][::-1], pi[1:t + 1, 1][::-1])
    own = jnp.arange(2 * tc) // tc == parity
    cp = jnp.stack([jnp.where(own, jnp.concatenate([a, a], axis=-1), 0.0) for a in parts], axis=1)
    cp = cp.reshape(ng // 2, 2, 4, p, 2 * tc).transpose(0, 2, 1, 3, 4).reshape(ng // 2, 8 * p, 2 * tc)

    row = jnp.arange(S5_TILE)
    l1 = (pr[t], pi[t])
    l2 = _cmul(*l1, *l1)
    l4 = _cmul(*l2, *l2)
    l8 = _cmul(*l4, *l4)
    rp = [(jnp.ones_like(lbr), jnp.zeros_like(lbr))]
    for _ in range(S5_TILE - 1):
        rp.append(_cmul(*rp[-1], *l1))
    planes = []
    for d in range(nd):
        valid = (lambda dist: row >= dist) if d == 0 else (lambda dist: row <= S5_TILE - 1 - dist)
        for (qr, qi), dist in ((l1, 1), (l2, 2), (l4, 4)):
            m = valid(dist).astype(F32)[:, None, None]
            planes += [m * qr[d][None], m * qi[d][None]]
        order = row if d == 0 else row[::-1]
        planes += [jnp.stack([rp[k][0][d] for k in range(S5_TILE)])[order],
                   jnp.stack([rp[k][1][d] for k in range(S5_TILE)])[order]]
        planes += [jnp.broadcast_to(l8[0][d], (S5_TILE, ng, p)), jnp.broadcast_to(l8[1][d], (S5_TILE, ng, p))]
    lamc = jnp.stack(planes)
    lamc = lamc.reshape(20, S5_TILE, ng // 2, 2 * p).transpose(2, 0, 1, 3)
    return kt, ws.astype(BF16), cp.astype(BF16), lamc


def _chunk_cumsum(g, rev):
    sub = 8
    nt = g.shape[0] // sub
    row = lax.broadcasted_iota(jnp.int32, (sub, g.shape[1]), 0)
    out = [None] * nt
    off = None
    for kk in (range(nt - 1, -1, -1) if rev else range(nt)):
        x = g[kk * sub:(kk + 1) * sub, :]
        for dist in (1, 2, 4):
            if rev:
                x = x + jnp.where(row < sub - dist, pltpu.roll(x, sub - dist, 0), 0.0)
            else:
                x = x + jnp.where(row >= dist, pltpu.roll(x, dist, 0), 0.0)
        if off is not None:
            x = x + off
        out[kk] = x
        e = 0 if rev else sub - 1
        off = jnp.broadcast_to(x[e:e + 1, :], x.shape)
    return jnp.concatenate(out, axis=0)


def _gla_log_decay(glr, gup, gbias, rev):
    c = GLA_CHUNK
    z = jnp.dot(glr.astype(BF16), gup, preferred_element_type=F32) + gbias
    g = (jnp.minimum(z, 0.0) - jnp.log(1.0 + jnp.exp(-jnp.abs(z)))) * (1.0 / GLA_GATE_NORM)
    return [_chunk_cumsum(g[n * c:(n + 1) * c, :], rev) for n in range(g.shape[0] // c)]


def _gla_chunk(q, k, v, gc, st, rev, need_out):
    c = GLA_CHUNK
    i_ref = c // 2 - 1 if rev else c // 2
    i_last = 0 if rev else c - 1
    g_ref = gc[i_ref:i_ref + 1, :]
    g_last = gc[i_last:i_last + 1, :]
    lane_head = lax.broadcasted_iota(jnp.int32, (1, GLA_KEY), 1) // GLA_DK

    stack = lambda a: jnp.concatenate(
        [jnp.where(lane_head == h, a, 0.0) for h in range(GLA_HEADS)], axis=0).astype(BF16)
    dv2 = 2 * GLA_DV
    vt = jnp.concatenate(
        [jnp.concatenate([v[:, p * dv2:p * dv2 + GLA_DV], v[:, p * dv2 + GLA_DV:(p + 1) * dv2]], axis=0).T
         for p in range(GLA_HEADS // 2)], axis=1).astype(BF16)
    kl = k * jnp.exp(g_last - gc)
    kv = jnp.dot(vt, stack(kl), preferred_element_type=F32)
    st_new = st * jnp.exp(g_last) + kv
    if not need_out:
        return None, st_new

    qe = q * jnp.exp(gc - g_ref)
    ke = (k * jnp.exp(g_ref - gc)).astype(BF16)
    qg = q * jnp.exp(gc)
    nt_dims = (((1,), (1,)), ((), ()))
    sc = lax.dot_general(stack(qe), ke, nt_dims, preferred_element_type=F32)
    rs = lax.broadcasted_iota(jnp.int32, (GLA_HEADS * c, c), 0) % c
    cs = lax.broadcasted_iota(jnp.int32, (GLA_HEADS * c, c), 1)
    keep = (rs <= cs) if rev else (rs >= cs)
    sc = jnp.where(keep, sc, 0.0).astype(BF16)
    vb = v.astype(BF16)
    oo = lax.dot_general(stack(qg), st.astype(BF16), nt_dims, preferred_element_type=F32)
    o = jnp.concatenate(
        [jnp.dot(sc[h * c:(h + 1) * c, :], vb[:, h * GLA_DV:(h + 1) * GLA_DV], preferred_element_type=F32)
         + oo[h * c:(h + 1) * c, :] for h in range(GLA_HEADS)], axis=1)
    return o, st_new


def _gla_kernel(cf_ref, cb_ref, cc_ref, gup_ref, gbias_ref, of_ref, ob_ref, stf_ref, stb_ref):
    c = GLA_CHUNK
    q0, k0, v0, r0 = 0, GLA_KEY, 2 * GLA_KEY, 2 * GLA_KEY + GLA_VAL

    def decay(ref, d):
        return _gla_log_decay(ref[:, r0:CM_W], gup_ref[d], gbias_ref[d], d == 1)

    def run(ref, n, gcs, d, st, need_out):
        rows = slice(n * c, (n + 1) * c)
        q = ref[rows, q0:k0] if need_out else None
        return _gla_chunk(q, ref[rows, k0:v0], ref[rows, v0:r0], gcs[n], st, d == 1, need_out)

    @pl.when(pl.program_id(1) == 0)
    def _():
        nctx = cc_ref.shape[0] // c
        gf, gb = decay(cc_ref, 0), decay(cc_ref, 1)
        stf = jnp.zeros(stf_ref.shape, F32)
        stb = jnp.zeros(stb_ref.shape, F32)
        for n in range(nctx):
            _, stf = run(cc_ref, n, gf, 0, stf, False)
            _, stb = run(cc_ref, nctx - 1 - n, gb, 1, stb, False)
        stf_ref[...] = stf
        stb_ref[...] = stb

    ncols, rows_per_col, _ = cf_ref.shape
    nch = rows_per_col // c
    stf = stf_ref[...]
    stb = stb_ref[...]
    for j in range(ncols):
        jb = ncols - 1 - j
        gf, gb = decay(cf_ref.at[j], 0), decay(cb_ref.at[jb], 1)
        for n in range(nch):
            o, stf = run(cf_ref.at[j], n, gf, 0, stf, True)
            of_ref[j, n * c:(n + 1) * c, :] = o
            m = nch - 1 - n
            o, stb = run(cb_ref.at[jb], m, gb, 1, stb, True)
            ob_ref[jb, m * c:(m + 1) * c, :] = o
    stf_ref[...] = stf
    stb_ref[...] = stb


def _gla(cm, cmc, gup, gbias):
    b, ncol, rows, _ = cm.shape
    lc = cmc.shape[1]
    cps = GLA_COLS_PER_STEP
    last = ncol // cps - 1
    col = lambda width, fn: pl.BlockSpec((None, cps, rows, width), fn)
    out_shape = jax.ShapeDtypeStruct((b, ncol, rows, GLA_VAL), F32)
    return pl.pallas_call(
        _gla_kernel,
        grid=(b, ncol // cps),
        in_specs=[col(CM_W, lambda bi, ci: (bi, ci, 0, 0)),
                  col(CM_W, lambda bi, ci: (bi, last - ci, 0, 0)),
                  pl.BlockSpec((None, lc, CM_W), lambda bi, ci: (bi, 0, 0)),
                  _const_spec(gup.shape), _const_spec(gbias.shape)],
        out_specs=[col(GLA_VAL, lambda bi, ci: (bi, ci, 0, 0)),
                   col(GLA_VAL, lambda bi, ci: (bi, last - ci, 0, 0))],
        out_shape=[out_shape, out_shape],
        scratch_shapes=[pltpu.VMEM((GLA_DV, GLA_KEY), F32), pltpu.VMEM((GLA_DV, GLA_KEY), F32)],
        compiler_params=_params("arbitrary", "arbitrary"),
    )(cm, cm, cmc, gup, gbias)


def _back_kernel(x1_ref, nat_ref, y_ref, of_ref, ob_ref, mod_ref, p16t_ref, pcolt_ref, dskip_ref, gluw_ref,
                 glub_ref, s5out_ref, gnorm_ref, glaout_ref, wo_ref, n3_ref, wg_ref, wu_ref, wd_ref, fin_ref,
                 out_ref, *, fchunk):
    m = mod_ref[...]
    tm, d = x1_ref.shape
    ng, nj, _ = y_ref.shape
    s5w = ng * S5_GROUP

    yp = _from_chunk_major(lambda g, tt: y_ref[g, :, tt * LANES:(tt + 1) * LANES].astype(F32), ng, nj)
    ys_all = jnp.dot(p16t_ref[...], yp.astype(BF16), preferred_element_type=F32)
    ocm = (of_ref[...] + ob_ref[...]).reshape(tm, GLA_VAL).astype(BF16)
    o_all = jnp.dot(pcolt_ref[...], ocm, preferred_element_type=F32)

    for r0 in range(0, tm, tm // BACK_ROW_SPLIT):
        rows = slice(r0, r0 + tm // BACK_ROW_SPLIT)
        ya = ys_all[rows, :] + dskip_ref[...] * nat_ref[rows, 0:s5w].astype(F32)
        ya = 0.5 * ya * (1.0 + jnp.tanh(0.7978845608028654 * (ya + 0.044715 * (ya * ya * ya))))
        gl = jnp.dot(ya.astype(BF16), gluw_ref[...], preferred_element_type=F32) + glub_ref[...]
        ya = ya * jax.nn.sigmoid(gl)

        heads = []
        for h in range(GLA_HEADS):
            oh = o_all[rows, h * GLA_DV:(h + 1) * GLA_DV]
            heads.append(oh * lax.rsqrt(jnp.mean(oh * oh, axis=-1, keepdims=True) + RMS_EPS))
        r = nat_ref[rows, s5w:s5w + GLA_VAL].astype(F32)
        yb = jnp.concatenate(heads, axis=1) * gnorm_ref[...] * (r * jax.nn.sigmoid(r))

        pa = jnp.dot(ya.astype(BF16), s5out_ref[...], preferred_element_type=F32)
        pb = jnp.dot(yb.astype(BF16), glaout_ref[...], preferred_element_type=F32)
        ga = nat_ref[rows, s5w + GLA_VAL:s5w + GLA_VAL + d].astype(F32)
        gb = nat_ref[rows, s5w + GLA_VAL + d:s5w + GLA_VAL + 2 * d].astype(F32)
        mg = jax.nn.sigmoid(ga) * pa + jax.nn.sigmoid(gb) * pb
        y = jnp.dot(mg.astype(BF16), wo_ref[...], preferred_element_type=F32)
        x2 = x1_ref[rows, :] + m[5:6] * y
        h = _rms_mod(x2, n3_ref[...], m[6:7], m[7:8]).astype(BF16)
        x3 = _swiglu_residual(x2, h, m[8:9], wg_ref, wu_ref, wd_ref, fchunk)
        ms = jnp.mean(x3 * x3, axis=-1, keepdims=True)
        out_ref[rows, :] = x3 * lax.rsqrt(ms + RMS_EPS) * fin_ref[...]


def _back(x1, nat, ys5, of, ob, mods3, w, tm, fchunk):
    b, l, d = x1.shape
    f = w["wg2"].shape[1]
    s5w = w["dskip"].shape[-1]
    ng = s5w // S5_GROUP
    nblk = l // tm
    tok = lambda width: pl.BlockSpec((None, tm, width), lambda bi, i: (bi, i, 0))
    colblk = pl.BlockSpec((None, GRID_W, tm // GRID_W, GLA_VAL), lambda bi, i: (bi, 0, i, 0))
    return pl.pallas_call(
        functools.partial(_back_kernel, fchunk=fchunk),
        grid=(b, nblk),
        in_specs=[tok(d), tok(NAT_W),
                  pl.BlockSpec((ng, tm // S5_CHUNK, S5_CHUNK * S5_GROUP), lambda bi, i: (0, bi * nblk + i, 0)),
                  colblk, colblk,
                  pl.BlockSpec((None, N_MOD, d), lambda bi, i: (bi, 0, 0)),
                  _const_spec((tm, tm)), _const_spec((tm, tm)),
                  _const_spec((1, s5w)), _const_spec((s5w, s5w)), _const_spec((1, s5w)),
                  _const_spec((s5w, d)), _const_spec((1, GLA_VAL)), _const_spec((GLA_VAL, d)),
                  _const_spec((d, d)), _const_spec((1, d)), _const_spec((d, f)), _const_spec((d, f)),
                  _const_spec((f, d)), _const_spec((1, d))],
        out_specs=tok(d),
        out_shape=jax.ShapeDtypeStruct((b, l, d), F32),
        compiler_params=_params("arbitrary", "arbitrary"),
    )(x1, nat, ys5, of, ob, mods3, jnp.asarray(_chunk_perm(tm).T, BF16), jnp.asarray(_column_perm(tm).T, BF16),
      w["dskip"], w["gluw"], w["glub"], w["s5out"], w["gnorm"], w["glaout"], w["wo"], w["n3"], w["wg2"],
      w["wu2"], w["wd2"], w["fin"])


def kernel(x, c, ctx, c_ctx, ada_w, ada_b, ffn1_norm, ffn1_w_gate, ffn1_w_up, ffn1_w_down, mix_norm, w_in,
           s5_lambda_re, s5_lambda_im, s5_log_dt, s5_b_re, s5_b_im, s5_c_re, s5_c_im, s5_d, s5_glu_w, s5_glu_b,
           s5_out, gla_gate_up, gla_gate_b, gla_norm, gla_out, w_o, ffn2_norm, ffn2_w_gate, ffn2_w_up,
           ffn2_w_down, final_norm):
    b, l, d = x.shape
    lc = ctx.shape[1]
    assert ada_w.shape[0] == 1 and b + 1 <= 8
    s5w = s5_d.shape[-1]
    fchunk = 256
    tm = min(512, l)
    assert tm % (GRID_W * 8) == 0 and l % tm == 0 and lc % (S5_CHUNK * S5_TILE) == 0 and lc % GLA_CHUNK == 0

    cvec = jnp.concatenate([c, c_ctx[None, :], jnp.zeros((8 - b - 1, d), F32)], axis=0)
    mods3 = _ada(cvec, ada_w[0], ada_b[0]).reshape(8, N_MOD, d)

    wi = w_in[0]
    o_q, o_k, o_v, o_r = s5w, s5w + GLA_KEY, s5w + 2 * GLA_KEY, s5w + 2 * GLA_KEY + GLA_VAL
    o_glr = o_r + GLA_VAL
    o_ga = o_glr + 2 * GLA_GATE_RANK
    wnat = jnp.concatenate([wi[:, :o_q], wi[:, o_r:o_glr], wi[:, o_ga:]], axis=1)
    wcm = jnp.concatenate([wi[:, o_q:o_k] * (GLA_DK ** -0.5), wi[:, o_k:o_r],
                           jnp.pad(wi[:, o_glr:o_ga], ((0, 0), (0, LANES - 2 * GLA_GATE_RANK)))], axis=1)
    gup = jnp.zeros((2, LANES, GLA_KEY), F32)
    gup = gup.at[0, 0:GLA_GATE_RANK].set(gla_gate_up[0, 0])
    gup = gup.at[1, GLA_GATE_RANK:2 * GLA_GATE_RANK].set(gla_gate_up[0, 1])
    row = lambda v: v.reshape(1, -1).astype(F32)
    w = dict(
        n1=row(ffn1_norm[0]), wg1=ffn1_w_gate[0].astype(BF16), wu1=ffn1_w_up[0].astype(BF16),
        wd1=ffn1_w_down[0].astype(BF16), n2=row(mix_norm[0]), wnat=wnat.astype(BF16), wcm=wcm.astype(BF16),
        dskip=row(s5_d[0]), gluw=s5_glu_w[0].astype(BF16), glub=row(s5_glu_b[0]), s5out=s5_out[0].astype(BF16),
        gnorm=row(gla_norm[0]), glaout=gla_out[0].astype(BF16), wo=w_o[0].astype(BF16),
        n3=row(ffn2_norm[0]), wg2=ffn2_w_gate[0].astype(BF16), wu2=ffn2_w_up[0].astype(BF16),
        wd2=ffn2_w_down[0].astype(BF16), fin=row(final_norm))

    lat_row = lambda bi: bi
    ctx_row = lambda bi: b
    x1 = _ffn(x, mods3, lat_row, w["n1"], w["wg1"], w["wu1"], w["wd1"], tm, fchunk, 0)
    c1 = _ffn(ctx, mods3, ctx_row, w["n1"], w["wg1"], w["wu1"], w["wd1"], lc, fchunk, 0)
    nat, xl, cm = _proj(x1, mods3, lat_row, w, tm, True)
    _, xc, cmc = _proj(c1, mods3, ctx_row, w, lc, False)

    kt, ws, cp, lamc = _s5_weights(s5_lambda_re[0], s5_lambda_im[0], s5_log_dt[0], s5_b_re[0], s5_b_im[0],
                                     s5_c_re[0], s5_c_im[0])
    ys5 = _s5(xc, xl, kt, ws, cp, lamc, gps=4, nb=b)

    of, ob = _gla(cm, cmc, gup.astype(BF16), gla_gate_b[0].reshape(2, 1, GLA_KEY).astype(F32))
    return _back(x1, nat, ys5, of, ob, mods3, w, tm, fchunk)
```

```python
import functools

import jax
import jax.numpy as jnp
from jax import lax
from jax.experimental import pallas as pl
from jax.experimental.pallas import tpu as pltpu

F32 = jnp.float32
BF16 = jnp.bfloat16

RMS_EPS = 1e-6
MACARON_WEIGHT = 0.5
GRID_W = 64
N_MOD = 9
S5_GROUP = 16
S5_STATE = 64
S5_CHUNK = 16
S5_TILE = 8
GLA_HEADS = 4
GLA_DK = 64
GLA_DV = 128
GLA_CHUNK = 64
GLA_GATE_RANK = 16
GLA_GATE_NORM = 16.0
GLA_COLS_PER_STEP = 4
BACK_ROW_SPLIT = 1
GLA_KEY = GLA_HEADS * GLA_DK
GLA_VAL = GLA_HEADS * GLA_DV
LANES = 128
NAT_W = 3072
CM_W = 2 * GLA_KEY + GLA_VAL + LANES
V7X_VMEM_LIMIT_BYTES = 56 * 1024 * 1024


def _params(*sem):
    return pltpu.CompilerParams(dimension_semantics=sem, vmem_limit_bytes=V7X_VMEM_LIMIT_BYTES)


def _const_spec(shape):
    nd = len(shape)
    return pl.BlockSpec(shape, lambda *_: (0,) * nd, pipeline_mode=pl.Buffered(1))


def _rms_mod(x, g, shift, scale):
    ms = jnp.mean(x * x, axis=-1, keepdims=True)
    return (x * lax.rsqrt(ms + RMS_EPS) * g) * (1.0 + scale) + shift


def _swiglu_residual(x, h, gate, wg_ref, wu_ref, wd_ref, fchunk):
    acc = None
    for f0 in range(0, wg_ref.shape[1], fchunk):
        gg = jnp.dot(h, wg_ref[:, f0:f0 + fchunk], preferred_element_type=F32)
        uu = jnp.dot(h, wu_ref[:, f0:f0 + fchunk], preferred_element_type=F32)
        a = (gg * jax.nn.sigmoid(gg) * uu).astype(BF16)
        o = jnp.dot(a, wd_ref[f0:f0 + fchunk, :], preferred_element_type=F32)
        acc = o if acc is None else acc + o
    return x + gate * (MACARON_WEIGHT * acc)


def _ada_kernel(c_ref, w_ref, b_ref, o_ref):
    cv = c_ref[...]
    s = cv * jax.nn.sigmoid(cv)
    o_ref[...] = jnp.dot(s, w_ref[...], preferred_element_type=F32,
                         precision=lax.Precision.HIGHEST) + b_ref[...]


def _ada(cvec, ada_w, ada_b):
    rows, d = cvec.shape
    n = ada_w.shape[1]
    bn = n // 8
    return pl.pallas_call(
        _ada_kernel,
        grid=(n // bn,),
        in_specs=[pl.BlockSpec((rows, d), lambda j: (0, 0)),
                  pl.BlockSpec((d, bn), lambda j: (0, j)),
                  pl.BlockSpec((1, bn), lambda j: (0, j))],
        out_specs=pl.BlockSpec((rows, bn), lambda j: (0, j)),
        out_shape=jax.ShapeDtypeStruct((rows, n), F32),
        compiler_params=_params("arbitrary"),
    )(cvec, ada_w, ada_b.reshape(1, n))


def _ffn_kernel(x_ref, mod_ref, n_ref, wg_ref, wu_ref, wd_ref, o_ref, *, fchunk, mod0):
    x = x_ref[...]
    m = mod_ref[...]
    h = _rms_mod(x, n_ref[...], m[mod0:mod0 + 1], m[mod0 + 1:mod0 + 2]).astype(BF16)
    o_ref[...] = _swiglu_residual(x, h, m[mod0 + 2:mod0 + 3], wg_ref, wu_ref, wd_ref, fchunk)


def _ffn(x, mods3, mod_row, norm, wg, wu, wd, tm, fchunk, mod0):
    b, l, d = x.shape
    f = wg.shape[1]
    tok = pl.BlockSpec((None, tm, d), lambda bi, i: (bi, i, 0))
    return pl.pallas_call(
        functools.partial(_ffn_kernel, fchunk=fchunk, mod0=mod0),
        grid=(b, l // tm),
        in_specs=[tok, pl.BlockSpec((None, N_MOD, d), lambda bi, i: (mod_row(bi), 0, 0)),
                  _const_spec((1, d)), _const_spec((d, f)), _const_spec((d, f)), _const_spec((f, d))],
        out_specs=tok,
        out_shape=jax.ShapeDtypeStruct((b, l, d), F32),
        compiler_params=_params("arbitrary", "arbitrary"),
    )(x, mods3, norm, wg, wu, wd)


def _piece_masks(rows):
    piece = lax.broadcasted_iota(jnp.int32, (rows, LANES), 1) // S5_GROUP
    return [piece == p for p in range(LANES // S5_GROUP)]


def _to_chunk_major(up, store):
    nj = up.shape[0] // S5_CHUNK
    npc = LANES // S5_GROUP
    masks = _piece_masks(nj)
    for gg in range(up.shape[1] // LANES):
        for tt in range(S5_CHUNK // npc):
            src = [up[(npc * tt + p) * nj:(npc * tt + p + 1) * nj, gg * LANES:(gg + 1) * LANES] for p in range(npc)]
            for gl in range(npc):
                acc = None
                for p in range(npc):
                    sh = ((p - gl) * S5_GROUP) % LANES
                    r = src[p] if sh == 0 else pltpu.roll(src[p], sh, 1)
                    acc = r if acc is None else jnp.where(masks[p], r, acc)
                store(gg * npc + gl, tt, acc)


def _from_chunk_major(load, ng, nj):
    npc = LANES // S5_GROUP
    masks = _piece_masks(nj)
    row_blocks = []
    for tt in range(S5_CHUNK // npc):
        per_p = [[] for _ in range(npc)]
        for gg in range(ng // npc):
            src = [load(gg * npc + gl, tt) for gl in range(npc)]
            for p in range(npc):
                acc = None
                for gl in range(npc):
                    sh = ((gl - p) * S5_GROUP) % LANES
                    r = src[gl] if sh == 0 else pltpu.roll(src[gl], sh, 1)
                    acc = r if acc is None else jnp.where(masks[gl], r, acc)
                per_p[p].append(acc)
        row_blocks += [jnp.concatenate(blk, axis=1) for blk in per_p]
    return jnp.concatenate(row_blocks, axis=0)


def _proj_kernel(x_ref, mod_ref, n_ref, wnat_ref, wcm_ref, nat_ref, xs5_ref, cm_ref, *, col_major):
    x = x_ref[...]
    m = mod_ref[...]
    tm = x.shape[0]
    h2 = _rms_mod(x, n_ref[...], m[3:4], m[4:5]).astype(BF16)
    u_bf = None
    for c0 in range(0, NAT_W, 1024):
        p = jnp.dot(h2, wnat_ref[:, c0:c0 + 1024], preferred_element_type=F32).astype(BF16)
        nat_ref[:, c0:c0 + 1024] = p
        if c0 == 0:
            u_bf = p[:, 0:xs5_ref.shape[0] * S5_GROUP]
    pc = jnp.dot(h2, wcm_ref[...], preferred_element_type=F32).astype(BF16)
    if col_major:
        pcm = pc.astype(F32).reshape(tm // GRID_W, GRID_W, CM_W)
        cm_ref[...] = jnp.swapaxes(pcm, 0, 1)
    else:
        cm_ref[...] = pc.astype(F32)
    uf = u_bf.astype(F32).reshape(tm // S5_CHUNK, S5_CHUNK, u_bf.shape[1])
    up = jnp.swapaxes(uf, 0, 1).reshape(tm, u_bf.shape[1])

    def store(g, tt, v):
        xs5_ref[g, :, tt * LANES:(tt + 1) * LANES] = v.astype(BF16)

    _to_chunk_major(up, store)


def _proj(x1, mods3, mod_row, w, tm, col_major):
    b, l, d = x1.shape
    ng = w["dskip"].shape[-1] // S5_GROUP
    nj = tm // S5_CHUNK
    nblk = l // tm
    tok = lambda width: pl.BlockSpec((None, tm, width), lambda bi, i: (bi, i, 0))
    if col_major:
        cm_spec = pl.BlockSpec((None, GRID_W, tm // GRID_W, CM_W), lambda bi, i: (bi, 0, i, 0))
        cm_shape = jax.ShapeDtypeStruct((b, GRID_W, l // GRID_W, CM_W), F32)
    else:
        cm_spec = tok(CM_W)
        cm_shape = jax.ShapeDtypeStruct((b, l, CM_W), F32)
    return pl.pallas_call(
        functools.partial(_proj_kernel, col_major=col_major),
        grid=(b, nblk),
        in_specs=[tok(d), pl.BlockSpec((None, N_MOD, d), lambda bi, i: (mod_row(bi), 0, 0)),
                  _const_spec((1, d)), _const_spec((d, NAT_W)), _const_spec((d, CM_W))],
        out_specs=[tok(NAT_W),
                   pl.BlockSpec((ng, nj, S5_CHUNK * S5_GROUP), lambda bi, i: (0, bi * nblk + i, 0)),
                   cm_spec],
        out_shape=[jax.ShapeDtypeStruct((b, l, NAT_W), BF16),
                   jax.ShapeDtypeStruct((ng, b * (l // S5_CHUNK), S5_CHUNK * S5_GROUP), BF16),
                   cm_shape],
        compiler_params=_params("arbitrary", "arbitrary"),
    )(x1, mods3, w["n2"], w["wnat"], w["wcm"])


def _s5_tile_prefix(sr, si, cst, fwd):
    ar, ai = sr, si
    for lvl, dist in enumerate((1, 2, 4)):
        sh = dist if fwd else S5_TILE - dist
        rr, ri = pltpu.roll(ar, sh, 0), pltpu.roll(ai, sh, 0)
        lr, li = cst[2 * lvl], cst[2 * lvl + 1]
        ar, ai = ar + (lr * rr - li * ri), ai + (lr * ri + li * rr)
    return ar, ai


def _s5_tile_carry(ar, ai, hr, hi, cst, fwd):
    row = lax.broadcasted_iota(jnp.int32, ar.shape, 0)
    keep = (row >= 1) if fwd else (row <= S5_TILE - 2)
    sh = 1 if fwd else S5_TILE - 1
    pr = jnp.where(keep, pltpu.roll(ar, sh, 0), 0.0)
    pi = jnp.where(keep, pltpu.roll(ai, sh, 0), 0.0)
    lpr, lpi, l8r, l8i = cst[6:10]
    hin_r = lpr * hr - lpi * hi + pr
    hin_i = lpr * hi + lpi * hr + pi
    e = S5_TILE - 1 if fwd else 0
    er = jnp.broadcast_to(ar[e:e + 1, :], ar.shape)
    ei = jnp.broadcast_to(ai[e:e + 1, :], ai.shape)
    return hin_r, hin_i, l8r * hr - l8i * hi + er, l8r * hi + l8i * hr + ei


def _s5_kernel(xc_ref, xl_ref, kt_ref, ws_ref, cp_ref, lam_ref, y_ref, s_ref, *, nb):
    npair = ws_ref.shape[0]
    jc = xc_ref.shape[1] // nb
    jl = xl_ref.shape[1] // nb
    jt = jc + jl
    for pp in range(npair):
        for src, j0, nj in ((xc_ref, 0, jc), (xl_ref, jc, jl)):
            s = jnp.dot(src[2 * pp], ws_ref[pp, 0:256, :], preferred_element_type=F32)
            s = s + jnp.dot(src[2 * pp + 1], ws_ref[pp, 256:512, :], preferred_element_type=F32)
            for bi in range(nb):
                s_ref[pp, bi * jt + j0:bi * jt + j0 + nj, :] = s[bi * nj:(bi + 1) * nj, :]

    nct = jc // S5_TILE
    nt = jt // S5_TILE
    zero = jnp.zeros((S5_TILE, LANES), F32)
    for pp in range(npair):

        def prefix(it, carry, pp=pp):
            r0 = pl.multiple_of(it * S5_TILE, S5_TILE)
            for c0, fwd in ((0, True), (256, False)):
                cst = [lam_ref[pp, (0 if fwd else 10) + k] for k in range(6)]
                ar, ai = _s5_tile_prefix(s_ref[pp, pl.ds(r0, S5_TILE), c0:c0 + 128],
                                         s_ref[pp, pl.ds(r0, S5_TILE), c0 + 128:c0 + 256], cst, fwd)
                s_ref[pp, pl.ds(r0, S5_TILE), c0:c0 + 128] = ar
                s_ref[pp, pl.ds(r0, S5_TILE), c0 + 128:c0 + 256] = ai
            return carry

        lax.fori_loop(0, nb * nt, prefix, 0, unroll=8)

        def step(it, carry, pp=pp):
            mb = jnp.where(it < nct, nct - 1 - it, nt - 1 - (it - nct))
            cf = [lam_ref[pp, k] for k in range(10)]
            cb = [lam_ref[pp, 10 + k] for k in range(10)]
            out = []
            for bi in range(nb):
                hfr, hfi, hbr, hbi = carry[bi]
                rf = pl.multiple_of(bi * jt + it * S5_TILE, S5_TILE)
                rb = pl.multiple_of(bi * jt + mb * S5_TILE, S5_TILE)
                fr, fi, hfr, hfi = _s5_tile_carry(s_ref[pp, pl.ds(rf, S5_TILE), 0:128],
                                                  s_ref[pp, pl.ds(rf, S5_TILE), 128:256], hfr, hfi, cf, True)
                s_ref[pp, pl.ds(rf, S5_TILE), 0:128] = fr
                s_ref[pp, pl.ds(rf, S5_TILE), 128:256] = fi
                br, bim, hbr, hbi = _s5_tile_carry(s_ref[pp, pl.ds(rb, S5_TILE), 256:384],
                                                   s_ref[pp, pl.ds(rb, S5_TILE), 384:512], hbr, hbi, cb, False)
                s_ref[pp, pl.ds(rb, S5_TILE), 256:384] = br
                s_ref[pp, pl.ds(rb, S5_TILE), 384:512] = bim
                out.append((hfr, hfi, hbr, hbi))
            return tuple(out)

        lax.fori_loop(0, nt, step, tuple((zero, zero, zero, zero) for _ in range(nb)))

    tc = S5_CHUNK * S5_GROUP
    width = kt_ref.shape[2]
    for pp in range(npair):
        toeps = []
        for gi in range(2):
            kt = kt_ref[2 * pp + gi]
            blocks = []
            for s in range(S5_CHUNK):
                sh = (width - S5_GROUP * (S5_CHUNK - 1 - s)) % width
                blocks.append((kt if sh == 0 else pltpu.roll(kt, sh, 1))[:, 0:tc])
            toeps.append(jnp.concatenate(blocks, axis=0).astype(BF16))
        for bi in range(nb):
            hin = s_ref[pp, bi * jt + jc:(bi + 1) * jt, :].astype(BF16)
            yp = jnp.dot(hin, cp_ref[pp], preferred_element_type=F32)
            for gi in range(2):
                g = 2 * pp + gi
                y = yp[:, gi * 256:(gi + 1) * 256] + jnp.dot(xl_ref[g, bi * jl:(bi + 1) * jl, :], toeps[gi],
                                                             preferred_element_type=F32)
                y_ref[g, bi * jl:(bi + 1) * jl, :] = y.astype(BF16)


def _s5(xc, xl, kt, ws, cp, lamc, gps, nb):
    ng, rc, _ = xc.shape
    rl = xl.shape[1]
    npair = gps // 2
    return pl.pallas_call(
        functools.partial(_s5_kernel, nb=nb),
        grid=(ng // gps,),
        in_specs=[pl.BlockSpec((gps, rc, 256), lambda i: (i, 0, 0)),
                  pl.BlockSpec((gps, rl, 256), lambda i: (i, 0, 0)),
                  pl.BlockSpec((gps,) + kt.shape[1:], lambda i: (i, 0, 0)),
                  pl.BlockSpec((npair, 512, 512), lambda i: (i, 0, 0)),
                  pl.BlockSpec((npair, 512, 512), lambda i: (i, 0, 0)),
                  pl.BlockSpec((npair, 20, S5_TILE, LANES), lambda i: (i, 0, 0, 0))],
        out_specs=pl.BlockSpec((gps, rl, 256), lambda i: (i, 0, 0)),
        out_shape=jax.ShapeDtypeStruct((ng, rl, 256), BF16),
        scratch_shapes=[pltpu.VMEM((npair, rc + rl, 512), F32)],
        compiler_params=_params("arbitrary"),
    )(xc, xl, kt, ws, cp, lamc)


def _cmul(ar, ai, br, bi):
    return ar * br - ai * bi, ar * bi + ai * br


def _s5_weights(lam_re, lam_im, log_dt, b_re, b_im, c_re, c_im):
    nd, ng, p = lam_re.shape
    t = S5_CHUNK
    tc = t * S5_GROUP
    lr = jnp.minimum(lam_re.astype(F32), -1e-4)
    li = lam_im.astype(F32)
    dt = jnp.exp(log_dt.astype(F32))[..., None]
    mag = jnp.exp(lr * dt)
    lbr, lbi = mag * jnp.cos(li * dt), mag * jnp.sin(li * dt)
    den = lr * lr + li * li
    fr = ((lbr - 1.0) * lr + lbi * li) / den
    fi = (lbi * lr - (lbr - 1.0) * li) / den
    bbr, bbi = _cmul(fr[..., None], fi[..., None], b_re.astype(F32), b_im.astype(F32))
    cr, ci = c_re.astype(F32), c_im.astype(F32)

    pr, pi = [jnp.ones_like(lbr)], [jnp.zeros_like(lbr)]
    for _ in range(t):
        nr, ni = _cmul(pr[-1], pi[-1], lbr, lbi)
        pr.append(nr)
        pi.append(ni)
    pr, pi = jnp.stack(pr), jnp.stack(pi)
    wr, wi = _cmul(pr[:t, ..., None], pi[:t, ..., None], bbr[None], bbi[None])

    kern = jnp.einsum("dgxp,kdgpc->dgkxc", cr, wr) - jnp.einsum("dgxp,kdgpc->dgkxc", ci, wi)
    ktf = kern[0].transpose(0, 3, 1, 2)
    ktb = kern[1][:, ::-1].transpose(0, 3, 1, 2)
    kt = jnp.concatenate([ktb[:, :, :t - 1], ktf[:, :, 0:1] + ktb[:, :, t - 1:t], ktf[:, :, 1:]], axis=2)
    kt = kt.reshape(ng, S5_GROUP, (2 * t - 1) * S5_GROUP)
    kt = jnp.pad(kt, ((0, 0), (0, 0), (0, 2 * tc - kt.shape[-1])))

    parity = (jnp.arange(ng) % 2)[:, None, None]

    gscp = lambda a: a.transpose(1, 0, 3, 2).reshape(ng, tc, p)
    parts = [gscp(wr[::-1, 0]), gscp(wi[::-1, 0]), gscp(wr[:, 1]), gscp(wi[:, 1])]
    ws = jnp.concatenate([a for a in parts for _ in range(2)], axis=-1)
    own = (jnp.arange(8 * p) // p) % 2 == parity
    ws = jnp.where(own, ws, 0.0).reshape(ng // 2, 2 * tc, 8 * p)

    def readout(d, powr, powi):
        mr, mi = _cmul(cr[d][None], ci[d][None], powr[:, :, None, :], powi[:, :, None, :])
        to_gptx = lambda a: a.transpose(1, 3, 0, 2).reshape(ng, p, tc)
        return [to_gptx(mr), to_gptx(-mi)]

    parts = readout(0, pr[1:t + 1, 0], pi[1:t + 1, 0]) + readout(1, pr[1:t + 1, 1][::-1], pi[1:t + 1, 1][::-1])
    own = jnp.arange(2 * tc) // tc == parity
    cp = jnp.stack([jnp.where(own, jnp.concatenate([a, a], axis=-1), 0.0) for a in parts], axis=1)
    cp = cp.reshape(ng // 2, 2, 4, p, 2 * tc).transpose(0, 2, 1, 3, 4).reshape(ng // 2, 8 * p, 2 * tc)

    row = jnp.arange(S5_TILE)
    l1 = (pr[t], pi[t])
    l2 = _cmul(*l1, *l1)
    l4 = _cmul(*l2, *l2)
    l8 = _cmul(*l4, *l4)
    rp = [(jnp.ones_like(lbr), jnp.zeros_like(lbr))]
    for _ in range(S5_TILE - 1):
        rp.append(_cmul(*rp[-1], *l1))
    planes = []
    for d in range(nd):
        valid = (lambda dist: row >= dist) if d == 0 else (lambda dist: row <= S5_TILE - 1 - dist)
        for (qr, qi), dist in ((l1, 1), (l2, 2), (l4, 4)):
            m = valid(dist).astype(F32)[:, None, None]
            planes += [m * qr[d][None], m * qi[d][None]]
        order = row if d == 0 else row[::-1]
        planes += [jnp.stack([rp[k][0][d] for k in range(S5_TILE)])[order],
                   jnp.stack([rp[k][1][d] for k in range(S5_TILE)])[order]]
        planes += [jnp.broadcast_to(l8[0][d], (S5_TILE, ng, p)), jnp.broadcast_to(l8[1][d], (S5_TILE, ng, p))]
    lamc = jnp.stack(planes)
    lamc = lamc.reshape(20, S5_TILE, ng // 2, 2 * p).transpose(2, 0, 1, 3)
    return kt, ws.astype(BF16), cp.astype(BF16), lamc


def _chunk_cumsum(g, rev):
    sub = 8
    nt = g.shape[0] // sub
    row = lax.broadcasted_iota(jnp.int32, (sub, g.shape[1]), 0)
    out = [None] * nt
    off = None
    for kk in (range(nt - 1, -1, -1) if rev else range(nt)):
        x = g[kk * sub:(kk + 1) * sub, :]
        for dist in (1, 2, 4):
            if rev:
                x = x + jnp.where(row < sub - dist, pltpu.roll(x, sub - dist, 0), 0.0)
            else:
                x = x + jnp.where(row >= dist, pltpu.roll(x, dist, 0), 0.0)
        if off is not None:
            x = x + off
        out[kk] = x
        e = 0 if rev else sub - 1
        off = jnp.broadcast_to(x[e:e + 1, :], x.shape)
    return jnp.concatenate(out, axis=0)


def _gla_log_decay(glr, gup, gbias, rev):
    c = GLA_CHUNK
    z = jnp.dot(glr.astype(BF16), gup, preferred_element_type=F32) + gbias
    g = (jnp.minimum(z, 0.0) - jnp.log(1.0 + jnp.exp(-jnp.abs(z)))) * (1.0 / GLA_GATE_NORM)
    return [_chunk_cumsum(g[n * c:(n + 1) * c, :], rev) for n in range(g.shape[0] // c)]


def _gla_chunk(q, k, v, gc, st, rev, need_out):
    c = GLA_CHUNK
    i_ref = c // 2 - 1 if rev else c // 2
    i_last = 0 if rev else c - 1
    g_ref = gc[i_ref:i_ref + 1, :]
    g_last = gc[i_last:i_last + 1, :]
    lane_head = lax.broadcasted_iota(jnp.int32, (1, GLA_KEY), 1) // GLA_DK

    stack = lambda a: jnp.concatenate(
        [jnp.where(lane_head == h, a, 0.0) for h in range(GLA_HEADS)], axis=0).astype(BF16)
    dv2 = 2 * GLA_DV
    vt = jnp.concatenate(
        [jnp.concatenate([v[:, p * dv2:p * dv2 + GLA_DV], v[:, p * dv2 + GLA_DV:(p + 1) * dv2]], axis=0).T
         for p in range(GLA_HEADS // 2)], axis=1).astype(BF16)
    kl = k * jnp.exp(g_last - gc)
    kv = jnp.dot(vt, stack(kl), preferred_element_type=F32)
    st_new = st * jnp.exp(g_last) + kv
    if not need_out:
        return None, st_new

    qe = q * jnp.exp(gc - g_ref)
    ke = (k * jnp.exp(g_ref - gc)).astype(BF16)
    qg = q * jnp.exp(gc)
    nt_dims = (((1,), (1,)), ((), ()))
    sc = lax.dot_general(stack(qe), ke, nt_dims, preferred_element_type=F32)
    rs = lax.broadcasted_iota(jnp.int32, (GLA_HEADS * c, c), 0) % c
    cs = lax.broadcasted_iota(jnp.int32, (GLA_HEADS * c, c), 1)
    keep = (rs <= cs) if rev else (rs >= cs)
    sc = jnp.where(keep, sc, 0.0).astype(BF16)
    vb = v.astype(BF16)
    oo = lax.dot_general(stack(qg), st.astype(BF16), nt_dims, preferred_element_type=F32)
    o = jnp.concatenate(
        [jnp.dot(sc[h * c:(h + 1) * c, :], vb[:, h * GLA_DV:(h + 1) * GLA_DV], preferred_element_type=F32)
         + oo[h * c:(h + 1) * c, :] for h in range(GLA_HEADS)], axis=1)
    return o, st_new


def _gla_kernel(cf_ref, cb_ref, cc_ref, gup_ref, gbias_ref, of_ref, ob_ref, stf_ref, stb_ref):
    c = GLA_CHUNK
    q0, k0, v0, r0 = 0, GLA_KEY, 2 * GLA_KEY, 2 * GLA_KEY + GLA_VAL

    def decay(ref, d):
        return _gla_log_decay(ref[:, r0:CM_W], gup_ref[d], gbias_ref[d], d == 1)

    def run(ref, n, gcs, d, st, need_out):
        rows = slice(n * c, (n + 1) * c)
        q = ref[rows, q0:k0] if need_out else None
        return _gla_chunk(q, ref[rows, k0:v0], ref[rows, v0:r0], gcs[n], st, d == 1, need_out)

    @pl.when(pl.program_id(1) == 0)
    def _():
        nctx = cc_ref.shape[0] // c
        gf, gb = decay(cc_ref, 0), decay(cc_ref, 1)
        stf = jnp.zeros(stf_ref.shape, F32)
        stb = jnp.zeros(stb_ref.shape, F32)
        for n in range(nctx):
            _, stf = run(cc_ref, n, gf, 0, stf, False)
            _, stb = run(cc_ref, nctx - 1 - n, gb, 1, stb, False)
        stf_ref[...] = stf
        stb_ref[...] = stb

    ncols, rows_per_col, _ = cf_ref.shape
    nch = rows_per_col // c
    stf = stf_ref[...]
    stb = stb_ref[...]
    for j in range(ncols):
        jb = ncols - 1 - j
        gf, gb = decay(cf_ref.at[j], 0), decay(cb_ref.at[jb], 1)
        for n in range(nch):
            o, stf = run(cf_ref.at[j], n, gf, 0, stf, True)
            of_ref[j, n * c:(n + 1) * c, :] = o
            m = nch - 1 - n
            o, stb = run(cb_ref.at[jb], m, gb, 1, stb, True)
            ob_ref[jb, m * c:(m + 1) * c, :] = o
    stf_ref[...] = stf
    stb_ref[...] = stb


def _gla(cm, cmc, gup, gbias):
    b, ncol, rows, _ = cm.shape
    lc = cmc.shape[1]
    cps = GLA_COLS_PER_STEP
    last = ncol // cps - 1
    col = lambda width, fn: pl.BlockSpec((None, cps, rows, width), fn)
    out_shape = jax.ShapeDtypeStruct((b, ncol, rows, GLA_VAL), F32)
    return pl.pallas_call(
        _gla_kernel,
        grid=(b, ncol // cps),
        in_specs=[col(CM_W, lambda bi, ci: (bi, ci, 0, 0)),
                  col(CM_W, lambda bi, ci: (bi, last - ci, 0, 0)),
                  pl.BlockSpec((None, lc, CM_W), lambda bi, ci: (bi, 0, 0)),
                  _const_spec(gup.shape), _const_spec(gbias.shape)],
        out_specs=[col(GLA_VAL, lambda bi, ci: (bi, ci, 0, 0)),
                   col(GLA_VAL, lambda bi, ci: (bi, last - ci, 0, 0))],
        out_shape=[out_shape, out_shape],
        scratch_shapes=[pltpu.VMEM((GLA_DV, GLA_KEY), F32), pltpu.VMEM((GLA_DV, GLA_KEY), F32)],
        compiler_params=_params("arbitrary", "arbitrary"),
    )(cm, cm, cmc, gup, gbias)


def _back_kernel(x1_ref, nat_ref, y_ref, of_ref, ob_ref, mod_ref, dskip_ref, gluw_ref,
                 glub_ref, s5out_ref, gnorm_ref, glaout_ref, wo_ref, n3_ref, wg_ref, wu_ref, wd_ref, fin_ref,
                 out_ref, *, fchunk):
    m = mod_ref[...]
    tm, d = x1_ref.shape
    ng, nj, _ = y_ref.shape
    s5w = ng * S5_GROUP

    yp = _from_chunk_major(lambda g, tt: y_ref[g, :, tt * LANES:(tt + 1) * LANES].astype(F32), ng, nj)
    ys_all = jnp.swapaxes(yp.reshape(S5_CHUNK, nj, s5w), 0, 1).reshape(tm, s5w)
    ocm = (of_ref[...] + ob_ref[...]).astype(BF16).astype(F32)
    o_all = jnp.swapaxes(ocm, 0, 1).reshape(tm, GLA_VAL)

    for r0 in range(0, tm, tm // BACK_ROW_SPLIT):
        rows = slice(r0, r0 + tm // BACK_ROW_SPLIT)
        ya = ys_all[rows, :] + dskip_ref[...] * nat_ref[rows, 0:s5w].astype(F32)
        ya = 0.5 * ya * (1.0 + jnp.tanh(0.7978845608028654 * (ya + 0.044715 * (ya * ya * ya))))
        gl = jnp.dot(ya.astype(BF16), gluw_ref[...], preferred_element_type=F32) + glub_ref[...]
        ya = ya * jax.nn.sigmoid(gl)

        heads = []
        for h in range(GLA_HEADS):
            oh = o_all[rows, h * GLA_DV:(h + 1) * GLA_DV]
            heads.append(oh * lax.rsqrt(jnp.mean(oh * oh, axis=-1, keepdims=True) + RMS_EPS))
        r = nat_ref[rows, s5w:s5w + GLA_VAL].astype(F32)
        yb = jnp.concatenate(heads, axis=1) * gnorm_ref[...] * (r * jax.nn.sigmoid(r))

        pa = jnp.dot(ya.astype(BF16), s5out_ref[...], preferred_element_type=F32)
        pb = jnp.dot(yb.astype(BF16), glaout_ref[...], preferred_element_type=F32)
        ga = nat_ref[rows, s5w + GLA_VAL:s5w + GLA_VAL + d].astype(F32)
        gb = nat_ref[rows, s5w + GLA_VAL + d:s5w + GLA_VAL + 2 * d].astype(F32)
        mg = jax.nn.sigmoid(ga) * pa + jax.nn.sigmoid(gb) * pb
        y = jnp.dot(mg.astype(BF16), wo_ref[...], preferred_element_type=F32)
        x2 = x1_ref[rows, :] + m[5:6] * y
        h = _rms_mod(x2, n3_ref[...], m[6:7], m[7:8]).astype(BF16)
        x3 = _swiglu_residual(x2, h, m[8:9], wg_ref, wu_ref, wd_ref, fchunk)
        ms = jnp.mean(x3 * x3, axis=-1, keepdims=True)
        out_ref[rows, :] = x3 * lax.rsqrt(ms + RMS_EPS) * fin_ref[...]


def _back(x1, nat, ys5, of, ob, mods3, w, tm, fchunk):
    b, l, d = x1.shape
    f = w["wg2"].shape[1]
    s5w = w["dskip"].shape[-1]
    ng = s5w // S5_GROUP
    nblk = l // tm
    tok = lambda width: pl.BlockSpec((None, tm, width), lambda bi, i: (bi, i, 0))
    colblk = pl.BlockSpec((None, GRID_W, tm // GRID_W, GLA_VAL), lambda bi, i: (bi, 0, i, 0))
    return pl.pallas_call(
        functools.partial(_back_kernel, fchunk=fchunk),
        grid=(b, nblk),
        in_specs=[tok(d), tok(NAT_W),
                  pl.BlockSpec((ng, tm // S5_CHUNK, S5_CHUNK * S5_GROUP), lambda bi, i: (0, bi * nblk + i, 0)),
                  colblk, colblk,
                  pl.BlockSpec((None, N_MOD, d), lambda bi, i: (bi, 0, 0)),
                  _const_spec((1, s5w)), _const_spec((s5w, s5w)), _const_spec((1, s5w)),
                  _const_spec((s5w, d)), _const_spec((1, GLA_VAL)), _const_spec((GLA_VAL, d)),
                  _const_spec((d, d)), _const_spec((1, d)), _const_spec((d, f)), _const_spec((d, f)),
                  _const_spec((f, d)), _const_spec((1, d))],
        out_specs=tok(d),
        out_shape=jax.ShapeDtypeStruct((b, l, d), F32),
        compiler_params=_params("arbitrary", "arbitrary"),
    )(x1, nat, ys5, of, ob, mods3, w["dskip"], w["gluw"], w["glub"], w["s5out"], w["gnorm"], w["glaout"], w["wo"], w["n3"], w["wg2"],
      w["wu2"], w["wd2"], w["fin"])


def kernel(x, c, ctx, c_ctx, ada_w, ada_b, ffn1_norm, ffn1_w_gate, ffn1_w_up, ffn1_w_down, mix_norm, w_in,
           s5_lambda_re, s5_lambda_im, s5_log_dt, s5_b_re, s5_b_im, s5_c_re, s5_c_im, s5_d, s5_glu_w, s5_glu_b,
           s5_out, gla_gate_up, gla_gate_b, gla_norm, gla_out, w_o, ffn2_norm, ffn2_w_gate, ffn2_w_up,
           ffn2_w_down, final_norm):
    b, l, d = x.shape
    lc = ctx.shape[1]
    assert ada_w.shape[0] == 1 and b + 1 <= 8
    s5w = s5_d.shape[-1]
    fchunk = 256
    tm = min(512, l)
    assert tm % (GRID_W * 8) == 0 and l % tm == 0 and lc % (S5_CHUNK * S5_TILE) == 0 and lc % GLA_CHUNK == 0

    cvec = jnp.concatenate([c, c_ctx[None, :], jnp.zeros((8 - b - 1, d), F32)], axis=0)
    mods3 = _ada(cvec, ada_w[0], ada_b[0]).reshape(8, N_MOD, d)

    wi = w_in[0]
    o_q, o_k, o_v, o_r = s5w, s5w + GLA_KEY, s5w + 2 * GLA_KEY, s5w + 2 * GLA_KEY + GLA_VAL
    o_glr = o_r + GLA_VAL
    o_ga = o_glr + 2 * GLA_GATE_RANK
    wnat = jnp.concatenate([wi[:, :o_q], wi[:, o_r:o_glr], wi[:, o_ga:]], axis=1)
    wcm = jnp.concatenate([wi[:, o_q:o_k] * (GLA_DK ** -0.5), wi[:, o_k:o_r],
                           jnp.pad(wi[:, o_glr:o_ga], ((0, 0), (0, LANES - 2 * GLA_GATE_RANK)))], axis=1)
    gup = jnp.zeros((2, LANES, GLA_KEY), F32)
    gup = gup.at[0, 0:GLA_GATE_RANK].set(gla_gate_up[0, 0])
    gup = gup.at[1, GLA_GATE_RANK:2 * GLA_GATE_RANK].set(gla_gate_up[0, 1])
    row = lambda v: v.reshape(1, -1).astype(F32)
    w = dict(
        n1=row(ffn1_norm[0]), wg1=ffn1_w_gate[0].astype(BF16), wu1=ffn1_w_up[0].astype(BF16),
        wd1=ffn1_w_down[0].astype(BF16), n2=row(mix_norm[0]), wnat=wnat.astype(BF16), wcm=wcm.astype(BF16),
        dskip=row(s5_d[0]), gluw=s5_glu_w[0].astype(BF16), glub=row(s5_glu_b[0]), s5out=s5_out[0].astype(BF16),
        gnorm=row(gla_norm[0]), glaout=gla_out[0].astype(BF16), wo=w_o[0].astype(BF16),
        n3=row(ffn2_norm[0]), wg2=ffn2_w_gate[0].astype(BF16), wu2=ffn2_w_up[0].astype(BF16),
        wd2=ffn2_w_down[0].astype(BF16), fin=row(final_norm))

    lat_row = lambda bi: bi
    ctx_row = lambda bi: b
    x1 = _ffn(x, mods3, lat_row, w["n1"], w["wg1"], w["wu1"], w["wd1"], tm, fchunk, 0)
    c1 = _ffn(ctx, mods3, ctx_row, w["n1"], w["wg1"], w["wu1"], w["wd1"], lc, fchunk, 0)
    nat, xl, cm = _proj(x1, mods3, lat_row, w, tm, True)
    _, xc, cmc = _proj(c1, mods3, ctx_row, w, lc, False)

    kt, ws, cp, lamc = _s5_weights(s5_lambda_re[0], s5_lambda_im[0], s5_log_dt[0], s5_b_re[0], s5_b_im[0],
                                     s5_c_re[0], s5_c_im[0])
    ys5 = _s5(xc, xl, kt, ws, cp, lamc, gps=4, nb=b)

    of, ob = _gla(cm, cmc, gup.astype(BF16), gla_gate_b[0].reshape(2, 1, GLA_KEY).astype(F32))
    return _back(x1, nat, ys5, of, ob, mods3, w, tm, fchunk)
```

```python
import functools

import jax
import jax.numpy as jnp
from jax import lax
from jax.experimental import pallas as pl
from jax.experimental.pallas import tpu as pltpu

F32 = jnp.float32
BF16 = jnp.bfloat16

RMS_EPS = 1e-6
MACARON_WEIGHT = 0.5
GRID_W = 64
N_MOD = 9
S5_GROUP = 16
S5_STATE = 64
S5_CHUNK = 16
S5_TILE = 8
S5_PREFIX_TILES = 4
GLA_HEADS = 4
GLA_DK = 64
GLA_DV = 128
GLA_CHUNK = 64
GLA_GATE_RANK = 16
GLA_GATE_NORM = 16.0
GLA_COLS_PER_STEP = 8
GLA_KEY = GLA_HEADS * GLA_DK
GLA_VAL = GLA_HEADS * GLA_DV
LANES = 128
NAT_W = 3072
CM_W = 2 * GLA_KEY + GLA_VAL + LANES
V7X_VMEM_LIMIT_BYTES = 56 * 1024 * 1024


def _params(*sem):
    return pltpu.CompilerParams(dimension_semantics=sem, vmem_limit_bytes=V7X_VMEM_LIMIT_BYTES)


def _const_spec(shape):
    nd = len(shape)
    return pl.BlockSpec(shape, lambda *_: (0,) * nd, pipeline_mode=pl.Buffered(1))


def _rms_mod(x, g, shift, scale):
    ms = jnp.mean(x * x, axis=-1, keepdims=True)
    return (x * lax.rsqrt(ms + RMS_EPS) * g) * (1.0 + scale) + shift


def _swiglu_residual(x, h, gate, wg_ref, wu_ref, wd_ref, fchunk):
    acc = None
    for f0 in range(0, wg_ref.shape[1], fchunk):
        gg = jnp.dot(h, wg_ref[:, f0:f0 + fchunk], preferred_element_type=F32)
        uu = jnp.dot(h, wu_ref[:, f0:f0 + fchunk], preferred_element_type=F32)
        a = (gg * jax.nn.sigmoid(gg) * uu).astype(BF16)
        o = jnp.dot(a, wd_ref[f0:f0 + fchunk, :], preferred_element_type=F32)
        acc = o if acc is None else acc + o
    return x + gate * (MACARON_WEIGHT * acc)


def _ada_kernel(c_ref, w_ref, b_ref, o_ref):
    cv = c_ref[...]
    s = cv * jax.nn.sigmoid(cv)
    o_ref[...] = jnp.dot(s, w_ref[...], preferred_element_type=F32,
                         precision=lax.Precision.HIGHEST) + b_ref[...]


def _ada(cvec, ada_w, ada_b):
    rows, d = cvec.shape
    n = ada_w.shape[1]
    bn = n // 8
    return pl.pallas_call(
        _ada_kernel,
        grid=(n // bn,),
        in_specs=[pl.BlockSpec((rows, d), lambda j: (0, 0)),
                  pl.BlockSpec((d, bn), lambda j: (0, j)),
                  pl.BlockSpec((1, bn), lambda j: (0, j))],
        out_specs=pl.BlockSpec((rows, bn), lambda j: (0, j)),
        out_shape=jax.ShapeDtypeStruct((rows, n), F32),
        compiler_params=_params("arbitrary"),
    )(cvec, ada_w, ada_b.reshape(1, n))


def _ffn_kernel(x_ref, mod_ref, n_ref, wg_ref, wu_ref, wd_ref, o_ref, *, fchunk, mod0):
    x = x_ref[...]
    m = mod_ref[...]
    h = _rms_mod(x, n_ref[...], m[mod0:mod0 + 1], m[mod0 + 1:mod0 + 2]).astype(BF16)
    o_ref[...] = _swiglu_residual(x, h, m[mod0 + 2:mod0 + 3], wg_ref, wu_ref, wd_ref, fchunk)


def _ffn(x, mods3, mod_row, norm, wg, wu, wd, tm, fchunk, mod0):
    b, l, d = x.shape
    f = wg.shape[1]
    tok = pl.BlockSpec((None, tm, d), lambda bi, i: (bi, i, 0))
    return pl.pallas_call(
        functools.partial(_ffn_kernel, fchunk=fchunk, mod0=mod0),
        grid=(b, l // tm),
        in_specs=[tok, pl.BlockSpec((None, N_MOD, d), lambda bi, i: (mod_row(bi), 0, 0)),
                  _const_spec((1, d)), _const_spec((d, f)), _const_spec((d, f)), _const_spec((f, d))],
        out_specs=tok,
        out_shape=jax.ShapeDtypeStruct((b, l, d), F32),
        compiler_params=_params("arbitrary", "arbitrary"),
    )(x, mods3, norm, wg, wu, wd)


def _transpose_pieces(v):
    n = len(v)
    piece = lax.broadcasted_iota(jnp.int32, v[0].shape, 1) // S5_GROUP
    s = n // 2
    while s >= 1:
        keep = (piece & s) == 0
        nv = list(v)
        for i in range(n):
            if i & s == 0:
                a, b = v[i], v[i + s]
                nv[i] = jnp.where(keep, a, pltpu.roll(b, s * S5_GROUP, 1))
                nv[i + s] = jnp.where(keep, pltpu.roll(a, LANES - s * S5_GROUP, 1), b)
        v = nv
        s //= 2
    return v


def _to_chunk_major(up, store):
    nj = up.shape[0] // S5_CHUNK
    npc = LANES // S5_GROUP
    for gg in range(up.shape[1] // LANES):
        for tt in range(S5_CHUNK // npc):
            src = [up[(npc * tt + p) * nj:(npc * tt + p + 1) * nj, gg * LANES:(gg + 1) * LANES] for p in range(npc)]
            for gl, v in enumerate(_transpose_pieces(src)):
                store(gg * npc + gl, tt, v)


def _from_chunk_major(load, ng, nj):
    npc = LANES // S5_GROUP
    row_blocks = []
    for tt in range(S5_CHUNK // npc):
        per_p = [[] for _ in range(npc)]
        for gg in range(ng // npc):
            out = _transpose_pieces([load(gg * npc + gl, tt) for gl in range(npc)])
            for p in range(npc):
                per_p[p].append(out[p])
        row_blocks += [jnp.concatenate(blk, axis=1) for blk in per_p]
    return jnp.concatenate(row_blocks, axis=0)


def _proj_kernel(x_ref, mod_ref, n_ref, wnat_ref, wcm_ref, nat_ref, xs5_ref, cm_ref, *, col_major):
    x = x_ref[...]
    m = mod_ref[...]
    tm = x.shape[0]
    h2 = _rms_mod(x, n_ref[...], m[3:4], m[4:5]).astype(BF16)
    u_bf = None
    for c0 in range(0, NAT_W, 1024):
        p = jnp.dot(h2, wnat_ref[:, c0:c0 + 1024], preferred_element_type=F32).astype(BF16)
        nat_ref[:, c0:c0 + 1024] = p
        if c0 == 0:
            u_bf = p[:, 0:xs5_ref.shape[0] * S5_GROUP]
    pc = jnp.dot(h2, wcm_ref[...], preferred_element_type=F32).astype(BF16)
    if col_major:
        pcm = pc.astype(F32).reshape(tm // GRID_W, GRID_W, CM_W)
        cm_ref[...] = jnp.swapaxes(pcm, 0, 1)
    else:
        cm_ref[...] = pc.astype(F32)
    uf = u_bf.astype(F32).reshape(tm // S5_CHUNK, S5_CHUNK, u_bf.shape[1])
    up = jnp.swapaxes(uf, 0, 1).reshape(tm, u_bf.shape[1])

    def store(g, tt, v):
        xs5_ref[g, :, tt * LANES:(tt + 1) * LANES] = v.astype(BF16)

    _to_chunk_major(up, store)


def _proj(x1, mods3, mod_row, w, tm, col_major):
    b, l, d = x1.shape
    ng = w["dskip"].shape[-1] // S5_GROUP
    nj = tm // S5_CHUNK
    nblk = l // tm
    tok = lambda width: pl.BlockSpec((None, tm, width), lambda bi, i: (bi, i, 0))
    if col_major:
        cm_spec = pl.BlockSpec((None, GRID_W, tm // GRID_W, CM_W), lambda bi, i: (bi, 0, i, 0))
        cm_shape = jax.ShapeDtypeStruct((b, GRID_W, l // GRID_W, CM_W), F32)
    else:
        cm_spec = tok(CM_W)
        cm_shape = jax.ShapeDtypeStruct((b, l, CM_W), F32)
    return pl.pallas_call(
        functools.partial(_proj_kernel, col_major=col_major),
        grid=(b, nblk),
        in_specs=[tok(d), pl.BlockSpec((None, N_MOD, d), lambda bi, i: (mod_row(bi), 0, 0)),
                  _const_spec((1, d)), _const_spec((d, NAT_W)), _const_spec((d, CM_W))],
        out_specs=[tok(NAT_W),
                   pl.BlockSpec((ng, nj, S5_CHUNK * S5_GROUP), lambda bi, i: (0, bi * nblk + i, 0)),
                   cm_spec],
        out_shape=[jax.ShapeDtypeStruct((b, l, NAT_W), BF16),
                   jax.ShapeDtypeStruct((ng, b * (l // S5_CHUNK), S5_CHUNK * S5_GROUP), BF16),
                   cm_shape],
        compiler_params=_params("arbitrary", "arbitrary"),
    )(x1, mods3, w["n2"], w["wnat"], w["wcm"])


def _s5_tile_prefix(sr, si, cst, fwd):
    ar, ai = sr, si
    for lvl, dist in enumerate((1, 2, 4)):
        sh = dist if fwd else S5_TILE - dist
        rr, ri = pltpu.roll(ar, sh, 0), pltpu.roll(ai, sh, 0)
        lr, li = cst[2 * lvl], cst[2 * lvl + 1]
        ar, ai = ar + (lr * rr - li * ri), ai + (lr * ri + li * rr)
    return ar, ai


def _s5_tile_carry(ar, ai, hr, hi, cst, fwd):
    row = lax.broadcasted_iota(jnp.int32, ar.shape, 0)
    keep = (row >= 1) if fwd else (row <= S5_TILE - 2)
    sh = 1 if fwd else S5_TILE - 1
    pr = jnp.where(keep, pltpu.roll(ar, sh, 0), 0.0)
    pi = jnp.where(keep, pltpu.roll(ai, sh, 0), 0.0)
    lpr, lpi, l8r, l8i = cst[6:10]
    hin_r = lpr * hr - lpi * hi + pr
    hin_i = lpr * hi + lpi * hr + pi
    e = S5_TILE - 1 if fwd else 0
    er = jnp.broadcast_to(ar[e:e + 1, :], ar.shape)
    ei = jnp.broadcast_to(ai[e:e + 1, :], ai.shape)
    return hin_r, hin_i, l8r * hr - l8i * hi + er, l8r * hi + l8i * hr + ei


def _s5_kernel(xc_ref, xl_ref, kt_ref, ws_ref, cp_ref, lam_ref, y_ref, s_ref, *, nb):
    npair = ws_ref.shape[0]
    jc = xc_ref.shape[1] // nb
    jl = xl_ref.shape[1] // nb
    jt = jc + jl
    for pp in range(npair):
        for src, j0, nj in ((xc_ref, 0, jc), (xl_ref, jc, jl)):
            s = jnp.dot(src[2 * pp], ws_ref[pp, 0:256, :], preferred_element_type=F32)
            s = s + jnp.dot(src[2 * pp + 1], ws_ref[pp, 256:512, :], preferred_element_type=F32)
            for bi in range(nb):
                s_ref[pp, bi * jt + j0:bi * jt + j0 + nj, :] = s[bi * nj:(bi + 1) * nj, :]

    nct = jc // S5_TILE
    nt = jt // S5_TILE
    zero = jnp.zeros((S5_TILE, LANES), F32)
    for pp in range(npair):

        def prefix(it, carry, pp=pp):
            tiles = []
            for u in range(S5_PREFIX_TILES):
                r0 = pl.multiple_of((it * S5_PREFIX_TILES + u) * S5_TILE, S5_TILE)
                for c0, fwd in ((0, True), (256, False)):
                    tiles.append((r0, c0, fwd, s_ref[pp, pl.ds(r0, S5_TILE), c0:c0 + 128],
                                  s_ref[pp, pl.ds(r0, S5_TILE), c0 + 128:c0 + 256]))
            cst = {fwd: [lam_ref[pp, (0 if fwd else 10) + k] for k in range(6)] for fwd in (True, False)}
            done = [(r0, c0) + _s5_tile_prefix(sr, si, cst[fwd], fwd) for r0, c0, fwd, sr, si in tiles]
            for r0, c0, ar, ai in done:
                s_ref[pp, pl.ds(r0, S5_TILE), c0:c0 + 128] = ar
                s_ref[pp, pl.ds(r0, S5_TILE), c0 + 128:c0 + 256] = ai
            return carry

        lax.fori_loop(0, nb * nt // S5_PREFIX_TILES, prefix, 0)

        def step(it, carry, pp=pp):
            mb = jnp.where(it < nct, nct - 1 - it, nt - 1 - (it - nct))
            cst = {True: [lam_ref[pp, k] for k in range(10)], False: [lam_ref[pp, 10 + k] for k in range(10)]}
            tiles = []
            for bi in range(nb):
                for c0, fwd, tile in ((0, True, it), (256, False, mb)):
                    r0 = pl.multiple_of(bi * jt + tile * S5_TILE, S5_TILE)
                    tiles.append((bi, r0, c0, fwd, s_ref[pp, pl.ds(r0, S5_TILE), c0:c0 + 128],
                                  s_ref[pp, pl.ds(r0, S5_TILE), c0 + 128:c0 + 256]))
            out = [[None, None] for _ in range(nb)]
            for bi, r0, c0, fwd, ar, ai in tiles:
                hr, hi = carry[bi][0 if fwd else 1]
                hin_r, hin_i, hr, hi = _s5_tile_carry(ar, ai, hr, hi, cst[fwd], fwd)
                out[bi][0 if fwd else 1] = (hr, hi)
                s_ref[pp, pl.ds(r0, S5_TILE), c0:c0 + 128] = hin_r
                s_ref[pp, pl.ds(r0, S5_TILE), c0 + 128:c0 + 256] = hin_i
            return tuple(tuple(o) for o in out)

        lax.fori_loop(0, nt, step, tuple(((zero, zero), (zero, zero)) for _ in range(nb)))

    tc = S5_CHUNK * S5_GROUP
    width = kt_ref.shape[2]
    for pp in range(npair):
        toeps = []
        for gi in range(2):
            kt = kt_ref[2 * pp + gi]
            blocks = []
            for s in range(S5_CHUNK):
                sh = (width - S5_GROUP * (S5_CHUNK - 1 - s)) % width
                blocks.append((kt if sh == 0 else pltpu.roll(kt, sh, 1))[:, 0:tc])
            toeps.append(jnp.concatenate(blocks, axis=0).astype(BF16))
        for bi in range(nb):
            hin = s_ref[pp, bi * jt + jc:(bi + 1) * jt, :].astype(BF16)
            yp = jnp.dot(hin, cp_ref[pp], preferred_element_type=F32)
            for gi in range(2):
                g = 2 * pp + gi
                y = yp[:, gi * 256:(gi + 1) * 256] + jnp.dot(xl_ref[g, bi * jl:(bi + 1) * jl, :], toeps[gi],
                                                             preferred_element_type=F32)
                y_ref[g, bi * jl:(bi + 1) * jl, :] = y.astype(BF16)


def _s5(xc, xl, kt, ws, cp, lamc, gps, nb):
    ng, rc, _ = xc.shape
    rl = xl.shape[1]
    npair = gps // 2
    return pl.pallas_call(
        functools.partial(_s5_kernel, nb=nb),
        grid=(ng // gps,),
        in_specs=[pl.BlockSpec((gps, rc, 256), lambda i: (i, 0, 0)),
                  pl.BlockSpec((gps, rl, 256), lambda i: (i, 0, 0)),
                  pl.BlockSpec((gps,) + kt.shape[1:], lambda i: (i, 0, 0)),
                  pl.BlockSpec((npair, 512, 512), lambda i: (i, 0, 0)),
                  pl.BlockSpec((npair, 512, 512), lambda i: (i, 0, 0)),
                  pl.BlockSpec((npair, 20, S5_TILE, LANES), lambda i: (i, 0, 0, 0))],
        out_specs=pl.BlockSpec((gps, rl, 256), lambda i: (i, 0, 0)),
        out_shape=jax.ShapeDtypeStruct((ng, rl, 256), BF16),
        scratch_shapes=[pltpu.VMEM((npair, rc + rl, 512), F32)],
        compiler_params=_params("arbitrary"),
    )(xc, xl, kt, ws, cp, lamc)


def _cmul(ar, ai, br, bi):
    return ar * br - ai * bi, ar * bi + ai * br


def _s5_weights(lam_re, lam_im, log_dt, b_re, b_im, c_re, c_im):
    nd, ng, p = lam_re.shape
    t = S5_CHUNK
    tc = t * S5_GROUP
    lr = jnp.minimum(lam_re.astype(F32), -1e-4)
    li = lam_im.astype(F32)
    dt = jnp.exp(log_dt.astype(F32))[..., None]
    mag = jnp.exp(lr * dt)
    lbr, lbi = mag * jnp.cos(li * dt), mag * jnp.sin(li * dt)
    den = lr * lr + li * li
    fr = ((lbr - 1.0) * lr + lbi * li) / den
    fi = (lbi * lr - (lbr - 1.0) * li) / den
    bbr, bbi = _cmul(fr[..., None], fi[..., None], b_re.astype(F32), b_im.astype(F32))
    cr, ci = c_re.astype(F32), c_im.astype(F32)

    pr, pi = [jnp.ones_like(lbr)], [jnp.zeros_like(lbr)]
    for _ in range(t):
        nr, ni = _cmul(pr[-1], pi[-1], lbr, lbi)
        pr.append(nr)
        pi.append(ni)
    pr, pi = jnp.stack(pr), jnp.stack(pi)
    wr, wi = _cmul(pr[:t, ..., None], pi[:t, ..., None], bbr[None], bbi[None])

    kern = jnp.einsum("dgxp,kdgpc->dgkxc", cr, wr) - jnp.einsum("dgxp,kdgpc->dgkxc", ci, wi)
    ktf = kern[0].transpose(0, 3, 1, 2)
    ktb = kern[1][:, ::-1].transpose(0, 3, 1, 2)
    kt = jnp.concatenate([ktb[:, :, :t - 1], ktf[:, :, 0:1] + ktb[:, :, t - 1:t], ktf[:, :, 1:]], axis=2)
    kt = kt.reshape(ng, S5_GROUP, (2 * t - 1) * S5_GROUP)
    kt = jnp.pad(kt, ((0, 0), (0, 0), (0, 2 * tc - kt.shape[-1])))

    parity = (jnp.arange(ng) % 2)[:, None, None]

    gscp = lambda a: a.transpose(1, 0, 3, 2).reshape(ng, tc, p)
    parts = [gscp(wr[::-1, 0]), gscp(wi[::-1, 0]), gscp(wr[:, 1]), gscp(wi[:, 1])]
    ws = jnp.concatenate([a for a in parts for _ in range(2)], axis=-1)
    own = (jnp.arange(8 * p) // p) % 2 == parity
    ws = jnp.where(own, ws, 0.0).reshape(ng // 2, 2 * tc, 8 * p)

    def readout(d, powr, powi):
        mr, mi = _cmul(cr[d][None], ci[d][None], powr[:, :, None, :], powi[:, :, None, :])
        to_gptx = lambda a: a.transpose(1, 3, 0, 2).reshape(ng, p, tc)
        return [to_gptx(mr), to_gptx(-mi)]

    parts = readout(0, pr[1:t + 1, 0], pi[1:t + 1, 0]) + readout(1, pr[1:t + 1, 1][::-1], pi[1:t + 1, 1][::-1])
    own = jnp.arange(2 * tc) // tc == parity
    cp = jnp.stack([jnp.where(own, jnp.concatenate([a, a], axis=-1), 0.0) for a in parts], axis=1)
    cp = cp.reshape(ng // 2, 2, 4, p, 2 * tc).transpose(0, 2, 1, 3, 4).reshape(ng // 2, 8 * p, 2 * tc)

    row = jnp.arange(S5_TILE)
    l1 = (pr[t], pi[t])
    l2 = _cmul(*l1, *l1)
    l4 = _cmul(*l2, *l2)
    l8 = _cmul(*l4, *l4)
    rp = [(jnp.ones_like(lbr), jnp.zeros_like(lbr))]
    for _ in range(S5_TILE - 1):
        rp.append(_cmul(*rp[-1], *l1))
    planes = []
    for d in range(nd):
        valid = (lambda dist: row >= dist) if d == 0 else (lambda dist: row <= S5_TILE - 1 - dist)
        for (qr, qi), dist in ((l1, 1), (l2, 2), (l4, 4)):
            m = valid(dist).astype(F32)[:, None, None]
            planes += [m * qr[d][None], m * qi[d][None]]
        order = row if d == 0 else row[::-1]
        planes += [jnp.stack([rp[k][0][d] for k in range(S5_TILE)])[order],
                   jnp.stack([rp[k][1][d] for k in range(S5_TILE)])[order]]
        planes += [jnp.broadcast_to(l8[0][d], (S5_TILE, ng, p)), jnp.broadcast_to(l8[1][d], (S5_TILE, ng, p))]
    lamc = jnp.stack(planes)
    lamc = lamc.reshape(20, S5_TILE, ng // 2, 2 * p).transpose(2, 0, 1, 3)
    return kt, ws.astype(BF16), cp.astype(BF16), lamc


def _chunk_cumsum(g, rev):
    sub = 8
    nt = g.shape[0] // sub
    row = lax.broadcasted_iota(jnp.int32, (sub, g.shape[1]), 0)
    out = [None] * nt
    off = None
    for kk in (range(nt - 1, -1, -1) if rev else range(nt)):
        x = g[kk * sub:(kk + 1) * sub, :]
        for dist in (1, 2, 4):
            if rev:
                x = x + jnp.where(row < sub - dist, pltpu.roll(x, sub - dist, 0), 0.0)
            else:
                x = x + jnp.where(row >= dist, pltpu.roll(x, dist, 0), 0.0)
        if off is not None:
            x = x + off
        out[kk] = x
        e = 0 if rev else sub - 1
        off = jnp.broadcast_to(x[e:e + 1, :], x.shape)
    return jnp.concatenate(out, axis=0)


def _gla_log_decay(glr, gup, gbias, rev):
    c = GLA_CHUNK
    z = jnp.dot(glr.astype(BF16), gup, preferred_element_type=F32) + gbias
    g = (jnp.minimum(z, 0.0) - jnp.log(1.0 + jnp.exp(-jnp.abs(z)))) * (1.0 / GLA_GATE_NORM)
    return [_chunk_cumsum(g[n * c:(n + 1) * c, :], rev) for n in range(g.shape[0] // c)]


def _gla_chunk(q, k, v, gc, st, rev, need_out):
    c = GLA_CHUNK
    i_ref = c // 2 - 1 if rev else c // 2
    i_last = 0 if rev else c - 1
    g_ref = gc[i_ref:i_ref + 1, :]
    g_last = gc[i_last:i_last + 1, :]
    lane_head = lax.broadcasted_iota(jnp.int32, (1, GLA_KEY), 1) // GLA_DK

    stack = lambda a: jnp.concatenate(
        [jnp.where(lane_head == h, a, 0.0) for h in range(GLA_HEADS)], axis=0).astype(BF16)
    dv2 = 2 * GLA_DV
    vt = jnp.concatenate(
        [jnp.concatenate([v[:, p * dv2:p * dv2 + GLA_DV], v[:, p * dv2 + GLA_DV:(p + 1) * dv2]], axis=0).T
         for p in range(GLA_HEADS // 2)], axis=1).astype(BF16)
    kl = k * jnp.exp(g_last - gc)
    kv = jnp.dot(vt, stack(kl), preferred_element_type=F32)
    st_new = st * jnp.exp(g_last) + kv
    if not need_out:
        return None, st_new

    qe = q * jnp.exp(gc - g_ref)
    ke = (k * jnp.exp(g_ref - gc)).astype(BF16)
    qg = q * jnp.exp(gc)
    nt_dims = (((1,), (1,)), ((), ()))
    sc = lax.dot_general(stack(qe), ke, nt_dims, preferred_element_type=F32)
    rs = lax.broadcasted_iota(jnp.int32, (GLA_HEADS * c, c), 0) % c
    cs = lax.broadcasted_iota(jnp.int32, (GLA_HEADS * c, c), 1)
    keep = (rs <= cs) if rev else (rs >= cs)
    sc = jnp.where(keep, sc, 0.0).astype(BF16)
    vb = v.astype(BF16)
    oo = lax.dot_general(stack(qg), st.astype(BF16), nt_dims, preferred_element_type=F32)
    o = jnp.concatenate(
        [jnp.dot(sc[h * c:(h + 1) * c, :], vb[:, h * GLA_DV:(h + 1) * GLA_DV], preferred_element_type=F32)
         + oo[h * c:(h + 1) * c, :] for h in range(GLA_HEADS)], axis=1)
    return o, st_new


def _gla_kernel(cf_ref, cb_ref, cc_ref, gup_ref, gbias_ref, of_ref, ob_ref, stf_ref, stb_ref):
    c = GLA_CHUNK
    q0, k0, v0, r0 = 0, GLA_KEY, 2 * GLA_KEY, 2 * GLA_KEY + GLA_VAL

    def decay(ref, d):
        return _gla_log_decay(ref[:, r0:CM_W], gup_ref[d], gbias_ref[d], d == 1)

    def run(ref, n, gcs, d, st, need_out):
        rows = slice(n * c, (n + 1) * c)
        q = ref[rows, q0:k0] if need_out else None
        return _gla_chunk(q, ref[rows, k0:v0], ref[rows, v0:r0], gcs[n], st, d == 1, need_out)

    @pl.when(pl.program_id(1) == 0)
    def _():
        nctx = cc_ref.shape[0] // c
        gf, gb = decay(cc_ref, 0), decay(cc_ref, 1)
        stf = jnp.zeros(stf_ref.shape, F32)
        stb = jnp.zeros(stb_ref.shape, F32)
        for n in range(nctx):
            _, stf = run(cc_ref, n, gf, 0, stf, False)
            _, stb = run(cc_ref, nctx - 1 - n, gb, 1, stb, False)
        stf_ref[...] = stf
        stb_ref[...] = stb

    ncols, rows_per_col, _ = cf_ref.shape
    nch = rows_per_col // c
    stf = stf_ref[...]
    stb = stb_ref[...]
    for j in range(ncols):
        jb = ncols - 1 - j
        gf, gb = decay(cf_ref.at[j], 0), decay(cb_ref.at[jb], 1)
        for n in range(nch):
            o, stf = run(cf_ref.at[j], n, gf, 0, stf, True)
            of_ref[j, n * c:(n + 1) * c, :] = o
            m = nch - 1 - n
            o, stb = run(cb_ref.at[jb], m, gb, 1, stb, True)
            ob_ref[jb, m * c:(m + 1) * c, :] = o
    stf_ref[...] = stf
    stb_ref[...] = stb


def _gla(cm, cmc, gup, gbias):
    b, ncol, rows, _ = cm.shape
    lc = cmc.shape[1]
    cps = GLA_COLS_PER_STEP
    last = ncol // cps - 1
    col = lambda width, fn: pl.BlockSpec((None, cps, rows, width), fn)
    out_shape = jax.ShapeDtypeStruct((b, ncol, rows, GLA_VAL), F32)
    return pl.pallas_call(
        _gla_kernel,
        grid=(b, ncol // cps),
        in_specs=[col(CM_W, lambda bi, ci: (bi, ci, 0, 0)),
                  col(CM_W, lambda bi, ci: (bi, last - ci, 0, 0)),
                  pl.BlockSpec((None, lc, CM_W), lambda bi, ci: (bi, 0, 0)),
                  _const_spec(gup.shape), _const_spec(gbias.shape)],
        out_specs=[col(GLA_VAL, lambda bi, ci: (bi, ci, 0, 0)),
                   col(GLA_VAL, lambda bi, ci: (bi, last - ci, 0, 0))],
        out_shape=[out_shape, out_shape],
        scratch_shapes=[pltpu.VMEM((GLA_DV, GLA_KEY), F32), pltpu.VMEM((GLA_DV, GLA_KEY), F32)],
        compiler_params=_params("arbitrary", "arbitrary"),
    )(cm, cm, cmc, gup, gbias)


def _back_kernel(x1_ref, nat_ref, y_ref, of_ref, ob_ref, mod_ref, dskip_ref, gluw_ref,
                 glub_ref, s5out_ref, gnorm_ref, glaout_ref, wo_ref, n3_ref, wg_ref, wu_ref, wd_ref, fin_ref,
                 out_ref, *, fchunk):
    m = mod_ref[...]
    tm, d = x1_ref.shape
    ng, nj, _ = y_ref.shape
    s5w = ng * S5_GROUP

    yp = _from_chunk_major(lambda g, tt: y_ref[g, :, tt * LANES:(tt + 1) * LANES].astype(F32), ng, nj)
    ys = jnp.swapaxes(yp.reshape(S5_CHUNK, nj, s5w), 0, 1).reshape(tm, s5w)
    ya = ys + dskip_ref[...] * nat_ref[:, 0:s5w].astype(F32)
    ya = 0.5 * ya * (1.0 + jnp.tanh(0.7978845608028654 * (ya + 0.044715 * (ya * ya * ya))))
    gl = jnp.dot(ya.astype(BF16), gluw_ref[...], preferred_element_type=F32) + glub_ref[...]
    ya = ya * jax.nn.sigmoid(gl)

    ocm = (of_ref[...] + ob_ref[...]).astype(BF16).astype(F32)
    o = jnp.swapaxes(ocm, 0, 1).reshape(tm, GLA_VAL)
    heads = []
    for h in range(GLA_HEADS):
        oh = o[:, h * GLA_DV:(h + 1) * GLA_DV]
        heads.append(oh * lax.rsqrt(jnp.mean(oh * oh, axis=-1, keepdims=True) + RMS_EPS))
    r = nat_ref[:, s5w:s5w + GLA_VAL].astype(F32)
    yb = jnp.concatenate(heads, axis=1) * gnorm_ref[...] * (r * jax.nn.sigmoid(r))

    pa = jnp.dot(ya.astype(BF16), s5out_ref[...], preferred_element_type=F32)
    pb = jnp.dot(yb.astype(BF16), glaout_ref[...], preferred_element_type=F32)
    ga = nat_ref[:, s5w + GLA_VAL:s5w + GLA_VAL + d].astype(F32)
    gb = nat_ref[:, s5w + GLA_VAL + d:s5w + GLA_VAL + 2 * d].astype(F32)
    mg = jax.nn.sigmoid(ga) * pa + jax.nn.sigmoid(gb) * pb
    y = jnp.dot(mg.astype(BF16), wo_ref[...], preferred_element_type=F32)
    x2 = x1_ref[...] + m[5:6] * y
    h = _rms_mod(x2, n3_ref[...], m[6:7], m[7:8]).astype(BF16)
    x3 = _swiglu_residual(x2, h, m[8:9], wg_ref, wu_ref, wd_ref, fchunk)
    ms = jnp.mean(x3 * x3, axis=-1, keepdims=True)
    out_ref[...] = x3 * lax.rsqrt(ms + RMS_EPS) * fin_ref[...]


def _back(x1, nat, ys5, of, ob, mods3, w, tm, fchunk):
    b, l, d = x1.shape
    f = w["wg2"].shape[1]
    s5w = w["dskip"].shape[-1]
    ng = s5w // S5_GROUP
    nblk = l // tm
    tok = lambda width: pl.BlockSpec((None, tm, width), lambda bi, i: (bi, i, 0))
    colblk = pl.BlockSpec((None, GRID_W, tm // GRID_W, GLA_VAL), lambda bi, i: (bi, 0, i, 0))
    return pl.pallas_call(
        functools.partial(_back_kernel, fchunk=fchunk),
        grid=(b, nblk),
        in_specs=[tok(d), tok(NAT_W),
                  pl.BlockSpec((ng, tm // S5_CHUNK, S5_CHUNK * S5_GROUP), lambda bi, i: (0, bi * nblk + i, 0)),
                  colblk, colblk,
                  pl.BlockSpec((None, N_MOD, d), lambda bi, i: (bi, 0, 0)),
                  _const_spec((1, s5w)), _const_spec((s5w, s5w)), _const_spec((1, s5w)),
                  _const_spec((s5w, d)), _const_spec((1, GLA_VAL)), _const_spec((GLA_VAL, d)),
                  _const_spec((d, d)), _const_spec((1, d)), _const_spec((d, f)), _const_spec((d, f)),
                  _const_spec((f, d)), _const_spec((1, d))],
        out_specs=tok(d),
        out_shape=jax.ShapeDtypeStruct((b, l, d), F32),
        compiler_params=_params("arbitrary", "arbitrary"),
    )(x1, nat, ys5, of, ob, mods3, w["dskip"], w["gluw"], w["glub"], w["s5out"], w["gnorm"], w["glaout"],
      w["wo"], w["n3"], w["wg2"], w["wu2"], w["wd2"], w["fin"])


def kernel(x, c, ctx, c_ctx, ada_w, ada_b, ffn1_norm, ffn1_w_gate, ffn1_w_up, ffn1_w_down, mix_norm, w_in,
           s5_lambda_re, s5_lambda_im, s5_log_dt, s5_b_re, s5_b_im, s5_c_re, s5_c_im, s5_d, s5_glu_w, s5_glu_b,
           s5_out, gla_gate_up, gla_gate_b, gla_norm, gla_out, w_o, ffn2_norm, ffn2_w_gate, ffn2_w_up,
           ffn2_w_down, final_norm):
    b, l, d = x.shape
    lc = ctx.shape[1]
    assert ada_w.shape[0] == 1 and b + 1 <= 8
    s5w = s5_d.shape[-1]
    fchunk = 256
    tm = min(512, l)
    assert tm % (GRID_W * 8) == 0 and l % tm == 0 and lc % (S5_CHUNK * S5_TILE) == 0 and lc % GLA_CHUNK == 0

    cvec = jnp.concatenate([c, c_ctx[None, :], jnp.zeros((8 - b - 1, d), F32)], axis=0)
    mods3 = _ada(cvec, ada_w[0], ada_b[0]).reshape(8, N_MOD, d)

    wi = w_in[0]
    o_q, o_k, o_v, o_r = s5w, s5w + GLA_KEY, s5w + 2 * GLA_KEY, s5w + 2 * GLA_KEY + GLA_VAL
    o_glr = o_r + GLA_VAL
    o_ga = o_glr + 2 * GLA_GATE_RANK
    wnat = jnp.concatenate([wi[:, :o_q], wi[:, o_r:o_glr], wi[:, o_ga:]], axis=1)
    wcm = jnp.concatenate([wi[:, o_q:o_k] * (GLA_DK ** -0.5), wi[:, o_k:o_r],
                           jnp.pad(wi[:, o_glr:o_ga], ((0, 0), (0, LANES - 2 * GLA_GATE_RANK)))], axis=1)
    gup = jnp.zeros((2, LANES, GLA_KEY), F32)
    gup = gup.at[0, 0:GLA_GATE_RANK].set(gla_gate_up[0, 0])
    gup = gup.at[1, GLA_GATE_RANK:2 * GLA_GATE_RANK].set(gla_gate_up[0, 1])
    row = lambda v: v.reshape(1, -1).astype(F32)
    w = dict(
        n1=row(ffn1_norm[0]), wg1=ffn1_w_gate[0].astype(BF16), wu1=ffn1_w_up[0].astype(BF16),
        wd1=ffn1_w_down[0].astype(BF16), n2=row(mix_norm[0]), wnat=wnat.astype(BF16), wcm=wcm.astype(BF16),
        dskip=row(s5_d[0]), gluw=s5_glu_w[0].astype(BF16), glub=row(s5_glu_b[0]), s5out=s5_out[0].astype(BF16),
        gnorm=row(gla_norm[0]), glaout=gla_out[0].astype(BF16), wo=w_o[0].astype(BF16),
        n3=row(ffn2_norm[0]), wg2=ffn2_w_gate[0].astype(BF16), wu2=ffn2_w_up[0].astype(BF16),
        wd2=ffn2_w_down[0].astype(BF16), fin=row(final_norm))

    lat_row = lambda bi: bi
    ctx_row = lambda bi: b
    x1 = _ffn(x, mods3, lat_row, w["n1"], w["wg1"], w["wu1"], w["wd1"], tm, fchunk, 0)
    c1 = _ffn(ctx, mods3, ctx_row, w["n1"], w["wg1"], w["wu1"], w["wd1"], lc, fchunk, 0)
    nat, xl, cm = _proj(x1, mods3, lat_row, w, min(2 * tm, l), True)
    _, xc, cmc = _proj(c1, mods3, ctx_row, w, lc, False)

    kt, ws, cp, lamc = _s5_weights(s5_lambda_re[0], s5_lambda_im[0], s5_log_dt[0], s5_b_re[0], s5_b_im[0],
                                     s5_c_re[0], s5_c_im[0])
    ys5 = _s5(xc, xl, kt, ws, cp, lamc, gps=4, nb=b)

    of, ob = _gla(cm, cmc, gup.astype(BF16), gla_gate_b[0].reshape(2, 1, GLA_KEY).astype(F32))
    return _back(x1, nat, ys5, of, ob, mods3, w, tm, fchunk)
```

```python
import functools

import jax
import jax.numpy as jnp
from jax import lax
from jax.experimental import pallas as pl
from jax.experimental.pallas import tpu as pltpu

F32 = jnp.float32
BF16 = jnp.bfloat16

RMS_EPS = 1e-6
MACARON_WEIGHT = 0.5
GRID_W = 64
N_MOD = 9
S5_GROUP = 16
S5_STATE = 64
S5_CHUNK = 16
S5_TILE = 8
S5_PREFIX_TILES = 4
GLA_HEADS = 4
GLA_DK = 64
GLA_DV = 128
GLA_CHUNK = 64
GLA_GATE_RANK = 16
GLA_GATE_NORM = 16.0
GLA_COLS_PER_STEP = 8
GLA_KEY = GLA_HEADS * GLA_DK
GLA_VAL = GLA_HEADS * GLA_DV
LANES = 128
NAT_W = 3072
CM_W = 2 * GLA_KEY + GLA_VAL + LANES
V7X_VMEM_LIMIT_BYTES = 56 * 1024 * 1024


def _params(*sem):
    return pltpu.CompilerParams(dimension_semantics=sem, vmem_limit_bytes=V7X_VMEM_LIMIT_BYTES)


def _const_spec(shape):
    nd = len(shape)
    return pl.BlockSpec(shape, lambda *_: (0,) * nd, pipeline_mode=pl.Buffered(1))


def _rms_mod(x, g, shift, scale):
    ms = jnp.mean(x * x, axis=-1, keepdims=True)
    return (x * lax.rsqrt(ms + RMS_EPS) * g) * (1.0 + scale) + shift


def _swiglu_residual(x, h, gate, wg_ref, wu_ref, wd_ref, fchunk):
    acc = None
    for f0 in range(0, wg_ref.shape[1], fchunk):
        gg = jnp.dot(h, wg_ref[:, f0:f0 + fchunk], preferred_element_type=F32)
        uu = jnp.dot(h, wu_ref[:, f0:f0 + fchunk], preferred_element_type=F32)
        a = (gg * jax.nn.sigmoid(gg) * uu).astype(BF16)
        o = jnp.dot(a, wd_ref[f0:f0 + fchunk, :], preferred_element_type=F32)
        acc = o if acc is None else acc + o
    return x + gate * (MACARON_WEIGHT * acc)


def _ada_kernel(c_ref, w_ref, b_ref, o_ref):
    cv = c_ref[...]
    s = cv * jax.nn.sigmoid(cv)
    o_ref[...] = jnp.dot(s, w_ref[...], preferred_element_type=F32,
                         precision=lax.Precision.HIGHEST) + b_ref[...]


def _ada(cvec, ada_w, ada_b):
    rows, d = cvec.shape
    n = ada_w.shape[1]
    bn = n // 8
    return pl.pallas_call(
        _ada_kernel,
        grid=(n // bn,),
        in_specs=[pl.BlockSpec((rows, d), lambda j: (0, 0)),
                  pl.BlockSpec((d, bn), lambda j: (0, j)),
                  pl.BlockSpec((1, bn), lambda j: (0, j))],
        out_specs=pl.BlockSpec((rows, bn), lambda j: (0, j)),
        out_shape=jax.ShapeDtypeStruct((rows, n), F32),
        compiler_params=_params("arbitrary"),
    )(cvec, ada_w, ada_b.reshape(1, n))


def _ffn_kernel(x_ref, mod_ref, n_ref, wg_ref, wu_ref, wd_ref, o_ref, *, fchunk, mod0):
    x = x_ref[...]
    m = mod_ref[...]
    h = _rms_mod(x, n_ref[...], m[mod0:mod0 + 1], m[mod0 + 1:mod0 + 2]).astype(BF16)
    o_ref[...] = _swiglu_residual(x, h, m[mod0 + 2:mod0 + 3], wg_ref, wu_ref, wd_ref, fchunk)


def _ffn(x, mods3, mod_row, norm, wg, wu, wd, tm, fchunk, mod0):
    b, l, d = x.shape
    f = wg.shape[1]
    tok = pl.BlockSpec((None, tm, d), lambda bi, i: (bi, i, 0))
    return pl.pallas_call(
        functools.partial(_ffn_kernel, fchunk=fchunk, mod0=mod0),
        grid=(b, l // tm),
        in_specs=[tok, pl.BlockSpec((None, N_MOD, d), lambda bi, i: (mod_row(bi), 0, 0)),
                  _const_spec((1, d)), _const_spec((d, f)), _const_spec((d, f)), _const_spec((f, d))],
        out_specs=tok,
        out_shape=jax.ShapeDtypeStruct((b, l, d), F32),
        compiler_params=_params("arbitrary", "arbitrary"),
    )(x, mods3, norm, wg, wu, wd)


def _transpose_pieces(v):
    n = len(v)
    piece = lax.broadcasted_iota(jnp.int32, v[0].shape, 1) // S5_GROUP
    s = n // 2
    while s >= 1:
        keep = (piece & s) == 0
        nv = list(v)
        for i in range(n):
            if i & s == 0:
                a, b = v[i], v[i + s]
                nv[i] = jnp.where(keep, a, pltpu.roll(b, s * S5_GROUP, 1))
                nv[i + s] = jnp.where(keep, pltpu.roll(a, LANES - s * S5_GROUP, 1), b)
        v = nv
        s //= 2
    return v


def _to_chunk_major(up, store):
    nj = up.shape[0] // S5_CHUNK
    npc = LANES // S5_GROUP
    for gg in range(up.shape[1] // LANES):
        for tt in range(S5_CHUNK // npc):
            src = [up[(npc * tt + p) * nj:(npc * tt + p + 1) * nj, gg * LANES:(gg + 1) * LANES] for p in range(npc)]
            for gl, v in enumerate(_transpose_pieces(src)):
                store(gg * npc + gl, tt, v)


def _from_chunk_major(load, ng, nj):
    npc = LANES // S5_GROUP
    row_blocks = []
    for tt in range(S5_CHUNK // npc):
        per_p = [[] for _ in range(npc)]
        for gg in range(ng // npc):
            out = _transpose_pieces([load(gg * npc + gl, tt) for gl in range(npc)])
            for p in range(npc):
                per_p[p].append(out[p])
        row_blocks += [jnp.concatenate(blk, axis=1) for blk in per_p]
    return jnp.concatenate(row_blocks, axis=0)


def _proj_kernel(x_ref, mod_ref, n_ref, wnat_ref, wcm_ref, nat_ref, xs5_ref, cm_ref, *, col_major):
    x = x_ref[...]
    m = mod_ref[...]
    tm = x.shape[0]
    h2 = _rms_mod(x, n_ref[...], m[3:4], m[4:5]).astype(BF16)
    u_bf = None
    for c0 in range(0, NAT_W, 1024):
        p = jnp.dot(h2, wnat_ref[:, c0:c0 + 1024], preferred_element_type=F32).astype(BF16)
        nat_ref[:, c0:c0 + 1024] = p
        if c0 == 0:
            u_bf = p[:, 0:xs5_ref.shape[0] * S5_GROUP]
    pc = jnp.dot(h2, wcm_ref[...], preferred_element_type=F32).astype(BF16)
    if col_major:
        pcm = pc.astype(F32).reshape(tm // GRID_W, GRID_W, CM_W)
        cm_ref[...] = jnp.swapaxes(pcm, 0, 1)
    else:
        cm_ref[...] = pc.astype(F32)
    uf = u_bf.astype(F32).reshape(tm // S5_CHUNK, S5_CHUNK, u_bf.shape[1])
    up = jnp.swapaxes(uf, 0, 1).reshape(tm, u_bf.shape[1])

    def store(g, tt, v):
        xs5_ref[g, :, tt * LANES:(tt + 1) * LANES] = v.astype(BF16)

    _to_chunk_major(up, store)


def _proj(x1, mods3, mod_row, w, tm, col_major):
    b, l, d = x1.shape
    ng = w["dskip"].shape[-1] // S5_GROUP
    nj = tm // S5_CHUNK
    nblk = l // tm
    tok = lambda width: pl.BlockSpec((None, tm, width), lambda bi, i: (bi, i, 0))
    if col_major:
        cm_spec = pl.BlockSpec((None, GRID_W, tm // GRID_W, CM_W), lambda bi, i: (bi, 0, i, 0))
        cm_shape = jax.ShapeDtypeStruct((b, GRID_W, l // GRID_W, CM_W), F32)
    else:
        cm_spec = tok(CM_W)
        cm_shape = jax.ShapeDtypeStruct((b, l, CM_W), F32)
    return pl.pallas_call(
        functools.partial(_proj_kernel, col_major=col_major),
        grid=(b, nblk),
        in_specs=[tok(d), pl.BlockSpec((None, N_MOD, d), lambda bi, i: (mod_row(bi), 0, 0)),
                  _const_spec((1, d)), _const_spec((d, NAT_W)), _const_spec((d, CM_W))],
        out_specs=[tok(NAT_W),
                   pl.BlockSpec((ng, nj, S5_CHUNK * S5_GROUP), lambda bi, i: (0, bi * nblk + i, 0)),
                   cm_spec],
        out_shape=[jax.ShapeDtypeStruct((b, l, NAT_W), BF16),
                   jax.ShapeDtypeStruct((ng, b * (l // S5_CHUNK), S5_CHUNK * S5_GROUP), BF16),
                   cm_shape],
        compiler_params=_params("arbitrary", "arbitrary"),
    )(x1, mods3, w["n2"], w["wnat"], w["wcm"])


def _s5_tile_prefix(sr, si, cst, fwd):
    ar, ai = sr, si
    for lvl, dist in enumerate((1, 2, 4)):
        sh = dist if fwd else S5_TILE - dist
        rr, ri = pltpu.roll(ar, sh, 0), pltpu.roll(ai, sh, 0)
        lr, li = cst[2 * lvl], cst[2 * lvl + 1]
        ar, ai = ar + (lr * rr - li * ri), ai + (lr * ri + li * rr)
    return ar, ai


def _s5_tile_carry(ar, ai, hr, hi, cst, fwd):
    row = lax.broadcasted_iota(jnp.int32, ar.shape, 0)
    keep = (row >= 1) if fwd else (row <= S5_TILE - 2)
    sh = 1 if fwd else S5_TILE - 1
    pr = jnp.where(keep, pltpu.roll(ar, sh, 0), 0.0)
    pi = jnp.where(keep, pltpu.roll(ai, sh, 0), 0.0)
    lpr, lpi, l8r, l8i = cst[6:10]
    hin_r = lpr * hr - lpi * hi + pr
    hin_i = lpr * hi + lpi * hr + pi
    e = S5_TILE - 1 if fwd else 0
    er = jnp.broadcast_to(ar[e:e + 1, :], ar.shape)
    ei = jnp.broadcast_to(ai[e:e + 1, :], ai.shape)
    return hin_r, hin_i, l8r * hr - l8i * hi + er, l8r * hi + l8i * hr + ei


def _s5_kernel(xc_ref, xl_ref, kt_ref, ws_ref, cp_ref, lam_ref, y_ref, s_ref, *, nb):
    npair = ws_ref.shape[0] // 2
    p = S5_STATE
    lo = lax.broadcasted_iota(jnp.int32, (ws_ref.shape[1], LANES), 1) < p
    half_turn = lambda a: pltpu.roll(a, p, 1)

    def state_in_operator(g):
        blocks = []
        for k in range(2):
            a = ws_ref[g, :, k * LANES:(k + 1) * LANES]
            if g % 2 == 0:
                blocks += [jnp.where(lo, a, 0.0), jnp.where(lo, half_turn(a), 0.0)]
            else:
                blocks += [jnp.where(lo, 0.0, half_turn(a)), jnp.where(lo, 0.0, a)]
        return jnp.concatenate(blocks, axis=1).astype(BF16)

    def state_out_operator(pp):
        zeros = jnp.zeros((p, cp_ref.shape[2]), BF16)
        rows = []
        for part in range(4):
            for gi in range(2):
                c = cp_ref[2 * pp + gi, part * p:(part + 1) * p, :]
                rows.append(jnp.concatenate([c, zeros] if gi == 0 else [zeros, c], axis=1))
        return jnp.concatenate(rows, axis=0)

    jc = xc_ref.shape[1] // nb
    jl = xl_ref.shape[1] // nb
    jt = jc + jl
    for pp in range(npair):
        w_in = [state_in_operator(2 * pp), state_in_operator(2 * pp + 1)]
        for src, j0, nj in ((xc_ref, 0, jc), (xl_ref, jc, jl)):
            s = jnp.dot(src[2 * pp], w_in[0], preferred_element_type=F32)
            s = s + jnp.dot(src[2 * pp + 1], w_in[1], preferred_element_type=F32)
            for bi in range(nb):
                s_ref[pp, bi * jt + j0:bi * jt + j0 + nj, :] = s[bi * nj:(bi + 1) * nj, :]

    nct = jc // S5_TILE
    nt = jt // S5_TILE
    zero = jnp.zeros((S5_TILE, LANES), F32)
    for pp in range(npair):

        def prefix(it, carry, pp=pp):
            tiles = []
            for u in range(S5_PREFIX_TILES):
                r0 = pl.multiple_of((it * S5_PREFIX_TILES + u) * S5_TILE, S5_TILE)
                for c0, fwd in ((0, True), (256, False)):
                    tiles.append((r0, c0, fwd, s_ref[pp, pl.ds(r0, S5_TILE), c0:c0 + 128],
                                  s_ref[pp, pl.ds(r0, S5_TILE), c0 + 128:c0 + 256]))
            cst = {fwd: [lam_ref[pp, (0 if fwd else 10) + k] for k in range(6)] for fwd in (True, False)}
            done = [(r0, c0) + _s5_tile_prefix(sr, si, cst[fwd], fwd) for r0, c0, fwd, sr, si in tiles]
            for r0, c0, ar, ai in done:
                s_ref[pp, pl.ds(r0, S5_TILE), c0:c0 + 128] = ar
                s_ref[pp, pl.ds(r0, S5_TILE), c0 + 128:c0 + 256] = ai
            return carry

        lax.fori_loop(0, nb * nt // S5_PREFIX_TILES, prefix, 0)

        def step(it, carry, pp=pp):
            mb = jnp.where(it < nct, nct - 1 - it, nt - 1 - (it - nct))
            cst = {True: [lam_ref[pp, k] for k in range(10)], False: [lam_ref[pp, 10 + k] for k in range(10)]}
            tiles = []
            for bi in range(nb):
                for c0, fwd, tile in ((0, True, it), (256, False, mb)):
                    r0 = pl.multiple_of(bi * jt + tile * S5_TILE, S5_TILE)
                    tiles.append((bi, r0, c0, fwd, s_ref[pp, pl.ds(r0, S5_TILE), c0:c0 + 128],
                                  s_ref[pp, pl.ds(r0, S5_TILE), c0 + 128:c0 + 256]))
            out = [[None, None] for _ in range(nb)]
            for bi, r0, c0, fwd, ar, ai in tiles:
                hr, hi = carry[bi][0 if fwd else 1]
                hin_r, hin_i, hr, hi = _s5_tile_carry(ar, ai, hr, hi, cst[fwd], fwd)
                out[bi][0 if fwd else 1] = (hr, hi)
                s_ref[pp, pl.ds(r0, S5_TILE), c0:c0 + 128] = hin_r
                s_ref[pp, pl.ds(r0, S5_TILE), c0 + 128:c0 + 256] = hin_i
            return tuple(tuple(o) for o in out)

        lax.fori_loop(0, nt, step, tuple(((zero, zero), (zero, zero)) for _ in range(nb)))

    tc = S5_CHUNK * S5_GROUP
    width = kt_ref.shape[2]
    for pp in range(npair):
        toeps = []
        for gi in range(2):
            kt = kt_ref[2 * pp + gi]
            blocks = []
            for s in range(S5_CHUNK):
                sh = (width - S5_GROUP * (S5_CHUNK - 1 - s)) % width
                blocks.append((kt if sh == 0 else pltpu.roll(kt, sh, 1))[:, 0:tc])
            toeps.append(jnp.concatenate(blocks, axis=0).astype(BF16))
        w_out = state_out_operator(pp)
        for bi in range(nb):
            hin = s_ref[pp, bi * jt + jc:(bi + 1) * jt, :].astype(BF16)
            yp = jnp.dot(hin, w_out, preferred_element_type=F32)
            for gi in range(2):
                g = 2 * pp + gi
                y = yp[:, gi * 256:(gi + 1) * 256] + jnp.dot(xl_ref[g, bi * jl:(bi + 1) * jl, :], toeps[gi],
                                                             preferred_element_type=F32)
                y_ref[g, bi * jl:(bi + 1) * jl, :] = y.astype(BF16)


def _s5(xc, xl, kt, ws, cp, lamc, gps, nb):
    ng, rc, _ = xc.shape
    rl = xl.shape[1]
    npair = gps // 2
    return pl.pallas_call(
        functools.partial(_s5_kernel, nb=nb),
        grid=(ng // gps,),
        in_specs=[pl.BlockSpec((gps, rc, 256), lambda i: (i, 0, 0)),
                  pl.BlockSpec((gps, rl, 256), lambda i: (i, 0, 0)),
                  pl.BlockSpec((gps,) + kt.shape[1:], lambda i: (i, 0, 0)),
                  pl.BlockSpec((gps,) + ws.shape[1:], lambda i: (i, 0, 0)),
                  pl.BlockSpec((gps,) + cp.shape[1:], lambda i: (i, 0, 0)),
                  pl.BlockSpec((npair, 20, S5_TILE, LANES), lambda i: (i, 0, 0, 0))],
        out_specs=pl.BlockSpec((gps, rl, 256), lambda i: (i, 0, 0)),
        out_shape=jax.ShapeDtypeStruct((ng, rl, 256), BF16),
        scratch_shapes=[pltpu.VMEM((npair, rc + rl, 512), F32)],
        compiler_params=_params("arbitrary"),
    )(xc, xl, kt, ws, cp, lamc)


def _cmul(ar, ai, br, bi):
    return ar * br - ai * bi, ar * bi + ai * br


def _s5_weights(lam_re, lam_im, log_dt, b_re, b_im, c_re, c_im):
    nd, ng, p = lam_re.shape
    t = S5_CHUNK
    tc = t * S5_GROUP
    lr = jnp.minimum(lam_re.astype(F32), -1e-4)
    li = lam_im.astype(F32)
    dt = jnp.exp(log_dt.astype(F32))[..., None]
    mag = jnp.exp(lr * dt)
    lbr, lbi = mag * jnp.cos(li * dt), mag * jnp.sin(li * dt)
    den = lr * lr + li * li
    fr = ((lbr - 1.0) * lr + lbi * li) / den
    fi = (lbi * lr - (lbr - 1.0) * li) / den
    bbr, bbi = _cmul(fr[..., None], fi[..., None], b_re.astype(F32), b_im.astype(F32))
    cr, ci = c_re.astype(F32), c_im.astype(F32)

    pr, pi = [jnp.ones_like(lbr)], [jnp.zeros_like(lbr)]
    for _ in range(t):
        nr, ni = _cmul(pr[-1], pi[-1], lbr, lbi)
        pr.append(nr)
        pi.append(ni)
    pr, pi = jnp.stack(pr), jnp.stack(pi)
    wr, wi = _cmul(pr[:t, ..., None], pi[:t, ..., None], bbr[None], bbi[None])

    kern = jnp.einsum("dgxp,kdgpc->dgkxc", cr, wr) - jnp.einsum("dgxp,kdgpc->dgkxc", ci, wi)
    ktf = kern[0].transpose(0, 3, 1, 2)
    ktb = kern[1][:, ::-1].transpose(0, 3, 1, 2)
    kt = jnp.concatenate([ktb[:, :, :t - 1], ktf[:, :, 0:1] + ktb[:, :, t - 1:t], ktf[:, :, 1:]], axis=2)
    kt = kt.reshape(ng, S5_GROUP, (2 * t - 1) * S5_GROUP)
    kt = jnp.pad(kt, ((0, 0), (0, 0), (0, 2 * tc - kt.shape[-1])))

    gscp = lambda a: a.transpose(1, 0, 3, 2).reshape(ng, tc, p)
    parts = [gscp(wr[::-1, 0]), gscp(wi[::-1, 0]), gscp(wr[:, 1]), gscp(wi[:, 1])]
    ws = jnp.concatenate(parts, axis=-1)

    def readout(d, powr, powi):
        mr, mi = _cmul(cr[d][None], ci[d][None], powr[:, :, None, :], powi[:, :, None, :])
        to_gptx = lambda a: a.transpose(1, 3, 0, 2).reshape(ng, p, tc)
        return [to_gptx(mr), to_gptx(-mi)]

    parts = readout(0, pr[1:t + 1, 0], pi[1:t + 1, 0]) + readout(1, pr[1:t + 1, 1][::-1], pi[1:t + 1, 1][::-1])
    cp = jnp.concatenate(parts, axis=1)

    row = jnp.arange(S5_TILE)
    l1 = (pr[t], pi[t])
    l2 = _cmul(*l1, *l1)
    l4 = _cmul(*l2, *l2)
    l8 = _cmul(*l4, *l4)
    rp = [(jnp.ones_like(lbr), jnp.zeros_like(lbr))]
    for _ in range(S5_TILE - 1):
        rp.append(_cmul(*rp[-1], *l1))
    planes = []
    for d in range(nd):
        valid = (lambda dist: row >= dist) if d == 0 else (lambda dist: row <= S5_TILE - 1 - dist)
        for (qr, qi), dist in ((l1, 1), (l2, 2), (l4, 4)):
            m = valid(dist).astype(F32)[:, None, None]
            planes += [m * qr[d][None], m * qi[d][None]]
        order = row if d == 0 else row[::-1]
        planes += [jnp.stack([rp[k][0][d] for k in range(S5_TILE)])[order],
                   jnp.stack([rp[k][1][d] for k in range(S5_TILE)])[order]]
        planes += [jnp.broadcast_to(l8[0][d], (S5_TILE, ng, p)), jnp.broadcast_to(l8[1][d], (S5_TILE, ng, p))]
    lamc = jnp.stack(planes)
    lamc = lamc.reshape(20, S5_TILE, ng // 2, 2 * p).transpose(2, 0, 1, 3)
    return kt, ws, cp.astype(BF16), lamc


def _chunk_cumsum(g, rev):
    sub = 8
    nt = g.shape[0] // sub
    row = lax.broadcasted_iota(jnp.int32, (sub, g.shape[1]), 0)
    out = [None] * nt
    off = None
    for kk in (range(nt - 1, -1, -1) if rev else range(nt)):
        x = g[kk * sub:(kk + 1) * sub, :]
        for dist in (1, 2, 4):
            if rev:
                x = x + jnp.where(row < sub - dist, pltpu.roll(x, sub - dist, 0), 0.0)
            else:
                x = x + jnp.where(row >= dist, pltpu.roll(x, dist, 0), 0.0)
        if off is not None:
            x = x + off
        out[kk] = x
        e = 0 if rev else sub - 1
        off = jnp.broadcast_to(x[e:e + 1, :], x.shape)
    return jnp.concatenate(out, axis=0)


def _gla_log_decay(glr, gup, gbias, rev):
    c = GLA_CHUNK
    z = jnp.dot(glr.astype(BF16), gup, preferred_element_type=F32) + gbias
    g = (jnp.minimum(z, 0.0) - jnp.log(1.0 + jnp.exp(-jnp.abs(z)))) * (1.0 / GLA_GATE_NORM)
    return [_chunk_cumsum(g[n * c:(n + 1) * c, :], rev) for n in range(g.shape[0] // c)]


def _gla_chunk(q, k, v, gc, st, rev, need_out):
    c = GLA_CHUNK
    i_ref = c // 2 - 1 if rev else c // 2
    i_last = 0 if rev else c - 1
    g_ref = gc[i_ref:i_ref + 1, :]
    g_last = gc[i_last:i_last + 1, :]
    lane_head = lax.broadcasted_iota(jnp.int32, (1, GLA_KEY), 1) // GLA_DK

    stack = lambda a: jnp.concatenate(
        [jnp.where(lane_head == h, a, 0.0) for h in range(GLA_HEADS)], axis=0).astype(BF16)
    dv2 = 2 * GLA_DV
    vt = jnp.concatenate(
        [jnp.concatenate([v[:, p * dv2:p * dv2 + GLA_DV], v[:, p * dv2 + GLA_DV:(p + 1) * dv2]], axis=0).T
         for p in range(GLA_HEADS // 2)], axis=1).astype(BF16)
    kl = k * jnp.exp(g_last - gc)
    kv = jnp.dot(vt, stack(kl), preferred_element_type=F32)
    st_new = st * jnp.exp(g_last) + kv
    if not need_out:
        return None, st_new

    qe = q * jnp.exp(gc - g_ref)
    ke = (k * jnp.exp(g_ref - gc)).astype(BF16)
    qg = q * jnp.exp(gc)
    nt_dims = (((1,), (1,)), ((), ()))
    sc = lax.dot_general(stack(qe), ke, nt_dims, preferred_element_type=F32)
    rs = lax.broadcasted_iota(jnp.int32, (GLA_HEADS * c, c), 0) % c
    cs = lax.broadcasted_iota(jnp.int32, (GLA_HEADS * c, c), 1)
    keep = (rs <= cs) if rev else (rs >= cs)
    sc = jnp.where(keep, sc, 0.0).astype(BF16)
    vb = v.astype(BF16)
    oo = lax.dot_general(stack(qg), st.astype(BF16), nt_dims, preferred_element_type=F32)
    o = jnp.concatenate(
        [jnp.dot(sc[h * c:(h + 1) * c, :], vb[:, h * GLA_DV:(h + 1) * GLA_DV], preferred_element_type=F32)
         + oo[h * c:(h + 1) * c, :] for h in range(GLA_HEADS)], axis=1)
    return o, st_new


def _gla_kernel(cf_ref, cb_ref, cc_ref, gup_ref, gbias_ref, of_ref, ob_ref, stf_ref, stb_ref):
    c = GLA_CHUNK
    q0, k0, v0, r0 = 0, GLA_KEY, 2 * GLA_KEY, 2 * GLA_KEY + GLA_VAL

    def decay(ref, d):
        return _gla_log_decay(ref[:, r0:CM_W], gup_ref[d], gbias_ref[d], d == 1)

    def run(ref, n, gcs, d, st, need_out):
        rows = slice(n * c, (n + 1) * c)
        q = ref[rows, q0:k0] if need_out else None
        return _gla_chunk(q, ref[rows, k0:v0], ref[rows, v0:r0], gcs[n], st, d == 1, need_out)

    @pl.when(pl.program_id(1) == 0)
    def _():
        nctx = cc_ref.shape[0] // c
        gf, gb = decay(cc_ref, 0), decay(cc_ref, 1)
        stf = jnp.zeros(stf_ref.shape, F32)
        stb = jnp.zeros(stb_ref.shape, F32)
        for n in range(nctx):
            _, stf = run(cc_ref, n, gf, 0, stf, False)
            _, stb = run(cc_ref, nctx - 1 - n, gb, 1, stb, False)
        stf_ref[...] = stf
        stb_ref[...] = stb

    ncols, rows_per_col, _ = cf_ref.shape
    nch = rows_per_col // c
    stf = stf_ref[...]
    stb = stb_ref[...]
    for j in range(ncols):
        jb = ncols - 1 - j
        gf, gb = decay(cf_ref.at[j], 0), decay(cb_ref.at[jb], 1)
        for n in range(nch):
            o, stf = run(cf_ref.at[j], n, gf, 0, stf, True)
            of_ref[j, n * c:(n + 1) * c, :] = o
            m = nch - 1 - n
            o, stb = run(cb_ref.at[jb], m, gb, 1, stb, True)
            ob_ref[jb, m * c:(m + 1) * c, :] = o
    stf_ref[...] = stf
    stb_ref[...] = stb


def _gla(cm, cmc, gup, gbias):
    b, ncol, rows, _ = cm.shape
    lc = cmc.shape[1]
    cps = GLA_COLS_PER_STEP
    last = ncol // cps - 1
    col = lambda width, fn: pl.BlockSpec((None, cps, rows, width), fn)
    out_shape = jax.ShapeDtypeStruct((b, ncol, rows, GLA_VAL), F32)
    return pl.pallas_call(
        _gla_kernel,
        grid=(b, ncol // cps),
        in_specs=[col(CM_W, lambda bi, ci: (bi, ci, 0, 0)),
                  col(CM_W, lambda bi, ci: (bi, last - ci, 0, 0)),
                  pl.BlockSpec((None, lc, CM_W), lambda bi, ci: (bi, 0, 0)),
                  _const_spec(gup.shape), _const_spec(gbias.shape)],
        out_specs=[col(GLA_VAL, lambda bi, ci: (bi, ci, 0, 0)),
                   col(GLA_VAL, lambda bi, ci: (bi, last - ci, 0, 0))],
        out_shape=[out_shape, out_shape],
        scratch_shapes=[pltpu.VMEM((GLA_DV, GLA_KEY), F32), pltpu.VMEM((GLA_DV, GLA_KEY), F32)],
        compiler_params=_params("arbitrary", "arbitrary"),
    )(cm, cm, cmc, gup, gbias)


def _back_kernel(x1_ref, nat_ref, y_ref, of_ref, ob_ref, mod_ref, dskip_ref, gluw_ref,
                 glub_ref, s5out_ref, gnorm_ref, glaout_ref, wo_ref, n3_ref, wg_ref, wu_ref, wd_ref, fin_ref,
                 out_ref, *, fchunk):
    m = mod_ref[...]
    tm, d = x1_ref.shape
    ng, nj, _ = y_ref.shape
    s5w = ng * S5_GROUP

    yp = _from_chunk_major(lambda g, tt: y_ref[g, :, tt * LANES:(tt + 1) * LANES].astype(F32), ng, nj)
    ys = jnp.swapaxes(yp.reshape(S5_CHUNK, nj, s5w), 0, 1).reshape(tm, s5w)
    ya = ys + dskip_ref[...] * nat_ref[:, 0:s5w].astype(F32)
    ya = 0.5 * ya * (1.0 + jnp.tanh(0.7978845608028654 * (ya + 0.044715 * (ya * ya * ya))))
    gl = jnp.dot(ya.astype(BF16), gluw_ref[...], preferred_element_type=F32) + glub_ref[...]
    ya = ya * jax.nn.sigmoid(gl)

    ocm = (of_ref[...] + ob_ref[...]).astype(BF16).astype(F32)
    o = jnp.swapaxes(ocm, 0, 1).reshape(tm, GLA_VAL)
    heads = []
    for h in range(GLA_HEADS):
        oh = o[:, h * GLA_DV:(h + 1) * GLA_DV]
        heads.append(oh * lax.rsqrt(jnp.mean(oh * oh, axis=-1, keepdims=True) + RMS_EPS))
    r = nat_ref[:, s5w:s5w + GLA_VAL].astype(F32)
    yb = jnp.concatenate(heads, axis=1) * gnorm_ref[...] * (r * jax.nn.sigmoid(r))

    pa = jnp.dot(ya.astype(BF16), s5out_ref[...], preferred_element_type=F32)
    pb = jnp.dot(yb.astype(BF16), glaout_ref[...], preferred_element_type=F32)
    ga = nat_ref[:, s5w + GLA_VAL:s5w + GLA_VAL + d].astype(F32)
    gb = nat_ref[:, s5w + GLA_VAL + d:s5w + GLA_VAL + 2 * d].astype(F32)
    mg = jax.nn.sigmoid(ga) * pa + jax.nn.sigmoid(gb) * pb
    y = jnp.dot(mg.astype(BF16), wo_ref[...], preferred_element_type=F32)
    x2 = x1_ref[...] + m[5:6] * y
    h = _rms_mod(x2, n3_ref[...], m[6:7], m[7:8]).astype(BF16)
    x3 = _swiglu_residual(x2, h, m[8:9], wg_ref, wu_ref, wd_ref, fchunk)
    ms = jnp.mean(x3 * x3, axis=-1, keepdims=True)
    out_ref[...] = x3 * lax.rsqrt(ms + RMS_EPS) * fin_ref[...]


def _back(x1, nat, ys5, of, ob, mods3, w, tm, fchunk):
    b, l, d = x1.shape
    f = w["wg2"].shape[1]
    s5w = w["dskip"].shape[-1]
    ng = s5w // S5_GROUP
    nblk = l // tm
    tok = lambda width: pl.BlockSpec((None, tm, width), lambda bi, i: (bi, i, 0))
    colblk = pl.BlockSpec((None, GRID_W, tm // GRID_W, GLA_VAL), lambda bi, i: (bi, 0, i, 0))
    return pl.pallas_call(
        functools.partial(_back_kernel, fchunk=fchunk),
        grid=(b, nblk),
        in_specs=[tok(d), tok(NAT_W),
                  pl.BlockSpec((ng, tm // S5_CHUNK, S5_CHUNK * S5_GROUP), lambda bi, i: (0, bi * nblk + i, 0)),
                  colblk, colblk,
                  pl.BlockSpec((None, N_MOD, d), lambda bi, i: (bi, 0, 0)),
                  _const_spec((1, s5w)), _const_spec((s5w, s5w)), _const_spec((1, s5w)),
                  _const_spec((s5w, d)), _const_spec((1, GLA_VAL)), _const_spec((GLA_VAL, d)),
                  _const_spec((d, d)), _const_spec((1, d)), _const_spec((d, f)), _const_spec((d, f)),
                  _const_spec((f, d)), _const_spec((1, d))],
        out_specs=tok(d),
        out_shape=jax.ShapeDtypeStruct((b, l, d), F32),
        compiler_params=_params("arbitrary", "arbitrary"),
    )(x1, nat, ys5, of, ob, mods3, w["dskip"], w["gluw"], w["glub"], w["s5out"], w["gnorm"], w["glaout"],
      w["wo"], w["n3"], w["wg2"], w["wu2"], w["wd2"], w["fin"])


def kernel(x, c, ctx, c_ctx, ada_w, ada_b, ffn1_norm, ffn1_w_gate, ffn1_w_up, ffn1_w_down, mix_norm, w_in,
           s5_lambda_re, s5_lambda_im, s5_log_dt, s5_b_re, s5_b_im, s5_c_re, s5_c_im, s5_d, s5_glu_w, s5_glu_b,
           s5_out, gla_gate_up, gla_gate_b, gla_norm, gla_out, w_o, ffn2_norm, ffn2_w_gate, ffn2_w_up,
           ffn2_w_down, final_norm):
    b, l, d = x.shape
    lc = ctx.shape[1]
    assert ada_w.shape[0] == 1 and b + 1 <= 8
    s5w = s5_d.shape[-1]
    fchunk = 256
    tm = min(512, l)
    assert tm % (GRID_W * 8) == 0 and l % tm == 0 and lc % (S5_CHUNK * S5_TILE) == 0 and lc % GLA_CHUNK == 0

    cvec = jnp.concatenate([c, c_ctx[None, :], jnp.zeros((8 - b - 1, d), F32)], axis=0)
    mods3 = _ada(cvec, ada_w[0], ada_b[0]).reshape(8, N_MOD, d)

    wi = w_in[0]
    o_q, o_k, o_v, o_r = s5w, s5w + GLA_KEY, s5w + 2 * GLA_KEY, s5w + 2 * GLA_KEY + GLA_VAL
    o_glr = o_r + GLA_VAL
    o_ga = o_glr + 2 * GLA_GATE_RANK
    wnat = jnp.concatenate([wi[:, :o_q], wi[:, o_r:o_glr], wi[:, o_ga:]], axis=1)
    wcm = jnp.concatenate([wi[:, o_q:o_k] * (GLA_DK ** -0.5), wi[:, o_k:o_r],
                           jnp.pad(wi[:, o_glr:o_ga], ((0, 0), (0, LANES - 2 * GLA_GATE_RANK)))], axis=1)
    gup = jnp.zeros((2, LANES, GLA_KEY), F32)
    gup = gup.at[0, 0:GLA_GATE_RANK].set(gla_gate_up[0, 0])
    gup = gup.at[1, GLA_GATE_RANK:2 * GLA_GATE_RANK].set(gla_gate_up[0, 1])
    row = lambda v: v.reshape(1, -1).astype(F32)
    w = dict(
        n1=row(ffn1_norm[0]), wg1=ffn1_w_gate[0].astype(BF16), wu1=ffn1_w_up[0].astype(BF16),
        wd1=ffn1_w_down[0].astype(BF16), n2=row(mix_norm[0]), wnat=wnat.astype(BF16), wcm=wcm.astype(BF16),
        dskip=row(s5_d[0]), gluw=s5_glu_w[0].astype(BF16), glub=row(s5_glu_b[0]), s5out=s5_out[0].astype(BF16),
        gnorm=row(gla_norm[0]), glaout=gla_out[0].astype(BF16), wo=w_o[0].astype(BF16),
        n3=row(ffn2_norm[0]), wg2=ffn2_w_gate[0].astype(BF16), wu2=ffn2_w_up[0].astype(BF16),
        wd2=ffn2_w_down[0].astype(BF16), fin=row(final_norm))

    lat_row = lambda bi: bi
    ctx_row = lambda bi: b
    x1 = _ffn(x, mods3, lat_row, w["n1"], w["wg1"], w["wu1"], w["wd1"], tm, fchunk, 0)
    c1 = _ffn(ctx, mods3, ctx_row, w["n1"], w["wg1"], w["wu1"], w["wd1"], lc, fchunk, 0)
    nat, xl, cm = _proj(x1, mods3, lat_row, w, min(2 * tm, l), True)
    _, xc, cmc = _proj(c1, mods3, ctx_row, w, lc, False)

    kt, ws, cp, lamc = _s5_weights(s5_lambda_re[0], s5_lambda_im[0], s5_log_dt[0], s5_b_re[0], s5_b_im[0],
                                     s5_c_re[0], s5_c_im[0])
    ys5 = _s5(xc, xl, kt, ws, cp, lamc, gps=4, nb=b)

    of, ob = _gla(cm, cmc, gup.astype(BF16), gla_gate_b[0].reshape(2, 1, GLA_KEY).astype(F32))
    return _back(x1, nat, ys5, of, ob, mods3, w, tm, fchunk)
```

```python
import functools

import jax
import jax.numpy as jnp
from jax import lax
from jax.experimental import pallas as pl
from jax.experimental.pallas import tpu as pltpu

F32 = jnp.float32
BF16 = jnp.bfloat16

RMS_EPS = 1e-6
MACARON_WEIGHT = 0.5
GRID_W = 64
N_MOD = 9
S5_GROUP = 16
S5_STATE = 64
S5_CHUNK = 16
S5_TILE = 8
S5_PREFIX_TILES = 4
GLA_HEADS = 4
GLA_DK = 64
GLA_DV = 128
GLA_CHUNK = 64
GLA_GATE_RANK = 16
GLA_GATE_NORM = 16.0
GLA_COLS_PER_STEP = 8
GLA_KEY = GLA_HEADS * GLA_DK
GLA_VAL = GLA_HEADS * GLA_DV
LANES = 128
NAT_W = 3072
CM_W = 2 * GLA_KEY + GLA_VAL + LANES
V7X_VMEM_LIMIT_BYTES = 56 * 1024 * 1024


def _params(*sem):
    return pltpu.CompilerParams(dimension_semantics=sem, vmem_limit_bytes=V7X_VMEM_LIMIT_BYTES)


def _const_spec(shape):
    nd = len(shape)
    return pl.BlockSpec(shape, lambda *_: (0,) * nd, pipeline_mode=pl.Buffered(1))


def _rms_mod(x, g, shift, scale):
    ms = jnp.mean(x * x, axis=-1, keepdims=True)
    return (x * lax.rsqrt(ms + RMS_EPS) * g) * (1.0 + scale) + shift


def _swiglu_residual(x, h, gate, wg_ref, wu_ref, wd_ref, fchunk):
    acc = None
    for f0 in range(0, wg_ref.shape[1], fchunk):
        gg = jnp.dot(h, wg_ref[:, f0:f0 + fchunk], preferred_element_type=F32)
        uu = jnp.dot(h, wu_ref[:, f0:f0 + fchunk], preferred_element_type=F32)
        a = (gg * jax.nn.sigmoid(gg) * uu).astype(BF16)
        o = jnp.dot(a, wd_ref[f0:f0 + fchunk, :], preferred_element_type=F32)
        acc = o if acc is None else acc + o
    return x + gate * (MACARON_WEIGHT * acc)


def _ada_kernel(c_ref, w_ref, b_ref, o_ref):
    cv = c_ref[...]
    s = cv * jax.nn.sigmoid(cv)
    o_ref[...] = jnp.dot(s, w_ref[...], preferred_element_type=F32,
                         precision=lax.Precision.HIGHEST) + b_ref[...]


def _ada(cvec, ada_w, ada_b):
    rows, d = cvec.shape
    n = ada_w.shape[1]
    bn = n // 8
    return pl.pallas_call(
        _ada_kernel,
        grid=(n // bn,),
        in_specs=[pl.BlockSpec((rows, d), lambda j: (0, 0)),
                  pl.BlockSpec((d, bn), lambda j: (0, j)),
                  pl.BlockSpec((1, bn), lambda j: (0, j))],
        out_specs=pl.BlockSpec((rows, bn), lambda j: (0, j)),
        out_shape=jax.ShapeDtypeStruct((rows, n), F32),
        compiler_params=_params("arbitrary"),
    )(cvec, ada_w, ada_b.reshape(1, n))


def _ffn_kernel(x_ref, mod_ref, n_ref, wg_ref, wu_ref, wd_ref, o_ref, *, fchunk, mod0):
    x = x_ref[...]
    m = mod_ref[...]
    h = _rms_mod(x, n_ref[...], m[mod0:mod0 + 1], m[mod0 + 1:mod0 + 2]).astype(BF16)
    o_ref[...] = _swiglu_residual(x, h, m[mod0 + 2:mod0 + 3], wg_ref, wu_ref, wd_ref, fchunk)


def _ffn(x, mods3, mod_row, norm, wg, wu, wd, tm, fchunk, mod0):
    b, l, d = x.shape
    f = wg.shape[1]
    tok = pl.BlockSpec((None, tm, d), lambda bi, i: (bi, i, 0))
    return pl.pallas_call(
        functools.partial(_ffn_kernel, fchunk=fchunk, mod0=mod0),
        grid=(b, l // tm),
        in_specs=[tok, pl.BlockSpec((None, N_MOD, d), lambda bi, i: (mod_row(bi), 0, 0)),
                  _const_spec((1, d)), _const_spec((d, f)), _const_spec((d, f)), _const_spec((f, d))],
        out_specs=tok,
        out_shape=jax.ShapeDtypeStruct((b, l, d), F32),
        compiler_params=_params("arbitrary", "arbitrary"),
    )(x, mods3, norm, wg, wu, wd)


def _transpose_pieces(v):
    n = len(v)
    piece = lax.broadcasted_iota(jnp.int32, v[0].shape, 1) // S5_GROUP
    s = n // 2
    while s >= 1:
        keep = (piece & s) == 0
        nv = list(v)
        for i in range(n):
            if i & s == 0:
                a, b = v[i], v[i + s]
                nv[i] = jnp.where(keep, a, pltpu.roll(b, s * S5_GROUP, 1))
                nv[i + s] = jnp.where(keep, pltpu.roll(a, LANES - s * S5_GROUP, 1), b)
        v = nv
        s //= 2
    return v


def _to_chunk_major(up, store):
    nj = up.shape[0] // S5_CHUNK
    npc = LANES // S5_GROUP
    for gg in range(up.shape[1] // LANES):
        for tt in range(S5_CHUNK // npc):
            src = [up[(npc * tt + p) * nj:(npc * tt + p + 1) * nj, gg * LANES:(gg + 1) * LANES] for p in range(npc)]
            for gl, v in enumerate(_transpose_pieces(src)):
                store(gg * npc + gl, tt, v)


def _from_chunk_major(load, ng, nj):
    npc = LANES // S5_GROUP
    row_blocks = []
    for tt in range(S5_CHUNK // npc):
        per_p = [[] for _ in range(npc)]
        for gg in range(ng // npc):
            out = _transpose_pieces([load(gg * npc + gl, tt) for gl in range(npc)])
            for p in range(npc):
                per_p[p].append(out[p])
        row_blocks += [jnp.concatenate(blk, axis=1) for blk in per_p]
    return jnp.concatenate(row_blocks, axis=0)


def _proj_kernel(x_ref, mod_ref, n_ref, wnat_ref, wcm_ref, nat_ref, xs5_ref, cm_ref, *, col_major):
    x = x_ref[...]
    m = mod_ref[...]
    tm = x.shape[0]
    h2 = _rms_mod(x, n_ref[...], m[3:4], m[4:5]).astype(BF16)
    u_bf = None
    for c0 in range(0, NAT_W, 1024):
        p = jnp.dot(h2, wnat_ref[:, c0:c0 + 1024], preferred_element_type=F32).astype(BF16)
        nat_ref[:, c0:c0 + 1024] = p
        if c0 == 0:
            u_bf = p[:, 0:xs5_ref.shape[0] * S5_GROUP]
    pc = jnp.dot(h2, wcm_ref[...], preferred_element_type=F32).astype(BF16)
    if col_major:
        pcm = pc.astype(F32).reshape(tm // GRID_W, GRID_W, CM_W)
        cm_ref[...] = jnp.swapaxes(pcm, 0, 1)
    else:
        cm_ref[...] = pc.astype(F32)
    uf = u_bf.astype(F32).reshape(tm // S5_CHUNK, S5_CHUNK, u_bf.shape[1])
    up = jnp.swapaxes(uf, 0, 1).reshape(tm, u_bf.shape[1])

    def store(g, tt, v):
        xs5_ref[g, :, tt * LANES:(tt + 1) * LANES] = v.astype(BF16)

    _to_chunk_major(up, store)


def _proj(x1, mods3, mod_row, w, tm, col_major):
    b, l, d = x1.shape
    ng = w["dskip"].shape[-1] // S5_GROUP
    nj = tm // S5_CHUNK
    nblk = l // tm
    tok = lambda width: pl.BlockSpec((None, tm, width), lambda bi, i: (bi, i, 0))
    if col_major:
        cm_spec = pl.BlockSpec((None, GRID_W, tm // GRID_W, CM_W), lambda bi, i: (bi, 0, i, 0))
        cm_shape = jax.ShapeDtypeStruct((b, GRID_W, l // GRID_W, CM_W), F32)
    else:
        cm_spec = tok(CM_W)
        cm_shape = jax.ShapeDtypeStruct((b, l, CM_W), F32)
    return pl.pallas_call(
        functools.partial(_proj_kernel, col_major=col_major),
        grid=(b, nblk),
        in_specs=[tok(d), pl.BlockSpec((None, N_MOD, d), lambda bi, i: (mod_row(bi), 0, 0)),
                  _const_spec((1, d)), _const_spec((d, NAT_W)), _const_spec((d, CM_W))],
        out_specs=[tok(NAT_W),
                   pl.BlockSpec((ng, nj, S5_CHUNK * S5_GROUP), lambda bi, i: (0, bi * nblk + i, 0)),
                   cm_spec],
        out_shape=[jax.ShapeDtypeStruct((b, l, NAT_W), BF16),
                   jax.ShapeDtypeStruct((ng, b * (l // S5_CHUNK), S5_CHUNK * S5_GROUP), BF16),
                   cm_shape],
        compiler_params=_params("arbitrary", "arbitrary"),
    )(x1, mods3, w["n2"], w["wnat"], w["wcm"])


def _s5_tile_prefix(sr, si, cst, fwd):
    ar, ai = sr, si
    for lvl, dist in enumerate((1, 2, 4)):
        sh = dist if fwd else S5_TILE - dist
        rr, ri = pltpu.roll(ar, sh, 0), pltpu.roll(ai, sh, 0)
        lr, li = cst[2 * lvl], cst[2 * lvl + 1]
        ar, ai = ar + (lr * rr - li * ri), ai + (lr * ri + li * rr)
    return ar, ai


def _s5_tile_carry(ar, ai, hr, hi, cst, fwd):
    row = lax.broadcasted_iota(jnp.int32, ar.shape, 0)
    keep = (row >= 1) if fwd else (row <= S5_TILE - 2)
    sh = 1 if fwd else S5_TILE - 1
    pr = jnp.where(keep, pltpu.roll(ar, sh, 0), 0.0)
    pi = jnp.where(keep, pltpu.roll(ai, sh, 0), 0.0)
    lpr, lpi, l8r, l8i = cst[6:10]
    hin_r = lpr * hr - lpi * hi + pr
    hin_i = lpr * hi + lpi * hr + pi
    e = S5_TILE - 1 if fwd else 0
    er = jnp.broadcast_to(ar[e:e + 1, :], ar.shape)
    ei = jnp.broadcast_to(ai[e:e + 1, :], ai.shape)
    return hin_r, hin_i, l8r * hr - l8i * hi + er, l8r * hi + l8i * hr + ei


def _s5_kernel(xc_ref, xl_ref, kt_ref, ws_ref, cp_ref, lam_ref, y_ref, s_ref, *, nb):
    npair = ws_ref.shape[0] // 2
    p = S5_STATE
    lo = lax.broadcasted_iota(jnp.int32, (ws_ref.shape[1], LANES), 1) < p
    half_turn = lambda a: pltpu.roll(a, p, 1)

    def state_in_operator(g):
        blocks = []
        for k in range(2):
            a = ws_ref[g, :, k * LANES:(k + 1) * LANES]
            if g % 2 == 0:
                blocks += [jnp.where(lo, a, 0.0), jnp.where(lo, half_turn(a), 0.0)]
            else:
                blocks += [jnp.where(lo, 0.0, half_turn(a)), jnp.where(lo, 0.0, a)]
        return jnp.concatenate(blocks, axis=1).astype(BF16)

    def state_out_operator(pp):
        zeros = jnp.zeros((p, cp_ref.shape[2]), BF16)
        rows = []
        for part in range(4):
            for gi in range(2):
                c = cp_ref[2 * pp + gi, part * p:(part + 1) * p, :]
                rows.append(jnp.concatenate([c, zeros] if gi == 0 else [zeros, c], axis=1))
        return jnp.concatenate(rows, axis=0)

    jc = xc_ref.shape[1] // nb
    jl = xl_ref.shape[1] // nb
    jt = jc + jl
    for pp in range(npair):
        w_in = [state_in_operator(2 * pp), state_in_operator(2 * pp + 1)]
        for src, j0, nj in ((xc_ref, 0, jc), (xl_ref, jc, jl)):
            s = jnp.dot(src[2 * pp], w_in[0], preferred_element_type=F32)
            s = s + jnp.dot(src[2 * pp + 1], w_in[1], preferred_element_type=F32)
            for bi in range(nb):
                s_ref[pp, bi * jt + j0:bi * jt + j0 + nj, :] = s[bi * nj:(bi + 1) * nj, :]

    nct = jc // S5_TILE
    nt = jt // S5_TILE
    zero = jnp.zeros((S5_TILE, LANES), F32)
    for pp in range(npair):

        def prefix(it, carry, pp=pp):
            tiles = []
            for u in range(S5_PREFIX_TILES):
                r0 = pl.multiple_of((it * S5_PREFIX_TILES + u) * S5_TILE, S5_TILE)
                for c0, fwd in ((0, True), (256, False)):
                    tiles.append((r0, c0, fwd, s_ref[pp, pl.ds(r0, S5_TILE), c0:c0 + 128],
                                  s_ref[pp, pl.ds(r0, S5_TILE), c0 + 128:c0 + 256]))
            cst = {fwd: [lam_ref[pp, (0 if fwd else 10) + k] for k in range(6)] for fwd in (True, False)}
            done = [(r0, c0) + _s5_tile_prefix(sr, si, cst[fwd], fwd) for r0, c0, fwd, sr, si in tiles]
            for r0, c0, ar, ai in done:
                s_ref[pp, pl.ds(r0, S5_TILE), c0:c0 + 128] = ar
                s_ref[pp, pl.ds(r0, S5_TILE), c0 + 128:c0 + 256] = ai
            return carry

        lax.fori_loop(0, nb * nt // S5_PREFIX_TILES, prefix, 0)

        def step(it, carry, pp=pp):
            mb = jnp.where(it < nct, nct - 1 - it, nt - 1 - (it - nct))
            cst = {True: [lam_ref[pp, k] for k in range(10)], False: [lam_ref[pp, 10 + k] for k in range(10)]}
            tiles = []
            for bi in range(nb):
                for c0, fwd, tile in ((0, True, it), (256, False, mb)):
                    r0 = pl.multiple_of(bi * jt + tile * S5_TILE, S5_TILE)
                    tiles.append((bi, r0, c0, fwd, s_ref[pp, pl.ds(r0, S5_TILE), c0:c0 + 128],
                                  s_ref[pp, pl.ds(r0, S5_TILE), c0 + 128:c0 + 256]))
            out = [[None, None] for _ in range(nb)]
            for bi, r0, c0, fwd, ar, ai in tiles:
                hr, hi = carry[bi][0 if fwd else 1]
                hin_r, hin_i, hr, hi = _s5_tile_carry(ar, ai, hr, hi, cst[fwd], fwd)
                out[bi][0 if fwd else 1] = (hr, hi)
                s_ref[pp, pl.ds(r0, S5_TILE), c0:c0 + 128] = hin_r
                s_ref[pp, pl.ds(r0, S5_TILE), c0 + 128:c0 + 256] = hin_i
            return tuple(tuple(o) for o in out)

        lax.fori_loop(0, nt, step, tuple(((zero, zero), (zero, zero)) for _ in range(nb)))

    tc = S5_CHUNK * S5_GROUP
    width = kt_ref.shape[2]
    for pp in range(npair):
        toeps = []
        for gi in range(2):
            kt = kt_ref[2 * pp + gi]
            blocks = []
            for s in range(S5_CHUNK):
                sh = (width - S5_GROUP * (S5_CHUNK - 1 - s)) % width
                blocks.append((kt if sh == 0 else pltpu.roll(kt, sh, 1))[:, 0:tc])
            toeps.append(jnp.concatenate(blocks, axis=0).astype(BF16))
        w_out = state_out_operator(pp)
        for bi in range(nb):
            hin = s_ref[pp, bi * jt + jc:(bi + 1) * jt, :].astype(BF16)
            yp = jnp.dot(hin, w_out, preferred_element_type=F32)
            for gi in range(2):
                g = 2 * pp + gi
                y = yp[:, gi * 256:(gi + 1) * 256] + jnp.dot(xl_ref[g, bi * jl:(bi + 1) * jl, :], toeps[gi],
                                                             preferred_element_type=F32)
                y_ref[g, bi * jl:(bi + 1) * jl, :] = y.astype(BF16)


def _s5(xc, xl, kt, ws, cp, lamc, gps, nb):
    ng, rc, _ = xc.shape
    rl = xl.shape[1]
    npair = gps // 2
    return pl.pallas_call(
        functools.partial(_s5_kernel, nb=nb),
        grid=(ng // gps,),
        in_specs=[pl.BlockSpec((gps, rc, 256), lambda i: (i, 0, 0)),
                  pl.BlockSpec((gps, rl, 256), lambda i: (i, 0, 0)),
                  pl.BlockSpec((gps,) + kt.shape[1:], lambda i: (i, 0, 0)),
                  pl.BlockSpec((gps,) + ws.shape[1:], lambda i: (i, 0, 0)),
                  pl.BlockSpec((gps,) + cp.shape[1:], lambda i: (i, 0, 0)),
                  pl.BlockSpec((npair, 20, S5_TILE, LANES), lambda i: (i, 0, 0, 0))],
        out_specs=pl.BlockSpec((gps, rl, 256), lambda i: (i, 0, 0)),
        out_shape=jax.ShapeDtypeStruct((ng, rl, 256), BF16),
        scratch_shapes=[pltpu.VMEM((npair, rc + rl, 512), F32)],
        compiler_params=_params("arbitrary"),
    )(xc, xl, kt, ws, cp, lamc)


def _cmul(ar, ai, br, bi):
    return ar * br - ai * bi, ar * bi + ai * br


def _s5_weights(lam_re, lam_im, log_dt, b_re, b_im, c_re, c_im):
    nd, ng, p = lam_re.shape
    t = S5_CHUNK
    tc = t * S5_GROUP
    lr = jnp.minimum(lam_re.astype(F32), -1e-4)
    li = lam_im.astype(F32)
    dt = jnp.exp(log_dt.astype(F32))[..., None]
    mag = jnp.exp(lr * dt)
    lbr, lbi = mag * jnp.cos(li * dt), mag * jnp.sin(li * dt)
    den = lr * lr + li * li
    fr = ((lbr - 1.0) * lr + lbi * li) / den
    fi = (lbi * lr - (lbr - 1.0) * li) / den
    bbr, bbi = _cmul(fr[..., None], fi[..., None], b_re.astype(F32), b_im.astype(F32))
    cr, ci = c_re.astype(F32), c_im.astype(F32)

    pr, pi = [jnp.ones_like(lbr)], [jnp.zeros_like(lbr)]
    for _ in range(t):
        nr, ni = _cmul(pr[-1], pi[-1], lbr, lbi)
        pr.append(nr)
        pi.append(ni)
    pr, pi = jnp.stack(pr), jnp.stack(pi)
    wr, wi = _cmul(pr[:t, ..., None], pi[:t, ..., None], bbr[None], bbi[None])

    kern = jnp.einsum("dgxp,kdgpc->dgkxc", cr, wr) - jnp.einsum("dgxp,kdgpc->dgkxc", ci, wi)
    ktf = kern[0].transpose(0, 3, 1, 2)
    ktb = kern[1][:, ::-1].transpose(0, 3, 1, 2)
    kt = jnp.concatenate([ktb[:, :, :t - 1], ktf[:, :, 0:1] + ktb[:, :, t - 1:t], ktf[:, :, 1:]], axis=2)
    kt = kt.reshape(ng, S5_GROUP, (2 * t - 1) * S5_GROUP)
    kt = jnp.pad(kt, ((0, 0), (0, 0), (0, 2 * tc - kt.shape[-1])))

    gscp = lambda a: a.transpose(1, 0, 3, 2).reshape(ng, tc, p)
    parts = [gscp(wr[::-1, 0]), gscp(wi[::-1, 0]), gscp(wr[:, 1]), gscp(wi[:, 1])]
    ws = jnp.concatenate(parts, axis=-1)

    def readout(d, powr, powi):
        mr, mi = _cmul(cr[d][None], ci[d][None], powr[:, :, None, :], powi[:, :, None, :])
        to_gptx = lambda a: a.transpose(1, 3, 0, 2).reshape(ng, p, tc)
        return [to_gptx(mr), to_gptx(-mi)]

    parts = readout(0, pr[1:t + 1, 0], pi[1:t + 1, 0]) + readout(1, pr[1:t + 1, 1][::-1], pi[1:t + 1, 1][::-1])
    cp = jnp.concatenate(parts, axis=1)

    row = jnp.arange(S5_TILE)
    l1 = (pr[t], pi[t])
    l2 = _cmul(*l1, *l1)
    l4 = _cmul(*l2, *l2)
    l8 = _cmul(*l4, *l4)
    rp = [(jnp.ones_like(lbr), jnp.zeros_like(lbr))]
    for _ in range(S5_TILE - 1):
        rp.append(_cmul(*rp[-1], *l1))
    planes = []
    for d in range(nd):
        valid = (lambda dist: row >= dist) if d == 0 else (lambda dist: row <= S5_TILE - 1 - dist)
        for (qr, qi), dist in ((l1, 1), (l2, 2), (l4, 4)):
            m = valid(dist).astype(F32)[:, None, None]
            planes += [m * qr[d][None], m * qi[d][None]]
        order = row if d == 0 else row[::-1]
        planes += [jnp.stack([rp[k][0][d] for k in range(S5_TILE)])[order],
                   jnp.stack([rp[k][1][d] for k in range(S5_TILE)])[order]]
        planes += [jnp.broadcast_to(l8[0][d], (S5_TILE, ng, p)), jnp.broadcast_to(l8[1][d], (S5_TILE, ng, p))]
    lamc = jnp.stack(planes)
    lamc = lamc.reshape(20, S5_TILE, ng // 2, 2 * p).transpose(2, 0, 1, 3)
    return kt, ws, cp.astype(BF16), lamc


def _chunk_cumsum(g, rev):
    sub = 8
    nt = g.shape[0] // sub
    row = lax.broadcasted_iota(jnp.int32, (sub, g.shape[1]), 0)
    out = [None] * nt
    off = None
    for kk in (range(nt - 1, -1, -1) if rev else range(nt)):
        x = g[kk * sub:(kk + 1) * sub, :]
        for dist in (1, 2, 4):
            if rev:
                x = x + jnp.where(row < sub - dist, pltpu.roll(x, sub - dist, 0), 0.0)
            else:
                x = x + jnp.where(row >= dist, pltpu.roll(x, dist, 0), 0.0)
        if off is not None:
            x = x + off
        out[kk] = x
        e = 0 if rev else sub - 1
        off = jnp.broadcast_to(x[e:e + 1, :], x.shape)
    return jnp.concatenate(out, axis=0)


def _gla_log_decay(glr, gup, gbias, rev):
    c = GLA_CHUNK
    z = jnp.dot(glr.astype(BF16), gup, preferred_element_type=F32) + gbias
    g = (jnp.minimum(z, 0.0) - jnp.log(1.0 + jnp.exp(-jnp.abs(z)))) * (1.0 / GLA_GATE_NORM)
    return [_chunk_cumsum(g[n * c:(n + 1) * c, :], rev) for n in range(g.shape[0] // c)]


def _gla_chunk(q, k, v, gc, st, rev, need_out):
    c = GLA_CHUNK
    i_ref = c // 2 - 1 if rev else c // 2
    i_last = 0 if rev else c - 1
    g_ref = gc[i_ref:i_ref + 1, :]
    g_last = gc[i_last:i_last + 1, :]
    lane_head = lax.broadcasted_iota(jnp.int32, (1, GLA_KEY), 1) // GLA_DK

    stack = lambda a: jnp.concatenate(
        [jnp.where(lane_head == h, a, 0.0) for h in range(GLA_HEADS)], axis=0).astype(BF16)
    dv2 = 2 * GLA_DV
    vt = jnp.concatenate(
        [jnp.concatenate([v[:, p * dv2:p * dv2 + GLA_DV], v[:, p * dv2 + GLA_DV:(p + 1) * dv2]], axis=0).T
         for p in range(GLA_HEADS // 2)], axis=1).astype(BF16)
    kl = k * jnp.exp(g_last - gc)
    kv = jnp.dot(vt, stack(kl), preferred_element_type=F32)
    st_new = st * jnp.exp(g_last) + kv
    if not need_out:
        return None, st_new

    qe = q * jnp.exp(gc - g_ref)
    ke = (k * jnp.exp(g_ref - gc)).astype(BF16)
    qg = q * jnp.exp(gc)
    nt_dims = (((1,), (1,)), ((), ()))
    sc = lax.dot_general(stack(qe), ke, nt_dims, preferred_element_type=F32)
    rs = lax.broadcasted_iota(jnp.int32, (GLA_HEADS * c, c), 0) % c
    cs = lax.broadcasted_iota(jnp.int32, (GLA_HEADS * c, c), 1)
    keep = (rs <= cs) if rev else (rs >= cs)
    sc = jnp.where(keep, sc, 0.0).astype(BF16)
    vb = v.astype(BF16)
    oo = lax.dot_general(stack(qg), st.astype(BF16), nt_dims, preferred_element_type=F32)
    o = jnp.concatenate(
        [jnp.dot(sc[h * c:(h + 1) * c, :], vb[:, h * GLA_DV:(h + 1) * GLA_DV], preferred_element_type=F32)
         + oo[h * c:(h + 1) * c, :] for h in range(GLA_HEADS)], axis=1)
    return o, st_new


def _gla_kernel(cf_ref, cb_ref, cc_ref, gup_ref, gbias_ref, of_ref, ob_ref, stf_ref, stb_ref):
    c = GLA_CHUNK
    q0, k0, v0, r0 = 0, GLA_KEY, 2 * GLA_KEY, 2 * GLA_KEY + GLA_VAL

    def decay(ref, d):
        return _gla_log_decay(ref[:, r0:CM_W], gup_ref[d], gbias_ref[d], d == 1)

    def run(ref, n, gcs, d, st, need_out):
        rows = slice(n * c, (n + 1) * c)
        q = ref[rows, q0:k0] if need_out else None
        return _gla_chunk(q, ref[rows, k0:v0], ref[rows, v0:r0], gcs[n], st, d == 1, need_out)

    @pl.when(pl.program_id(1) == 0)
    def _():
        nctx = cc_ref.shape[0] // c
        gf, gb = decay(cc_ref, 0), decay(cc_ref, 1)
        stf = jnp.zeros(stf_ref.shape, F32)
        stb = jnp.zeros(stb_ref.shape, F32)
        for n in range(nctx):
            _, stf = run(cc_ref, n, gf, 0, stf, False)
            _, stb = run(cc_ref, nctx - 1 - n, gb, 1, stb, False)
        stf_ref[...] = stf
        stb_ref[...] = stb

    ncols, rows_per_col, _ = cf_ref.shape
    nch = rows_per_col // c
    stf = stf_ref[...]
    stb = stb_ref[...]
    for j in range(ncols):
        jb = ncols - 1 - j
        gf, gb = decay(cf_ref.at[j], 0), decay(cb_ref.at[jb], 1)
        for n in range(nch):
            o, stf = run(cf_ref.at[j], n, gf, 0, stf, True)
            of_ref[j, n * c:(n + 1) * c, :] = o
            m = nch - 1 - n
            o, stb = run(cb_ref.at[jb], m, gb, 1, stb, True)
            ob_ref[jb, m * c:(m + 1) * c, :] = o
    stf_ref[...] = stf
    stb_ref[...] = stb


def _gla(cm, cmc, gup, gbias):
    b, ncol, rows, _ = cm.shape
    lc = cmc.shape[1]
    cps = GLA_COLS_PER_STEP
    last = ncol // cps - 1
    col = lambda width, fn: pl.BlockSpec((None, cps, rows, width), fn)
    out_shape = jax.ShapeDtypeStruct((b, ncol, rows, GLA_VAL), F32)
    return pl.pallas_call(
        _gla_kernel,
        grid=(b, ncol // cps),
        in_specs=[col(CM_W, lambda bi, ci: (bi, ci, 0, 0)),
                  col(CM_W, lambda bi, ci: (bi, last - ci, 0, 0)),
                  pl.BlockSpec((None, lc, CM_W), lambda bi, ci: (bi, 0, 0)),
                  _const_spec(gup.shape), _const_spec(gbias.shape)],
        out_specs=[col(GLA_VAL, lambda bi, ci: (bi, ci, 0, 0)),
                   col(GLA_VAL, lambda bi, ci: (bi, last - ci, 0, 0))],
        out_shape=[out_shape, out_shape],
        scratch_shapes=[pltpu.VMEM((GLA_DV, GLA_KEY), F32), pltpu.VMEM((GLA_DV, GLA_KEY), F32)],
        compiler_params=_params("arbitrary", "arbitrary"),
    )(cm, cm, cmc, gup, gbias)


def _back_kernel(x1_ref, nat_ref, y_ref, of_ref, ob_ref, mod_ref, dskip_ref, gluw_ref,
                 glub_ref, s5out_ref, gnorm_ref, glaout_ref, wo_ref, n3_ref, wg_ref, wu_ref, wd_ref, fin_ref,
                 out_ref, *, fchunk):
    m = mod_ref[...]
    tm, d = x1_ref.shape
    ng, nj, _ = y_ref.shape
    s5w = ng * S5_GROUP

    yp = _from_chunk_major(lambda g, tt: y_ref[g, :, tt * LANES:(tt + 1) * LANES].astype(F32), ng, nj)
    ys = jnp.swapaxes(yp.reshape(S5_CHUNK, nj, s5w), 0, 1).reshape(tm, s5w)
    ya = ys + dskip_ref[...] * nat_ref[:, 0:s5w].astype(F32)
    ya = 0.5 * ya * (1.0 + jnp.tanh(0.7978845608028654 * (ya + 0.044715 * (ya * ya * ya))))
    gl = jnp.dot(ya.astype(BF16), gluw_ref[...], preferred_element_type=F32) + glub_ref[...]
    ya = ya * jax.nn.sigmoid(gl)

    ocm = (of_ref[...] + ob_ref[...]).astype(BF16).astype(F32)
    o = jnp.swapaxes(ocm, 0, 1).reshape(tm, GLA_VAL)
    heads = []
    for h in range(GLA_HEADS):
        oh = o[:, h * GLA_DV:(h + 1) * GLA_DV]
        heads.append(oh * lax.rsqrt(jnp.mean(oh * oh, axis=-1, keepdims=True) + RMS_EPS))
    r = nat_ref[:, s5w:s5w + GLA_VAL].astype(F32)
    yb = jnp.concatenate(heads, axis=1) * gnorm_ref[...] * (r * jax.nn.sigmoid(r))

    pa = jnp.dot(ya.astype(BF16), s5out_ref[...], preferred_element_type=F32)
    pb = jnp.dot(yb.astype(BF16), glaout_ref[...], preferred_element_type=F32)
    ga = nat_ref[:, s5w + GLA_VAL:s5w + GLA_VAL + d].astype(F32)
    gb = nat_ref[:, s5w + GLA_VAL + d:s5w + GLA_VAL + 2 * d].astype(F32)
    mg = jax.nn.sigmoid(ga) * pa + jax.nn.sigmoid(gb) * pb
    y = jnp.dot(mg.astype(BF16), wo_ref[...], preferred_element_type=F32)
    x2 = x1_ref[...] + m[5:6] * y
    h = _rms_mod(x2, n3_ref[...], m[6:7], m[7:8]).astype(BF16)
    x3 = _swiglu_residual(x2, h, m[8:9], wg_ref, wu_ref, wd_ref, fchunk)
    ms = jnp.mean(x3 * x3, axis=-1, keepdims=True)
    out_ref[...] = x3 * lax.rsqrt(ms + RMS_EPS) * fin_ref[...]


def _back(x1, nat, ys5, of, ob, mods3, w, tm, fchunk):
    b, l, d = x1.shape
    f = w["wg2"].shape[1]
    s5w = w["dskip"].shape[-1]
    ng = s5w // S5_GROUP
    nblk = l // tm
    tok = lambda width: pl.BlockSpec((None, tm, width), lambda bi, i: (bi, i, 0))
    colblk = pl.BlockSpec((None, GRID_W, tm // GRID_W, GLA_VAL), lambda bi, i: (bi, 0, i, 0))
    return pl.pallas_call(
        functools.partial(_back_kernel, fchunk=fchunk),
        grid=(b, nblk),
        in_specs=[tok(d), tok(NAT_W),
                  pl.BlockSpec((ng, tm // S5_CHUNK, S5_CHUNK * S5_GROUP), lambda bi, i: (0, bi * nblk + i, 0)),
                  colblk, colblk,
                  pl.BlockSpec((None, N_MOD, d), lambda bi, i: (bi, 0, 0)),
                  _const_spec((1, s5w)), _const_spec((s5w, s5w)), _const_spec((1, s5w)),
                  _const_spec((s5w, d)), _const_spec((1, GLA_VAL)), _const_spec((GLA_VAL, d)),
                  _const_spec((d, d)), _const_spec((1, d)), _const_spec((d, f)), _const_spec((d, f)),
                  _const_spec((f, d)), _const_spec((1, d))],
        out_specs=tok(d),
        out_shape=jax.ShapeDtypeStruct((b, l, d), F32),
        compiler_params=_params("arbitrary", "arbitrary"),
    )(x1, nat, ys5, of, ob, mods3, w["dskip"], w["gluw"], w["glub"], w["s5out"], w["gnorm"], w["glaout"],
      w["wo"], w["n3"], w["wg2"], w["wu2"], w["wd2"], w["fin"])


def kernel(x, c, ctx, c_ctx, ada_w, ada_b, ffn1_norm, ffn1_w_gate, ffn1_w_up, ffn1_w_down, mix_norm, w_in,
           s5_lambda_re, s5_lambda_im, s5_log_dt, s5_b_re, s5_b_im, s5_c_re, s5_c_im, s5_d, s5_glu_w, s5_glu_b,
           s5_out, gla_gate_up, gla_gate_b, gla_norm, gla_out, w_o, ffn2_norm, ffn2_w_gate, ffn2_w_up,
           ffn2_w_down, final_norm):
    b, l, d = x.shape
    lc = ctx.shape[1]
    assert ada_w.shape[0] == 1 and b + 1 <= 8
    s5w = s5_d.shape[-1]
    fchunk = 256
    tm = min(512, l)
    assert tm % (GRID_W * 8) == 0 and l % tm == 0 and lc % (S5_CHUNK * S5_TILE) == 0 and lc % GLA_CHUNK == 0

    cvec = jnp.concatenate([c, c_ctx[None, :], jnp.zeros((8 - b - 1, d), F32)], axis=0)
    mods3 = _ada(cvec, ada_w[0], ada_b[0]).reshape(8, N_MOD, d)

    wi = w_in[0]
    o_q, o_k, o_v, o_r = s5w, s5w + GLA_KEY, s5w + 2 * GLA_KEY, s5w + 2 * GLA_KEY + GLA_VAL
    o_glr = o_r + GLA_VAL
    o_ga = o_glr + 2 * GLA_GATE_RANK
    wnat = jnp.concatenate([wi[:, :o_q], wi[:, o_r:o_glr], wi[:, o_ga:]], axis=1)
    wcm = jnp.concatenate([wi[:, o_q:o_k] * (GLA_DK ** -0.5), wi[:, o_k:o_r],
                           jnp.pad(wi[:, o_glr:o_ga], ((0, 0), (0, LANES - 2 * GLA_GATE_RANK)))], axis=1)
    gup = jnp.zeros((2, LANES, GLA_KEY), F32)
    gup = gup.at[0, 0:GLA_GATE_RANK].set(gla_gate_up[0, 0])
    gup = gup.at[1, GLA_GATE_RANK:2 * GLA_GATE_RANK].set(gla_gate_up[0, 1])
    row = lambda v: v.reshape(1, -1).astype(F32)
    w = dict(
        n1=row(ffn1_norm[0]), wg1=ffn1_w_gate[0].astype(BF16), wu1=ffn1_w_up[0].astype(BF16),
        wd1=ffn1_w_down[0].astype(BF16), n2=row(mix_norm[0]), wnat=wnat.astype(BF16), wcm=wcm.astype(BF16),
        dskip=row(s5_d[0]), gluw=s5_glu_w[0].astype(BF16), glub=row(s5_glu_b[0]), s5out=s5_out[0].astype(BF16),
        gnorm=row(gla_norm[0]), glaout=gla_out[0].astype(BF16), wo=w_o[0].astype(BF16),
        n3=row(ffn2_norm[0]), wg2=ffn2_w_gate[0].astype(BF16), wu2=ffn2_w_up[0].astype(BF16),
        wd2=ffn2_w_down[0].astype(BF16), fin=row(final_norm))

    lat_row = lambda bi: bi
    ctx_row = lambda bi: b
    x1 = _ffn(x, mods3, lat_row, w["n1"], w["wg1"], w["wu1"], w["wd1"], tm, fchunk, 0)
    tc = min(tm, b * lc)
    c1 = _ffn(ctx.reshape(1, b * lc, d), mods3, ctx_row, w["n1"], w["wg1"], w["wu1"], w["wd1"], tc, fchunk, 0)
    nat, xl, cm = _proj(x1, mods3, lat_row, w, min(2 * tm, l), True)
    _, xc, cmc = _proj(c1, mods3, ctx_row, w, tc, False)
    cmc = cmc.reshape(b, lc, CM_W)

    kt, ws, cp, lamc = _s5_weights(s5_lambda_re[0], s5_lambda_im[0], s5_log_dt[0], s5_b_re[0], s5_b_im[0],
                                     s5_c_re[0], s5_c_im[0])
    ys5 = _s5(xc, xl, kt, ws, cp, lamc, gps=4, nb=b)

    of, ob = _gla(cm, cmc, gup.astype(BF16), gla_gate_b[0].reshape(2, 1, GLA_KEY).astype(F32))
    return _back(x1, nat, ys5, of, ob, mods3, w, tm, fchunk)
```

```python
import functools

import jax
import jax.numpy as jnp
from jax import lax
from jax.experimental import pallas as pl
from jax.experimental.pallas import tpu as pltpu

F32 = jnp.float32
BF16 = jnp.bfloat16

RMS_EPS = 1e-6
MACARON_WEIGHT = 0.5
GRID_W = 64
N_MOD = 9
S5_GROUP = 16
S5_STATE = 64
S5_CHUNK = 16
S5_TILE = 8
S5_PREFIX_TILES = 4
GLA_HEADS = 4
GLA_DK = 64
GLA_DV = 128
GLA_CHUNK = 64
GLA_GATE_RANK = 16
GLA_GATE_NORM = 16.0
GLA_COLS_PER_STEP = 8
GLA_KEY = GLA_HEADS * GLA_DK
GLA_VAL = GLA_HEADS * GLA_DV
LANES = 128
SUBLANES = 8
MXU_TILE = 256
GELU_C0 = 0.7978845608028654
GELU_C1 = 0.044715
NAT_W = 3072
CM_W = 2 * GLA_KEY + GLA_VAL + LANES
V7X_VMEM_LIMIT_BYTES = 56 * 1024 * 1024


def _params(*sem):
    return pltpu.CompilerParams(dimension_semantics=sem, vmem_limit_bytes=V7X_VMEM_LIMIT_BYTES)


def _const_spec(shape):
    nd = len(shape)
    return pl.BlockSpec(shape, lambda *_: (0,) * nd, pipeline_mode=pl.Buffered(1))


def _rms_mod(x, g, shift, scale):
    ms = jnp.mean(x * x, axis=-1, keepdims=True)
    return (x * lax.rsqrt(ms + RMS_EPS) * g) * (1.0 + scale) + shift


def _swiglu_residual(x, h, gate, wg_ref, wu_ref, wd_ref, fchunk):
    acc = None
    for f0 in range(0, wg_ref.shape[1], fchunk):
        gg = jnp.dot(h, wg_ref[:, f0:f0 + fchunk], preferred_element_type=F32)
        uu = jnp.dot(h, wu_ref[:, f0:f0 + fchunk], preferred_element_type=F32)
        a = (gg * jax.nn.sigmoid(gg) * uu).astype(BF16)
        o = jnp.dot(a, wd_ref[f0:f0 + fchunk, :], preferred_element_type=F32)
        acc = o if acc is None else acc + o
    return x + gate * (MACARON_WEIGHT * acc)


def _ada_kernel(c_ref, w_ref, b_ref, o_ref):
    cv = c_ref[...]
    s = cv * jax.nn.sigmoid(cv)
    o_ref[...] = jnp.dot(s, w_ref[...], preferred_element_type=F32,
                         precision=lax.Precision.HIGHEST) + b_ref[...]


def _ada(cvec, ada_w, ada_b):
    rows, d = cvec.shape
    n = ada_w.shape[1]
    bn = n // N_MOD if n % N_MOD == 0 else n
    return pl.pallas_call(
        _ada_kernel,
        grid=(n // bn,),
        in_specs=[pl.BlockSpec((rows, d), lambda j: (0, 0)),
                  pl.BlockSpec((d, bn), lambda j: (0, j)),
                  pl.BlockSpec((1, bn), lambda j: (0, j))],
        out_specs=pl.BlockSpec((rows, bn), lambda j: (0, j)),
        out_shape=jax.ShapeDtypeStruct((rows, n), F32),
        compiler_params=_params("arbitrary"),
    )(cvec, ada_w, ada_b.reshape(1, n))


def _ffn_kernel(x_ref, mod_ref, n_ref, wg_ref, wu_ref, wd_ref, o_ref, *, fchunk, mod0):
    x = x_ref[...]
    m = mod_ref[...]
    h = _rms_mod(x, n_ref[...], m[mod0:mod0 + 1], m[mod0 + 1:mod0 + 2]).astype(BF16)
    o_ref[...] = _swiglu_residual(x, h, m[mod0 + 2:mod0 + 3], wg_ref, wu_ref, wd_ref, fchunk)


def _ffn(x, mods3, mod_row, norm, wg, wu, wd, tm, fchunk, mod0):
    b, l, d = x.shape
    f = wg.shape[1]
    tok = pl.BlockSpec((None, tm, d), lambda bi, i: (bi, i, 0))
    return pl.pallas_call(
        functools.partial(_ffn_kernel, fchunk=fchunk, mod0=mod0),
        grid=(b, l // tm),
        in_specs=[tok, pl.BlockSpec((None, N_MOD, d), lambda bi, i: (mod_row(bi), 0, 0)),
                  _const_spec((1, d)), _const_spec((d, f)), _const_spec((d, f)), _const_spec((f, d))],
        out_specs=tok,
        out_shape=jax.ShapeDtypeStruct((b, l, d), F32),
        compiler_params=_params("arbitrary", "arbitrary"),
    )(x, mods3, norm, wg, wu, wd)


def _transpose_pieces(v):
    n = len(v)
    piece = lax.broadcasted_iota(jnp.int32, v[0].shape, 1) // S5_GROUP
    s = n // 2
    while s >= 1:
        keep = (piece & s) == 0
        nv = list(v)
        for i in range(n):
            if i & s == 0:
                a, b = v[i], v[i + s]
                nv[i] = jnp.where(keep, a, pltpu.roll(b, s * S5_GROUP, 1))
                nv[i + s] = jnp.where(keep, pltpu.roll(a, LANES - s * S5_GROUP, 1), b)
        v = nv
        s //= 2
    return v


def _to_chunk_major(up, store):
    nj = up.shape[0] // S5_CHUNK
    npc = LANES // S5_GROUP
    for gg in range(up.shape[1] // LANES):
        for tt in range(S5_CHUNK // npc):
            src = [up[(npc * tt + p) * nj:(npc * tt + p + 1) * nj, gg * LANES:(gg + 1) * LANES] for p in range(npc)]
            for gl, v in enumerate(_transpose_pieces(src)):
                store(gg * npc + gl, tt, v)


def _from_chunk_major(load, ng, nj):
    npc = LANES // S5_GROUP
    row_blocks = []
    for tt in range(S5_CHUNK // npc):
        per_p = [[] for _ in range(npc)]
        for gg in range(ng // npc):
            out = _transpose_pieces([load(gg * npc + gl, tt) for gl in range(npc)])
            for p in range(npc):
                per_p[p].append(out[p])
        row_blocks += [jnp.concatenate(blk, axis=1) for blk in per_p]
    return jnp.concatenate(row_blocks, axis=0)


def _proj_kernel(x_ref, mod_ref, n_ref, wnat_ref, wcm_ref, nat_ref, xs5_ref, cm_ref, *, col_major):
    x = x_ref[...]
    m = mod_ref[...]
    tm = x.shape[0]
    h2 = _rms_mod(x, n_ref[...], m[3:4], m[4:5]).astype(BF16)
    u_bf = None
    ncol = 4 * MXU_TILE
    for c0 in range(0, NAT_W, ncol):
        p = jnp.dot(h2, wnat_ref[:, c0:c0 + ncol], preferred_element_type=F32).astype(BF16)
        nat_ref[:, c0:c0 + ncol] = p
        if c0 == 0:
            u_bf = p[:, 0:xs5_ref.shape[0] * S5_GROUP]
    pc = jnp.dot(h2, wcm_ref[...], preferred_element_type=F32).astype(BF16)
    if col_major:
        pcm = pc.astype(F32).reshape(tm // GRID_W, GRID_W, CM_W)
        cm_ref[...] = jnp.swapaxes(pcm, 0, 1)
    else:
        cm_ref[...] = pc.astype(F32)
    uf = u_bf.astype(F32).reshape(tm // S5_CHUNK, S5_CHUNK, u_bf.shape[1])
    up = jnp.swapaxes(uf, 0, 1).reshape(tm, u_bf.shape[1])

    def store(g, tt, v):
        xs5_ref[g, :, tt * LANES:(tt + 1) * LANES] = v.astype(BF16)

    _to_chunk_major(up, store)


def _proj(x1, mods3, mod_row, w, tm, col_major):
    b, l, d = x1.shape
    ng = w["dskip"].shape[-1] // S5_GROUP
    nj = tm // S5_CHUNK
    nblk = l // tm
    tok = lambda width: pl.BlockSpec((None, tm, width), lambda bi, i: (bi, i, 0))
    if col_major:
        cm_spec = pl.BlockSpec((None, GRID_W, tm // GRID_W, CM_W), lambda bi, i: (bi, 0, i, 0))
        cm_shape = jax.ShapeDtypeStruct((b, GRID_W, l // GRID_W, CM_W), F32)
    else:
        cm_spec = tok(CM_W)
        cm_shape = jax.ShapeDtypeStruct((b, l, CM_W), F32)
    return pl.pallas_call(
        functools.partial(_proj_kernel, col_major=col_major),
        grid=(b, nblk),
        in_specs=[tok(d), pl.BlockSpec((None, N_MOD, d), lambda bi, i: (mod_row(bi), 0, 0)),
                  _const_spec((1, d)), _const_spec((d, NAT_W)), _const_spec((d, CM_W))],
        out_specs=[tok(NAT_W),
                   pl.BlockSpec((ng, nj, S5_CHUNK * S5_GROUP), lambda bi, i: (0, bi * nblk + i, 0)),
                   cm_spec],
        out_shape=[jax.ShapeDtypeStruct((b, l, NAT_W), BF16),
                   jax.ShapeDtypeStruct((ng, b * (l // S5_CHUNK), S5_CHUNK * S5_GROUP), BF16),
                   cm_shape],
        compiler_params=_params("arbitrary", "arbitrary"),
    )(x1, mods3, w["n2"], w["wnat"], w["wcm"])


def _s5_tile_prefix(sr, si, cst, fwd):
    ar, ai = sr, si
    for lvl, dist in enumerate((1, 2, 4)):
        sh = dist if fwd else S5_TILE - dist
        rr, ri = pltpu.roll(ar, sh, 0), pltpu.roll(ai, sh, 0)
        lr, li = cst[2 * lvl], cst[2 * lvl + 1]
        ar, ai = ar + (lr * rr - li * ri), ai + (lr * ri + li * rr)
    return ar, ai


def _s5_tile_carry(ar, ai, hr, hi, cst, fwd):
    row = lax.broadcasted_iota(jnp.int32, ar.shape, 0)
    keep = (row >= 1) if fwd else (row <= S5_TILE - 2)
    sh = 1 if fwd else S5_TILE - 1
    pr = jnp.where(keep, pltpu.roll(ar, sh, 0), 0.0)
    pi = jnp.where(keep, pltpu.roll(ai, sh, 0), 0.0)
    lpr, lpi, l8r, l8i = cst[6:10]
    hin_r = lpr * hr - lpi * hi + pr
    hin_i = lpr * hi + lpi * hr + pi
    e = S5_TILE - 1 if fwd else 0
    er = jnp.broadcast_to(ar[e:e + 1, :], ar.shape)
    ei = jnp.broadcast_to(ai[e:e + 1, :], ai.shape)
    return hin_r, hin_i, l8r * hr - l8i * hi + er, l8r * hi + l8i * hr + ei


def _s5_kernel(xc_ref, xl_ref, kt_ref, ws_ref, cp_ref, lam_ref, y_ref, s_ref, *, nb):
    npair = ws_ref.shape[0] // 2
    p = S5_STATE
    lo = lax.broadcasted_iota(jnp.int32, (ws_ref.shape[1], LANES), 1) < p
    half_turn = lambda a: pltpu.roll(a, p, 1)

    def state_in_operator(g):
        blocks = []
        for k in range(2):
            a = ws_ref[g, :, k * LANES:(k + 1) * LANES]
            if g % 2 == 0:
                blocks += [jnp.where(lo, a, 0.0), jnp.where(lo, half_turn(a), 0.0)]
            else:
                blocks += [jnp.where(lo, 0.0, half_turn(a)), jnp.where(lo, 0.0, a)]
        return jnp.concatenate(blocks, axis=1).astype(BF16)

    def state_out_operator(pp):
        zeros = jnp.zeros((p, cp_ref.shape[2]), BF16)
        rows = []
        for part in range(4):
            for gi in range(2):
                c = cp_ref[2 * pp + gi, part * p:(part + 1) * p, :]
                rows.append(jnp.concatenate([c, zeros] if gi == 0 else [zeros, c], axis=1))
        return jnp.concatenate(rows, axis=0)

    jc = xc_ref.shape[1] // nb
    jl = xl_ref.shape[1] // nb
    jt = jc + jl
    for pp in range(npair):
        w_in = [state_in_operator(2 * pp), state_in_operator(2 * pp + 1)]
        for src, j0, nj in ((xc_ref, 0, jc), (xl_ref, jc, jl)):
            s = jnp.dot(src[2 * pp], w_in[0], preferred_element_type=F32)
            s = s + jnp.dot(src[2 * pp + 1], w_in[1], preferred_element_type=F32)
            for bi in range(nb):
                s_ref[pp, bi * jt + j0:bi * jt + j0 + nj, :] = s[bi * nj:(bi + 1) * nj, :]

    nct = jc // S5_TILE
    nt = jt // S5_TILE
    zero = jnp.zeros((S5_TILE, LANES), F32)
    for pp in range(npair):

        def prefix(it, carry, pp=pp):
            tiles = []
            for u in range(S5_PREFIX_TILES):
                r0 = pl.multiple_of((it * S5_PREFIX_TILES + u) * S5_TILE, S5_TILE)
                for c0, fwd in ((0, True), (256, False)):
                    tiles.append((r0, c0, fwd, s_ref[pp, pl.ds(r0, S5_TILE), c0:c0 + 128],
                                  s_ref[pp, pl.ds(r0, S5_TILE), c0 + 128:c0 + 256]))
            cst = {fwd: [lam_ref[pp, (0 if fwd else 10) + k] for k in range(6)] for fwd in (True, False)}
            done = [(r0, c0) + _s5_tile_prefix(sr, si, cst[fwd], fwd) for r0, c0, fwd, sr, si in tiles]
            for r0, c0, ar, ai in done:
                s_ref[pp, pl.ds(r0, S5_TILE), c0:c0 + 128] = ar
                s_ref[pp, pl.ds(r0, S5_TILE), c0 + 128:c0 + 256] = ai
            return carry

        lax.fori_loop(0, nb * nt // S5_PREFIX_TILES, prefix, 0)

        def step(it, carry, pp=pp):
            mb = jnp.where(it < nct, nct - 1 - it, nt - 1 - (it - nct))
            cst = {True: [lam_ref[pp, k] for k in range(10)], False: [lam_ref[pp, 10 + k] for k in range(10)]}
            tiles = []
            for bi in range(nb):
                for c0, fwd, tile in ((0, True, it), (256, False, mb)):
                    r0 = pl.multiple_of(bi * jt + tile * S5_TILE, S5_TILE)
                    tiles.append((bi, r0, c0, fwd, s_ref[pp, pl.ds(r0, S5_TILE), c0:c0 + 128],
                                  s_ref[pp, pl.ds(r0, S5_TILE), c0 + 128:c0 + 256]))
            out = [[None, None] for _ in range(nb)]
            for bi, r0, c0, fwd, ar, ai in tiles:
                hr, hi = carry[bi][0 if fwd else 1]
                hin_r, hin_i, hr, hi = _s5_tile_carry(ar, ai, hr, hi, cst[fwd], fwd)
                out[bi][0 if fwd else 1] = (hr, hi)
                s_ref[pp, pl.ds(r0, S5_TILE), c0:c0 + 128] = hin_r
                s_ref[pp, pl.ds(r0, S5_TILE), c0 + 128:c0 + 256] = hin_i
            return tuple(tuple(o) for o in out)

        lax.fori_loop(0, nt, step, tuple(((zero, zero), (zero, zero)) for _ in range(nb)))

    tc = S5_CHUNK * S5_GROUP
    width = kt_ref.shape[2]
    for pp in range(npair):
        toeps = []
        for gi in range(2):
            kt = kt_ref[2 * pp + gi]
            blocks = []
            for s in range(S5_CHUNK):
                sh = (width - S5_GROUP * (S5_CHUNK - 1 - s)) % width
                blocks.append((kt if sh == 0 else pltpu.roll(kt, sh, 1))[:, 0:tc])
            toeps.append(jnp.concatenate(blocks, axis=0).astype(BF16))
        w_out = state_out_operator(pp)
        for bi in range(nb):
            hin = s_ref[pp, bi * jt + jc:(bi + 1) * jt, :].astype(BF16)
            yp = jnp.dot(hin, w_out, preferred_element_type=F32)
            for gi in range(2):
                g = 2 * pp + gi
                y = yp[:, gi * 256:(gi + 1) * 256] + jnp.dot(xl_ref[g, bi * jl:(bi + 1) * jl, :], toeps[gi],
                                                             preferred_element_type=F32)
                y_ref[g, bi * jl:(bi + 1) * jl, :] = y.astype(BF16)


def _s5(xc, xl, kt, ws, cp, lamc, gps, nb):
    ng, rc, _ = xc.shape
    rl = xl.shape[1]
    npair = gps // 2
    return pl.pallas_call(
        functools.partial(_s5_kernel, nb=nb),
        grid=(ng // gps,),
        in_specs=[pl.BlockSpec((gps, rc, 256), lambda i: (i, 0, 0)),
                  pl.BlockSpec((gps, rl, 256), lambda i: (i, 0, 0)),
                  pl.BlockSpec((gps,) + kt.shape[1:], lambda i: (i, 0, 0)),
                  pl.BlockSpec((gps,) + ws.shape[1:], lambda i: (i, 0, 0)),
                  pl.BlockSpec((gps,) + cp.shape[1:], lambda i: (i, 0, 0)),
                  pl.BlockSpec((npair, 20, S5_TILE, LANES), lambda i: (i, 0, 0, 0))],
        out_specs=pl.BlockSpec((gps, rl, 256), lambda i: (i, 0, 0)),
        out_shape=jax.ShapeDtypeStruct((ng, rl, 256), BF16),
        scratch_shapes=[pltpu.VMEM((npair, rc + rl, 512), F32)],
        compiler_params=_params("arbitrary"),
    )(xc, xl, kt, ws, cp, lamc)


def _cmul(ar, ai, br, bi):
    return ar * br - ai * bi, ar * bi + ai * br


def _s5_weights(lam_re, lam_im, log_dt, b_re, b_im, c_re, c_im):
    nd, ng, p = lam_re.shape
    t = S5_CHUNK
    tc = t * S5_GROUP
    lr = jnp.minimum(lam_re.astype(F32), -1e-4)
    li = lam_im.astype(F32)
    dt = jnp.exp(log_dt.astype(F32))[..., None]
    mag = jnp.exp(lr * dt)
    lbr, lbi = mag * jnp.cos(li * dt), mag * jnp.sin(li * dt)
    den = lr * lr + li * li
    fr = ((lbr - 1.0) * lr + lbi * li) / den
    fi = (lbi * lr - (lbr - 1.0) * li) / den
    bbr, bbi = _cmul(fr[..., None], fi[..., None], b_re.astype(F32), b_im.astype(F32))
    cr, ci = c_re.astype(F32), c_im.astype(F32)

    pr, pi = [jnp.ones_like(lbr)], [jnp.zeros_like(lbr)]
    for _ in range(t):
        nr, ni = _cmul(pr[-1], pi[-1], lbr, lbi)
        pr.append(nr)
        pi.append(ni)
    pr, pi = jnp.stack(pr), jnp.stack(pi)
    wr, wi = _cmul(pr[:t, ..., None], pi[:t, ..., None], bbr[None], bbi[None])

    kern = jnp.einsum("dgxp,kdgpc->dgkxc", cr, wr) - jnp.einsum("dgxp,kdgpc->dgkxc", ci, wi)
    ktf = kern[0].transpose(0, 3, 1, 2)
    ktb = kern[1][:, ::-1].transpose(0, 3, 1, 2)
    kt = jnp.concatenate([ktb[:, :, :t - 1], ktf[:, :, 0:1] + ktb[:, :, t - 1:t], ktf[:, :, 1:]], axis=2)
    kt = kt.reshape(ng, S5_GROUP, (2 * t - 1) * S5_GROUP)
    kt = jnp.pad(kt, ((0, 0), (0, 0), (0, 2 * tc - kt.shape[-1])))

    gscp = lambda a: a.transpose(1, 0, 3, 2).reshape(ng, tc, p)
    parts = [gscp(wr[::-1, 0]), gscp(wi[::-1, 0]), gscp(wr[:, 1]), gscp(wi[:, 1])]
    ws = jnp.concatenate(parts, axis=-1)

    def readout(d, powr, powi):
        mr, mi = _cmul(cr[d][None], ci[d][None], powr[:, :, None, :], powi[:, :, None, :])
        to_gptx = lambda a: a.transpose(1, 3, 0, 2).reshape(ng, p, tc)
        return [to_gptx(mr), to_gptx(-mi)]

    parts = readout(0, pr[1:t + 1, 0], pi[1:t + 1, 0]) + readout(1, pr[1:t + 1, 1][::-1], pi[1:t + 1, 1][::-1])
    cp = jnp.concatenate(parts, axis=1)

    row = jnp.arange(S5_TILE)
    l1 = (pr[t], pi[t])
    l2 = _cmul(*l1, *l1)
    l4 = _cmul(*l2, *l2)
    l8 = _cmul(*l4, *l4)
    rp = [(jnp.ones_like(lbr), jnp.zeros_like(lbr))]
    for _ in range(S5_TILE - 1):
        rp.append(_cmul(*rp[-1], *l1))
    planes = []
    for d in range(nd):
        valid = (lambda dist: row >= dist) if d == 0 else (lambda dist: row <= S5_TILE - 1 - dist)
        for (qr, qi), dist in ((l1, 1), (l2, 2), (l4, 4)):
            m = valid(dist).astype(F32)[:, None, None]
            planes += [m * qr[d][None], m * qi[d][None]]
        order = row if d == 0 else row[::-1]
        planes += [jnp.stack([rp[k][0][d] for k in range(S5_TILE)])[order],
                   jnp.stack([rp[k][1][d] for k in range(S5_TILE)])[order]]
        planes += [jnp.broadcast_to(l8[0][d], (S5_TILE, ng, p)), jnp.broadcast_to(l8[1][d], (S5_TILE, ng, p))]
    lamc = jnp.stack(planes)
    lamc = lamc.reshape(20, S5_TILE, ng // 2, 2 * p).transpose(2, 0, 1, 3)
    return kt, ws, cp.astype(BF16), lamc


def _chunk_cumsum(g, rev):
    sub = 8
    nt = g.shape[0] // sub
    row = lax.broadcasted_iota(jnp.int32, (sub, g.shape[1]), 0)
    out = [None] * nt
    off = None
    for kk in (range(nt - 1, -1, -1) if rev else range(nt)):
        x = g[kk * sub:(kk + 1) * sub, :]
        for dist in (1, 2, 4):
            if rev:
                x = x + jnp.where(row < sub - dist, pltpu.roll(x, sub - dist, 0), 0.0)
            else:
                x = x + jnp.where(row >= dist, pltpu.roll(x, dist, 0), 0.0)
        if off is not None:
            x = x + off
        out[kk] = x
        e = 0 if rev else sub - 1
        off = jnp.broadcast_to(x[e:e + 1, :], x.shape)
    return jnp.concatenate(out, axis=0)


def _gla_log_decay(glr, gup, gbias, rev):
    c = GLA_CHUNK
    z = jnp.dot(glr.astype(BF16), gup, preferred_element_type=F32) + gbias
    g = (jnp.minimum(z, 0.0) - jnp.log(1.0 + jnp.exp(-jnp.abs(z)))) * (1.0 / GLA_GATE_NORM)
    return [_chunk_cumsum(g[n * c:(n + 1) * c, :], rev) for n in range(g.shape[0] // c)]


def _gla_chunk(q, k, v, gc, st, rev, need_out):
    c = GLA_CHUNK
    i_ref = c // 2 - 1 if rev else c // 2
    i_last = 0 if rev else c - 1
    g_ref = gc[i_ref:i_ref + 1, :]
    g_last = gc[i_last:i_last + 1, :]
    lane_head = lax.broadcasted_iota(jnp.int32, (1, GLA_KEY), 1) // GLA_DK

    stack = lambda a: jnp.concatenate(
        [jnp.where(lane_head == h, a, 0.0) for h in range(GLA_HEADS)], axis=0).astype(BF16)
    dv2 = 2 * GLA_DV
    vt = jnp.concatenate(
        [jnp.concatenate([v[:, p * dv2:p * dv2 + GLA_DV], v[:, p * dv2 + GLA_DV:(p + 1) * dv2]], axis=0).T
         for p in range(GLA_HEADS // 2)], axis=1).astype(BF16)
    kl = k * jnp.exp(g_last - gc)
    kv = jnp.dot(vt, stack(kl), preferred_element_type=F32)
    st_new = st * jnp.exp(g_last) + kv
    if not need_out:
        return None, st_new

    qe = q * jnp.exp(gc - g_ref)
    ke = (k * jnp.exp(g_ref - gc)).astype(BF16)
    qg = q * jnp.exp(gc)
    nt_dims = (((1,), (1,)), ((), ()))
    sc = lax.dot_general(stack(qe), ke, nt_dims, preferred_element_type=F32)
    rs = lax.broadcasted_iota(jnp.int32, (GLA_HEADS * c, c), 0) % c
    cs = lax.broadcasted_iota(jnp.int32, (GLA_HEADS * c, c), 1)
    keep = (rs <= cs) if rev else (rs >= cs)
    sc = jnp.where(keep, sc, 0.0).astype(BF16)
    vb = v.astype(BF16)
    oo = lax.dot_general(stack(qg), st.astype(BF16), nt_dims, preferred_element_type=F32)
    o = jnp.concatenate(
        [jnp.dot(sc[h * c:(h + 1) * c, :], vb[:, h * GLA_DV:(h + 1) * GLA_DV], preferred_element_type=F32)
         + oo[h * c:(h + 1) * c, :] for h in range(GLA_HEADS)], axis=1)
    return o, st_new


def _gla_kernel(cf_ref, cb_ref, cc_ref, gup_ref, gbias_ref, of_ref, ob_ref, stf_ref, stb_ref):
    c = GLA_CHUNK
    q0, k0, v0, r0 = 0, GLA_KEY, 2 * GLA_KEY, 2 * GLA_KEY + GLA_VAL

    def decay(ref, d):
        return _gla_log_decay(ref[:, r0:CM_W], gup_ref[d], gbias_ref[d], d == 1)

    def run(ref, n, gcs, d, st, need_out):
        rows = slice(n * c, (n + 1) * c)
        q = ref[rows, q0:k0] if need_out else None
        return _gla_chunk(q, ref[rows, k0:v0], ref[rows, v0:r0], gcs[n], st, d == 1, need_out)

    @pl.when(pl.program_id(1) == 0)
    def _():
        nctx = cc_ref.shape[0] // c
        gf, gb = decay(cc_ref, 0), decay(cc_ref, 1)
        stf = jnp.zeros(stf_ref.shape, F32)
        stb = jnp.zeros(stb_ref.shape, F32)
        for n in range(nctx):
            _, stf = run(cc_ref, n, gf, 0, stf, False)
            _, stb = run(cc_ref, nctx - 1 - n, gb, 1, stb, False)
        stf_ref[...] = stf
        stb_ref[...] = stb

    ncols, rows_per_col, _ = cf_ref.shape
    nch = rows_per_col // c
    stf = stf_ref[...]
    stb = stb_ref[...]
    for j in range(ncols):
        jb = ncols - 1 - j
        gf, gb = decay(cf_ref.at[j], 0), decay(cb_ref.at[jb], 1)
        for n in range(nch):
            o, stf = run(cf_ref.at[j], n, gf, 0, stf, True)
            of_ref[j, n * c:(n + 1) * c, :] = o
            m = nch - 1 - n
            o, stb = run(cb_ref.at[jb], m, gb, 1, stb, True)
            ob_ref[jb, m * c:(m + 1) * c, :] = o
    stf_ref[...] = stf
    stb_ref[...] = stb


def _gla(cm, cmc, gup, gbias):
    b, ncol, rows, _ = cm.shape
    lc = cmc.shape[1]
    cps = GLA_COLS_PER_STEP
    last = ncol // cps - 1
    col = lambda width, fn: pl.BlockSpec((None, cps, rows, width), fn)
    out_shape = jax.ShapeDtypeStruct((b, ncol, rows, GLA_VAL), F32)
    return pl.pallas_call(
        _gla_kernel,
        grid=(b, ncol // cps),
        in_specs=[col(CM_W, lambda bi, ci: (bi, ci, 0, 0)),
                  col(CM_W, lambda bi, ci: (bi, last - ci, 0, 0)),
                  pl.BlockSpec((None, lc, CM_W), lambda bi, ci: (bi, 0, 0)),
                  _const_spec(gup.shape), _const_spec(gbias.shape)],
        out_specs=[col(GLA_VAL, lambda bi, ci: (bi, ci, 0, 0)),
                   col(GLA_VAL, lambda bi, ci: (bi, last - ci, 0, 0))],
        out_shape=[out_shape, out_shape],
        scratch_shapes=[pltpu.VMEM((GLA_DV, GLA_KEY), F32), pltpu.VMEM((GLA_DV, GLA_KEY), F32)],
        compiler_params=_params("arbitrary", "arbitrary"),
    )(cm, cm, cmc, gup, gbias)


def _back_kernel(x1_ref, nat_ref, y_ref, of_ref, ob_ref, mod_ref, dskip_ref, gluw_ref,
                 glub_ref, s5out_ref, gnorm_ref, glaout_ref, wo_ref, n3_ref, wg_ref, wu_ref, wd_ref, fin_ref,
                 out_ref, *, fchunk):
    m = mod_ref[...]
    tm, d = x1_ref.shape
    ng, nj, _ = y_ref.shape
    s5w = ng * S5_GROUP

    yp = _from_chunk_major(lambda g, tt: y_ref[g, :, tt * LANES:(tt + 1) * LANES].astype(F32), ng, nj)
    ys = jnp.swapaxes(yp.reshape(S5_CHUNK, nj, s5w), 0, 1).reshape(tm, s5w)
    ya = ys + dskip_ref[...] * nat_ref[:, 0:s5w].astype(F32)
    ya = 0.5 * ya * (1.0 + jnp.tanh(GELU_C0 * (ya + GELU_C1 * (ya * ya * ya))))
    gl = jnp.dot(ya.astype(BF16), gluw_ref[...], preferred_element_type=F32) + glub_ref[...]
    ya = ya * jax.nn.sigmoid(gl)

    ocm = (of_ref[...] + ob_ref[...]).astype(BF16).astype(F32)
    o = jnp.swapaxes(ocm, 0, 1).reshape(tm, GLA_VAL)
    heads = []
    for h in range(GLA_HEADS):
        oh = o[:, h * GLA_DV:(h + 1) * GLA_DV]
        heads.append(oh * lax.rsqrt(jnp.mean(oh * oh, axis=-1, keepdims=True) + RMS_EPS))
    r = nat_ref[:, s5w:s5w + GLA_VAL].astype(F32)
    yb = jnp.concatenate(heads, axis=1) * gnorm_ref[...] * (r * jax.nn.sigmoid(r))

    pa = jnp.dot(ya.astype(BF16), s5out_ref[...], preferred_element_type=F32)
    pb = jnp.dot(yb.astype(BF16), glaout_ref[...], preferred_element_type=F32)
    ga = nat_ref[:, s5w + GLA_VAL:s5w + GLA_VAL + d].astype(F32)
    gb = nat_ref[:, s5w + GLA_VAL + d:s5w + GLA_VAL + 2 * d].astype(F32)
    mg = jax.nn.sigmoid(ga) * pa + jax.nn.sigmoid(gb) * pb
    y = jnp.dot(mg.astype(BF16), wo_ref[...], preferred_element_type=F32)
    x2 = x1_ref[...] + m[5:6] * y
    h = _rms_mod(x2, n3_ref[...], m[6:7], m[7:8]).astype(BF16)
    x3 = _swiglu_residual(x2, h, m[8:9], wg_ref, wu_ref, wd_ref, fchunk)
    ms = jnp.mean(x3 * x3, axis=-1, keepdims=True)
    out_ref[...] = x3 * lax.rsqrt(ms + RMS_EPS) * fin_ref[...]


def _back(x1, nat, ys5, of, ob, mods3, w, tm, fchunk):
    b, l, d = x1.shape
    f = w["wg2"].shape[1]
    s5w = w["dskip"].shape[-1]
    ng = s5w // S5_GROUP
    nblk = l // tm
    tok = lambda width: pl.BlockSpec((None, tm, width), lambda bi, i: (bi, i, 0))
    colblk = pl.BlockSpec((None, GRID_W, tm // GRID_W, GLA_VAL), lambda bi, i: (bi, 0, i, 0))
    return pl.pallas_call(
        functools.partial(_back_kernel, fchunk=fchunk),
        grid=(b, nblk),
        in_specs=[tok(d), tok(NAT_W),
                  pl.BlockSpec((ng, tm // S5_CHUNK, S5_CHUNK * S5_GROUP), lambda bi, i: (0, bi * nblk + i, 0)),
                  colblk, colblk,
                  pl.BlockSpec((None, N_MOD, d), lambda bi, i: (bi, 0, 0)),
                  _const_spec((1, s5w)), _const_spec((s5w, s5w)), _const_spec((1, s5w)),
                  _const_spec((s5w, d)), _const_spec((1, GLA_VAL)), _const_spec((GLA_VAL, d)),
                  _const_spec((d, d)), _const_spec((1, d)), _const_spec((d, f)), _const_spec((d, f)),
                  _const_spec((f, d)), _const_spec((1, d))],
        out_specs=tok(d),
        out_shape=jax.ShapeDtypeStruct((b, l, d), F32),
        compiler_params=_params("arbitrary", "arbitrary"),
    )(x1, nat, ys5, of, ob, mods3, w["dskip"], w["gluw"], w["glub"], w["s5out"], w["gnorm"], w["glaout"],
      w["wo"], w["n3"], w["wg2"], w["wu2"], w["wd2"], w["fin"])


def kernel(x, c, ctx, c_ctx, ada_w, ada_b, ffn1_norm, ffn1_w_gate, ffn1_w_up, ffn1_w_down, mix_norm, w_in,
           s5_lambda_re, s5_lambda_im, s5_log_dt, s5_b_re, s5_b_im, s5_c_re, s5_c_im, s5_d, s5_glu_w, s5_glu_b,
           s5_out, gla_gate_up, gla_gate_b, gla_norm, gla_out, w_o, ffn2_norm, ffn2_w_gate, ffn2_w_up,
           ffn2_w_down, final_norm):
    b, l, d = x.shape
    lc = ctx.shape[1]
    assert ada_w.shape[0] == 1 and b + 1 <= SUBLANES
    s5w = s5_d.shape[-1]
    fchunk = MXU_TILE
    tm = min(512, l)
    assert tm % (GRID_W * 8) == 0 and l % tm == 0 and lc % (S5_CHUNK * S5_TILE) == 0 and lc % GLA_CHUNK == 0

    cvec = jnp.concatenate([c, c_ctx[None, :], jnp.zeros((SUBLANES - b - 1, d), F32)], axis=0)
    mods3 = _ada(cvec, ada_w[0], ada_b[0]).reshape(SUBLANES, N_MOD, d)

    wi = w_in[0]
    o_q, o_k, o_v, o_r = s5w, s5w + GLA_KEY, s5w + 2 * GLA_KEY, s5w + 2 * GLA_KEY + GLA_VAL
    o_glr = o_r + GLA_VAL
    o_ga = o_glr + 2 * GLA_GATE_RANK
    wnat = jnp.concatenate([wi[:, :o_q], wi[:, o_r:o_glr], wi[:, o_ga:]], axis=1)
    wcm = jnp.concatenate([wi[:, o_q:o_k] * (GLA_DK ** -0.5), wi[:, o_k:o_r],
                           jnp.pad(wi[:, o_glr:o_ga], ((0, 0), (0, LANES - 2 * GLA_GATE_RANK)))], axis=1)
    gup = jnp.zeros((2, LANES, GLA_KEY), F32)
    gup = gup.at[0, 0:GLA_GATE_RANK].set(gla_gate_up[0, 0])
    gup = gup.at[1, GLA_GATE_RANK:2 * GLA_GATE_RANK].set(gla_gate_up[0, 1])
    row = lambda v: v.reshape(1, -1).astype(F32)
    w = dict(
        n1=row(ffn1_norm[0]), wg1=ffn1_w_gate[0].astype(BF16), wu1=ffn1_w_up[0].astype(BF16),
        wd1=ffn1_w_down[0].astype(BF16), n2=row(mix_norm[0]), wnat=wnat.astype(BF16), wcm=wcm.astype(BF16),
        dskip=row(s5_d[0]), gluw=s5_glu_w[0].astype(BF16), glub=row(s5_glu_b[0]), s5out=s5_out[0].astype(BF16),
        gnorm=row(gla_norm[0]), glaout=gla_out[0].astype(BF16), wo=w_o[0].astype(BF16),
        n3=row(ffn2_norm[0]), wg2=ffn2_w_gate[0].astype(BF16), wu2=ffn2_w_up[0].astype(BF16),
        wd2=ffn2_w_down[0].astype(BF16), fin=row(final_norm))

    lat_row = lambda bi: bi
    ctx_row = lambda bi: b
    x1 = _ffn(x, mods3, lat_row, w["n1"], w["wg1"], w["wu1"], w["wd1"], tm, fchunk, 0)
    tc = min(tm, b * lc)
    c1 = _ffn(ctx.reshape(1, b * lc, d), mods3, ctx_row, w["n1"], w["wg1"], w["wu1"], w["wd1"], tc, fchunk, 0)
    nat, xl, cm = _proj(x1, mods3, lat_row, w, min(2 * tm, l), True)
    _, xc, cmc = _proj(c1, mods3, ctx_row, w, tc, False)
    cmc = cmc.reshape(b, lc, CM_W)

    kt, ws, cp, lamc = _s5_weights(s5_lambda_re[0], s5_lambda_im[0], s5_log_dt[0], s5_b_re[0], s5_b_im[0],
                                     s5_c_re[0], s5_c_im[0])
    ys5 = _s5(xc, xl, kt, ws, cp, lamc, gps=4, nb=b)

    of, ob = _gla(cm, cmc, gup.astype(BF16), gla_gate_b[0].reshape(2, 1, GLA_KEY).astype(F32))
    return _back(x1, nat, ys5, of, ob, mods3, w, tm, fchunk)
```

```python
import functools

import jax
import jax.numpy as jnp
from jax import lax
from jax.experimental import pallas as pl
from jax.experimental.pallas import tpu as pltpu

F32 = jnp.float32
BF16 = jnp.bfloat16

RMS_EPS = 1e-6
MACARON_WEIGHT = 0.5
GRID_W = 64
N_MOD = 9
S5_GROUP = 16
S5_STATE = 64
S5_CHUNK = 16
S5_TILE = 8
S5_PREFIX_TILES = 4
GLA_HEADS = 4
GLA_DK = 64
GLA_DV = 128
GLA_CHUNK = 64
GLA_GATE_RANK = 16
GLA_GATE_NORM = 16.0
GLA_COLS_PER_STEP = 8
GLA_KEY = GLA_HEADS * GLA_DK
GLA_VAL = GLA_HEADS * GLA_DV
LANES = 128
SUBLANES = 8
MXU_TILE = 256
GELU_C0 = 0.7978845608028654
GELU_C1 = 0.044715
NAT_W = 3072
CM_W = 2 * GLA_KEY + GLA_VAL + LANES
V7X_VMEM_LIMIT_BYTES = 56 * 1024 * 1024


def _params(*sem):
    return pltpu.CompilerParams(dimension_semantics=sem, vmem_limit_bytes=V7X_VMEM_LIMIT_BYTES)


def _const_spec(shape):
    nd = len(shape)
    return pl.BlockSpec(shape, lambda *_: (0,) * nd, pipeline_mode=pl.Buffered(1))


def _rms_mod(x, g, shift, scale):
    ms = jnp.mean(x * x, axis=-1, keepdims=True)
    return (x * lax.rsqrt(ms + RMS_EPS) * g) * (1.0 + scale) + shift


def _swiglu_residual(x, h, gate, wg_ref, wu_ref, wd_ref, fchunk):
    acc = None
    for f0 in range(0, wg_ref.shape[1], fchunk):
        gg = jnp.dot(h, wg_ref[:, f0:f0 + fchunk], preferred_element_type=F32)
        uu = jnp.dot(h, wu_ref[:, f0:f0 + fchunk], preferred_element_type=F32)
        a = (gg * jax.nn.sigmoid(gg) * uu).astype(BF16)
        o = jnp.dot(a, wd_ref[f0:f0 + fchunk, :], preferred_element_type=F32)
        acc = o if acc is None else acc + o
    return x + gate * (MACARON_WEIGHT * acc)


def _ada_kernel(c_ref, w_ref, b_ref, o_ref):
    cv = c_ref[...]
    s = cv * jax.nn.sigmoid(cv)
    o_ref[...] = jnp.dot(s, w_ref[...], preferred_element_type=F32,
                         precision=lax.Precision.HIGHEST) + b_ref[...]


def _ada(cvec, ada_w, ada_b):
    rows, d = cvec.shape
    n = ada_w.shape[1]
    bn = n // N_MOD if n % N_MOD == 0 else n
    return pl.pallas_call(
        _ada_kernel,
        grid=(n // bn,),
        in_specs=[pl.BlockSpec((rows, d), lambda j: (0, 0)),
                  pl.BlockSpec((d, bn), lambda j: (0, j)),
                  pl.BlockSpec((1, bn), lambda j: (0, j))],
        out_specs=pl.BlockSpec((rows, bn), lambda j: (0, j)),
        out_shape=jax.ShapeDtypeStruct((rows, n), F32),
        compiler_params=_params("arbitrary"),
    )(cvec, ada_w, ada_b.reshape(1, n))


def _ffn_kernel(x_ref, mod_ref, n_ref, wg_ref, wu_ref, wd_ref, o_ref, *, fchunk, mod0):
    x = x_ref[...]
    m = mod_ref[...]
    h = _rms_mod(x, n_ref[...], m[mod0:mod0 + 1], m[mod0 + 1:mod0 + 2]).astype(BF16)
    o_ref[...] = _swiglu_residual(x, h, m[mod0 + 2:mod0 + 3], wg_ref, wu_ref, wd_ref, fchunk)


def _ffn(x, mods3, mod_row, norm, wg, wu, wd, tm, fchunk, mod0):
    b, l, d = x.shape
    f = wg.shape[1]
    tok = pl.BlockSpec((None, tm, d), lambda bi, i: (bi, i, 0))
    return pl.pallas_call(
        functools.partial(_ffn_kernel, fchunk=fchunk, mod0=mod0),
        grid=(b, l // tm),
        in_specs=[tok, pl.BlockSpec((None, N_MOD, d), lambda bi, i: (mod_row(bi), 0, 0)),
                  _const_spec((1, d)), _const_spec((d, f)), _const_spec((d, f)), _const_spec((f, d))],
        out_specs=tok,
        out_shape=jax.ShapeDtypeStruct((b, l, d), F32),
        compiler_params=_params("arbitrary", "arbitrary"),
    )(x, mods3, norm, wg, wu, wd)


def _transpose_pieces(v):
    n = len(v)
    piece = lax.broadcasted_iota(jnp.int32, v[0].shape, 1) // S5_GROUP
    s = n // 2
    while s >= 1:
        keep = (piece & s) == 0
        nv = list(v)
        for i in range(n):
            if i & s == 0:
                a, b = v[i], v[i + s]
                nv[i] = jnp.where(keep, a, pltpu.roll(b, s * S5_GROUP, 1))
                nv[i + s] = jnp.where(keep, pltpu.roll(a, LANES - s * S5_GROUP, 1), b)
        v = nv
        s //= 2
    return v


def _to_chunk_major(up, store):
    nj = up.shape[0] // S5_CHUNK
    npc = LANES // S5_GROUP
    for gg in range(up.shape[1] // LANES):
        for tt in range(S5_CHUNK // npc):
            src = [up[(npc * tt + p) * nj:(npc * tt + p + 1) * nj, gg * LANES:(gg + 1) * LANES] for p in range(npc)]
            for gl, v in enumerate(_transpose_pieces(src)):
                store(gg * npc + gl, tt, v)


def _from_chunk_major(load, ng, nj):
    npc = LANES // S5_GROUP
    row_blocks = []
    for tt in range(S5_CHUNK // npc):
        per_p = [[] for _ in range(npc)]
        for gg in range(ng // npc):
            out = _transpose_pieces([load(gg * npc + gl, tt) for gl in range(npc)])
            for p in range(npc):
                per_p[p].append(out[p])
        row_blocks += [jnp.concatenate(blk, axis=1) for blk in per_p]
    return jnp.concatenate(row_blocks, axis=0)


def _proj_kernel(x_ref, mod_ref, n_ref, wnat_ref, wcm_ref, nat_ref, xs5_ref, cm_ref, *, col_major):
    x = x_ref[...]
    m = mod_ref[...]
    tm = x.shape[0]
    h2 = _rms_mod(x, n_ref[...], m[3:4], m[4:5]).astype(BF16)
    u = None
    ncol = 4 * MXU_TILE
    for c0 in range(0, NAT_W, ncol):
        p = jnp.dot(h2, wnat_ref[:, c0:c0 + ncol], preferred_element_type=F32)
        nat_ref[:, c0:c0 + ncol] = p.astype(BF16)
        if c0 == 0:
            u = p[:, 0:xs5_ref.shape[0] * S5_GROUP]
    pc = jnp.dot(h2, wcm_ref[...], preferred_element_type=F32)
    if col_major:
        pcm = pc.reshape(tm // GRID_W, GRID_W, CM_W)
        cm_ref[...] = jnp.swapaxes(pcm, 0, 1)
    else:
        cm_ref[...] = pc
    uf = u.reshape(tm // S5_CHUNK, S5_CHUNK, u.shape[1])
    up = jnp.swapaxes(uf, 0, 1).reshape(tm, u.shape[1])

    def store(g, tt, v):
        xs5_ref[g, :, tt * LANES:(tt + 1) * LANES] = v.astype(BF16)

    _to_chunk_major(up, store)


def _proj(x1, mods3, mod_row, w, tm, col_major):
    b, l, d = x1.shape
    ng = w["dskip"].shape[-1] // S5_GROUP
    nj = tm // S5_CHUNK
    nblk = l // tm
    tok = lambda width: pl.BlockSpec((None, tm, width), lambda bi, i: (bi, i, 0))
    if col_major:
        cm_spec = pl.BlockSpec((None, GRID_W, tm // GRID_W, CM_W), lambda bi, i: (bi, 0, i, 0))
        cm_shape = jax.ShapeDtypeStruct((b, GRID_W, l // GRID_W, CM_W), F32)
    else:
        cm_spec = tok(CM_W)
        cm_shape = jax.ShapeDtypeStruct((b, l, CM_W), F32)
    return pl.pallas_call(
        functools.partial(_proj_kernel, col_major=col_major),
        grid=(b, nblk),
        in_specs=[tok(d), pl.BlockSpec((None, N_MOD, d), lambda bi, i: (mod_row(bi), 0, 0)),
                  _const_spec((1, d)), _const_spec((d, NAT_W)), _const_spec((d, CM_W))],
        out_specs=[tok(NAT_W),
                   pl.BlockSpec((ng, nj, S5_CHUNK * S5_GROUP), lambda bi, i: (0, bi * nblk + i, 0)),
                   cm_spec],
        out_shape=[jax.ShapeDtypeStruct((b, l, NAT_W), BF16),
                   jax.ShapeDtypeStruct((ng, b * (l // S5_CHUNK), S5_CHUNK * S5_GROUP), BF16),
                   cm_shape],
        compiler_params=_params("arbitrary", "arbitrary"),
    )(x1, mods3, w["n2"], w["wnat"], w["wcm"])


def _s5_tile_prefix(sr, si, cst, fwd):
    ar, ai = sr, si
    for lvl, dist in enumerate((1, 2, 4)):
        sh = dist if fwd else S5_TILE - dist
        rr, ri = pltpu.roll(ar, sh, 0), pltpu.roll(ai, sh, 0)
        lr, li = cst[2 * lvl], cst[2 * lvl + 1]
        ar, ai = ar + (lr * rr - li * ri), ai + (lr * ri + li * rr)
    return ar, ai


def _s5_tile_carry(ar, ai, hr, hi, cst, fwd):
    row = lax.broadcasted_iota(jnp.int32, ar.shape, 0)
    keep = (row >= 1) if fwd else (row <= S5_TILE - 2)
    sh = 1 if fwd else S5_TILE - 1
    pr = jnp.where(keep, pltpu.roll(ar, sh, 0), 0.0)
    pi = jnp.where(keep, pltpu.roll(ai, sh, 0), 0.0)
    lpr, lpi, l8r, l8i = cst[6:10]
    hin_r = lpr * hr - lpi * hi + pr
    hin_i = lpr * hi + lpi * hr + pi
    e = S5_TILE - 1 if fwd else 0
    er = jnp.broadcast_to(ar[e:e + 1, :], ar.shape)
    ei = jnp.broadcast_to(ai[e:e + 1, :], ai.shape)
    return hin_r, hin_i, l8r * hr - l8i * hi + er, l8r * hi + l8i * hr + ei


def _s5_kernel(xc_ref, xl_ref, kt_ref, ws_ref, cp_ref, lam_ref, y_ref, s_ref, *, nb):
    npair = ws_ref.shape[0] // 2
    p = S5_STATE
    lo = lax.broadcasted_iota(jnp.int32, (ws_ref.shape[1], LANES), 1) < p
    half_turn = lambda a: pltpu.roll(a, p, 1)

    def state_in_operator(g):
        blocks = []
        for k in range(2):
            a = ws_ref[g, :, k * LANES:(k + 1) * LANES]
            if g % 2 == 0:
                blocks += [jnp.where(lo, a, 0.0), jnp.where(lo, half_turn(a), 0.0)]
            else:
                blocks += [jnp.where(lo, 0.0, half_turn(a)), jnp.where(lo, 0.0, a)]
        return jnp.concatenate(blocks, axis=1).astype(BF16)

    def state_out_operator(pp):
        zeros = jnp.zeros((p, cp_ref.shape[2]), BF16)
        rows = []
        for part in range(4):
            for gi in range(2):
                c = cp_ref[2 * pp + gi, part * p:(part + 1) * p, :]
                rows.append(jnp.concatenate([c, zeros] if gi == 0 else [zeros, c], axis=1))
        return jnp.concatenate(rows, axis=0)

    jc = xc_ref.shape[1] // nb
    jl = xl_ref.shape[1] // nb
    jt = jc + jl
    for pp in range(npair):
        w_in = [state_in_operator(2 * pp), state_in_operator(2 * pp + 1)]
        for src, j0, nj in ((xc_ref, 0, jc), (xl_ref, jc, jl)):
            s = jnp.dot(src[2 * pp], w_in[0], preferred_element_type=F32)
            s = s + jnp.dot(src[2 * pp + 1], w_in[1], preferred_element_type=F32)
            for bi in range(nb):
                s_ref[pp, bi * jt + j0:bi * jt + j0 + nj, :] = s[bi * nj:(bi + 1) * nj, :]

    nct = jc // S5_TILE
    nt = jt // S5_TILE
    zero = jnp.zeros((S5_TILE, LANES), F32)
    for pp in range(npair):

        def prefix(it, carry, pp=pp):
            tiles = []
            for u in range(S5_PREFIX_TILES):
                r0 = pl.multiple_of((it * S5_PREFIX_TILES + u) * S5_TILE, S5_TILE)
                for c0, fwd in ((0, True), (256, False)):
                    tiles.append((r0, c0, fwd, s_ref[pp, pl.ds(r0, S5_TILE), c0:c0 + 128],
                                  s_ref[pp, pl.ds(r0, S5_TILE), c0 + 128:c0 + 256]))
            cst = {fwd: [lam_ref[pp, (0 if fwd else 10) + k] for k in range(6)] for fwd in (True, False)}
            done = [(r0, c0) + _s5_tile_prefix(sr, si, cst[fwd], fwd) for r0, c0, fwd, sr, si in tiles]
            for r0, c0, ar, ai in done:
                s_ref[pp, pl.ds(r0, S5_TILE), c0:c0 + 128] = ar
                s_ref[pp, pl.ds(r0, S5_TILE), c0 + 128:c0 + 256] = ai
            return carry

        lax.fori_loop(0, nb * nt // S5_PREFIX_TILES, prefix, 0)

        def step(it, carry, pp=pp):
            mb = jnp.where(it < nct, nct - 1 - it, nt - 1 - (it - nct))
            cst = {True: [lam_ref[pp, k] for k in range(10)], False: [lam_ref[pp, 10 + k] for k in range(10)]}
            tiles = []
            for bi in range(nb):
                for c0, fwd, tile in ((0, True, it), (256, False, mb)):
                    r0 = pl.multiple_of(bi * jt + tile * S5_TILE, S5_TILE)
                    tiles.append((bi, r0, c0, fwd, s_ref[pp, pl.ds(r0, S5_TILE), c0:c0 + 128],
                                  s_ref[pp, pl.ds(r0, S5_TILE), c0 + 128:c0 + 256]))
            out = [[None, None] for _ in range(nb)]
            for bi, r0, c0, fwd, ar, ai in tiles:
                hr, hi = carry[bi][0 if fwd else 1]
                hin_r, hin_i, hr, hi = _s5_tile_carry(ar, ai, hr, hi, cst[fwd], fwd)
                out[bi][0 if fwd else 1] = (hr, hi)
                s_ref[pp, pl.ds(r0, S5_TILE), c0:c0 + 128] = hin_r
                s_ref[pp, pl.ds(r0, S5_TILE), c0 + 128:c0 + 256] = hin_i
            return tuple(tuple(o) for o in out)

        lax.fori_loop(0, nt, step, tuple(((zero, zero), (zero, zero)) for _ in range(nb)))

    tc = S5_CHUNK * S5_GROUP
    width = kt_ref.shape[2]
    for pp in range(npair):
        toeps = []
        for gi in range(2):
            kt = kt_ref[2 * pp + gi]
            blocks = []
            for s in range(S5_CHUNK):
                sh = (width - S5_GROUP * (S5_CHUNK - 1 - s)) % width
                blocks.append((kt if sh == 0 else pltpu.roll(kt, sh, 1))[:, 0:tc])
            toeps.append(jnp.concatenate(blocks, axis=0).astype(BF16))
        w_out = state_out_operator(pp)
        for bi in range(nb):
            hin = s_ref[pp, bi * jt + jc:(bi + 1) * jt, :].astype(BF16)
            yp = jnp.dot(hin, w_out, preferred_element_type=F32)
            for gi in range(2):
                g = 2 * pp + gi
                y = yp[:, gi * 256:(gi + 1) * 256] + jnp.dot(xl_ref[g, bi * jl:(bi + 1) * jl, :], toeps[gi],
                                                             preferred_element_type=F32)
                y_ref[g, bi * jl:(bi + 1) * jl, :] = y.astype(BF16)


def _s5(xc, xl, kt, ws, cp, lamc, gps, nb):
    ng, rc, _ = xc.shape
    rl = xl.shape[1]
    npair = gps // 2
    return pl.pallas_call(
        functools.partial(_s5_kernel, nb=nb),
        grid=(ng // gps,),
        in_specs=[pl.BlockSpec((gps, rc, 256), lambda i: (i, 0, 0)),
                  pl.BlockSpec((gps, rl, 256), lambda i: (i, 0, 0)),
                  pl.BlockSpec((gps,) + kt.shape[1:], lambda i: (i, 0, 0)),
                  pl.BlockSpec((gps,) + ws.shape[1:], lambda i: (i, 0, 0)),
                  pl.BlockSpec((gps,) + cp.shape[1:], lambda i: (i, 0, 0)),
                  pl.BlockSpec((npair, 20, S5_TILE, LANES), lambda i: (i, 0, 0, 0))],
        out_specs=pl.BlockSpec((gps, rl, 256), lambda i: (i, 0, 0)),
        out_shape=jax.ShapeDtypeStruct((ng, rl, 256), BF16),
        scratch_shapes=[pltpu.VMEM((npair, rc + rl, 512), F32)],
        compiler_params=_params("arbitrary"),
    )(xc, xl, kt, ws, cp, lamc)


def _cmul(ar, ai, br, bi):
    return ar * br - ai * bi, ar * bi + ai * br


def _s5_weights(lam_re, lam_im, log_dt, b_re, b_im, c_re, c_im):
    nd, ng, p = lam_re.shape
    t = S5_CHUNK
    tc = t * S5_GROUP
    lr = jnp.minimum(lam_re.astype(F32), -1e-4)
    li = lam_im.astype(F32)
    dt = jnp.exp(log_dt.astype(F32))[..., None]
    mag = jnp.exp(lr * dt)
    lbr, lbi = mag * jnp.cos(li * dt), mag * jnp.sin(li * dt)
    den = lr * lr + li * li
    fr = ((lbr - 1.0) * lr + lbi * li) / den
    fi = (lbi * lr - (lbr - 1.0) * li) / den
    bbr, bbi = _cmul(fr[..., None], fi[..., None], b_re.astype(F32), b_im.astype(F32))
    cr, ci = c_re.astype(F32), c_im.astype(F32)

    pr, pi = [jnp.ones_like(lbr)], [jnp.zeros_like(lbr)]
    for _ in range(t):
        nr, ni = _cmul(pr[-1], pi[-1], lbr, lbi)
        pr.append(nr)
        pi.append(ni)
    pr, pi = jnp.stack(pr), jnp.stack(pi)
    wr, wi = _cmul(pr[:t, ..., None], pi[:t, ..., None], bbr[None], bbi[None])

    kern = jnp.einsum("dgxp,kdgpc->dgkxc", cr, wr) - jnp.einsum("dgxp,kdgpc->dgkxc", ci, wi)
    ktf = kern[0].transpose(0, 3, 1, 2)
    ktb = kern[1][:, ::-1].transpose(0, 3, 1, 2)
    kt = jnp.concatenate([ktb[:, :, :t - 1], ktf[:, :, 0:1] + ktb[:, :, t - 1:t], ktf[:, :, 1:]], axis=2)
    kt = kt.reshape(ng, S5_GROUP, (2 * t - 1) * S5_GROUP)
    kt = jnp.pad(kt, ((0, 0), (0, 0), (0, 2 * tc - kt.shape[-1])))

    gscp = lambda a: a.transpose(1, 0, 3, 2).reshape(ng, tc, p)
    parts = [gscp(wr[::-1, 0]), gscp(wi[::-1, 0]), gscp(wr[:, 1]), gscp(wi[:, 1])]
    ws = jnp.concatenate(parts, axis=-1)

    def readout(d, powr, powi):
        mr, mi = _cmul(cr[d][None], ci[d][None], powr[:, :, None, :], powi[:, :, None, :])
        to_gptx = lambda a: a.transpose(1, 3, 0, 2).reshape(ng, p, tc)
        return [to_gptx(mr), to_gptx(-mi)]

    parts = readout(0, pr[1:t + 1, 0], pi[1:t + 1, 0]) + readout(1, pr[1:t + 1, 1][::-1], pi[1:t + 1, 1][::-1])
    cp = jnp.concatenate(parts, axis=1)

    row = jnp.arange(S5_TILE)
    l1 = (pr[t], pi[t])
    l2 = _cmul(*l1, *l1)
    l4 = _cmul(*l2, *l2)
    l8 = _cmul(*l4, *l4)
    rp = [(jnp.ones_like(lbr), jnp.zeros_like(lbr))]
    for _ in range(S5_TILE - 1):
        rp.append(_cmul(*rp[-1], *l1))
    planes = []
    for d in range(nd):
        valid = (lambda dist: row >= dist) if d == 0 else (lambda dist: row <= S5_TILE - 1 - dist)
        for (qr, qi), dist in ((l1, 1), (l2, 2), (l4, 4)):
            m = valid(dist).astype(F32)[:, None, None]
            planes += [m * qr[d][None], m * qi[d][None]]
        order = row if d == 0 else row[::-1]
        planes += [jnp.stack([rp[k][0][d] for k in range(S5_TILE)])[order],
                   jnp.stack([rp[k][1][d] for k in range(S5_TILE)])[order]]
        planes += [jnp.broadcast_to(l8[0][d], (S5_TILE, ng, p)), jnp.broadcast_to(l8[1][d], (S5_TILE, ng, p))]
    lamc = jnp.stack(planes)
    lamc = lamc.reshape(20, S5_TILE, ng // 2, 2 * p).transpose(2, 0, 1, 3)
    return kt, ws, cp.astype(BF16), lamc


def _chunk_cumsum(g, rev):
    sub = 8
    nt = g.shape[0] // sub
    row = lax.broadcasted_iota(jnp.int32, (sub, g.shape[1]), 0)
    out = [None] * nt
    off = None
    for kk in (range(nt - 1, -1, -1) if rev else range(nt)):
        x = g[kk * sub:(kk + 1) * sub, :]
        for dist in (1, 2, 4):
            if rev:
                x = x + jnp.where(row < sub - dist, pltpu.roll(x, sub - dist, 0), 0.0)
            else:
                x = x + jnp.where(row >= dist, pltpu.roll(x, dist, 0), 0.0)
        if off is not None:
            x = x + off
        out[kk] = x
        e = 0 if rev else sub - 1
        off = jnp.broadcast_to(x[e:e + 1, :], x.shape)
    return jnp.concatenate(out, axis=0)


def _gla_log_decay(glr, gup, gbias, rev):
    c = GLA_CHUNK
    z = jnp.dot(glr.astype(BF16), gup, preferred_element_type=F32) + gbias
    g = (jnp.minimum(z, 0.0) - jnp.log(1.0 + jnp.exp(-jnp.abs(z)))) * (1.0 / GLA_GATE_NORM)
    return [_chunk_cumsum(g[n * c:(n + 1) * c, :], rev) for n in range(g.shape[0] // c)]


def _gla_chunk(q, k, v, gc, st, rev, need_out):
    c = GLA_CHUNK
    i_ref = c // 2 - 1 if rev else c // 2
    i_last = 0 if rev else c - 1
    g_ref = gc[i_ref:i_ref + 1, :]
    g_last = gc[i_last:i_last + 1, :]
    lane_head = lax.broadcasted_iota(jnp.int32, (1, GLA_KEY), 1) // GLA_DK

    stack = lambda a: jnp.concatenate(
        [jnp.where(lane_head == h, a, 0.0) for h in range(GLA_HEADS)], axis=0).astype(BF16)
    dv2 = 2 * GLA_DV
    vt = jnp.concatenate(
        [jnp.concatenate([v[:, p * dv2:p * dv2 + GLA_DV], v[:, p * dv2 + GLA_DV:(p + 1) * dv2]], axis=0).T
         for p in range(GLA_HEADS // 2)], axis=1).astype(BF16)
    kl = k * jnp.exp(g_last - gc)
    kv = jnp.dot(vt, stack(kl), preferred_element_type=F32)
    st_new = st * jnp.exp(g_last) + kv
    if not need_out:
        return None, st_new

    qe = q * jnp.exp(gc - g_ref)
    ke = (k * jnp.exp(g_ref - gc)).astype(BF16)
    qg = q * jnp.exp(gc)
    nt_dims = (((1,), (1,)), ((), ()))
    sc = lax.dot_general(stack(qe), ke, nt_dims, preferred_element_type=F32)
    rs = lax.broadcasted_iota(jnp.int32, (GLA_HEADS * c, c), 0) % c
    cs = lax.broadcasted_iota(jnp.int32, (GLA_HEADS * c, c), 1)
    keep = (rs <= cs) if rev else (rs >= cs)
    sc = jnp.where(keep, sc, 0.0).astype(BF16)
    vb = v.astype(BF16)
    oo = lax.dot_general(stack(qg), st.astype(BF16), nt_dims, preferred_element_type=F32)
    o = jnp.concatenate(
        [jnp.dot(sc[h * c:(h + 1) * c, :], vb[:, h * GLA_DV:(h + 1) * GLA_DV], preferred_element_type=F32)
         + oo[h * c:(h + 1) * c, :] for h in range(GLA_HEADS)], axis=1)
    return o, st_new


def _gla_kernel(cf_ref, cb_ref, cc_ref, gup_ref, gbias_ref, of_ref, ob_ref, stf_ref, stb_ref):
    c = GLA_CHUNK
    q0, k0, v0, r0 = 0, GLA_KEY, 2 * GLA_KEY, 2 * GLA_KEY + GLA_VAL

    def decay(ref, d):
        return _gla_log_decay(ref[:, r0:CM_W], gup_ref[d], gbias_ref[d], d == 1)

    def run(ref, n, gcs, d, st, need_out):
        rows = slice(n * c, (n + 1) * c)
        q = ref[rows, q0:k0] if need_out else None
        return _gla_chunk(q, ref[rows, k0:v0], ref[rows, v0:r0], gcs[n], st, d == 1, need_out)

    @pl.when(pl.program_id(1) == 0)
    def _():
        nctx = cc_ref.shape[0] // c
        gf, gb = decay(cc_ref, 0), decay(cc_ref, 1)
        stf = jnp.zeros(stf_ref.shape, F32)
        stb = jnp.zeros(stb_ref.shape, F32)
        for n in range(nctx):
            _, stf = run(cc_ref, n, gf, 0, stf, False)
            _, stb = run(cc_ref, nctx - 1 - n, gb, 1, stb, False)
        stf_ref[...] = stf
        stb_ref[...] = stb

    ncols, rows_per_col, _ = cf_ref.shape
    nch = rows_per_col // c
    stf = stf_ref[...]
    stb = stb_ref[...]
    for j in range(ncols):
        jb = ncols - 1 - j
        gf, gb = decay(cf_ref.at[j], 0), decay(cb_ref.at[jb], 1)
        for n in range(nch):
            o, stf = run(cf_ref.at[j], n, gf, 0, stf, True)
            of_ref[j, n * c:(n + 1) * c, :] = o
            m = nch - 1 - n
            o, stb = run(cb_ref.at[jb], m, gb, 1, stb, True)
            ob_ref[jb, m * c:(m + 1) * c, :] = o
    stf_ref[...] = stf
    stb_ref[...] = stb


def _gla(cm, cmc, gup, gbias):
    b, ncol, rows, _ = cm.shape
    lc = cmc.shape[1]
    cps = GLA_COLS_PER_STEP
    last = ncol // cps - 1
    col = lambda width, fn: pl.BlockSpec((None, cps, rows, width), fn)
    out_shape = jax.ShapeDtypeStruct((b, ncol, rows, GLA_VAL), F32)
    return pl.pallas_call(
        _gla_kernel,
        grid=(b, ncol // cps),
        in_specs=[col(CM_W, lambda bi, ci: (bi, ci, 0, 0)),
                  col(CM_W, lambda bi, ci: (bi, last - ci, 0, 0)),
                  pl.BlockSpec((None, lc, CM_W), lambda bi, ci: (bi, 0, 0)),
                  _const_spec(gup.shape), _const_spec(gbias.shape)],
        out_specs=[col(GLA_VAL, lambda bi, ci: (bi, ci, 0, 0)),
                   col(GLA_VAL, lambda bi, ci: (bi, last - ci, 0, 0))],
        out_shape=[out_shape, out_shape],
        scratch_shapes=[pltpu.VMEM((GLA_DV, GLA_KEY), F32), pltpu.VMEM((GLA_DV, GLA_KEY), F32)],
        compiler_params=_params("arbitrary", "arbitrary"),
    )(cm, cm, cmc, gup, gbias)


def _back_kernel(x1_ref, nat_ref, y_ref, of_ref, ob_ref, mod_ref, dskip_ref, gluw_ref,
                 glub_ref, s5out_ref, gnorm_ref, glaout_ref, wo_ref, n3_ref, wg_ref, wu_ref, wd_ref, fin_ref,
                 out_ref, *, fchunk):
    m = mod_ref[...]
    tm, d = x1_ref.shape
    ng, nj, _ = y_ref.shape
    s5w = ng * S5_GROUP

    yp = _from_chunk_major(lambda g, tt: y_ref[g, :, tt * LANES:(tt + 1) * LANES].astype(F32), ng, nj)
    ys = jnp.swapaxes(yp.reshape(S5_CHUNK, nj, s5w), 0, 1).reshape(tm, s5w)
    ya = ys + dskip_ref[...] * nat_ref[:, 0:s5w].astype(F32)
    ya = 0.5 * ya * (1.0 + jnp.tanh(GELU_C0 * (ya + GELU_C1 * (ya * ya * ya))))
    gl = jnp.dot(ya.astype(BF16), gluw_ref[...], preferred_element_type=F32) + glub_ref[...]
    ya = ya * jax.nn.sigmoid(gl)

    ocm = of_ref[...] + ob_ref[...]
    o = jnp.swapaxes(ocm, 0, 1).reshape(tm, GLA_VAL)
    heads = []
    for h in range(GLA_HEADS):
        oh = o[:, h * GLA_DV:(h + 1) * GLA_DV]
        heads.append(oh * lax.rsqrt(jnp.mean(oh * oh, axis=-1, keepdims=True) + RMS_EPS))
    r = nat_ref[:, s5w:s5w + GLA_VAL].astype(F32)
    yb = jnp.concatenate(heads, axis=1) * gnorm_ref[...] * (r * jax.nn.sigmoid(r))

    pa = jnp.dot(ya.astype(BF16), s5out_ref[...], preferred_element_type=F32)
    pb = jnp.dot(yb.astype(BF16), glaout_ref[...], preferred_element_type=F32)
    ga = nat_ref[:, s5w + GLA_VAL:s5w + GLA_VAL + d].astype(F32)
    gb = nat_ref[:, s5w + GLA_VAL + d:s5w + GLA_VAL + 2 * d].astype(F32)
    mg = jax.nn.sigmoid(ga) * pa + jax.nn.sigmoid(gb) * pb
    y = jnp.dot(mg.astype(BF16), wo_ref[...], preferred_element_type=F32)
    x2 = x1_ref[...] + m[5:6] * y
    h = _rms_mod(x2, n3_ref[...], m[6:7], m[7:8]).astype(BF16)
    x3 = _swiglu_residual(x2, h, m[8:9], wg_ref, wu_ref, wd_ref, fchunk)
    ms = jnp.mean(x3 * x3, axis=-1, keepdims=True)
    out_ref[...] = x3 * lax.rsqrt(ms + RMS_EPS) * fin_ref[...]


def _back(x1, nat, ys5, of, ob, mods3, w, tm, fchunk):
    b, l, d = x1.shape
    f = w["wg2"].shape[1]
    s5w = w["dskip"].shape[-1]
    ng = s5w // S5_GROUP
    nblk = l // tm
    tok = lambda width: pl.BlockSpec((None, tm, width), lambda bi, i: (bi, i, 0))
    colblk = pl.BlockSpec((None, GRID_W, tm // GRID_W, GLA_VAL), lambda bi, i: (bi, 0, i, 0))
    return pl.pallas_call(
        functools.partial(_back_kernel, fchunk=fchunk),
        grid=(b, nblk),
        in_specs=[tok(d), tok(NAT_W),
                  pl.BlockSpec((ng, tm // S5_CHUNK, S5_CHUNK * S5_GROUP), lambda bi, i: (0, bi * nblk + i, 0)),
                  colblk, colblk,
                  pl.BlockSpec((None, N_MOD, d), lambda bi, i: (bi, 0, 0)),
                  _const_spec((1, s5w)), _const_spec((s5w, s5w)), _const_spec((1, s5w)),
                  _const_spec((s5w, d)), _const_spec((1, GLA_VAL)), _const_spec((GLA_VAL, d)),
                  _const_spec((d, d)), _const_spec((1, d)), _const_spec((d, f)), _const_spec((d, f)),
                  _const_spec((f, d)), _const_spec((1, d))],
        out_specs=tok(d),
        out_shape=jax.ShapeDtypeStruct((b, l, d), F32),
        compiler_params=_params("arbitrary", "arbitrary"),
    )(x1, nat, ys5, of, ob, mods3, w["dskip"], w["gluw"], w["glub"], w["s5out"], w["gnorm"], w["glaout"],
      w["wo"], w["n3"], w["wg2"], w["wu2"], w["wd2"], w["fin"])


def kernel(x, c, ctx, c_ctx, ada_w, ada_b, ffn1_norm, ffn1_w_gate, ffn1_w_up, ffn1_w_down, mix_norm, w_in,
           s5_lambda_re, s5_lambda_im, s5_log_dt, s5_b_re, s5_b_im, s5_c_re, s5_c_im, s5_d, s5_glu_w, s5_glu_b,
           s5_out, gla_gate_up, gla_gate_b, gla_norm, gla_out, w_o, ffn2_norm, ffn2_w_gate, ffn2_w_up,
           ffn2_w_down, final_norm):
    b, l, d = x.shape
    lc = ctx.shape[1]
    assert ada_w.shape[0] == 1 and b + 1 <= SUBLANES
    s5w = s5_d.shape[-1]
    fchunk = MXU_TILE
    tm = min(512, l)
    assert tm % (GRID_W * 8) == 0 and l % tm == 0 and lc % (S5_CHUNK * S5_TILE) == 0 and lc % GLA_CHUNK == 0

    cvec = jnp.concatenate([c, c_ctx[None, :], jnp.zeros((SUBLANES - b - 1, d), F32)], axis=0)
    mods3 = _ada(cvec, ada_w[0], ada_b[0]).reshape(SUBLANES, N_MOD, d)

    wi = w_in[0]
    o_q, o_k, o_v, o_r = s5w, s5w + GLA_KEY, s5w + 2 * GLA_KEY, s5w + 2 * GLA_KEY + GLA_VAL
    o_glr = o_r + GLA_VAL
    o_ga = o_glr + 2 * GLA_GATE_RANK
    wnat = jnp.concatenate([wi[:, :o_q], wi[:, o_r:o_glr], wi[:, o_ga:]], axis=1)
    wcm = jnp.concatenate([wi[:, o_q:o_k] * (GLA_DK ** -0.5), wi[:, o_k:o_r],
                           jnp.pad(wi[:, o_glr:o_ga], ((0, 0), (0, LANES - 2 * GLA_GATE_RANK)))], axis=1)
    gup = jnp.zeros((2, LANES, GLA_KEY), F32)
    gup = gup.at[0, 0:GLA_GATE_RANK].set(gla_gate_up[0, 0])
    gup = gup.at[1, GLA_GATE_RANK:2 * GLA_GATE_RANK].set(gla_gate_up[0, 1])
    row = lambda v: v.reshape(1, -1).astype(F32)
    w = dict(
        n1=row(ffn1_norm[0]), wg1=ffn1_w_gate[0].astype(BF16), wu1=ffn1_w_up[0].astype(BF16),
        wd1=ffn1_w_down[0].astype(BF16), n2=row(mix_norm[0]), wnat=wnat.astype(BF16), wcm=wcm.astype(BF16),
        dskip=row(s5_d[0]), gluw=s5_glu_w[0].astype(BF16), glub=row(s5_glu_b[0]), s5out=s5_out[0].astype(BF16),
        gnorm=row(gla_norm[0]), glaout=gla_out[0].astype(BF16), wo=w_o[0].astype(BF16),
        n3=row(ffn2_norm[0]), wg2=ffn2_w_gate[0].astype(BF16), wu2=ffn2_w_up[0].astype(BF16),
        wd2=ffn2_w_down[0].astype(BF16), fin=row(final_norm))

    lat_row = lambda bi: bi
    ctx_row = lambda bi: b
    x1 = _ffn(x, mods3, lat_row, w["n1"], w["wg1"], w["wu1"], w["wd1"], tm, fchunk, 0)
    tc = min(tm, b * lc)
    c1 = _ffn(ctx.reshape(1, b * lc, d), mods3, ctx_row, w["n1"], w["wg1"], w["wu1"], w["wd1"], tc, fchunk, 0)
    nat, xl, cm = _proj(x1, mods3, lat_row, w, min(2 * tm, l), True)
    _, xc, cmc = _proj(c1, mods3, ctx_row, w, tc, False)
    cmc = cmc.reshape(b, lc, CM_W)

    kt, ws, cp, lamc = _s5_weights(s5_lambda_re[0], s5_lambda_im[0], s5_log_dt[0], s5_b_re[0], s5_b_im[0],
                                     s5_c_re[0], s5_c_im[0])
    ys5 = _s5(xc, xl, kt, ws, cp, lamc, gps=4, nb=b)

    of, ob = _gla(cm, cmc, gup.astype(BF16), gla_gate_b[0].reshape(2, 1, GLA_KEY).astype(F32))
    return _back(x1, nat, ys5, of, ob, mods3, w, tm, fchunk)
```

```python
import functools

import jax
import jax.numpy as jnp
from jax import lax
from jax.experimental import pallas as pl
from jax.experimental.pallas import tpu as pltpu

F32 = jnp.float32
BF16 = jnp.bfloat16

RMS_EPS = 1e-6
MACARON_WEIGHT = 0.5
GRID_W = 64
N_MOD = 9
S5_GROUP = 16
S5_STATE = 64
S5_CHUNK = 16
S5_TILE = 8
S5_PREFIX_TILES = 4
GLA_HEADS = 4
GLA_DK = 64
GLA_DV = 128
GLA_CHUNK = 64
GLA_GATE_RANK = 16
GLA_GATE_NORM = 16.0
GLA_COLS_PER_STEP = 8
GLA_KEY = GLA_HEADS * GLA_DK
GLA_VAL = GLA_HEADS * GLA_DV
LANES = 128
SUBLANES = 8
MXU_TILE = 256
GELU_C0 = 0.7978845608028654
GELU_C1 = 0.044715
NAT_W = 3072
CM_W = 2 * GLA_KEY + GLA_VAL + LANES
V7X_VMEM_LIMIT_BYTES = 56 * 1024 * 1024


def _params(*sem):
    return pltpu.CompilerParams(dimension_semantics=sem, vmem_limit_bytes=V7X_VMEM_LIMIT_BYTES)


def _const_spec(shape):
    nd = len(shape)
    return pl.BlockSpec(shape, lambda *_: (0,) * nd, pipeline_mode=pl.Buffered(1))


def _rms_mod(x, g, shift, scale):
    ms = jnp.mean(x * x, axis=-1, keepdims=True)
    return x * lax.rsqrt(ms + RMS_EPS) * (g * (1.0 + scale)) + shift


def _swiglu_residual(x, h, gate, wg_ref, wu_ref, wd_ref, fchunk):
    acc = None
    for f0 in range(0, wg_ref.shape[1], fchunk):
        gg = jnp.dot(h, wg_ref[:, f0:f0 + fchunk], preferred_element_type=F32)
        uu = jnp.dot(h, wu_ref[:, f0:f0 + fchunk], preferred_element_type=F32)
        a = (gg * jax.nn.sigmoid(gg) * uu).astype(BF16)
        o = jnp.dot(a, wd_ref[f0:f0 + fchunk, :], preferred_element_type=F32)
        acc = o if acc is None else acc + o
    return x + acc * (MACARON_WEIGHT * gate)


def _ada_kernel(c_ref, w_ref, b_ref, o_ref):
    cv = c_ref[...]
    s = cv * jax.nn.sigmoid(cv)
    o_ref[...] = jnp.dot(s, w_ref[...], preferred_element_type=F32,
                         precision=lax.Precision.HIGHEST) + b_ref[...]


def _ada(cvec, ada_w, ada_b):
    rows, d = cvec.shape
    n = ada_w.shape[1]
    bn = n // N_MOD if n % N_MOD == 0 else n
    return pl.pallas_call(
        _ada_kernel,
        grid=(n // bn,),
        in_specs=[pl.BlockSpec((rows, d), lambda j: (0, 0)),
                  pl.BlockSpec((d, bn), lambda j: (0, j)),
                  pl.BlockSpec((1, bn), lambda j: (0, j))],
        out_specs=pl.BlockSpec((rows, bn), lambda j: (0, j)),
        out_shape=jax.ShapeDtypeStruct((rows, n), F32),
        compiler_params=_params("arbitrary"),
    )(cvec, ada_w, ada_b.reshape(1, n))


def _ffn_kernel(x_ref, mod_ref, n_ref, wg_ref, wu_ref, wd_ref, o_ref, *, fchunk, mod0):
    x = x_ref[...]
    m = mod_ref[...]
    h = _rms_mod(x, n_ref[...], m[mod0:mod0 + 1], m[mod0 + 1:mod0 + 2]).astype(BF16)
    o_ref[...] = _swiglu_residual(x, h, m[mod0 + 2:mod0 + 3], wg_ref, wu_ref, wd_ref, fchunk)


def _ffn(x, mods3, mod_row, norm, wg, wu, wd, tm, fchunk, mod0):
    b, l, d = x.shape
    f = wg.shape[1]
    tok = pl.BlockSpec((None, tm, d), lambda bi, i: (bi, i, 0))
    return pl.pallas_call(
        functools.partial(_ffn_kernel, fchunk=fchunk, mod0=mod0),
        grid=(b, l // tm),
        in_specs=[tok, pl.BlockSpec((None, N_MOD, d), lambda bi, i: (mod_row(bi), 0, 0)),
                  _const_spec((1, d)), _const_spec((d, f)), _const_spec((d, f)), _const_spec((f, d))],
        out_specs=tok,
        out_shape=jax.ShapeDtypeStruct((b, l, d), F32),
        compiler_params=_params("arbitrary", "arbitrary"),
    )(x, mods3, norm, wg, wu, wd)


def _transpose_pieces(v):
    n = len(v)
    piece = lax.broadcasted_iota(jnp.int32, v[0].shape, 1) // S5_GROUP
    s = n // 2
    while s >= 1:
        keep = (piece & s) == 0
        nv = list(v)
        for i in range(n):
            if i & s == 0:
                a, b = v[i], v[i + s]
                nv[i] = jnp.where(keep, a, pltpu.roll(b, s * S5_GROUP, 1))
                nv[i + s] = jnp.where(keep, pltpu.roll(a, LANES - s * S5_GROUP, 1), b)
        v = nv
        s //= 2
    return v


def _to_chunk_major(up, store):
    nj = up.shape[0] // S5_CHUNK
    npc = LANES // S5_GROUP
    for gg in range(up.shape[1] // LANES):
        for tt in range(S5_CHUNK // npc):
            src = [up[(npc * tt + p) * nj:(npc * tt + p + 1) * nj, gg * LANES:(gg + 1) * LANES] for p in range(npc)]
            for gl, v in enumerate(_transpose_pieces(src)):
                store(gg * npc + gl, tt, v)


def _from_chunk_major(load, ng, nj):
    npc = LANES // S5_GROUP
    row_blocks = []
    for tt in range(S5_CHUNK // npc):
        per_p = [[] for _ in range(npc)]
        for gg in range(ng // npc):
            out = _transpose_pieces([load(gg * npc + gl, tt) for gl in range(npc)])
            for p in range(npc):
                per_p[p].append(out[p])
        row_blocks += [jnp.concatenate(blk, axis=1) for blk in per_p]
    return jnp.concatenate(row_blocks, axis=0)


def _proj_kernel(x_ref, mod_ref, n_ref, wnat_ref, wcm_ref, nat_ref, xs5_ref, cm_ref, *, col_major):
    x = x_ref[...]
    m = mod_ref[...]
    tm = x.shape[0]
    h2 = _rms_mod(x, n_ref[...], m[3:4], m[4:5]).astype(BF16)
    u = None
    ncol = 4 * MXU_TILE
    for c0 in range(0, NAT_W, ncol):
        p = jnp.dot(h2, wnat_ref[:, c0:c0 + ncol], preferred_element_type=F32)
        nat_ref[:, c0:c0 + ncol] = p.astype(BF16)
        if c0 == 0:
            u = p[:, 0:xs5_ref.shape[0] * S5_GROUP]
    pc = jnp.dot(h2, wcm_ref[...], preferred_element_type=F32)
    if col_major:
        pcm = pc.reshape(tm // GRID_W, GRID_W, CM_W)
        cm_ref[...] = jnp.swapaxes(pcm, 0, 1)
    else:
        cm_ref[...] = pc
    uf = u.reshape(tm // S5_CHUNK, S5_CHUNK, u.shape[1])
    up = jnp.swapaxes(uf, 0, 1).reshape(tm, u.shape[1])

    def store(g, tt, v):
        xs5_ref[g, :, tt * LANES:(tt + 1) * LANES] = v.astype(BF16)

    _to_chunk_major(up, store)


def _proj(x1, mods3, mod_row, w, tm, col_major):
    b, l, d = x1.shape
    ng = w["dskip"].shape[-1] // S5_GROUP
    nj = tm // S5_CHUNK
    nblk = l // tm
    tok = lambda width: pl.BlockSpec((None, tm, width), lambda bi, i: (bi, i, 0))
    if col_major:
        cm_spec = pl.BlockSpec((None, GRID_W, tm // GRID_W, CM_W), lambda bi, i: (bi, 0, i, 0))
        cm_shape = jax.ShapeDtypeStruct((b, GRID_W, l // GRID_W, CM_W), F32)
    else:
        cm_spec = tok(CM_W)
        cm_shape = jax.ShapeDtypeStruct((b, l, CM_W), F32)
    return pl.pallas_call(
        functools.partial(_proj_kernel, col_major=col_major),
        grid=(b, nblk),
        in_specs=[tok(d), pl.BlockSpec((None, N_MOD, d), lambda bi, i: (mod_row(bi), 0, 0)),
                  _const_spec((1, d)), _const_spec((d, NAT_W)), _const_spec((d, CM_W))],
        out_specs=[tok(NAT_W),
                   pl.BlockSpec((ng, nj, S5_CHUNK * S5_GROUP), lambda bi, i: (0, bi * nblk + i, 0)),
                   cm_spec],
        out_shape=[jax.ShapeDtypeStruct((b, l, NAT_W), BF16),
                   jax.ShapeDtypeStruct((ng, b * (l // S5_CHUNK), S5_CHUNK * S5_GROUP), BF16),
                   cm_shape],
        compiler_params=_params("arbitrary", "arbitrary"),
    )(x1, mods3, w["n2"], w["wnat"], w["wcm"])


def _s5_tile_prefix(sr, si, cst, fwd):
    ar, ai = sr, si
    for lvl, dist in enumerate((1, 2, 4)):
        sh = dist if fwd else S5_TILE - dist
        rr, ri = pltpu.roll(ar, sh, 0), pltpu.roll(ai, sh, 0)
        lr, li = cst[2 * lvl], cst[2 * lvl + 1]
        ar, ai = ar + (lr * rr - li * ri), ai + (lr * ri + li * rr)
    return ar, ai


def _s5_tile_carry(ar, ai, hr, hi, cst, fwd):
    row = lax.broadcasted_iota(jnp.int32, ar.shape, 0)
    keep = (row >= 1) if fwd else (row <= S5_TILE - 2)
    sh = 1 if fwd else S5_TILE - 1
    pr = jnp.where(keep, pltpu.roll(ar, sh, 0), 0.0)
    pi = jnp.where(keep, pltpu.roll(ai, sh, 0), 0.0)
    lpr, lpi, l8r, l8i = cst[6:10]
    hin_r = lpr * hr - lpi * hi + pr
    hin_i = lpr * hi + lpi * hr + pi
    e = S5_TILE - 1 if fwd else 0
    er = jnp.broadcast_to(ar[e:e + 1, :], ar.shape)
    ei = jnp.broadcast_to(ai[e:e + 1, :], ai.shape)
    return hin_r, hin_i, l8r * hr - l8i * hi + er, l8r * hi + l8i * hr + ei


def _s5_kernel(xc_ref, xl_ref, kt_ref, ws_ref, cp_ref, lam_ref, y_ref, s_ref, *, nb):
    npair = ws_ref.shape[0] // 2
    p = S5_STATE
    lo = lax.broadcasted_iota(jnp.int32, (ws_ref.shape[1], LANES), 1) < p
    half_turn = lambda a: pltpu.roll(a, p, 1)

    def state_in_operator(g):
        blocks = []
        for k in range(2):
            a = ws_ref[g, :, k * LANES:(k + 1) * LANES]
            if g % 2 == 0:
                blocks += [jnp.where(lo, a, 0.0), jnp.where(lo, half_turn(a), 0.0)]
            else:
                blocks += [jnp.where(lo, 0.0, half_turn(a)), jnp.where(lo, 0.0, a)]
        return jnp.concatenate(blocks, axis=1).astype(BF16)

    def state_out_operator(pp):
        zeros = jnp.zeros((p, cp_ref.shape[2]), BF16)
        rows = []
        for part in range(4):
            for gi in range(2):
                c = cp_ref[2 * pp + gi, part * p:(part + 1) * p, :]
                rows.append(jnp.concatenate([c, zeros] if gi == 0 else [zeros, c], axis=1))
        return jnp.concatenate(rows, axis=0)

    jc = xc_ref.shape[1] // nb
    jl = xl_ref.shape[1] // nb
    jt = jc + jl
    for pp in range(npair):
        w_in = [state_in_operator(2 * pp), state_in_operator(2 * pp + 1)]
        for src, j0, nj in ((xc_ref, 0, jc), (xl_ref, jc, jl)):
            s = jnp.dot(src[2 * pp], w_in[0], preferred_element_type=F32)
            s = s + jnp.dot(src[2 * pp + 1], w_in[1], preferred_element_type=F32)
            for bi in range(nb):
                s_ref[pp, bi * jt + j0:bi * jt + j0 + nj, :] = s[bi * nj:(bi + 1) * nj, :]

    nct = jc // S5_TILE
    nt = jt // S5_TILE
    zero = jnp.zeros((S5_TILE, LANES), F32)
    for pp in range(npair):

        def prefix(it, carry, pp=pp):
            tiles = []
            for u in range(S5_PREFIX_TILES):
                r0 = pl.multiple_of((it * S5_PREFIX_TILES + u) * S5_TILE, S5_TILE)
                for c0, fwd in ((0, True), (256, False)):
                    tiles.append((r0, c0, fwd, s_ref[pp, pl.ds(r0, S5_TILE), c0:c0 + 128],
                                  s_ref[pp, pl.ds(r0, S5_TILE), c0 + 128:c0 + 256]))
            cst = {fwd: [lam_ref[pp, (0 if fwd else 10) + k] for k in range(6)] for fwd in (True, False)}
            done = [(r0, c0) + _s5_tile_prefix(sr, si, cst[fwd], fwd) for r0, c0, fwd, sr, si in tiles]
            for r0, c0, ar, ai in done:
                s_ref[pp, pl.ds(r0, S5_TILE), c0:c0 + 128] = ar
                s_ref[pp, pl.ds(r0, S5_TILE), c0 + 128:c0 + 256] = ai
            return carry

        lax.fori_loop(0, nb * nt // S5_PREFIX_TILES, prefix, 0)

        def step(it, carry, pp=pp):
            mb = jnp.where(it < nct, nct - 1 - it, nt - 1 - (it - nct))
            cst = {True: [lam_ref[pp, k] for k in range(10)], False: [lam_ref[pp, 10 + k] for k in range(10)]}
            tiles = []
            for bi in range(nb):
                for c0, fwd, tile in ((0, True, it), (256, False, mb)):
                    r0 = pl.multiple_of(bi * jt + tile * S5_TILE, S5_TILE)
                    tiles.append((bi, r0, c0, fwd, s_ref[pp, pl.ds(r0, S5_TILE), c0:c0 + 128],
                                  s_ref[pp, pl.ds(r0, S5_TILE), c0 + 128:c0 + 256]))
            out = [[None, None] for _ in range(nb)]
            for bi, r0, c0, fwd, ar, ai in tiles:
                hr, hi = carry[bi][0 if fwd else 1]
                hin_r, hin_i, hr, hi = _s5_tile_carry(ar, ai, hr, hi, cst[fwd], fwd)
                out[bi][0 if fwd else 1] = (hr, hi)
                s_ref[pp, pl.ds(r0, S5_TILE), c0:c0 + 128] = hin_r
                s_ref[pp, pl.ds(r0, S5_TILE), c0 + 128:c0 + 256] = hin_i
            return tuple(tuple(o) for o in out)

        lax.fori_loop(0, nt, step, tuple(((zero, zero), (zero, zero)) for _ in range(nb)))

    tc = S5_CHUNK * S5_GROUP
    width = kt_ref.shape[2]
    for pp in range(npair):
        toeps = []
        for gi in range(2):
            kt = kt_ref[2 * pp + gi]
            blocks = []
            for s in range(S5_CHUNK):
                sh = (width - S5_GROUP * (S5_CHUNK - 1 - s)) % width
                blocks.append((kt if sh == 0 else pltpu.roll(kt, sh, 1))[:, 0:tc])
            toeps.append(jnp.concatenate(blocks, axis=0).astype(BF16))
        w_out = state_out_operator(pp)
        for bi in range(nb):
            hin = s_ref[pp, bi * jt + jc:(bi + 1) * jt, :].astype(BF16)
            yp = jnp.dot(hin, w_out, preferred_element_type=F32)
            for gi in range(2):
                g = 2 * pp + gi
                y = yp[:, gi * 256:(gi + 1) * 256] + jnp.dot(xl_ref[g, bi * jl:(bi + 1) * jl, :], toeps[gi],
                                                             preferred_element_type=F32)
                y_ref[g, bi * jl:(bi + 1) * jl, :] = y.astype(BF16)


def _s5(xc, xl, kt, ws, cp, lamc, gps, nb):
    ng, rc, _ = xc.shape
    rl = xl.shape[1]
    npair = gps // 2
    return pl.pallas_call(
        functools.partial(_s5_kernel, nb=nb),
        grid=(ng // gps,),
        in_specs=[pl.BlockSpec((gps, rc, 256), lambda i: (i, 0, 0)),
                  pl.BlockSpec((gps, rl, 256), lambda i: (i, 0, 0)),
                  pl.BlockSpec((gps,) + kt.shape[1:], lambda i: (i, 0, 0)),
                  pl.BlockSpec((gps,) + ws.shape[1:], lambda i: (i, 0, 0)),
                  pl.BlockSpec((gps,) + cp.shape[1:], lambda i: (i, 0, 0)),
                  pl.BlockSpec((npair, 20, S5_TILE, LANES), lambda i: (i, 0, 0, 0))],
        out_specs=pl.BlockSpec((gps, rl, 256), lambda i: (i, 0, 0)),
        out_shape=jax.ShapeDtypeStruct((ng, rl, 256), BF16),
        scratch_shapes=[pltpu.VMEM((npair, rc + rl, 512), F32)],
        compiler_params=_params("arbitrary"),
    )(xc, xl, kt, ws, cp, lamc)


def _cmul(ar, ai, br, bi):
    return ar * br - ai * bi, ar * bi + ai * br


def _s5_weights(lam_re, lam_im, log_dt, b_re, b_im, c_re, c_im):
    nd, ng, p = lam_re.shape
    t = S5_CHUNK
    tc = t * S5_GROUP
    lr = jnp.minimum(lam_re.astype(F32), -1e-4)
    li = lam_im.astype(F32)
    dt = jnp.exp(log_dt.astype(F32))[..., None]
    mag = jnp.exp(lr * dt)
    lbr, lbi = mag * jnp.cos(li * dt), mag * jnp.sin(li * dt)
    den = lr * lr + li * li
    fr = ((lbr - 1.0) * lr + lbi * li) / den
    fi = (lbi * lr - (lbr - 1.0) * li) / den
    bbr, bbi = _cmul(fr[..., None], fi[..., None], b_re.astype(F32), b_im.astype(F32))
    cr, ci = c_re.astype(F32), c_im.astype(F32)

    pr, pi = [jnp.ones_like(lbr)], [jnp.zeros_like(lbr)]
    for _ in range(t):
        nr, ni = _cmul(pr[-1], pi[-1], lbr, lbi)
        pr.append(nr)
        pi.append(ni)
    pr, pi = jnp.stack(pr), jnp.stack(pi)
    wr, wi = _cmul(pr[:t, ..., None], pi[:t, ..., None], bbr[None], bbi[None])

    kern = jnp.einsum("dgxp,kdgpc->dgkxc", cr, wr) - jnp.einsum("dgxp,kdgpc->dgkxc", ci, wi)
    ktf = kern[0].transpose(0, 3, 1, 2)
    ktb = kern[1][:, ::-1].transpose(0, 3, 1, 2)
    kt = jnp.concatenate([ktb[:, :, :t - 1], ktf[:, :, 0:1] + ktb[:, :, t - 1:t], ktf[:, :, 1:]], axis=2)
    kt = kt.reshape(ng, S5_GROUP, (2 * t - 1) * S5_GROUP)
    kt = jnp.pad(kt, ((0, 0), (0, 0), (0, 2 * tc - kt.shape[-1])))

    gscp = lambda a: a.transpose(1, 0, 3, 2).reshape(ng, tc, p)
    parts = [gscp(wr[::-1, 0]), gscp(wi[::-1, 0]), gscp(wr[:, 1]), gscp(wi[:, 1])]
    ws = jnp.concatenate(parts, axis=-1)

    def readout(d, powr, powi):
        mr, mi = _cmul(cr[d][None], ci[d][None], powr[:, :, None, :], powi[:, :, None, :])
        to_gptx = lambda a: a.transpose(1, 3, 0, 2).reshape(ng, p, tc)
        return [to_gptx(mr), to_gptx(-mi)]

    parts = readout(0, pr[1:t + 1, 0], pi[1:t + 1, 0]) + readout(1, pr[1:t + 1, 1][::-1], pi[1:t + 1, 1][::-1])
    cp = jnp.concatenate(parts, axis=1)

    row = jnp.arange(S5_TILE)
    l1 = (pr[t], pi[t])
    l2 = _cmul(*l1, *l1)
    l4 = _cmul(*l2, *l2)
    l8 = _cmul(*l4, *l4)
    rp = [(jnp.ones_like(lbr), jnp.zeros_like(lbr))]
    for _ in range(S5_TILE - 1):
        rp.append(_cmul(*rp[-1], *l1))
    planes = []
    for d in range(nd):
        valid = (lambda dist: row >= dist) if d == 0 else (lambda dist: row <= S5_TILE - 1 - dist)
        for (qr, qi), dist in ((l1, 1), (l2, 2), (l4, 4)):
            m = valid(dist).astype(F32)[:, None, None]
            planes += [m * qr[d][None], m * qi[d][None]]
        order = row if d == 0 else row[::-1]
        planes += [jnp.stack([rp[k][0][d] for k in range(S5_TILE)])[order],
                   jnp.stack([rp[k][1][d] for k in range(S5_TILE)])[order]]
        planes += [jnp.broadcast_to(l8[0][d], (S5_TILE, ng, p)), jnp.broadcast_to(l8[1][d], (S5_TILE, ng, p))]
    lamc = jnp.stack(planes)
    lamc = lamc.reshape(20, S5_TILE, ng // 2, 2 * p).transpose(2, 0, 1, 3)
    return kt, ws, cp.astype(BF16), lamc


def _chunk_cumsum(g, rev):
    sub = 8
    nt = g.shape[0] // sub
    row = lax.broadcasted_iota(jnp.int32, (sub, g.shape[1]), 0)
    out = [None] * nt
    off = None
    for kk in (range(nt - 1, -1, -1) if rev else range(nt)):
        x = g[kk * sub:(kk + 1) * sub, :]
        for dist in (1, 2, 4):
            if rev:
                x = x + jnp.where(row < sub - dist, pltpu.roll(x, sub - dist, 0), 0.0)
            else:
                x = x + jnp.where(row >= dist, pltpu.roll(x, dist, 0), 0.0)
        if off is not None:
            x = x + off
        out[kk] = x
        e = 0 if rev else sub - 1
        off = jnp.broadcast_to(x[e:e + 1, :], x.shape)
    return jnp.concatenate(out, axis=0)


def _gla_log_decay(glr, gup, gbias, rev):
    c = GLA_CHUNK
    z = jnp.dot(glr.astype(BF16), gup, preferred_element_type=F32) + gbias
    g = (jnp.minimum(z, 0.0) - jnp.log(1.0 + jnp.exp(-jnp.abs(z)))) * (1.0 / GLA_GATE_NORM)
    return [_chunk_cumsum(g[n * c:(n + 1) * c, :], rev) for n in range(g.shape[0] // c)]


def _gla_chunk(q, k, v, gc, st, rev, need_out):
    c = GLA_CHUNK
    i_ref = c // 2 - 1 if rev else c // 2
    i_last = 0 if rev else c - 1
    g_ref = gc[i_ref:i_ref + 1, :]
    g_last = gc[i_last:i_last + 1, :]
    lane_head = lax.broadcasted_iota(jnp.int32, (1, GLA_KEY), 1) // GLA_DK

    stack = lambda a: jnp.concatenate(
        [jnp.where(lane_head == h, a, 0.0) for h in range(GLA_HEADS)], axis=0).astype(BF16)
    dv2 = 2 * GLA_DV
    vt = jnp.concatenate(
        [jnp.concatenate([v[:, p * dv2:p * dv2 + GLA_DV], v[:, p * dv2 + GLA_DV:(p + 1) * dv2]], axis=0).T
         for p in range(GLA_HEADS // 2)], axis=1).astype(BF16)
    kl = k * jnp.exp(g_last - gc)
    kv = jnp.dot(vt, stack(kl), preferred_element_type=F32)
    st_new = st * jnp.exp(g_last) + kv
    if not need_out:
        return None, st_new

    qe = q * jnp.exp(gc - g_ref)
    ke = (k * jnp.exp(g_ref - gc)).astype(BF16)
    qg = q * jnp.exp(gc)
    nt_dims = (((1,), (1,)), ((), ()))
    sc = lax.dot_general(stack(qe), ke, nt_dims, preferred_element_type=F32)
    rs = lax.broadcasted_iota(jnp.int32, (GLA_HEADS * c, c), 0) % c
    cs = lax.broadcasted_iota(jnp.int32, (GLA_HEADS * c, c), 1)
    keep = (rs <= cs) if rev else (rs >= cs)
    sc = jnp.where(keep, sc, 0.0).astype(BF16)
    vb = v.astype(BF16)
    oo = lax.dot_general(stack(qg), st.astype(BF16), nt_dims, preferred_element_type=F32)
    o = jnp.concatenate(
        [jnp.dot(sc[h * c:(h + 1) * c, :], vb[:, h * GLA_DV:(h + 1) * GLA_DV], preferred_element_type=F32)
         + oo[h * c:(h + 1) * c, :] for h in range(GLA_HEADS)], axis=1)
    return o, st_new


def _gla_kernel(cf_ref, cb_ref, cc_ref, gup_ref, gbias_ref, of_ref, ob_ref, stf_ref, stb_ref):
    c = GLA_CHUNK
    q0, k0, v0, r0 = 0, GLA_KEY, 2 * GLA_KEY, 2 * GLA_KEY + GLA_VAL

    def decay(ref, d):
        return _gla_log_decay(ref[:, r0:CM_W], gup_ref[d], gbias_ref[d], d == 1)

    def run(ref, n, gcs, d, st, need_out):
        rows = slice(n * c, (n + 1) * c)
        q = ref[rows, q0:k0] if need_out else None
        return _gla_chunk(q, ref[rows, k0:v0], ref[rows, v0:r0], gcs[n], st, d == 1, need_out)

    @pl.when(pl.program_id(1) == 0)
    def _():
        nctx = cc_ref.shape[0] // c
        gf, gb = decay(cc_ref, 0), decay(cc_ref, 1)
        stf = jnp.zeros(stf_ref.shape, F32)
        stb = jnp.zeros(stb_ref.shape, F32)
        for n in range(nctx):
            _, stf = run(cc_ref, n, gf, 0, stf, False)
            _, stb = run(cc_ref, nctx - 1 - n, gb, 1, stb, False)
        stf_ref[...] = stf
        stb_ref[...] = stb

    ncols, rows_per_col, _ = cf_ref.shape
    nch = rows_per_col // c
    stf = stf_ref[...]
    stb = stb_ref[...]
    for j in range(ncols):
        jb = ncols - 1 - j
        gf, gb = decay(cf_ref.at[j], 0), decay(cb_ref.at[jb], 1)
        for n in range(nch):
            o, stf = run(cf_ref.at[j], n, gf, 0, stf, True)
            of_ref[j, n * c:(n + 1) * c, :] = o
            m = nch - 1 - n
            o, stb = run(cb_ref.at[jb], m, gb, 1, stb, True)
            ob_ref[jb, m * c:(m + 1) * c, :] = o
    stf_ref[...] = stf
    stb_ref[...] = stb


def _gla(cm, cmc, gup, gbias):
    b, ncol, rows, _ = cm.shape
    lc = cmc.shape[1]
    cps = GLA_COLS_PER_STEP
    last = ncol // cps - 1
    col = lambda width, fn: pl.BlockSpec((None, cps, rows, width), fn)
    out_shape = jax.ShapeDtypeStruct((b, ncol, rows, GLA_VAL), F32)
    return pl.pallas_call(
        _gla_kernel,
        grid=(b, ncol // cps),
        in_specs=[col(CM_W, lambda bi, ci: (bi, ci, 0, 0)),
                  col(CM_W, lambda bi, ci: (bi, last - ci, 0, 0)),
                  pl.BlockSpec((None, lc, CM_W), lambda bi, ci: (bi, 0, 0)),
                  _const_spec(gup.shape), _const_spec(gbias.shape)],
        out_specs=[col(GLA_VAL, lambda bi, ci: (bi, ci, 0, 0)),
                   col(GLA_VAL, lambda bi, ci: (bi, last - ci, 0, 0))],
        out_shape=[out_shape, out_shape],
        scratch_shapes=[pltpu.VMEM((GLA_DV, GLA_KEY), F32), pltpu.VMEM((GLA_DV, GLA_KEY), F32)],
        compiler_params=_params("arbitrary", "arbitrary"),
    )(cm, cm, cmc, gup, gbias)


def _back_kernel(x1_ref, nat_ref, y_ref, of_ref, ob_ref, mod_ref, dskip_ref, gluw_ref,
                 glub_ref, s5out_ref, gnorm_ref, glaout_ref, wo_ref, n3_ref, wg_ref, wu_ref, wd_ref, fin_ref,
                 out_ref, *, fchunk):
    m = mod_ref[...]
    tm, d = x1_ref.shape
    ng, nj, _ = y_ref.shape
    s5w = ng * S5_GROUP

    yp = _from_chunk_major(lambda g, tt: y_ref[g, :, tt * LANES:(tt + 1) * LANES].astype(F32), ng, nj)
    ys = jnp.swapaxes(yp.reshape(S5_CHUNK, nj, s5w), 0, 1).reshape(tm, s5w)
    ya = ys + dskip_ref[...] * nat_ref[:, 0:s5w].astype(F32)
    ya = 0.5 * ya * (1.0 + jnp.tanh(GELU_C0 * (ya + GELU_C1 * (ya * ya * ya))))
    gl = jnp.dot(ya.astype(BF16), gluw_ref[...], preferred_element_type=F32) + glub_ref[...]
    ya = ya * jax.nn.sigmoid(gl)

    ocm = of_ref[...] + ob_ref[...]
    o = jnp.swapaxes(ocm, 0, 1).reshape(tm, GLA_VAL)
    heads = []
    for h in range(GLA_HEADS):
        oh = o[:, h * GLA_DV:(h + 1) * GLA_DV]
        heads.append(oh * lax.rsqrt(jnp.mean(oh * oh, axis=-1, keepdims=True) + RMS_EPS))
    r = nat_ref[:, s5w:s5w + GLA_VAL].astype(F32)
    yb = jnp.concatenate(heads, axis=1) * gnorm_ref[...] * (r * jax.nn.sigmoid(r))

    pa = jnp.dot(ya.astype(BF16), s5out_ref[...], preferred_element_type=F32)
    pb = jnp.dot(yb.astype(BF16), glaout_ref[...], preferred_element_type=F32)
    ga = nat_ref[:, s5w + GLA_VAL:s5w + GLA_VAL + d].astype(F32)
    gb = nat_ref[:, s5w + GLA_VAL + d:s5w + GLA_VAL + 2 * d].astype(F32)
    mg = jax.nn.sigmoid(ga) * pa + jax.nn.sigmoid(gb) * pb
    y = jnp.dot(mg.astype(BF16), wo_ref[...], preferred_element_type=F32)
    x2 = x1_ref[...] + m[5:6] * y
    h = _rms_mod(x2, n3_ref[...], m[6:7], m[7:8]).astype(BF16)
    x3 = _swiglu_residual(x2, h, m[8:9], wg_ref, wu_ref, wd_ref, fchunk)
    ms = jnp.mean(x3 * x3, axis=-1, keepdims=True)
    out_ref[...] = x3 * lax.rsqrt(ms + RMS_EPS) * fin_ref[...]


def _back(x1, nat, ys5, of, ob, mods3, w, tm, fchunk):
    b, l, d = x1.shape
    f = w["wg2"].shape[1]
    s5w = w["dskip"].shape[-1]
    ng = s5w // S5_GROUP
    nblk = l // tm
    tok = lambda width: pl.BlockSpec((None, tm, width), lambda bi, i: (bi, i, 0))
    colblk = pl.BlockSpec((None, GRID_W, tm // GRID_W, GLA_VAL), lambda bi, i: (bi, 0, i, 0))
    return pl.pallas_call(
        functools.partial(_back_kernel, fchunk=fchunk),
        grid=(b, nblk),
        in_specs=[tok(d), tok(NAT_W),
                  pl.BlockSpec((ng, tm // S5_CHUNK, S5_CHUNK * S5_GROUP), lambda bi, i: (0, bi * nblk + i, 0)),
                  colblk, colblk,
                  pl.BlockSpec((None, N_MOD, d), lambda bi, i: (bi, 0, 0)),
                  _const_spec((1, s5w)), _const_spec((s5w, s5w)), _const_spec((1, s5w)),
                  _const_spec((s5w, d)), _const_spec((1, GLA_VAL)), _const_spec((GLA_VAL, d)),
                  _const_spec((d, d)), _const_spec((1, d)), _const_spec((d, f)), _const_spec((d, f)),
                  _const_spec((f, d)), _const_spec((1, d))],
        out_specs=tok(d),
        out_shape=jax.ShapeDtypeStruct((b, l, d), F32),
        compiler_params=_params("arbitrary", "arbitrary"),
    )(x1, nat, ys5, of, ob, mods3, w["dskip"], w["gluw"], w["glub"], w["s5out"], w["gnorm"], w["glaout"],
      w["wo"], w["n3"], w["wg2"], w["wu2"], w["wd2"], w["fin"])


def kernel(x, c, ctx, c_ctx, ada_w, ada_b, ffn1_norm, ffn1_w_gate, ffn1_w_up, ffn1_w_down, mix_norm, w_in,
           s5_lambda_re, s5_lambda_im, s5_log_dt, s5_b_re, s5_b_im, s5_c_re, s5_c_im, s5_d, s5_glu_w, s5_glu_b,
           s5_out, gla_gate_up, gla_gate_b, gla_norm, gla_out, w_o, ffn2_norm, ffn2_w_gate, ffn2_w_up,
           ffn2_w_down, final_norm):
    b, l, d = x.shape
    lc = ctx.shape[1]
    assert ada_w.shape[0] == 1 and b + 1 <= SUBLANES
    s5w = s5_d.shape[-1]
    fchunk = MXU_TILE
    tm = min(512, l)
    assert tm % (GRID_W * 8) == 0 and l % tm == 0 and lc % (S5_CHUNK * S5_TILE) == 0 and lc % GLA_CHUNK == 0

    cvec = jnp.concatenate([c, c_ctx[None, :], jnp.zeros((SUBLANES - b - 1, d), F32)], axis=0)
    mods3 = _ada(cvec, ada_w[0], ada_b[0]).reshape(SUBLANES, N_MOD, d)

    wi = w_in[0]
    o_q, o_k, o_v, o_r = s5w, s5w + GLA_KEY, s5w + 2 * GLA_KEY, s5w + 2 * GLA_KEY + GLA_VAL
    o_glr = o_r + GLA_VAL
    o_ga = o_glr + 2 * GLA_GATE_RANK
    wnat = jnp.concatenate([wi[:, :o_q], wi[:, o_r:o_glr], wi[:, o_ga:]], axis=1)
    wcm = jnp.concatenate([wi[:, o_q:o_k] * (GLA_DK ** -0.5), wi[:, o_k:o_r],
                           jnp.pad(wi[:, o_glr:o_ga], ((0, 0), (0, LANES - 2 * GLA_GATE_RANK)))], axis=1)
    gup = jnp.zeros((2, LANES, GLA_KEY), F32)
    gup = gup.at[0, 0:GLA_GATE_RANK].set(gla_gate_up[0, 0])
    gup = gup.at[1, GLA_GATE_RANK:2 * GLA_GATE_RANK].set(gla_gate_up[0, 1])
    row = lambda v: v.reshape(1, -1).astype(F32)
    w = dict(
        n1=row(ffn1_norm[0]), wg1=ffn1_w_gate[0].astype(BF16), wu1=ffn1_w_up[0].astype(BF16),
        wd1=ffn1_w_down[0].astype(BF16), n2=row(mix_norm[0]), wnat=wnat.astype(BF16), wcm=wcm.astype(BF16),
        dskip=row(s5_d[0]), gluw=s5_glu_w[0].astype(BF16), glub=row(s5_glu_b[0]), s5out=s5_out[0].astype(BF16),
        gnorm=row(gla_norm[0]), glaout=gla_out[0].astype(BF16), wo=w_o[0].astype(BF16),
        n3=row(ffn2_norm[0]), wg2=ffn2_w_gate[0].astype(BF16), wu2=ffn2_w_up[0].astype(BF16),
        wd2=ffn2_w_down[0].astype(BF16), fin=row(final_norm))

    lat_row = lambda bi: bi
    ctx_row = lambda bi: b
    x1 = _ffn(x, mods3, lat_row, w["n1"], w["wg1"], w["wu1"], w["wd1"], tm, fchunk, 0)
    tc = min(tm, b * lc)
    c1 = _ffn(ctx.reshape(1, b * lc, d), mods3, ctx_row, w["n1"], w["wg1"], w["wu1"], w["wd1"], tc, fchunk, 0)
    nat, xl, cm = _proj(x1, mods3, lat_row, w, min(2 * tm, l), True)
    _, xc, cmc = _proj(c1, mods3, ctx_row, w, tc, False)
    cmc = cmc.reshape(b, lc, CM_W)

    kt, ws, cp, lamc = _s5_weights(s5_lambda_re[0], s5_lambda_im[0], s5_log_dt[0], s5_b_re[0], s5_b_im[0],
                                     s5_c_re[0], s5_c_im[0])
    ys5 = _s5(xc, xl, kt, ws, cp, lamc, gps=4, nb=b)

    of, ob = _gla(cm, cmc, gup.astype(BF16), gla_gate_b[0].reshape(2, 1, GLA_KEY).astype(F32))
    return _back(x1, nat, ys5, of, ob, mods3, w, tm, fchunk)
```

```python
import functools

import jax
import jax.numpy as jnp
from jax import lax
from jax.experimental import pallas as pl
from jax.experimental.pallas import tpu as pltpu

F32 = jnp.float32
BF16 = jnp.bfloat16

RMS_EPS = 1e-6
MACARON_WEIGHT = 0.5
GRID_W = 64
N_MOD = 9
S5_GROUP = 16
S5_STATE = 64
S5_CHUNK = 16
S5_TILE = 8
S5_PREFIX_TILES = 4
GLA_HEADS = 4
GLA_DK = 64
GLA_DV = 128
GLA_CHUNK = 64
GLA_GATE_RANK = 16
GLA_GATE_NORM = 16.0
GLA_COLS_PER_STEP = 8
GLA_KEY = GLA_HEADS * GLA_DK
GLA_VAL = GLA_HEADS * GLA_DV
LANES = 128
SUBLANES = 8
MXU_TILE = 256
GELU_C0 = 0.7978845608028654
GELU_C1 = 0.044715
NAT_W = 3072
CM_W = 2 * GLA_KEY + GLA_VAL + LANES
V7X_VMEM_LIMIT_BYTES = 56 * 1024 * 1024


def _params(*sem):
    return pltpu.CompilerParams(dimension_semantics=sem, vmem_limit_bytes=V7X_VMEM_LIMIT_BYTES)


def _const_spec(shape):
    nd = len(shape)
    return pl.BlockSpec(shape, lambda *_: (0,) * nd, pipeline_mode=pl.Buffered(1))


def _rms_mod(x, g, shift, scale):
    ms = jnp.mean(x * x, axis=-1, keepdims=True)
    return x * lax.rsqrt(ms + RMS_EPS) * (g * (1.0 + scale)) + shift


def _swiglu_residual(x, h, gate, wg_ref, wu_ref, wd_ref, fchunk):
    acc = None
    for f0 in range(0, wg_ref.shape[1], fchunk):
        gg = jnp.dot(h, wg_ref[:, f0:f0 + fchunk], preferred_element_type=F32)
        uu = jnp.dot(h, wu_ref[:, f0:f0 + fchunk], preferred_element_type=F32)
        a = (gg * jax.nn.sigmoid(gg) * uu).astype(BF16)
        o = jnp.dot(a, wd_ref[f0:f0 + fchunk, :], preferred_element_type=F32)
        acc = o if acc is None else acc + o
    return x + acc * (MACARON_WEIGHT * gate)


def _ada_kernel(c_ref, w_ref, b_ref, o_ref):
    cv = c_ref[...]
    s = cv * jax.nn.sigmoid(cv)
    o_ref[...] = jnp.dot(s, w_ref[...], preferred_element_type=F32,
                         precision=lax.Precision.HIGHEST) + b_ref[...]


def _ada(cvec, ada_w, ada_b):
    rows, d = cvec.shape
    n = ada_w.shape[1]
    bn = n // N_MOD if n % N_MOD == 0 else n
    return pl.pallas_call(
        _ada_kernel,
        grid=(n // bn,),
        in_specs=[pl.BlockSpec((rows, d), lambda j: (0, 0)),
                  pl.BlockSpec((d, bn), lambda j: (0, j)),
                  pl.BlockSpec((1, bn), lambda j: (0, j))],
        out_specs=pl.BlockSpec((rows, bn), lambda j: (0, j)),
        out_shape=jax.ShapeDtypeStruct((rows, n), F32),
        compiler_params=_params("arbitrary"),
    )(cvec, ada_w, ada_b.reshape(1, n))


def _ffn_kernel(x_ref, mod_ref, n_ref, wg_ref, wu_ref, wd_ref, o_ref, *, fchunk, mod0):
    x = x_ref[...]
    m = mod_ref[...]
    h = _rms_mod(x, n_ref[...], m[mod0:mod0 + 1], m[mod0 + 1:mod0 + 2]).astype(BF16)
    o_ref[...] = _swiglu_residual(x, h, m[mod0 + 2:mod0 + 3], wg_ref, wu_ref, wd_ref, fchunk)


def _ffn(x, mods3, mod_row, norm, wg, wu, wd, tm, fchunk, mod0):
    b, l, d = x.shape
    f = wg.shape[1]
    tok = pl.BlockSpec((None, tm, d), lambda bi, i: (bi, i, 0))
    return pl.pallas_call(
        functools.partial(_ffn_kernel, fchunk=fchunk, mod0=mod0),
        grid=(b, l // tm),
        in_specs=[tok, pl.BlockSpec((None, N_MOD, d), lambda bi, i: (mod_row(bi), 0, 0)),
                  _const_spec((1, d)), _const_spec((d, f)), _const_spec((d, f)), _const_spec((f, d))],
        out_specs=tok,
        out_shape=jax.ShapeDtypeStruct((b, l, d), F32),
        compiler_params=_params("arbitrary", "arbitrary"),
    )(x, mods3, norm, wg, wu, wd)


def _transpose_pieces(v):
    n = len(v)
    piece = lax.broadcasted_iota(jnp.int32, v[0].shape, 1) // S5_GROUP
    s = n // 2
    while s >= 1:
        keep = (piece & s) == 0
        nv = list(v)
        for i in range(n):
            if i & s == 0:
                a, b = v[i], v[i + s]
                nv[i] = jnp.where(keep, a, pltpu.roll(b, s * S5_GROUP, 1))
                nv[i + s] = jnp.where(keep, pltpu.roll(a, LANES - s * S5_GROUP, 1), b)
        v = nv
        s //= 2
    return v


def _to_chunk_major(up, store):
    nj = up.shape[0] // S5_CHUNK
    npc = LANES // S5_GROUP
    for gg in range(up.shape[1] // LANES):
        for tt in range(S5_CHUNK // npc):
            src = [up[(npc * tt + p) * nj:(npc * tt + p + 1) * nj, gg * LANES:(gg + 1) * LANES] for p in range(npc)]
            for gl, v in enumerate(_transpose_pieces(src)):
                store(gg * npc + gl, tt, v)


def _from_chunk_major(load, ng, nj):
    npc = LANES // S5_GROUP
    row_blocks = []
    for tt in range(S5_CHUNK // npc):
        per_p = [[] for _ in range(npc)]
        for gg in range(ng // npc):
            out = _transpose_pieces([load(gg * npc + gl, tt) for gl in range(npc)])
            for p in range(npc):
                per_p[p].append(out[p])
        row_blocks += [jnp.concatenate(blk, axis=1) for blk in per_p]
    return jnp.concatenate(row_blocks, axis=0)


def _proj_kernel(x_ref, mod_ref, n_ref, wnat_ref, wcm_ref, nat_ref, xs5_ref, cm_ref, *, col_major):
    x = x_ref[...]
    m = mod_ref[...]
    tm = x.shape[0]
    h2 = _rms_mod(x, n_ref[...], m[3:4], m[4:5]).astype(BF16)
    u = None
    ncol = 4 * MXU_TILE
    for c0 in range(0, NAT_W, ncol):
        p = jnp.dot(h2, wnat_ref[:, c0:c0 + ncol], preferred_element_type=F32)
        nat_ref[:, c0:c0 + ncol] = p.astype(BF16)
        if c0 == 0:
            u = p[:, 0:xs5_ref.shape[0] * S5_GROUP]
    pc = jnp.dot(h2, wcm_ref[...], preferred_element_type=F32)
    if col_major:
        pcm = pc.reshape(tm // GRID_W, GRID_W, CM_W)
        cm_ref[...] = jnp.swapaxes(pcm, 0, 1)
    else:
        cm_ref[...] = pc
    uf = u.reshape(tm // S5_CHUNK, S5_CHUNK, u.shape[1])
    up = jnp.swapaxes(uf, 0, 1).reshape(tm, u.shape[1])

    def store(g, tt, v):
        xs5_ref[g, :, tt * LANES:(tt + 1) * LANES] = v.astype(BF16)

    _to_chunk_major(up, store)


def _proj(x1, mods3, mod_row, w, tm, col_major):
    b, l, d = x1.shape
    ng = w["dskip"].shape[-1] // S5_GROUP
    nj = tm // S5_CHUNK
    nblk = l // tm
    tok = lambda width: pl.BlockSpec((None, tm, width), lambda bi, i: (bi, i, 0))
    if col_major:
        cm_spec = pl.BlockSpec((None, GRID_W, tm // GRID_W, CM_W), lambda bi, i: (bi, 0, i, 0))
        cm_shape = jax.ShapeDtypeStruct((b, GRID_W, l // GRID_W, CM_W), F32)
    else:
        cm_spec = tok(CM_W)
        cm_shape = jax.ShapeDtypeStruct((b, l, CM_W), F32)
    return pl.pallas_call(
        functools.partial(_proj_kernel, col_major=col_major),
        grid=(b, nblk),
        in_specs=[tok(d), pl.BlockSpec((None, N_MOD, d), lambda bi, i: (mod_row(bi), 0, 0)),
                  _const_spec((1, d)), _const_spec((d, NAT_W)), _const_spec((d, CM_W))],
        out_specs=[tok(NAT_W),
                   pl.BlockSpec((ng, nj, S5_CHUNK * S5_GROUP), lambda bi, i: (0, bi * nblk + i, 0)),
                   cm_spec],
        out_shape=[jax.ShapeDtypeStruct((b, l, NAT_W), BF16),
                   jax.ShapeDtypeStruct((ng, b * (l // S5_CHUNK), S5_CHUNK * S5_GROUP), BF16),
                   cm_shape],
        compiler_params=_params("arbitrary", "arbitrary"),
    )(x1, mods3, w["n2"], w["wnat"], w["wcm"])


def _s5_tile_prefix(sr, si, cst, fwd):
    ar, ai = sr, si
    for lvl, dist in enumerate((1, 2, 4)):
        sh = dist if fwd else S5_TILE - dist
        rr, ri = pltpu.roll(ar, sh, 0), pltpu.roll(ai, sh, 0)
        lr, li = cst[2 * lvl], cst[2 * lvl + 1]
        ar, ai = ar + (lr * rr - li * ri), ai + (lr * ri + li * rr)
    return ar, ai


def _s5_tile_carry(ar, ai, hr, hi, cst, fwd):
    row = lax.broadcasted_iota(jnp.int32, ar.shape, 0)
    keep = (row >= 1) if fwd else (row <= S5_TILE - 2)
    sh = 1 if fwd else S5_TILE - 1
    pr = jnp.where(keep, pltpu.roll(ar, sh, 0), 0.0)
    pi = jnp.where(keep, pltpu.roll(ai, sh, 0), 0.0)
    lpr, lpi, l8r, l8i = cst[6:10]
    hin_r = lpr * hr - lpi * hi + pr
    hin_i = lpr * hi + lpi * hr + pi
    e = S5_TILE - 1 if fwd else 0
    er = jnp.broadcast_to(ar[e:e + 1, :], ar.shape)
    ei = jnp.broadcast_to(ai[e:e + 1, :], ai.shape)
    return hin_r, hin_i, l8r * hr - l8i * hi + er, l8r * hi + l8i * hr + ei


def _s5_kernel(xc_ref, xl_ref, kt_ref, ws_ref, cp_ref, lam_ref, y_ref, s_ref, *, nb):
    npair = ws_ref.shape[0] // 2
    p = S5_STATE
    lo = lax.broadcasted_iota(jnp.int32, (ws_ref.shape[1], LANES), 1) < p
    half_turn = lambda a: pltpu.roll(a, p, 1)

    def state_in_operator(g):
        blocks = []
        for k in range(2):
            a = ws_ref[g, :, k * LANES:(k + 1) * LANES]
            if g % 2 == 0:
                blocks += [jnp.where(lo, a, 0.0), jnp.where(lo, half_turn(a), 0.0)]
            else:
                blocks += [jnp.where(lo, 0.0, half_turn(a)), jnp.where(lo, 0.0, a)]
        return jnp.concatenate(blocks, axis=1).astype(BF16)

    def state_out_operator(pp):
        zeros = jnp.zeros((p, cp_ref.shape[2]), BF16)
        rows = []
        for part in range(4):
            for gi in range(2):
                c = cp_ref[2 * pp + gi, part * p:(part + 1) * p, :]
                rows.append(jnp.concatenate([c, zeros] if gi == 0 else [zeros, c], axis=1))
        return jnp.concatenate(rows, axis=0)

    jc = xc_ref.shape[1] // nb
    jl = xl_ref.shape[1] // nb
    jt = jc + jl
    for pp in range(npair):
        w_in = [state_in_operator(2 * pp), state_in_operator(2 * pp + 1)]
        for src, j0, nj in ((xc_ref, 0, jc), (xl_ref, jc, jl)):
            s = jnp.dot(src[2 * pp], w_in[0], preferred_element_type=F32)
            s = s + jnp.dot(src[2 * pp + 1], w_in[1], preferred_element_type=F32)
            for bi in range(nb):
                s_ref[pp, bi * jt + j0:bi * jt + j0 + nj, :] = s[bi * nj:(bi + 1) * nj, :]

    nct = jc // S5_TILE
    nt = jt // S5_TILE
    zero = jnp.zeros((S5_TILE, LANES), F32)
    for pp in range(npair):

        def prefix(it, carry, pp=pp):
            tiles = []
            for u in range(S5_PREFIX_TILES):
                r0 = pl.multiple_of((it * S5_PREFIX_TILES + u) * S5_TILE, S5_TILE)
                for c0, fwd in ((0, True), (256, False)):
                    tiles.append((r0, c0, fwd, s_ref[pp, pl.ds(r0, S5_TILE), c0:c0 + 128],
                                  s_ref[pp, pl.ds(r0, S5_TILE), c0 + 128:c0 + 256]))
            cst = {fwd: [lam_ref[pp, (0 if fwd else 10) + k] for k in range(6)] for fwd in (True, False)}
            done = [(r0, c0) + _s5_tile_prefix(sr, si, cst[fwd], fwd) for r0, c0, fwd, sr, si in tiles]
            for r0, c0, ar, ai in done:
                s_ref[pp, pl.ds(r0, S5_TILE), c0:c0 + 128] = ar
                s_ref[pp, pl.ds(r0, S5_TILE), c0 + 128:c0 + 256] = ai
            return carry

        lax.fori_loop(0, nb * nt // S5_PREFIX_TILES, prefix, 0)

        def step(it, carry, pp=pp):
            mb = jnp.where(it < nct, nct - 1 - it, nt - 1 - (it - nct))
            cst = {True: [lam_ref[pp, k] for k in range(10)], False: [lam_ref[pp, 10 + k] for k in range(10)]}
            tiles = []
            for bi in range(nb):
                for c0, fwd, tile in ((0, True, it), (256, False, mb)):
                    r0 = pl.multiple_of(bi * jt + tile * S5_TILE, S5_TILE)
                    tiles.append((bi, r0, c0, fwd, s_ref[pp, pl.ds(r0, S5_TILE), c0:c0 + 128],
                                  s_ref[pp, pl.ds(r0, S5_TILE), c0 + 128:c0 + 256]))
            out = [[None, None] for _ in range(nb)]
            for bi, r0, c0, fwd, ar, ai in tiles:
                hr, hi = carry[bi][0 if fwd else 1]
                hin_r, hin_i, hr, hi = _s5_tile_carry(ar, ai, hr, hi, cst[fwd], fwd)
                out[bi][0 if fwd else 1] = (hr, hi)
                s_ref[pp, pl.ds(r0, S5_TILE), c0:c0 + 128] = hin_r
                s_ref[pp, pl.ds(r0, S5_TILE), c0 + 128:c0 + 256] = hin_i
            return tuple(tuple(o) for o in out)

        lax.fori_loop(0, nt, step, tuple(((zero, zero), (zero, zero)) for _ in range(nb)))

    tc = S5_CHUNK * S5_GROUP
    width = kt_ref.shape[2]
    for pp in range(npair):
        toeps = []
        for gi in range(2):
            kt = kt_ref[2 * pp + gi]
            blocks = []
            for s in range(S5_CHUNK):
                sh = (width - S5_GROUP * (S5_CHUNK - 1 - s)) % width
                blocks.append((kt if sh == 0 else pltpu.roll(kt, sh, 1))[:, 0:tc])
            toeps.append(jnp.concatenate(blocks, axis=0).astype(BF16))
        w_out = state_out_operator(pp)
        for bi in range(nb):
            hin = s_ref[pp, bi * jt + jc:(bi + 1) * jt, :].astype(BF16)
            yp = jnp.dot(hin, w_out, preferred_element_type=F32)
            for gi in range(2):
                g = 2 * pp + gi
                y = yp[:, gi * 256:(gi + 1) * 256] + jnp.dot(xl_ref[g, bi * jl:(bi + 1) * jl, :], toeps[gi],
                                                             preferred_element_type=F32)
                y_ref[g, bi * jl:(bi + 1) * jl, :] = y.astype(BF16)


def _s5(xc, xl, kt, ws, cp, lamc, gps, nb):
    ng, rc, _ = xc.shape
    rl = xl.shape[1]
    npair = gps // 2
    return pl.pallas_call(
        functools.partial(_s5_kernel, nb=nb),
        grid=(ng // gps,),
        in_specs=[pl.BlockSpec((gps, rc, 256), lambda i: (i, 0, 0)),
                  pl.BlockSpec((gps, rl, 256), lambda i: (i, 0, 0)),
                  pl.BlockSpec((gps,) + kt.shape[1:], lambda i: (i, 0, 0)),
                  pl.BlockSpec((gps,) + ws.shape[1:], lambda i: (i, 0, 0)),
                  pl.BlockSpec((gps,) + cp.shape[1:], lambda i: (i, 0, 0)),
                  pl.BlockSpec((npair, 20, S5_TILE, LANES), lambda i: (i, 0, 0, 0))],
        out_specs=pl.BlockSpec((gps, rl, 256), lambda i: (i, 0, 0)),
        out_shape=jax.ShapeDtypeStruct((ng, rl, 256), BF16),
        scratch_shapes=[pltpu.VMEM((npair, rc + rl, 512), F32)],
        compiler_params=_params("arbitrary"),
    )(xc, xl, kt, ws, cp, lamc)


def _cmul(ar, ai, br, bi):
    return ar * br - ai * bi, ar * bi + ai * br


def _s5_weights(lam_re, lam_im, log_dt, b_re, b_im, c_re, c_im):
    nd, ng, p = lam_re.shape
    t = S5_CHUNK
    tc = t * S5_GROUP
    lr = jnp.minimum(lam_re.astype(F32), -1e-4)
    li = lam_im.astype(F32)
    dt = jnp.exp(log_dt.astype(F32))[..., None]
    mag = jnp.exp(lr * dt)
    lbr, lbi = mag * jnp.cos(li * dt), mag * jnp.sin(li * dt)
    den = lr * lr + li * li
    fr = ((lbr - 1.0) * lr + lbi * li) / den
    fi = (lbi * lr - (lbr - 1.0) * li) / den
    bbr, bbi = _cmul(fr[..., None], fi[..., None], b_re.astype(F32), b_im.astype(F32))
    cr, ci = c_re.astype(F32), c_im.astype(F32)

    pr, pi = [jnp.ones_like(lbr)], [jnp.zeros_like(lbr)]
    for _ in range(t):
        nr, ni = _cmul(pr[-1], pi[-1], lbr, lbi)
        pr.append(nr)
        pi.append(ni)
    pr, pi = jnp.stack(pr), jnp.stack(pi)
    wr, wi = _cmul(pr[:t, ..., None], pi[:t, ..., None], bbr[None], bbi[None])

    def taps(d, lags):
        powr = pr[lags, d].transpose(1, 0, 2)[:, :, None, :]
        powi = pi[lags, d].transpose(1, 0, 2)[:, :, None, :]
        mr, mi = _cmul(cr[d][:, None], ci[d][:, None], powr, powi)
        mr, mi = mr.reshape(ng, tc, p), mi.reshape(ng, tc, p)
        return jnp.einsum("gpc,gmp->gcm", bbr[d], mr) - jnp.einsum("gpc,gmp->gcm", bbi[d], mi)

    lag = jnp.arange(t)
    kt = (jnp.pad(taps(1, lag[::-1]), ((0, 0), (0, 0), (0, tc)))
          + jnp.pad(taps(0, lag), ((0, 0), (0, 0), (tc - S5_GROUP, S5_GROUP))))

    gscp = lambda a: a.transpose(1, 0, 3, 2).reshape(ng, tc, p)
    parts = [gscp(wr[::-1, 0]), gscp(wi[::-1, 0]), gscp(wr[:, 1]), gscp(wi[:, 1])]
    ws = jnp.concatenate(parts, axis=-1)

    def readout(d, powr, powi):
        mr, mi = _cmul(cr[d][None], ci[d][None], powr[:, :, None, :], powi[:, :, None, :])
        to_gptx = lambda a: a.transpose(1, 3, 0, 2).reshape(ng, p, tc)
        return [to_gptx(mr), to_gptx(-mi)]

    parts = readout(0, pr[1:t + 1, 0], pi[1:t + 1, 0]) + readout(1, pr[1:t + 1, 1][::-1], pi[1:t + 1, 1][::-1])
    cp = jnp.concatenate(parts, axis=1)

    row = jnp.arange(S5_TILE)
    l1 = (pr[t], pi[t])
    l2 = _cmul(*l1, *l1)
    l4 = _cmul(*l2, *l2)
    l8 = _cmul(*l4, *l4)
    rp = [(jnp.ones_like(lbr), jnp.zeros_like(lbr))]
    for _ in range(S5_TILE - 1):
        rp.append(_cmul(*rp[-1], *l1))
    planes = []
    for d in range(nd):
        valid = (lambda dist: row >= dist) if d == 0 else (lambda dist: row <= S5_TILE - 1 - dist)
        for (qr, qi), dist in ((l1, 1), (l2, 2), (l4, 4)):
            m = valid(dist).astype(F32)[:, None, None]
            planes += [m * qr[d][None], m * qi[d][None]]
        order = row if d == 0 else row[::-1]
        planes += [jnp.stack([rp[k][0][d] for k in range(S5_TILE)])[order],
                   jnp.stack([rp[k][1][d] for k in range(S5_TILE)])[order]]
        planes += [jnp.broadcast_to(l8[0][d], (S5_TILE, ng, p)), jnp.broadcast_to(l8[1][d], (S5_TILE, ng, p))]
    lamc = jnp.stack(planes)
    lamc = lamc.reshape(20, S5_TILE, ng // 2, 2 * p).transpose(2, 0, 1, 3)
    return kt, ws, cp.astype(BF16), lamc


def _chunk_cumsum(g, rev):
    sub = 8
    nt = g.shape[0] // sub
    row = lax.broadcasted_iota(jnp.int32, (sub, g.shape[1]), 0)
    out = [None] * nt
    off = None
    for kk in (range(nt - 1, -1, -1) if rev else range(nt)):
        x = g[kk * sub:(kk + 1) * sub, :]
        for dist in (1, 2, 4):
            if rev:
                x = x + jnp.where(row < sub - dist, pltpu.roll(x, sub - dist, 0), 0.0)
            else:
                x = x + jnp.where(row >= dist, pltpu.roll(x, dist, 0), 0.0)
        if off is not None:
            x = x + off
        out[kk] = x
        e = 0 if rev else sub - 1
        off = jnp.broadcast_to(x[e:e + 1, :], x.shape)
    return jnp.concatenate(out, axis=0)


def _gla_log_decay(glr, gup, gbias, rev):
    c = GLA_CHUNK
    z = jnp.dot(glr.astype(BF16), gup, preferred_element_type=F32) + gbias
    g = (jnp.minimum(z, 0.0) - jnp.log(1.0 + jnp.exp(-jnp.abs(z)))) * (1.0 / GLA_GATE_NORM)
    return [_chunk_cumsum(g[n * c:(n + 1) * c, :], rev) for n in range(g.shape[0] // c)]


def _gla_chunk(q, k, v, gc, st, rev, need_out):
    c = GLA_CHUNK
    i_ref = c // 2 - 1 if rev else c // 2
    i_last = 0 if rev else c - 1
    g_ref = gc[i_ref:i_ref + 1, :]
    g_last = gc[i_last:i_last + 1, :]
    lane_head = lax.broadcasted_iota(jnp.int32, (1, GLA_KEY), 1) // GLA_DK

    stack = lambda a: jnp.concatenate(
        [jnp.where(lane_head == h, a, 0.0) for h in range(GLA_HEADS)], axis=0).astype(BF16)
    dv2 = 2 * GLA_DV
    vt = jnp.concatenate(
        [jnp.concatenate([v[:, p * dv2:p * dv2 + GLA_DV], v[:, p * dv2 + GLA_DV:(p + 1) * dv2]], axis=0).T
         for p in range(GLA_HEADS // 2)], axis=1).astype(BF16)
    kl = k * jnp.exp(g_last - gc)
    kv = jnp.dot(vt, stack(kl), preferred_element_type=F32)
    st_new = st * jnp.exp(g_last) + kv
    if not need_out:
        return None, st_new

    qe = q * jnp.exp(gc - g_ref)
    ke = (k * jnp.exp(g_ref - gc)).astype(BF16)
    qg = q * jnp.exp(gc)
    nt_dims = (((1,), (1,)), ((), ()))
    sc = lax.dot_general(stack(qe), ke, nt_dims, preferred_element_type=F32)
    rs = lax.broadcasted_iota(jnp.int32, (GLA_HEADS * c, c), 0) % c
    cs = lax.broadcasted_iota(jnp.int32, (GLA_HEADS * c, c), 1)
    keep = (rs <= cs) if rev else (rs >= cs)
    sc = jnp.where(keep, sc, 0.0).astype(BF16)
    vb = v.astype(BF16)
    oo = lax.dot_general(stack(qg), st.astype(BF16), nt_dims, preferred_element_type=F32)
    o = jnp.concatenate(
        [jnp.dot(sc[h * c:(h + 1) * c, :], vb[:, h * GLA_DV:(h + 1) * GLA_DV], preferred_element_type=F32)
         + oo[h * c:(h + 1) * c, :] for h in range(GLA_HEADS)], axis=1)
    return o, st_new


def _gla_kernel(cf_ref, cb_ref, cc_ref, gup_ref, gbias_ref, of_ref, ob_ref, stf_ref, stb_ref):
    c = GLA_CHUNK
    q0, k0, v0, r0 = 0, GLA_KEY, 2 * GLA_KEY, 2 * GLA_KEY + GLA_VAL

    def decay(ref, d):
        return _gla_log_decay(ref[:, r0:CM_W], gup_ref[d], gbias_ref[d], d == 1)

    def run(ref, n, gcs, d, st, need_out):
        rows = slice(n * c, (n + 1) * c)
        q = ref[rows, q0:k0] if need_out else None
        return _gla_chunk(q, ref[rows, k0:v0], ref[rows, v0:r0], gcs[n], st, d == 1, need_out)

    @pl.when(pl.program_id(1) == 0)
    def _():
        nctx = cc_ref.shape[0] // c
        gf, gb = decay(cc_ref, 0), decay(cc_ref, 1)
        stf = jnp.zeros(stf_ref.shape, F32)
        stb = jnp.zeros(stb_ref.shape, F32)
        for n in range(nctx):
            _, stf = run(cc_ref, n, gf, 0, stf, False)
            _, stb = run(cc_ref, nctx - 1 - n, gb, 1, stb, False)
        stf_ref[...] = stf
        stb_ref[...] = stb

    ncols, rows_per_col, _ = cf_ref.shape
    nch = rows_per_col // c
    stf = stf_ref[...]
    stb = stb_ref[...]
    for j in range(ncols):
        jb = ncols - 1 - j
        gf, gb = decay(cf_ref.at[j], 0), decay(cb_ref.at[jb], 1)
        for n in range(nch):
            o, stf = run(cf_ref.at[j], n, gf, 0, stf, True)
            of_ref[j, n * c:(n + 1) * c, :] = o
            m = nch - 1 - n
            o, stb = run(cb_ref.at[jb], m, gb, 1, stb, True)
            ob_ref[jb, m * c:(m + 1) * c, :] = o
    stf_ref[...] = stf
    stb_ref[...] = stb


def _gla(cm, cmc, gup, gbias):
    b, ncol, rows, _ = cm.shape
    lc = cmc.shape[1]
    cps = GLA_COLS_PER_STEP
    last = ncol // cps - 1
    col = lambda width, fn: pl.BlockSpec((None, cps, rows, width), fn)
    out_shape = jax.ShapeDtypeStruct((b, ncol, rows, GLA_VAL), F32)
    return pl.pallas_call(
        _gla_kernel,
        grid=(b, ncol // cps),
        in_specs=[col(CM_W, lambda bi, ci: (bi, ci, 0, 0)),
                  col(CM_W, lambda bi, ci: (bi, last - ci, 0, 0)),
                  pl.BlockSpec((None, lc, CM_W), lambda bi, ci: (bi, 0, 0)),
                  _const_spec(gup.shape), _const_spec(gbias.shape)],
        out_specs=[col(GLA_VAL, lambda bi, ci: (bi, ci, 0, 0)),
                   col(GLA_VAL, lambda bi, ci: (bi, last - ci, 0, 0))],
        out_shape=[out_shape, out_shape],
        scratch_shapes=[pltpu.VMEM((GLA_DV, GLA_KEY), F32), pltpu.VMEM((GLA_DV, GLA_KEY), F32)],
        compiler_params=_params("arbitrary", "arbitrary"),
    )(cm, cm, cmc, gup, gbias)


def _back_kernel(x1_ref, nat_ref, y_ref, of_ref, ob_ref, mod_ref, dskip_ref, gluw_ref,
                 glub_ref, s5out_ref, gnorm_ref, glaout_ref, wo_ref, n3_ref, wg_ref, wu_ref, wd_ref, fin_ref,
                 out_ref, *, fchunk):
    m = mod_ref[...]
    tm, d = x1_ref.shape
    ng, nj, _ = y_ref.shape
    s5w = ng * S5_GROUP

    yp = _from_chunk_major(lambda g, tt: y_ref[g, :, tt * LANES:(tt + 1) * LANES].astype(F32), ng, nj)
    ys = jnp.swapaxes(yp.reshape(S5_CHUNK, nj, s5w), 0, 1).reshape(tm, s5w)
    ya = ys + dskip_ref[...] * nat_ref[:, 0:s5w].astype(F32)
    ya = 0.5 * ya * (1.0 + jnp.tanh(GELU_C0 * (ya + GELU_C1 * (ya * ya * ya))))
    gl = jnp.dot(ya.astype(BF16), gluw_ref[...], preferred_element_type=F32) + glub_ref[...]
    ya = ya * jax.nn.sigmoid(gl)

    ocm = of_ref[...] + ob_ref[...]
    o = jnp.swapaxes(ocm, 0, 1).reshape(tm, GLA_VAL)
    heads = []
    for h in range(GLA_HEADS):
        oh = o[:, h * GLA_DV:(h + 1) * GLA_DV]
        heads.append(oh * lax.rsqrt(jnp.mean(oh * oh, axis=-1, keepdims=True) + RMS_EPS))
    r = nat_ref[:, s5w:s5w + GLA_VAL].astype(F32)
    yb = jnp.concatenate(heads, axis=1) * gnorm_ref[...] * (r * jax.nn.sigmoid(r))

    pa = jnp.dot(ya.astype(BF16), s5out_ref[...], preferred_element_type=F32)
    pb = jnp.dot(yb.astype(BF16), glaout_ref[...], preferred_element_type=F32)
    ga = nat_ref[:, s5w + GLA_VAL:s5w + GLA_VAL + d].astype(F32)
    gb = nat_ref[:, s5w + GLA_VAL + d:s5w + GLA_VAL + 2 * d].astype(F32)
    mg = jax.nn.sigmoid(ga) * pa + jax.nn.sigmoid(gb) * pb
    y = jnp.dot(mg.astype(BF16), wo_ref[...], preferred_element_type=F32)
    x2 = x1_ref[...] + m[5:6] * y
    h = _rms_mod(x2, n3_ref[...], m[6:7], m[7:8]).astype(BF16)
    x3 = _swiglu_residual(x2, h, m[8:9], wg_ref, wu_ref, wd_ref, fchunk)
    ms = jnp.mean(x3 * x3, axis=-1, keepdims=True)
    out_ref[...] = x3 * lax.rsqrt(ms + RMS_EPS) * fin_ref[...]


def _back(x1, nat, ys5, of, ob, mods3, w, tm, fchunk):
    b, l, d = x1.shape
    f = w["wg2"].shape[1]
    s5w = w["dskip"].shape[-1]
    ng = s5w // S5_GROUP
    nblk = l // tm
    tok = lambda width: pl.BlockSpec((None, tm, width), lambda bi, i: (bi, i, 0))
    colblk = pl.BlockSpec((None, GRID_W, tm // GRID_W, GLA_VAL), lambda bi, i: (bi, 0, i, 0))
    return pl.pallas_call(
        functools.partial(_back_kernel, fchunk=fchunk),
        grid=(b, nblk),
        in_specs=[tok(d), tok(NAT_W),
                  pl.BlockSpec((ng, tm // S5_CHUNK, S5_CHUNK * S5_GROUP), lambda bi, i: (0, bi * nblk + i, 0)),
                  colblk, colblk,
                  pl.BlockSpec((None, N_MOD, d), lambda bi, i: (bi, 0, 0)),
                  _const_spec((1, s5w)), _const_spec((s5w, s5w)), _const_spec((1, s5w)),
                  _const_spec((s5w, d)), _const_spec((1, GLA_VAL)), _const_spec((GLA_VAL, d)),
                  _const_spec((d, d)), _const_spec((1, d)), _const_spec((d, f)), _const_spec((d, f)),
                  _const_spec((f, d)), _const_spec((1, d))],
        out_specs=tok(d),
        out_shape=jax.ShapeDtypeStruct((b, l, d), F32),
        compiler_params=_params("arbitrary", "arbitrary"),
    )(x1, nat, ys5, of, ob, mods3, w["dskip"], w["gluw"], w["glub"], w["s5out"], w["gnorm"], w["glaout"],
      w["wo"], w["n3"], w["wg2"], w["wu2"], w["wd2"], w["fin"])


def kernel(x, c, ctx, c_ctx, ada_w, ada_b, ffn1_norm, ffn1_w_gate, ffn1_w_up, ffn1_w_down, mix_norm, w_in,
           s5_lambda_re, s5_lambda_im, s5_log_dt, s5_b_re, s5_b_im, s5_c_re, s5_c_im, s5_d, s5_glu_w, s5_glu_b,
           s5_out, gla_gate_up, gla_gate_b, gla_norm, gla_out, w_o, ffn2_norm, ffn2_w_gate, ffn2_w_up,
           ffn2_w_down, final_norm):
    b, l, d = x.shape
    lc = ctx.shape[1]
    assert ada_w.shape[0] == 1 and b + 1 <= SUBLANES
    s5w = s5_d.shape[-1]
    fchunk = MXU_TILE
    tm = min(512, l)
    assert tm % (GRID_W * 8) == 0 and l % tm == 0 and lc % (S5_CHUNK * S5_TILE) == 0 and lc % GLA_CHUNK == 0

    cvec = jnp.concatenate([c, c_ctx[None, :], jnp.zeros((SUBLANES - b - 1, d), F32)], axis=0)
    mods3 = _ada(cvec, ada_w[0], ada_b[0]).reshape(SUBLANES, N_MOD, d)

    wi = w_in[0]
    o_q, o_k, o_v, o_r = s5w, s5w + GLA_KEY, s5w + 2 * GLA_KEY, s5w + 2 * GLA_KEY + GLA_VAL
    o_glr = o_r + GLA_VAL
    o_ga = o_glr + 2 * GLA_GATE_RANK
    wnat = jnp.concatenate([wi[:, :o_q], wi[:, o_r:o_glr], wi[:, o_ga:]], axis=1)
    wcm = jnp.concatenate([wi[:, o_q:o_k] * (GLA_DK ** -0.5), wi[:, o_k:o_r],
                           jnp.pad(wi[:, o_glr:o_ga], ((0, 0), (0, LANES - 2 * GLA_GATE_RANK)))], axis=1)
    gup = jnp.zeros((2, LANES, GLA_KEY), F32)
    gup = gup.at[0, 0:GLA_GATE_RANK].set(gla_gate_up[0, 0])
    gup = gup.at[1, GLA_GATE_RANK:2 * GLA_GATE_RANK].set(gla_gate_up[0, 1])
    row = lambda v: v.reshape(1, -1).astype(F32)
    w = dict(
        n1=row(ffn1_norm[0]), wg1=ffn1_w_gate[0].astype(BF16), wu1=ffn1_w_up[0].astype(BF16),
        wd1=ffn1_w_down[0].astype(BF16), n2=row(mix_norm[0]), wnat=wnat.astype(BF16), wcm=wcm.astype(BF16),
        dskip=row(s5_d[0]), gluw=s5_glu_w[0].astype(BF16), glub=row(s5_glu_b[0]), s5out=s5_out[0].astype(BF16),
        gnorm=row(gla_norm[0]), glaout=gla_out[0].astype(BF16), wo=w_o[0].astype(BF16),
        n3=row(ffn2_norm[0]), wg2=ffn2_w_gate[0].astype(BF16), wu2=ffn2_w_up[0].astype(BF16),
        wd2=ffn2_w_down[0].astype(BF16), fin=row(final_norm))

    lat_row = lambda bi: bi
    ctx_row = lambda bi: b
    x1 = _ffn(x, mods3, lat_row, w["n1"], w["wg1"], w["wu1"], w["wd1"], tm, fchunk, 0)
    tc = min(tm, b * lc)
    c1 = _ffn(ctx.reshape(1, b * lc, d), mods3, ctx_row, w["n1"], w["wg1"], w["wu1"], w["wd1"], tc, fchunk, 0)
    nat, xl, cm = _proj(x1, mods3, lat_row, w, min(2 * tm, l), True)
    _, xc, cmc = _proj(c1, mods3, ctx_row, w, tc, False)
    cmc = cmc.reshape(b, lc, CM_W)

    kt, ws, cp, lamc = _s5_weights(s5_lambda_re[0], s5_lambda_im[0], s5_log_dt[0], s5_b_re[0], s5_b_im[0],
                                     s5_c_re[0], s5_c_im[0])
    ys5 = _s5(xc, xl, kt, ws, cp, lamc, gps=4, nb=b)

    of, ob = _gla(cm, cmc, gup.astype(BF16), gla_gate_b[0].reshape(2, 1, GLA_KEY).astype(F32))
    return _back(x1, nat, ys5, of, ob, mods3, w, tm, fchunk)
```

```python
import functools

import jax
import jax.numpy as jnp
from jax import lax
from jax.experimental import pallas as pl
from jax.experimental.pallas import tpu as pltpu

F32 = jnp.float32
BF16 = jnp.bfloat16

RMS_EPS = 1e-6
MACARON_WEIGHT = 0.5
GRID_W = 64
N_MOD = 9
S5_GROUP = 16
S5_STATE = 64
S5_CHUNK = 16
S5_TILE = 8
S5_PREFIX_TILES = 4
GLA_HEADS = 4
GLA_DK = 64
GLA_DV = 128
GLA_CHUNK = 64
GLA_GATE_RANK = 16
GLA_GATE_NORM = 16.0
GLA_COLS_PER_STEP = 8
GLA_KEY = GLA_HEADS * GLA_DK
GLA_VAL = GLA_HEADS * GLA_DV
LANES = 128
SUBLANES = 8
MXU_TILE = 256
GELU_C0 = 0.7978845608028654
GELU_C1 = 0.044715
NAT_W = 3072
CM_W = 2 * GLA_KEY + GLA_VAL + LANES
V7X_VMEM_LIMIT_BYTES = 56 * 1024 * 1024


def _params(*sem):
    return pltpu.CompilerParams(dimension_semantics=sem, vmem_limit_bytes=V7X_VMEM_LIMIT_BYTES)


def _const_spec(shape):
    nd = len(shape)
    return pl.BlockSpec(shape, lambda *_: (0,) * nd, pipeline_mode=pl.Buffered(1))


def _rms_mod(x, g, shift, scale):
    ms = jnp.mean(x * x, axis=-1, keepdims=True)
    return x * lax.rsqrt(ms + RMS_EPS) * (g * (1.0 + scale)) + shift


def _swiglu_residual(x, h, gate, wg_ref, wu_ref, wd_ref, fchunk):
    acc = None
    for f0 in range(0, wg_ref.shape[1], fchunk):
        gg = jnp.dot(h, wg_ref[:, f0:f0 + fchunk], preferred_element_type=F32)
        uu = jnp.dot(h, wu_ref[:, f0:f0 + fchunk], preferred_element_type=F32)
        a = (gg * jax.nn.sigmoid(gg) * uu).astype(BF16)
        o = jnp.dot(a, wd_ref[f0:f0 + fchunk, :], preferred_element_type=F32)
        acc = o if acc is None else acc + o
    return x + acc * (MACARON_WEIGHT * gate)


def _ada_kernel(c_ref, w_ref, b_ref, o_ref):
    cv = c_ref[...]
    s = cv * jax.nn.sigmoid(cv)
    w = w_ref[...]
    s_hi, w_hi = s.astype(BF16), w.astype(BF16)
    s_lo = (s - s_hi.astype(F32)).astype(BF16)
    w_lo = (w - w_hi.astype(F32)).astype(BF16)
    dot = lambda a, b: jnp.dot(a, b, preferred_element_type=F32)
    o_ref[...] = dot(s_hi, w_hi) + dot(s_lo, w_hi) + dot(s_hi, w_lo) + b_ref[...]


def _ada(cvec, ada_w, ada_b):
    rows, d = cvec.shape
    n = ada_w.shape[1]
    bn = n // N_MOD if n % N_MOD == 0 else n
    return pl.pallas_call(
        _ada_kernel,
        grid=(n // bn,),
        in_specs=[pl.BlockSpec((rows, d), lambda j: (0, 0)),
                  pl.BlockSpec((d, bn), lambda j: (0, j)),
                  pl.BlockSpec((1, bn), lambda j: (0, j))],
        out_specs=pl.BlockSpec((rows, bn), lambda j: (0, j)),
        out_shape=jax.ShapeDtypeStruct((rows, n), F32),
        compiler_params=_params("arbitrary"),
    )(cvec, ada_w, ada_b.reshape(1, n))


def _ffn_kernel(x_ref, mod_ref, n_ref, wg_ref, wu_ref, wd_ref, o_ref, *, fchunk, mod0):
    x = x_ref[...]
    m = mod_ref[...]
    h = _rms_mod(x, n_ref[...], m[mod0:mod0 + 1], m[mod0 + 1:mod0 + 2]).astype(BF16)
    o_ref[...] = _swiglu_residual(x, h, m[mod0 + 2:mod0 + 3], wg_ref, wu_ref, wd_ref, fchunk)


def _ffn(x, mods3, mod_row, norm, wg, wu, wd, tm, fchunk, mod0):
    b, l, d = x.shape
    f = wg.shape[1]
    tok = pl.BlockSpec((None, tm, d), lambda bi, i: (bi, i, 0))
    return pl.pallas_call(
        functools.partial(_ffn_kernel, fchunk=fchunk, mod0=mod0),
        grid=(b, l // tm),
        in_specs=[tok, pl.BlockSpec((None, N_MOD, d), lambda bi, i: (mod_row(bi), 0, 0)),
                  _const_spec((1, d)), _const_spec((d, f)), _const_spec((d, f)), _const_spec((f, d))],
        out_specs=tok,
        out_shape=jax.ShapeDtypeStruct((b, l, d), F32),
        compiler_params=_params("arbitrary", "arbitrary"),
    )(x, mods3, norm, wg, wu, wd)


def _transpose_pieces(v):
    n = len(v)
    piece = lax.broadcasted_iota(jnp.int32, v[0].shape, 1) // S5_GROUP
    s = n // 2
    while s >= 1:
        keep = (piece & s) == 0
        nv = list(v)
        for i in range(n):
            if i & s == 0:
                a, b = v[i], v[i + s]
                nv[i] = jnp.where(keep, a, pltpu.roll(b, s * S5_GROUP, 1))
                nv[i + s] = jnp.where(keep, pltpu.roll(a, LANES - s * S5_GROUP, 1), b)
        v = nv
        s //= 2
    return v


def _to_chunk_major(up, store):
    nj = up.shape[0] // S5_CHUNK
    npc = LANES // S5_GROUP
    for gg in range(up.shape[1] // LANES):
        for tt in range(S5_CHUNK // npc):
            src = [up[(npc * tt + p) * nj:(npc * tt + p + 1) * nj, gg * LANES:(gg + 1) * LANES] for p in range(npc)]
            for gl, v in enumerate(_transpose_pieces(src)):
                store(gg * npc + gl, tt, v)


def _from_chunk_major(load, ng, nj):
    npc = LANES // S5_GROUP
    row_blocks = []
    for tt in range(S5_CHUNK // npc):
        per_p = [[] for _ in range(npc)]
        for gg in range(ng // npc):
            out = _transpose_pieces([load(gg * npc + gl, tt) for gl in range(npc)])
            for p in range(npc):
                per_p[p].append(out[p])
        row_blocks += [jnp.concatenate(blk, axis=1) for blk in per_p]
    return jnp.concatenate(row_blocks, axis=0)


def _proj_kernel(x_ref, mod_ref, n_ref, wnat_ref, wcm_ref, nat_ref, xs5_ref, cm_ref, *, col_major):
    x = x_ref[...]
    m = mod_ref[...]
    tm = x.shape[0]
    h2 = _rms_mod(x, n_ref[...], m[3:4], m[4:5]).astype(BF16)
    u = None
    ncol = 4 * MXU_TILE
    for c0 in range(0, NAT_W, ncol):
        p = jnp.dot(h2, wnat_ref[:, c0:c0 + ncol], preferred_element_type=F32)
        nat_ref[:, c0:c0 + ncol] = p.astype(BF16)
        if c0 == 0:
            u = p[:, 0:xs5_ref.shape[0] * S5_GROUP]
    pc = jnp.dot(h2, wcm_ref[...], preferred_element_type=F32)
    if col_major:
        pcm = pc.reshape(tm // GRID_W, GRID_W, CM_W)
        cm_ref[...] = jnp.swapaxes(pcm, 0, 1)
    else:
        cm_ref[...] = pc
    uf = u.reshape(tm // S5_CHUNK, S5_CHUNK, u.shape[1])
    up = jnp.swapaxes(uf, 0, 1).reshape(tm, u.shape[1])

    def store(g, tt, v):
        xs5_ref[g, :, tt * LANES:(tt + 1) * LANES] = v.astype(BF16)

    _to_chunk_major(up, store)


def _proj(x1, mods3, mod_row, w, tm, col_major):
    b, l, d = x1.shape
    ng = w["dskip"].shape[-1] // S5_GROUP
    nj = tm // S5_CHUNK
    nblk = l // tm
    tok = lambda width: pl.BlockSpec((None, tm, width), lambda bi, i: (bi, i, 0))
    if col_major:
        cm_spec = pl.BlockSpec((None, GRID_W, tm // GRID_W, CM_W), lambda bi, i: (bi, 0, i, 0))
        cm_shape = jax.ShapeDtypeStruct((b, GRID_W, l // GRID_W, CM_W), F32)
    else:
        cm_spec = tok(CM_W)
        cm_shape = jax.ShapeDtypeStruct((b, l, CM_W), F32)
    return pl.pallas_call(
        functools.partial(_proj_kernel, col_major=col_major),
        grid=(b, nblk),
        in_specs=[tok(d), pl.BlockSpec((None, N_MOD, d), lambda bi, i: (mod_row(bi), 0, 0)),
                  _const_spec((1, d)), _const_spec((d, NAT_W)), _const_spec((d, CM_W))],
        out_specs=[tok(NAT_W),
                   pl.BlockSpec((ng, nj, S5_CHUNK * S5_GROUP), lambda bi, i: (0, bi * nblk + i, 0)),
                   cm_spec],
        out_shape=[jax.ShapeDtypeStruct((b, l, NAT_W), BF16),
                   jax.ShapeDtypeStruct((ng, b * (l // S5_CHUNK), S5_CHUNK * S5_GROUP), BF16),
                   cm_shape],
        compiler_params=_params("arbitrary", "arbitrary"),
    )(x1, mods3, w["n2"], w["wnat"], w["wcm"])


def _s5_tile_prefix(sr, si, cst, fwd):
    ar, ai = sr, si
    for lvl, dist in enumerate((1, 2, 4)):
        sh = dist if fwd else S5_TILE - dist
        rr, ri = pltpu.roll(ar, sh, 0), pltpu.roll(ai, sh, 0)
        lr, li = cst[2 * lvl], cst[2 * lvl + 1]
        ar, ai = ar + (lr * rr - li * ri), ai + (lr * ri + li * rr)
    return ar, ai


def _s5_tile_carry(ar, ai, hr, hi, cst, fwd):
    row = lax.broadcasted_iota(jnp.int32, ar.shape, 0)
    keep = (row >= 1) if fwd else (row <= S5_TILE - 2)
    sh = 1 if fwd else S5_TILE - 1
    pr = jnp.where(keep, pltpu.roll(ar, sh, 0), 0.0)
    pi = jnp.where(keep, pltpu.roll(ai, sh, 0), 0.0)
    lpr, lpi, l8r, l8i = cst[6:10]
    hin_r = lpr * hr - lpi * hi + pr
    hin_i = lpr * hi + lpi * hr + pi
    e = S5_TILE - 1 if fwd else 0
    er = jnp.broadcast_to(ar[e:e + 1, :], ar.shape)
    ei = jnp.broadcast_to(ai[e:e + 1, :], ai.shape)
    return hin_r, hin_i, l8r * hr - l8i * hi + er, l8r * hi + l8i * hr + ei


def _s5_kernel(xc_ref, xl_ref, kt_ref, ws_ref, cp_ref, lam_ref, y_ref, s_ref, *, nb):
    npair = ws_ref.shape[0] // 2
    p = S5_STATE
    lo = lax.broadcasted_iota(jnp.int32, (ws_ref.shape[1], LANES), 1) < p
    half_turn = lambda a: pltpu.roll(a, p, 1)

    def state_in_operator(g):
        blocks = []
        for k in range(2):
            a = ws_ref[g, :, k * LANES:(k + 1) * LANES]
            if g % 2 == 0:
                blocks += [jnp.where(lo, a, 0.0), jnp.where(lo, half_turn(a), 0.0)]
            else:
                blocks += [jnp.where(lo, 0.0, half_turn(a)), jnp.where(lo, 0.0, a)]
        return jnp.concatenate(blocks, axis=1).astype(BF16)

    def state_out_operator(pp):
        zeros = jnp.zeros((p, cp_ref.shape[2]), BF16)
        rows = []
        for part in range(4):
            for gi in range(2):
                c = cp_ref[2 * pp + gi, part * p:(part + 1) * p, :]
                rows.append(jnp.concatenate([c, zeros] if gi == 0 else [zeros, c], axis=1))
        return jnp.concatenate(rows, axis=0)

    jc = xc_ref.shape[1] // nb
    jl = xl_ref.shape[1] // nb
    jt = jc + jl
    for pp in range(npair):
        w_in = [state_in_operator(2 * pp), state_in_operator(2 * pp + 1)]
        for src, j0, nj in ((xc_ref, 0, jc), (xl_ref, jc, jl)):
            s = jnp.dot(src[2 * pp], w_in[0], preferred_element_type=F32)
            s = s + jnp.dot(src[2 * pp + 1], w_in[1], preferred_element_type=F32)
            for bi in range(nb):
                s_ref[pp, bi * jt + j0:bi * jt + j0 + nj, :] = s[bi * nj:(bi + 1) * nj, :]

    nct = jc // S5_TILE
    nt = jt // S5_TILE
    zero = jnp.zeros((S5_TILE, LANES), F32)
    for pp in range(npair):

        def prefix(it, carry, pp=pp):
            tiles = []
            for u in range(S5_PREFIX_TILES):
                r0 = pl.multiple_of((it * S5_PREFIX_TILES + u) * S5_TILE, S5_TILE)
                for c0, fwd in ((0, True), (256, False)):
                    tiles.append((r0, c0, fwd, s_ref[pp, pl.ds(r0, S5_TILE), c0:c0 + 128],
                                  s_ref[pp, pl.ds(r0, S5_TILE), c0 + 128:c0 + 256]))
            cst = {fwd: [lam_ref[pp, (0 if fwd else 10) + k] for k in range(6)] for fwd in (True, False)}
            done = [(r0, c0) + _s5_tile_prefix(sr, si, cst[fwd], fwd) for r0, c0, fwd, sr, si in tiles]
            for r0, c0, ar, ai in done:
                s_ref[pp, pl.ds(r0, S5_TILE), c0:c0 + 128] = ar
                s_ref[pp, pl.ds(r0, S5_TILE), c0 + 128:c0 + 256] = ai
            return carry

        lax.fori_loop(0, nb * nt // S5_PREFIX_TILES, prefix, 0)

        def step(it, carry, pp=pp):
            mb = jnp.where(it < nct, nct - 1 - it, nt - 1 - (it - nct))
            cst = {True: [lam_ref[pp, k] for k in range(10)], False: [lam_ref[pp, 10 + k] for k in range(10)]}
            tiles = []
            for bi in range(nb):
                for c0, fwd, tile in ((0, True, it), (256, False, mb)):
                    r0 = pl.multiple_of(bi * jt + tile * S5_TILE, S5_TILE)
                    tiles.append((bi, r0, c0, fwd, s_ref[pp, pl.ds(r0, S5_TILE), c0:c0 + 128],
                                  s_ref[pp, pl.ds(r0, S5_TILE), c0 + 128:c0 + 256]))
            out = [[None, None] for _ in range(nb)]
            for bi, r0, c0, fwd, ar, ai in tiles:
                hr, hi = carry[bi][0 if fwd else 1]
                hin_r, hin_i, hr, hi = _s5_tile_carry(ar, ai, hr, hi, cst[fwd], fwd)
                out[bi][0 if fwd else 1] = (hr, hi)
                s_ref[pp, pl.ds(r0, S5_TILE), c0:c0 + 128] = hin_r
                s_ref[pp, pl.ds(r0, S5_TILE), c0 + 128:c0 + 256] = hin_i
            return tuple(tuple(o) for o in out)

        lax.fori_loop(0, nt, step, tuple(((zero, zero), (zero, zero)) for _ in range(nb)))

    tc = S5_CHUNK * S5_GROUP
    width = kt_ref.shape[2]
    for pp in range(npair):
        toeps = []
        for gi in range(2):
            kt = kt_ref[2 * pp + gi]
            blocks = []
            for s in range(S5_CHUNK):
                sh = (width - S5_GROUP * (S5_CHUNK - 1 - s)) % width
                blocks.append((kt if sh == 0 else pltpu.roll(kt, sh, 1))[:, 0:tc])
            toeps.append(jnp.concatenate(blocks, axis=0).astype(BF16))
        w_out = state_out_operator(pp)
        for bi in range(nb):
            hin = s_ref[pp, bi * jt + jc:(bi + 1) * jt, :].astype(BF16)
            yp = jnp.dot(hin, w_out, preferred_element_type=F32)
            for gi in range(2):
                g = 2 * pp + gi
                y = yp[:, gi * 256:(gi + 1) * 256] + jnp.dot(xl_ref[g, bi * jl:(bi + 1) * jl, :], toeps[gi],
                                                             preferred_element_type=F32)
                y_ref[g, bi * jl:(bi + 1) * jl, :] = y.astype(BF16)


def _s5(xc, xl, kt, ws, cp, lamc, gps, nb):
    ng, rc, _ = xc.shape
    rl = xl.shape[1]
    npair = gps // 2
    return pl.pallas_call(
        functools.partial(_s5_kernel, nb=nb),
        grid=(ng // gps,),
        in_specs=[pl.BlockSpec((gps, rc, 256), lambda i: (i, 0, 0)),
                  pl.BlockSpec((gps, rl, 256), lambda i: (i, 0, 0)),
                  pl.BlockSpec((gps,) + kt.shape[1:], lambda i: (i, 0, 0)),
                  pl.BlockSpec((gps,) + ws.shape[1:], lambda i: (i, 0, 0)),
                  pl.BlockSpec((gps,) + cp.shape[1:], lambda i: (i, 0, 0)),
                  pl.BlockSpec((npair, 20, S5_TILE, LANES), lambda i: (i, 0, 0, 0))],
        out_specs=pl.BlockSpec((gps, rl, 256), lambda i: (i, 0, 0)),
        out_shape=jax.ShapeDtypeStruct((ng, rl, 256), BF16),
        scratch_shapes=[pltpu.VMEM((npair, rc + rl, 512), F32)],
        compiler_params=_params("arbitrary"),
    )(xc, xl, kt, ws, cp, lamc)


def _cmul(ar, ai, br, bi):
    return ar * br - ai * bi, ar * bi + ai * br


def _s5_weights(lam_re, lam_im, log_dt, b_re, b_im, c_re, c_im):
    nd, ng, p = lam_re.shape
    t = S5_CHUNK
    tc = t * S5_GROUP
    lr = jnp.minimum(lam_re.astype(F32), -1e-4)
    li = lam_im.astype(F32)
    dt = jnp.exp(log_dt.astype(F32))[..., None]
    mag = jnp.exp(lr * dt)
    lbr, lbi = mag * jnp.cos(li * dt), mag * jnp.sin(li * dt)
    den = lr * lr + li * li
    fr = ((lbr - 1.0) * lr + lbi * li) / den
    fi = (lbi * lr - (lbr - 1.0) * li) / den
    bbr, bbi = _cmul(fr[..., None], fi[..., None], b_re.astype(F32), b_im.astype(F32))
    cr, ci = c_re.astype(F32), c_im.astype(F32)

    pr, pi = [jnp.ones_like(lbr)], [jnp.zeros_like(lbr)]
    for _ in range(t):
        nr, ni = _cmul(pr[-1], pi[-1], lbr, lbi)
        pr.append(nr)
        pi.append(ni)
    pr, pi = jnp.stack(pr), jnp.stack(pi)
    wr, wi = _cmul(pr[:t, ..., None], pi[:t, ..., None], bbr[None], bbi[None])

    kern = jnp.einsum("dgxp,kdgpc->dgkxc", cr, wr) - jnp.einsum("dgxp,kdgpc->dgkxc", ci, wi)
    ktf = kern[0].transpose(0, 3, 1, 2)
    ktb = kern[1][:, ::-1].transpose(0, 3, 1, 2)
    kt = jnp.concatenate([ktb[:, :, :t - 1], ktf[:, :, 0:1] + ktb[:, :, t - 1:t], ktf[:, :, 1:]], axis=2)
    kt = kt.reshape(ng, S5_GROUP, (2 * t - 1) * S5_GROUP)
    kt = jnp.pad(kt, ((0, 0), (0, 0), (0, 2 * tc - kt.shape[-1])))

    gscp = lambda a: a.transpose(1, 0, 3, 2).reshape(ng, tc, p)
    parts = [gscp(wr[::-1, 0]), gscp(wi[::-1, 0]), gscp(wr[:, 1]), gscp(wi[:, 1])]
    ws = jnp.concatenate(parts, axis=-1)

    def readout(d, powr, powi):
        mr, mi = _cmul(cr[d][None], ci[d][None], powr[:, :, None, :], powi[:, :, None, :])
        to_gptx = lambda a: a.transpose(1, 3, 0, 2).reshape(ng, p, tc)
        return [to_gptx(mr), to_gptx(-mi)]

    parts = readout(0, pr[1:t + 1, 0], pi[1:t + 1, 0]) + readout(1, pr[1:t + 1, 1][::-1], pi[1:t + 1, 1][::-1])
    cp = jnp.concatenate(parts, axis=1)

    row = jnp.arange(S5_TILE)
    l1 = (pr[t], pi[t])
    l2 = _cmul(*l1, *l1)
    l4 = _cmul(*l2, *l2)
    l8 = _cmul(*l4, *l4)
    rp = [(jnp.ones_like(lbr), jnp.zeros_like(lbr))]
    for _ in range(S5_TILE - 1):
        rp.append(_cmul(*rp[-1], *l1))
    planes = []
    for d in range(nd):
        valid = (lambda dist: row >= dist) if d == 0 else (lambda dist: row <= S5_TILE - 1 - dist)
        for (qr, qi), dist in ((l1, 1), (l2, 2), (l4, 4)):
            m = valid(dist).astype(F32)[:, None, None]
            planes += [m * qr[d][None], m * qi[d][None]]
        order = row if d == 0 else row[::-1]
        planes += [jnp.stack([rp[k][0][d] for k in range(S5_TILE)])[order],
                   jnp.stack([rp[k][1][d] for k in range(S5_TILE)])[order]]
        planes += [jnp.broadcast_to(l8[0][d], (S5_TILE, ng, p)), jnp.broadcast_to(l8[1][d], (S5_TILE, ng, p))]
    lamc = jnp.stack(planes)
    lamc = lamc.reshape(20, S5_TILE, ng // 2, 2 * p).transpose(2, 0, 1, 3)
    return kt, ws, cp.astype(BF16), lamc


def _chunk_cumsum(g, rev):
    sub = 8
    nt = g.shape[0] // sub
    row = lax.broadcasted_iota(jnp.int32, (sub, g.shape[1]), 0)
    out = [None] * nt
    off = None
    for kk in (range(nt - 1, -1, -1) if rev else range(nt)):
        x = g[kk * sub:(kk + 1) * sub, :]
        for dist in (1, 2, 4):
            if rev:
                x = x + jnp.where(row < sub - dist, pltpu.roll(x, sub - dist, 0), 0.0)
            else:
                x = x + jnp.where(row >= dist, pltpu.roll(x, dist, 0), 0.0)
        if off is not None:
            x = x + off
        out[kk] = x
        e = 0 if rev else sub - 1
        off = jnp.broadcast_to(x[e:e + 1, :], x.shape)
    return jnp.concatenate(out, axis=0)


def _gla_log_decay(glr, gup, gbias, rev):
    c = GLA_CHUNK
    z = jnp.dot(glr.astype(BF16), gup, preferred_element_type=F32) + gbias
    g = (jnp.minimum(z, 0.0) - jnp.log(1.0 + jnp.exp(-jnp.abs(z)))) * (1.0 / GLA_GATE_NORM)
    return [_chunk_cumsum(g[n * c:(n + 1) * c, :], rev) for n in range(g.shape[0] // c)]


def _gla_chunk(q, k, v, gc, st, rev, need_out):
    c = GLA_CHUNK
    i_ref = c // 2 - 1 if rev else c // 2
    i_last = 0 if rev else c - 1
    g_ref = gc[i_ref:i_ref + 1, :]
    g_last = gc[i_last:i_last + 1, :]
    lane_head = lax.broadcasted_iota(jnp.int32, (1, GLA_KEY), 1) // GLA_DK

    stack = lambda a: jnp.concatenate(
        [jnp.where(lane_head == h, a, 0.0) for h in range(GLA_HEADS)], axis=0).astype(BF16)
    dv2 = 2 * GLA_DV
    vt = jnp.concatenate(
        [jnp.concatenate([v[:, p * dv2:p * dv2 + GLA_DV], v[:, p * dv2 + GLA_DV:(p + 1) * dv2]], axis=0).T
         for p in range(GLA_HEADS // 2)], axis=1).astype(BF16)
    kl = k * jnp.exp(g_last - gc)
    kv = jnp.dot(vt, stack(kl), preferred_element_type=F32)
    st_new = st * jnp.exp(g_last) + kv
    if not need_out:
        return None, st_new

    qe = q * jnp.exp(gc - g_ref)
    ke = (k * jnp.exp(g_ref - gc)).astype(BF16)
    qg = q * jnp.exp(gc)
    nt_dims = (((1,), (1,)), ((), ()))
    sc = lax.dot_general(stack(qe), ke, nt_dims, preferred_element_type=F32)
    rs = lax.broadcasted_iota(jnp.int32, (GLA_HEADS * c, c), 0) % c
    cs = lax.broadcasted_iota(jnp.int32, (GLA_HEADS * c, c), 1)
    keep = (rs <= cs) if rev else (rs >= cs)
    sc = jnp.where(keep, sc, 0.0).astype(BF16)
    vb = v.astype(BF16)
    oo = lax.dot_general(stack(qg), st.astype(BF16), nt_dims, preferred_element_type=F32)
    o = jnp.concatenate(
        [jnp.dot(sc[h * c:(h + 1) * c, :], vb[:, h * GLA_DV:(h + 1) * GLA_DV], preferred_element_type=F32)
         + oo[h * c:(h + 1) * c, :] for h in range(GLA_HEADS)], axis=1)
    return o, st_new


def _gla_kernel(cf_ref, cb_ref, cc_ref, gup_ref, gbias_ref, of_ref, ob_ref, stf_ref, stb_ref):
    c = GLA_CHUNK
    q0, k0, v0, r0 = 0, GLA_KEY, 2 * GLA_KEY, 2 * GLA_KEY + GLA_VAL

    def decay(ref, d):
        return _gla_log_decay(ref[:, r0:CM_W], gup_ref[d], gbias_ref[d], d == 1)

    def run(ref, n, gcs, d, st, need_out):
        rows = slice(n * c, (n + 1) * c)
        q = ref[rows, q0:k0] if need_out else None
        return _gla_chunk(q, ref[rows, k0:v0], ref[rows, v0:r0], gcs[n], st, d == 1, need_out)

    @pl.when(pl.program_id(1) == 0)
    def _():
        nctx = cc_ref.shape[0] // c
        gf, gb = decay(cc_ref, 0), decay(cc_ref, 1)
        stf = jnp.zeros(stf_ref.shape, F32)
        stb = jnp.zeros(stb_ref.shape, F32)
        for n in range(nctx):
            _, stf = run(cc_ref, n, gf, 0, stf, False)
            _, stb = run(cc_ref, nctx - 1 - n, gb, 1, stb, False)
        stf_ref[...] = stf
        stb_ref[...] = stb

    ncols, rows_per_col, _ = cf_ref.shape
    nch = rows_per_col // c
    stf = stf_ref[...]
    stb = stb_ref[...]
    for j in range(ncols):
        jb = ncols - 1 - j
        gf, gb = decay(cf_ref.at[j], 0), decay(cb_ref.at[jb], 1)
        for n in range(nch):
            o, stf = run(cf_ref.at[j], n, gf, 0, stf, True)
            of_ref[j, n * c:(n + 1) * c, :] = o
            m = nch - 1 - n
            o, stb = run(cb_ref.at[jb], m, gb, 1, stb, True)
            ob_ref[jb, m * c:(m + 1) * c, :] = o
    stf_ref[...] = stf
    stb_ref[...] = stb


def _gla(cm, cmc, gup, gbias):
    b, ncol, rows, _ = cm.shape
    lc = cmc.shape[1]
    cps = GLA_COLS_PER_STEP
    last = ncol // cps - 1
    col = lambda width, fn: pl.BlockSpec((None, cps, rows, width), fn)
    out_shape = jax.ShapeDtypeStruct((b, ncol, rows, GLA_VAL), F32)
    return pl.pallas_call(
        _gla_kernel,
        grid=(b, ncol // cps),
        in_specs=[col(CM_W, lambda bi, ci: (bi, ci, 0, 0)),
                  col(CM_W, lambda bi, ci: (bi, last - ci, 0, 0)),
                  pl.BlockSpec((None, lc, CM_W), lambda bi, ci: (bi, 0, 0)),
                  _const_spec(gup.shape), _const_spec(gbias.shape)],
        out_specs=[col(GLA_VAL, lambda bi, ci: (bi, ci, 0, 0)),
                   col(GLA_VAL, lambda bi, ci: (bi, last - ci, 0, 0))],
        out_shape=[out_shape, out_shape],
        scratch_shapes=[pltpu.VMEM((GLA_DV, GLA_KEY), F32), pltpu.VMEM((GLA_DV, GLA_KEY), F32)],
        compiler_params=_params("arbitrary", "arbitrary"),
    )(cm, cm, cmc, gup, gbias)


def _back_kernel(x1_ref, nat_ref, y_ref, of_ref, ob_ref, mod_ref, dskip_ref, gluw_ref,
                 glub_ref, s5out_ref, gnorm_ref, glaout_ref, wo_ref, n3_ref, wg_ref, wu_ref, wd_ref, fin_ref,
                 out_ref, *, fchunk):
    m = mod_ref[...]
    tm, d = x1_ref.shape
    ng, nj, _ = y_ref.shape
    s5w = ng * S5_GROUP

    yp = _from_chunk_major(lambda g, tt: y_ref[g, :, tt * LANES:(tt + 1) * LANES].astype(F32), ng, nj)
    ys = jnp.swapaxes(yp.reshape(S5_CHUNK, nj, s5w), 0, 1).reshape(tm, s5w)
    ya = ys + dskip_ref[...] * nat_ref[:, 0:s5w].astype(F32)
    ya = 0.5 * ya * (1.0 + jnp.tanh(GELU_C0 * (ya + GELU_C1 * (ya * ya * ya))))
    gl = jnp.dot(ya.astype(BF16), gluw_ref[...], preferred_element_type=F32) + glub_ref[...]
    ya = ya * jax.nn.sigmoid(gl)

    ocm = of_ref[...] + ob_ref[...]
    o = jnp.swapaxes(ocm, 0, 1).reshape(tm, GLA_VAL)
    heads = []
    for h in range(GLA_HEADS):
        oh = o[:, h * GLA_DV:(h + 1) * GLA_DV]
        heads.append(oh * lax.rsqrt(jnp.mean(oh * oh, axis=-1, keepdims=True) + RMS_EPS))
    r = nat_ref[:, s5w:s5w + GLA_VAL].astype(F32)
    yb = jnp.concatenate(heads, axis=1) * gnorm_ref[...] * (r * jax.nn.sigmoid(r))

    pa = jnp.dot(ya.astype(BF16), s5out_ref[...], preferred_element_type=F32)
    pb = jnp.dot(yb.astype(BF16), glaout_ref[...], preferred_element_type=F32)
    ga = nat_ref[:, s5w + GLA_VAL:s5w + GLA_VAL + d].astype(F32)
    gb = nat_ref[:, s5w + GLA_VAL + d:s5w + GLA_VAL + 2 * d].astype(F32)
    mg = jax.nn.sigmoid(ga) * pa + jax.nn.sigmoid(gb) * pb
    y = jnp.dot(mg.astype(BF16), wo_ref[...], preferred_element_type=F32)
    x2 = x1_ref[...] + m[5:6] * y
    h = _rms_mod(x2, n3_ref[...], m[6:7], m[7:8]).astype(BF16)
    x3 = _swiglu_residual(x2, h, m[8:9], wg_ref, wu_ref, wd_ref, fchunk)
    ms = jnp.mean(x3 * x3, axis=-1, keepdims=True)
    out_ref[...] = x3 * lax.rsqrt(ms + RMS_EPS) * fin_ref[...]


def _back(x1, nat, ys5, of, ob, mods3, w, tm, fchunk):
    b, l, d = x1.shape
    f = w["wg2"].shape[1]
    s5w = w["dskip"].shape[-1]
    ng = s5w // S5_GROUP
    nblk = l // tm
    tok = lambda width: pl.BlockSpec((None, tm, width), lambda bi, i: (bi, i, 0))
    colblk = pl.BlockSpec((None, GRID_W, tm // GRID_W, GLA_VAL), lambda bi, i: (bi, 0, i, 0))
    return pl.pallas_call(
        functools.partial(_back_kernel, fchunk=fchunk),
        grid=(b, nblk),
        in_specs=[tok(d), tok(NAT_W),
                  pl.BlockSpec((ng, tm // S5_CHUNK, S5_CHUNK * S5_GROUP), lambda bi, i: (0, bi * nblk + i, 0)),
                  colblk, colblk,
                  pl.BlockSpec((None, N_MOD, d), lambda bi, i: (bi, 0, 0)),
                  _const_spec((1, s5w)), _const_spec((s5w, s5w)), _const_spec((1, s5w)),
                  _const_spec((s5w, d)), _const_spec((1, GLA_VAL)), _const_spec((GLA_VAL, d)),
                  _const_spec((d, d)), _const_spec((1, d)), _const_spec((d, f)), _const_spec((d, f)),
                  _const_spec((f, d)), _const_spec((1, d))],
        out_specs=tok(d),
        out_shape=jax.ShapeDtypeStruct((b, l, d), F32),
        compiler_params=_params("arbitrary", "arbitrary"),
    )(x1, nat, ys5, of, ob, mods3, w["dskip"], w["gluw"], w["glub"], w["s5out"], w["gnorm"], w["glaout"],
      w["wo"], w["n3"], w["wg2"], w["wu2"], w["wd2"], w["fin"])


def kernel(x, c, ctx, c_ctx, ada_w, ada_b, ffn1_norm, ffn1_w_gate, ffn1_w_up, ffn1_w_down, mix_norm, w_in,
           s5_lambda_re, s5_lambda_im, s5_log_dt, s5_b_re, s5_b_im, s5_c_re, s5_c_im, s5_d, s5_glu_w, s5_glu_b,
           s5_out, gla_gate_up, gla_gate_b, gla_norm, gla_out, w_o, ffn2_norm, ffn2_w_gate, ffn2_w_up,
           ffn2_w_down, final_norm):
    b, l, d = x.shape
    lc = ctx.shape[1]
    assert ada_w.shape[0] == 1 and b + 1 <= SUBLANES
    s5w = s5_d.shape[-1]
    fchunk = MXU_TILE
    tm = min(512, l)
    assert tm % (GRID_W * 8) == 0 and l % tm == 0 and lc % (S5_CHUNK * S5_TILE) == 0 and lc % GLA_CHUNK == 0

    cvec = jnp.concatenate([c, c_ctx[None, :], jnp.zeros((SUBLANES - b - 1, d), F32)], axis=0)
    mods3 = _ada(cvec, ada_w[0], ada_b[0]).reshape(SUBLANES, N_MOD, d)

    wi = w_in[0]
    o_q, o_k, o_v, o_r = s5w, s5w + GLA_KEY, s5w + 2 * GLA_KEY, s5w + 2 * GLA_KEY + GLA_VAL
    o_glr = o_r + GLA_VAL
    o_ga = o_glr + 2 * GLA_GATE_RANK
    wnat = jnp.concatenate([wi[:, :o_q], wi[:, o_r:o_glr], wi[:, o_ga:]], axis=1)
    wcm = jnp.concatenate([wi[:, o_q:o_k] * (GLA_DK ** -0.5), wi[:, o_k:o_r],
                           jnp.pad(wi[:, o_glr:o_ga], ((0, 0), (0, LANES - 2 * GLA_GATE_RANK)))], axis=1)
    gup = jnp.zeros((2, LANES, GLA_KEY), F32)
    gup = gup.at[0, 0:GLA_GATE_RANK].set(gla_gate_up[0, 0])
    gup = gup.at[1, GLA_GATE_RANK:2 * GLA_GATE_RANK].set(gla_gate_up[0, 1])
    row = lambda v: v.reshape(1, -1).astype(F32)
    w = dict(
        n1=row(ffn1_norm[0]), wg1=ffn1_w_gate[0].astype(BF16), wu1=ffn1_w_up[0].astype(BF16),
        wd1=ffn1_w_down[0].astype(BF16), n2=row(mix_norm[0]), wnat=wnat.astype(BF16), wcm=wcm.astype(BF16),
        dskip=row(s5_d[0]), gluw=s5_glu_w[0].astype(BF16), glub=row(s5_glu_b[0]), s5out=s5_out[0].astype(BF16),
        gnorm=row(gla_norm[0]), glaout=gla_out[0].astype(BF16), wo=w_o[0].astype(BF16),
        n3=row(ffn2_norm[0]), wg2=ffn2_w_gate[0].astype(BF16), wu2=ffn2_w_up[0].astype(BF16),
        wd2=ffn2_w_down[0].astype(BF16), fin=row(final_norm))

    lat_row = lambda bi: bi
    ctx_row = lambda bi: b
    x1 = _ffn(x, mods3, lat_row, w["n1"], w["wg1"], w["wu1"], w["wd1"], tm, fchunk, 0)
    tc = min(tm, b * lc)
    c1 = _ffn(ctx.reshape(1, b * lc, d), mods3, ctx_row, w["n1"], w["wg1"], w["wu1"], w["wd1"], tc, fchunk, 0)
    nat, xl, cm = _proj(x1, mods3, lat_row, w, min(2 * tm, l), True)
    _, xc, cmc = _proj(c1, mods3, ctx_row, w, tc, False)
    cmc = cmc.reshape(b, lc, CM_W)

    kt, ws, cp, lamc = _s5_weights(s5_lambda_re[0], s5_lambda_im[0], s5_log_dt[0], s5_b_re[0], s5_b_im[0],
                                     s5_c_re[0], s5_c_im[0])
    ys5 = _s5(xc, xl, kt, ws, cp, lamc, gps=4, nb=b)

    of, ob = _gla(cm, cmc, gup.astype(BF16), gla_gate_b[0].reshape(2, 1, GLA_KEY).astype(F32))
    return _back(x1, nat, ys5, of, ob, mods3, w, tm, fchunk)
```
